```python
import functools
import jax
import jax.numpy as jnp
from jax import lax
import numpy as np

D_MODEL = 2048
BATCH = 8
SEQ = 4096
DEPTH = 2

GRID_W = 64
CTX_LEN = 256
EPS = 1e-6
N_MOD = 6

HG_HEADS = 8
HG_DK = 128
HG_DV = D_MODEL // 2 // HG_HEADS
HG_F = HG_HEADS * HG_DK
HG_V = HG_HEADS * HG_DV
HG_CHUNK = 32
CONV_C = D_MODEL // 2
CONV_W = 31
HG_SCAN_COLS = 3 * HG_F + HG_V
EV_IN = HG_SCAN_COLS + HG_V + 2 * CONV_C
EV_MIX = HG_V + CONV_C

RET_HEADS = 8
RET_DK = D_MODEL // RET_HEADS
RET_DV = 2 * RET_DK
RET_QK = RET_HEADS * RET_DK
RET_V = RET_HEADS * RET_DV
RET_QKV_COLS = 2 * RET_QK + RET_V
RET_IN = RET_QKV_COLS + RET_V
RET_CHUNK = 128
ROPE_BASE = 10000.0

N_EXPERTS = 64
EXPERT_FF = D_MODEL // 4
SHARED_FF = D_MODEL // 4
TOP_K = 8
N_GROUPS = 8
TOPK_GROUPS = 4
ROUTED_SCALE = 2.5
MOE_BLOCK = 256

kernel_name = 'hybrid_hgrn2_conv_retention_moe_dit'


def rms_norm(x, g=None):
    xf = x.astype(jnp.float32)
    y = xf * lax.rsqrt(jnp.mean(xf * xf, axis=-1, keepdims=True) + EPS)
    if g is not None:
        y = y * g.astype(jnp.float32)
    return y.astype(x.dtype)


def layer_norm(x, g, b):
    xf = x.astype(jnp.float32)
    mu = jnp.mean(xf, axis=-1, keepdims=True)
    xc = xf - mu
    var = jnp.mean(xc * xc, axis=-1, keepdims=True)
    return (xc * lax.rsqrt(var + EPS) * g.astype(jnp.float32) + b.astype(jnp.float32)).astype(x.dtype)


def adaln(cond, w, b):
    return jax.nn.silu(cond) @ w + b


def modulate(h, shift, scale):
    return h * (1.0 + scale) + shift


def to_chunks(t, chunk):
    b, h, n, d = t.shape
    return t.reshape(b, h, n // chunk, chunk, d).transpose(2, 0, 1, 3, 4)


def from_chunks(t):
    n, b, h, c, d = t.shape
    return t.transpose(1, 2, 0, 3, 4).reshape(b, h, n * c, d)


def gla_chunk_scan(s0, q, k, v, logf, *, want_out):
    c = HG_CHUNK
    causal = jnp.tril(jnp.ones((c, c), bool))[:, :, None]
    xs = tuple(to_chunks(t.astype(jnp.float32), c) for t in (q, k, v, logf))

    def step(s, inp):
        qc, kc, vc, fc = inp
        b = jnp.cumsum(fc, axis=2)
        b_end = b[:, :, -1:, :]
        s_new = jnp.exp(b_end)[:, :, 0, :, None] * s + jnp.einsum('bhjd,bhje->bhde', kc * jnp.exp(b_end - b), vc)
        if not want_out:
            return s_new, None
        inter = jnp.einsum('bhid,bhde->bhie', qc * jnp.exp(b), s)
        decay = jnp.exp(jnp.where(causal, b[:, :, :, None, :] - b[:, :, None, :, :], -jnp.inf))
        scores = jnp.einsum('bhid,bhjd,bhijd->bhij', qc, kc, decay)
        return s_new, inter + jnp.einsum('bhij,bhje->bhie', scores, vc)

    s_fin, o = lax.scan(step, s0, xs)
    return s_fin, (from_chunks(o).astype(v.dtype) if want_out else None)


def retention_chunk_scan(s0, q, k, v, *, log_g, want_out):
    c = RET_CHUNK
    idx = jnp.arange(c, dtype=jnp.float32)
    lg = log_g.astype(jnp.float32)[:, None]
    rel = idx[:, None] - idx[None, :]
    dmask = jnp.exp(jnp.where(rel >= 0, lg[:, :, None] * rel, -jnp.inf))
    q_dec = jnp.exp(lg * (idx + 1.0))[:, :, None]
    k_dec = jnp.exp(lg * (c - 1.0 - idx))[:, :, None]
    c_dec = jnp.exp(lg * c)[:, :, None]
    xs = tuple(to_chunks(t.astype(jnp.float32), c) for t in (q, k, v))

    def step(s, inp):
        qc, kc, vc = inp
        s_new = c_dec * s + jnp.einsum('bhjd,bhje->bhde', kc * k_dec, vc)
        if not want_out:
            return s_new, None
        scores = jnp.einsum('bhid,bhjd->bhij', qc, kc) * dmask
        inter = jnp.einsum('bhid,bhde->bhie', qc * q_dec, s)
        return s_new, inter + jnp.einsum('bhij,bhje->bhie', scores, vc)

    s_fin, o = lax.scan(step, s0, xs)
    return s_fin, (from_chunks(o).astype(v.dtype) if want_out else None)


def prefix_scan(scan_fn, lat, ctx, reverse, ctx_out):
    if reverse:
        lat = tuple(jnp.flip(t, axis=2) for t in lat)
        ctx = tuple(jnp.flip(t, axis=2) for t in ctx)
    q, v = ctx[0], ctx[2]
    s0 = jnp.zeros(q.shape[:2] + (q.shape[-1], v.shape[-1]), jnp.float32)
    s_ctx, o_ctx = scan_fn(s0, *ctx, want_out=ctx_out)
    _, o_lat = scan_fn(s_ctx, *lat, want_out=True)
    if reverse:
        o_lat = jnp.flip(o_lat, axis=2)
        o_ctx = None if o_ctx is None else jnp.flip(o_ctx, axis=2)
    return o_lat, o_ctx


def depthwise_conv(u, w, b):
    y = lax.conv_general_dilated(u, w[:, None, :], window_strides=(1,),
                                 padding=[(CONV_W // 2, CONV_W // 2)],
                                 dimension_numbers=('NWC', 'WIO', 'NWC'),
                                 feature_group_count=u.shape[-1])
    return y + b


def rope_axis(x, pos):
    half = x.shape[-1] // 2
    inv = 1.0 / (ROPE_BASE ** (jnp.arange(half, dtype=jnp.float32) / half))
    ang = pos.astype(jnp.float32)[:, None] * inv
    cos, sin = jnp.cos(ang)[:, None, :], jnp.sin(ang)[:, None, :]
    x1, x2 = x[..., :half].astype(jnp.float32), x[..., half:].astype(jnp.float32)
    return jnp.concatenate([x1 * cos - x2 * sin, x2 * cos + x1 * sin], axis=-1).astype(x.dtype)


def rope_2d(x):
    t = jnp.arange(x.shape[1])
    row, col = t // GRID_W, t % GRID_W
    dh = x.shape[-1] // 2
    return jnp.concatenate([rope_axis(x[..., :dh], row), rope_axis(x[..., dh:], col)], axis=-1)


def hgrn2_conformer_mixer(hx, hc, w_in, w_out, lb, hg_gain, conv_w, conv_b, cn_g, cn_b, ctx_out):
    def heads(t):
        b, n, _ = t.shape
        return t.reshape(b, n, HG_HEADS, -1).transpose(0, 2, 1, 3)

    def forget(fr):
        f = lb + (1.0 - lb) * jax.nn.sigmoid(fr.astype(jnp.float32))
        return heads((1.0 - f).astype(fr.dtype)), heads(jnp.log(f))

    def scan_inputs(p):
        q = heads(jax.nn.silu(p[..., :HG_F]))
        k_f, lf_f = forget(p[..., HG_F:2 * HG_F])
        k_b, lf_b = forget(p[..., 2 * HG_F:3 * HG_F])
        v = heads(p[..., 3 * HG_F:HG_SCAN_COLS])
        return (q, k_f, v, lf_f), (q, k_b, v, lf_b)

    def readout(p, o):
        b, n, _ = p.shape
        gate = p[..., HG_SCAN_COLS:HG_SCAN_COLS + HG_V]
        a0 = HG_SCAN_COLS + HG_V
        glu = p[..., a0:a0 + CONV_C] * jax.nn.sigmoid(p[..., a0 + CONV_C:])
        u = jax.nn.silu(layer_norm(depthwise_conv(glu, conv_w, conv_b), cn_g, cn_b))
        r = rms_norm(o.transpose(0, 2, 1, 3).reshape(b, n, HG_V), hg_gain) * jax.nn.silu(gate)
        return jnp.concatenate([r, u], axis=-1) @ w_out

    px = hx @ w_in
    pc = hc @ (w_in if ctx_out else w_in[:, :HG_SCAN_COLS])
    lat_f, lat_b = scan_inputs(px)
    ctx_f, ctx_b = scan_inputs(pc)
    of_x, of_c = prefix_scan(gla_chunk_scan, lat_f, ctx_f, False, ctx_out)
    ob_x, ob_c = prefix_scan(gla_chunk_scan, lat_b, ctx_b, True, ctx_out)
    yx = readout(px, of_x + ob_x)
    yc = readout(pc, of_c + ob_c) if ctx_out else None
    return yx, yc


def retention_mixer(hx, hc, w_in, w_out, decay_logit, ctx_out):
    def scan_inputs(p, rotate):
        b, n, _ = p.shape
        q = p[..., :RET_QK].reshape(b, n, RET_HEADS, RET_DK)
        k = p[..., RET_QK:2 * RET_QK].reshape(b, n, RET_HEADS, RET_DK) * (RET_DK ** -0.5)
        v = p[..., 2 * RET_QK:RET_QKV_COLS].reshape(b, n, RET_HEADS, RET_DV)
        if rotate:
            q, k = rope_2d(q), rope_2d(k)
        return tuple(t.transpose(0, 2, 1, 3) for t in (q, k, v))

    def readout(p, o):
        b, n, _ = p.shape
        o = rms_norm(o.transpose(0, 2, 1, 3)).reshape(b, n, RET_V)
        return (jax.nn.silu(p[..., RET_QKV_COLS:]) * o) @ w_out

    px = hx @ w_in
    pc = hc @ (w_in if ctx_out else w_in[:, :RET_QKV_COLS])
    lat = scan_inputs(px, True)
    ctx = scan_inputs(pc, False)
    log_g = jax.nn.log_sigmoid(decay_logit.astype(jnp.float32))
    of_x, of_c = prefix_scan(functools.partial(retention_chunk_scan, log_g=log_g[0]), lat, ctx, False, ctx_out)
    ob_x, ob_c = prefix_scan(functools.partial(retention_chunk_scan, log_g=log_g[1]), lat, ctx, True, ctx_out)
    yx = readout(px, of_x + ob_x)
    yc = readout(pc, of_c + ob_c) if ctx_out else None
    return yx, yc


def swiglu(x, wg, wu, wd):
    return (jax.nn.silu(x @ wg) * (x @ wu)) @ wd


def routed_experts(x, eidx, wts, w_gate, w_up, w_down):
    t, d = x.shape
    k = eidx.shape[1]
    a = t * k
    e_flat = eidx.reshape(-1)
    order = jnp.argsort(e_flat)
    e_sorted = e_flat[order]
    tok_sorted = order // k
    w_sorted = wts.reshape(-1)[order]
    counts = jnp.bincount(e_flat, length=N_EXPERTS)
    starts = jnp.cumsum(counts) - counts
    padded = (counts + MOE_BLOCK - 1) // MOE_BLOCK * MOE_BLOCK
    pad_end = jnp.cumsum(padded)
    pad_starts = pad_end - padded
    dest = pad_starts[e_sorted] + (jnp.arange(a) - starts[e_sorted])
    n_blocks = -(-a // MOE_BLOCK) + N_EXPERTS
    p = n_blocks * MOE_BLOCK
    buf_tok = jnp.full((p,), t, jnp.int32).at[dest].set(tok_sorted.astype(jnp.int32))
    buf_w = jnp.zeros((p,), wts.dtype).at[dest].set(w_sorted)
    blk_e = jnp.clip(jnp.searchsorted(pad_end, jnp.arange(n_blocks) * MOE_BLOCK, side='right'), 0, N_EXPERTS - 1)
    x_pad = jnp.concatenate([x, jnp.zeros((1, d), x.dtype)], axis=0)

    def step(y, inp):
        tok, w, e = inp
        xb = x_pad[tok]
        out = swiglu(xb, w_gate[e], w_up[e], w_down[e]) * w.astype(x.dtype)[:, None]
        return y.at[tok].add(out), None

    y, _ = lax.scan(step, jnp.zeros((t + 1, d), x.dtype),
                    (buf_tok.reshape(n_blocks, MOE_BLOCK), buf_w.reshape(n_blocks, MOE_BLOCK), blk_e))
    return y[:t]


def moe_ffn(h, router_w, router_b, exp_gate, exp_up, exp_down, sh_gate, sh_up, sh_down):
    t = h.shape[0]
    s = jax.nn.sigmoid((h @ router_w).astype(jnp.float32))
    sel = s + router_b.astype(jnp.float32)
    grp_score = lax.top_k(sel.reshape(t, N_GROUPS, N_EXPERTS // N_GROUPS), 2)[0].sum(-1)
    _, gidx = lax.top_k(grp_score, TOPK_GROUPS)
    gmask = (gidx[..., None] == jnp.arange(N_GROUPS)).any(axis=1)
    sel = jnp.where(jnp.repeat(gmask, N_EXPERTS // N_GROUPS, axis=1), sel, -jnp.inf)
    _, eidx = lax.top_k(sel, TOP_K)
    w = jnp.take_along_axis(s, eidx, axis=1)
    w = w / jnp.sum(w, axis=-1, keepdims=True) * ROUTED_SCALE
    return swiglu(h, sh_gate, sh_up, sh_down) + routed_experts(h, eidx, w, exp_gate, exp_up, exp_down)


def setup_inputs(seed: int = 0) -> dict:
    key = jax.random.key(seed)
    ks = iter(jax.random.split(key, 32))
    f32 = jnp.float32

    def nrm(shape, scale):
        return jax.random.normal(next(ks), shape, f32) * scale

    d = D_MODEL
    n_even = (DEPTH + 1) // 2
    n_odd = DEPTH // 2
    ret_logit = jnp.asarray(np.log(2.0 ** (5 + np.arange(RET_HEADS)) - 1.0), f32)
    return {
        'x': nrm((BATCH, SEQ, d), 1.0),
        'c': nrm((BATCH, d), 1.0),
        'ctx': nrm((BATCH, CTX_LEN, d), 1.0),
        'c_ctx': nrm((d,), 1.0),
        'ada_w': nrm((DEPTH, d, N_MOD * d), 0.5 * d ** -0.5),
        'ada_b': nrm((DEPTH, N_MOD * d), 0.02),
        'norm_mix': 1.0 + nrm((DEPTH, d), 0.02),
        'norm_ffn': 1.0 + nrm((DEPTH, d), 0.02),
        'norm_final': 1.0 + nrm((d,), 0.02),
        'ev_w_in': nrm((n_even, d, EV_IN), d ** -0.5),
        'ev_w_out': nrm((n_even, EV_MIX, d), EV_MIX ** -0.5),
        'hgrn_lb': nrm((DEPTH + 1, HG_F), 0.1),
        'hgrn_norm': 1.0 + nrm((n_even, HG_V), 0.02),
        'conv_w': nrm((n_even, CONV_W, CONV_C), CONV_W ** -0.5),
        'conv_b': nrm((n_even, CONV_C), 0.02),
        'conv_norm_g': 1.0 + nrm((n_even, CONV_C), 0.02),
        'conv_norm_b': nrm((n_even, CONV_C), 0.02),
        'ret_w_in': nrm((n_odd, d, RET_IN), d ** -0.5),
        'ret_w_out': nrm((n_odd, RET_V, d), RET_V ** -0.5),
        'ret_decay': ret_logit[None, None, :] + nrm((n_odd, 2, RET_HEADS), 0.1),
        'router_w': nrm((DEPTH, d, N_EXPERTS), d ** -0.5),
        'router_b': nrm((DEPTH, N_EXPERTS), 0.01),
        'exp_gate': nrm((DEPTH, N_EXPERTS, d, EXPERT_FF), d ** -0.5),
        'exp_up': nrm((DEPTH, N_EXPERTS, d, EXPERT_FF), d ** -0.5),
        'exp_down': nrm((DEPTH, N_EXPERTS, EXPERT_FF, d), EXPERT_FF ** -0.5),
        'sh_gate': nrm((DEPTH, d, SHARED_FF), d ** -0.5),
        'sh_up': nrm((DEPTH, d, SHARED_FF), d ** -0.5),
        'sh_down': nrm((DEPTH, SHARED_FF, d), SHARED_FF ** -0.5),
    }


def reference(x, c, ctx, c_ctx, ada_w, ada_b, norm_mix, norm_ffn, norm_final,
              ev_w_in, ev_w_out, hgrn_lb, hgrn_norm, conv_w, conv_b, conv_norm_g, conv_norm_b,
              ret_w_in, ret_w_out, ret_decay,
              router_w, router_b, exp_gate, exp_up, exp_down, sh_gate, sh_up, sh_down):
    b, n, d = x.shape
    lb_all = jnp.cumsum(jax.nn.softmax(hgrn_lb.astype(jnp.float32), axis=0), axis=0)
    xc = ctx
    for l in range(DEPTH):
        last = l == DEPTH - 1
        j = l // 2
        mx = jnp.split(adaln(c, ada_w[l], ada_b[l])[:, None, :], N_MOD, axis=-1)
        mc = jnp.split(adaln(c_ctx, ada_w[l], ada_b[l]), N_MOD, axis=-1)
        hx = modulate(rms_norm(x, norm_mix[l]), mx[0], mx[1])
        hc = modulate(rms_norm(xc, norm_mix[l]), mc[0], mc[1])
        if l % 2 == 0:
            yx, yc = hgrn2_conformer_mixer(hx, hc, ev_w_in[j], ev_w_out[j], lb_all[l], hgrn_norm[j],
                                           conv_w[j], conv_b[j], conv_norm_g[j], conv_norm_b[j],
                                           ctx_out=not last)
        else:
            yx, yc = retention_mixer(hx, hc, ret_w_in[j], ret_w_out[j], ret_decay[j], ctx_out=not last)
        x = x + mx[2] * yx
        hx = modulate(rms_norm(x, norm_ffn[l]), mx[3], mx[4])
        moe_w = (router_w[l], router_b[l], exp_gate[l], exp_up[l], exp_down[l], sh_gate[l], sh_up[l], sh_down[l])
        if last:
            x = x + mx[5] * moe_ffn(hx.reshape(-1, d), *moe_w).reshape(b, n, d)
        else:
            xc = xc + mc[2] * yc
            hc = modulate(rms_norm(xc, norm_ffn[l]), mc[3], mc[4])
            y = moe_ffn(jnp.concatenate([hx.reshape(-1, d), hc.reshape(-1, d)], axis=0), *moe_w)
            x = x + mx[5] * y[:b * n].reshape(b, n, d)
            xc = xc + mc[5] * y[b * n:].reshape(xc.shape)
    return rms_norm(x, norm_final)
```

```python
import collections
import functools

import jax
import jax.numpy as jnp
from jax import lax
from jax.experimental import pallas as pl
from jax.experimental.pallas import tpu as pltpu

F32 = jnp.float32
BF16 = jnp.bfloat16

D_MODEL = 2048
N_MOD = 6
EPS = 1e-6
GRID_W = 64
ROPE_BASE = 10000.0

HG_HEADS = 8
HG_DK = 128
HG_DV = 128
HG_F = HG_HEADS * HG_DK
HG_V = HG_HEADS * HG_DV
CONV_C = D_MODEL // 2
CONV_W = 31
CONV_HALO = 16
HG_CHUNK = 64

RET_HEADS = 8
RET_DK = D_MODEL // RET_HEADS
RET_DV = 2 * RET_DK
RET_QK = RET_HEADS * RET_DK
RET_V = RET_HEADS * RET_DV
RET_CHUNK = 128

N_EXPERTS = 64
EXPERT_FF = D_MODEL // 4
TOP_K = 8
N_GROUPS = 8
GROUP_SIZE = N_EXPERTS // N_GROUPS
TOPK_GROUPS = 4
ROUTED_SCALE = 2.5
MOE_BLOCK = 256

MOD_ROWS = 16
VMEM_LIMIT = 56 * 1024 * 1024

Geo = collections.namedtuple("Geo", "B N Lc BN BL T")


def _geo(b, n, lc):
    return Geo(b, n, lc, b * n, b * lc, b * n + b * lc)


def _pick_tile(geo, cands):
    for t in cands:
        if geo.N % t == 0 and geo.BL % t == 0:
            return t
    raise ValueError("no row tile fits the sequence lengths")


def _mod_row(i, tm, geo):
    return jnp.where(i < geo.BN // tm, i // (geo.N // tm), geo.B)


def _mod_spec(m, tm, geo, ngrid=1):
    if ngrid == 1:
        return pl.BlockSpec((1, 1, D_MODEL), lambda i: (_mod_row(i, tm, geo) * N_MOD + m, 0, 0))
    return pl.BlockSpec((1, 1, D_MODEL), lambda i, j: (_mod_row(i, tm, geo) * N_MOD + m, 0, j))


def _params(*sem):
    return pltpu.CompilerParams(dimension_semantics=sem, vmem_limit_bytes=VMEM_LIMIT)


def _sigmoid(x):
    return jax.nn.sigmoid(x)


def _silu(x):
    return x * jax.nn.sigmoid(x)


def _adaln_kernel(c_ref, w_ref, b_ref, o_ref):
    a = _silu(c_ref[...]).astype(BF16)
    o_ref[...] = jnp.dot(a, w_ref[...].astype(BF16), preferred_element_type=F32) + b_ref[...]


def _adaln(cond, w, b):
    k, n = w.shape
    tn = 1024
    return pl.pallas_call(
        _adaln_kernel,
        out_shape=jax.ShapeDtypeStruct((MOD_ROWS, n), F32),
        grid=(n // tn,),
        in_specs=[pl.BlockSpec((MOD_ROWS, k), lambda j: (0, 0)),
                  pl.BlockSpec((k, tn), lambda j: (0, j)),
                  pl.BlockSpec((1, tn), lambda j: (0, j))],
        out_specs=pl.BlockSpec((MOD_ROWS, tn), lambda j: (0, j)),
        compiler_params=_params("arbitrary"),
    )(cond, w, b.reshape(1, n))


def _normmod_kernel(x_ref, g_ref, sh_ref, sc_ref, *o_refs):
    x = x_ref[...]
    y = x * lax.rsqrt(jnp.mean(x * x, axis=-1, keepdims=True) + EPS) * g_ref[...]
    h = y * (1.0 + sc_ref[0]) + sh_ref[0]
    o_refs[0][...] = h.astype(BF16)
    if len(o_refs) > 1:
        o_refs[1][...] = h


def _normmod(x, g, mod3, m_shift, m_scale, geo, want_f32):
    tm = _pick_tile(geo, (256, 128))
    spec = pl.BlockSpec((tm, D_MODEL), lambda i: (i, 0))
    out_shape = [jax.ShapeDtypeStruct((geo.T, D_MODEL), BF16)]
    out_specs = [spec]
    if want_f32:
        out_shape.append(jax.ShapeDtypeStruct((geo.T, D_MODEL), F32))
        out_specs.append(spec)
    return pl.pallas_call(
        _normmod_kernel,
        out_shape=out_shape,
        grid=(geo.T // tm,),
        in_specs=[spec, pl.BlockSpec((1, D_MODEL), lambda i: (0, 0)),
                  _mod_spec(m_shift, tm, geo), _mod_spec(m_scale, tm, geo)],
        out_specs=out_specs,
        compiler_params=_params("arbitrary"),
    )(x, g.reshape(1, D_MODEL), mod3, mod3)


def _final_norm_kernel(x_ref, g_ref, o_ref):
    x = x_ref[...]
    o_ref[...] = x * lax.rsqrt(jnp.mean(x * x, axis=-1, keepdims=True) + EPS) * g_ref[...]


def _final_norm(x, g, geo):
    tm = _pick_tile(geo, (256, 128))
    spec = pl.BlockSpec((tm, D_MODEL), lambda i: (i, 0))
    return pl.pallas_call(
        _final_norm_kernel,
        out_shape=jax.ShapeDtypeStruct((geo.BN, D_MODEL), F32),
        grid=(geo.BN // tm,),
        in_specs=[spec, pl.BlockSpec((1, D_MODEL), lambda i: (0, 0))],
        out_specs=spec,
        compiler_params=_params("arbitrary"),
    )(x, g.reshape(1, D_MODEL))


def _mm_kernel(a_ref, w_ref, o_ref):
    o_ref[...] = jnp.dot(a_ref[...], w_ref[...], preferred_element_type=F32)


def _mm_resid_kernel(a_ref, w_ref, x_ref, m_ref, o_ref):
    y = jnp.dot(a_ref[...], w_ref[...], preferred_element_type=F32)
    o_ref[...] = x_ref[...] + m_ref[0] * y


def _matmul(a, w, geo):
    k, n = w.shape
    tm = _pick_tile(geo, (1024, 512, 256, 128))
    tn = 512
    return pl.pallas_call(
        _mm_kernel,
        out_shape=jax.ShapeDtypeStruct((geo.T, n), F32),
        grid=(geo.T // tm, n // tn),
        in_specs=[pl.BlockSpec((tm, k), lambda i, j: (i, 0)),
                  pl.BlockSpec((k, tn), lambda i, j: (0, j))],
        out_specs=pl.BlockSpec((tm, tn), lambda i, j: (i, j)),
        compiler_params=_params("arbitrary", "arbitrary"),
    )(a, w)


def _matmul_resid(a, w, x, mod3, m_gate, geo):
    k, n = w.shape
    tm = _pick_tile(geo, (1024, 512, 256, 128))
    tn = 512
    return pl.pallas_call(
        _mm_resid_kernel,
        out_shape=jax.ShapeDtypeStruct((geo.T, n), F32),
        grid=(geo.T // tm, n // tn),
        in_specs=[pl.BlockSpec((tm, k), lambda i, j: (i, 0)),
                  pl.BlockSpec((k, tn), lambda i, j: (0, j)),
                  pl.BlockSpec((tm, tn), lambda i, j: (i, j)),
                  pl.BlockSpec((1, 1, tn), lambda i, j: (_mod_row(i, tm, geo) * N_MOD + m_gate, 0, j))],
        out_specs=pl.BlockSpec((tm, tn), lambda i, j: (i, j)),
        compiler_params=_params("arbitrary", "arbitrary"),
    )(a, w, x, mod3)


def _chunk_index(b, i, chunk, geo, reverse):
    nc = geo.Lc // chunk
    nl = geo.N // chunk
    ctx0 = (geo.BN + b * geo.Lc) // chunk
    lat0 = (b * geo.N) // chunk
    if reverse:
        return jnp.where(i < nc, ctx0 + (nc - 1 - i), lat0 + (nl - 1 - (i - nc)))
    return jnp.where(i < nc, ctx0 + i, lat0 + (i - nc))


def _split_dot(tri_bf, x):
    hi = x.astype(BF16)
    r1 = x - hi.astype(F32)
    mid = r1.astype(BF16)
    lo = (r1 - mid.astype(F32)).astype(BF16)
    return (jnp.dot(tri_bf, hi, preferred_element_type=F32)
            + jnp.dot(tri_bf, mid, preferred_element_type=F32)
            + jnp.dot(tri_bf, lo, preferred_element_type=F32))


def _hgrn_kernel(q_ref, f_ref, v_ref, lbp_ref, o_ref, st_ref, *, layer, reverse):
    c = HG_CHUNK
    i = pl.program_id(1)

    @pl.when(i == 0)
    def _():
        st_ref[...] = jnp.zeros_like(st_ref)

    lbp = lbp_ref[...]
    e = jnp.exp(lbp - jnp.max(lbp, axis=0, keepdims=True))
    sm = e / jnp.sum(e, axis=0, keepdims=True)
    lb = sm[0:1]
    for r in range(1, layer + 1):
        lb = lb + sm[r:r + 1]

    row = lax.broadcasted_iota(jnp.int32, (c, c), 0)
    col = lax.broadcasted_iota(jnp.int32, (c, c), 1)
    tri = (col >= row) if reverse else (col <= row)
    tri_bf = jnp.where(tri, 1.0, 0.0).astype(BF16)
    nt = (((1,), (1,)), ((), ()))

    for h in range(HG_HEADS):
        sl = slice(h * HG_DK, (h + 1) * HG_DK)
        qh = _silu(q_ref[:, sl])
        lbh = lb[:, sl]
        fg = lbh + (1.0 - lbh) * _sigmoid(f_ref[:, sl])
        kh = 1.0 - fg
        bcum = _split_dot(tri_bf, jnp.log(fg))
        bmid = bcum[c // 2:c // 2 + 1]
        bend = bcum[0:1] if reverse else bcum[c - 1:c]
        vh = v_ref[:, sl]
        vb = vh.astype(BF16)
        a = (qh * jnp.exp(bcum - bmid)).astype(BF16)
        kd = (kh * jnp.exp(bmid - bcum)).astype(BF16)
        s = lax.dot_general(a, kd, nt, preferred_element_type=F32)
        s = jnp.where(tri, s, 0.0)
        intra = jnp.dot(s.astype(BF16), vb, preferred_element_type=F32)
        st = st_ref[h]
        inter = lax.dot_general((qh * jnp.exp(bcum)).astype(BF16), st.astype(BF16), nt,
                                preferred_element_type=F32)
        o_ref[:, sl] = inter + intra
        kd2 = (kh * jnp.exp(bend - bcum)).astype(BF16)
        st_ref[h] = st * jnp.exp(bend) + jnp.dot(vh.T.astype(BF16), kd2, preferred_element_type=F32)


def _hgrn_scan(p, lb_param, layer, geo, reverse):
    c = HG_CHUNK
    steps = (geo.Lc + geo.N) // c
    fcol = 2 if reverse else 1

    def spec(colblk):
        return pl.BlockSpec((c, HG_F), lambda b, i: (_chunk_index(b, i, c, geo, reverse), colblk))

    return pl.pallas_call(
        functools.partial(_hgrn_kernel, layer=layer, reverse=reverse),
        out_shape=jax.ShapeDtypeStruct((geo.T, HG_V), F32),
        grid=(geo.B, steps),
        in_specs=[spec(0), spec(fcol), spec(3),
                  pl.BlockSpec(lb_param.shape, lambda b, i: (0, 0))],
        out_specs=spec(0),
        scratch_shapes=[pltpu.VMEM((HG_HEADS, HG_DV, HG_DK), F32)],
        compiler_params=_params("arbitrary", "arbitrary"),
    )(p, p, p, lb_param)


def _ev_readout_kernel(of_ref, ob_ref, gate_ref, a_ref, b_ref, ap_ref, bp_ref, an_ref, bn_ref,
                       gain_ref, cw_ref, cb_ref, lng_ref, lnb_ref, o_ref, ext_ref, conv_ref, *, tm, geo):
    i = pl.program_id(0)
    n_lat = geo.BN // tm
    tpl = geo.N // tm
    tpc = geo.Lc // tm
    j = jnp.where(i < n_lat, i % tpl, (i - n_lat) % tpc)
    per = jnp.where(i < n_lat, tpl, tpc)
    keep_prev = jnp.where(j == 0, 0.0, 1.0)
    keep_next = jnp.where(j == per - 1, 0.0, 1.0)

    o = of_ref[...] + ob_ref[...]
    r = o * lax.rsqrt(jnp.mean(o * o, axis=-1, keepdims=True) + EPS) * gain_ref[...]
    o_ref[:, :HG_V] = (r * _silu(gate_ref[...])).astype(BF16)

    ext_ref[0:CONV_HALO] = ap_ref[...] * _sigmoid(bp_ref[...]) * keep_prev
    ext_ref[CONV_HALO:CONV_HALO + tm] = a_ref[...] * _sigmoid(b_ref[...])
    ext_ref[CONV_HALO + tm:2 * CONV_HALO + tm] = an_ref[...] * _sigmoid(bn_ref[...]) * keep_next
    off = CONV_HALO - CONV_W // 2
    for cj in range(CONV_C // 128):
        cs = slice(cj * 128, (cj + 1) * 128)
        acc = jnp.broadcast_to(cb_ref[:, cs], (tm, 128))
        for k in range(CONV_W):
            acc = acc + cw_ref[k:k + 1, cs] * ext_ref[off + k:off + k + tm, cs]
        conv_ref[:, cs] = acc
    acc = conv_ref[...]
    mu = jnp.mean(acc, axis=-1, keepdims=True)
    xc = acc - mu
    var = jnp.mean(xc * xc, axis=-1, keepdims=True)
    u = xc * lax.rsqrt(var + EPS) * lng_ref[...] + lnb_ref[...]
    o_ref[:, HG_V:] = _silu(u).astype(BF16)


def _ev_readout(p, of, ob, gain, cw, cb, lng, lnb, geo):
    tm = _pick_tile(geo, (128,))
    hb = tm // CONV_HALO
    nhalo = geo.T // CONV_HALO

    def row(colblk):
        return pl.BlockSpec((tm, HG_V), lambda i: (i, colblk))

    def prev(colblk):
        return pl.BlockSpec((CONV_HALO, CONV_C), lambda i: (jnp.maximum(i * hb - 1, 0), colblk))

    def nxt(colblk):
        return pl.BlockSpec((CONV_HALO, CONV_C), lambda i: (jnp.minimum((i + 1) * hb, nhalo - 1), colblk))

    def vec(n):
        return pl.BlockSpec((n, CONV_C), lambda i: (0, 0))

    return pl.pallas_call(
        functools.partial(_ev_readout_kernel, tm=tm, geo=geo),
        out_shape=jax.ShapeDtypeStruct((geo.T, HG_V + CONV_C), BF16),
        grid=(geo.T // tm,),
        in_specs=[row(0), row(0), row(4), row(5), row(6), prev(5), prev(6), nxt(5), nxt(6),
                  vec(1), vec(CONV_W), vec(1), vec(1), vec(1)],
        out_specs=pl.BlockSpec((tm, HG_V + CONV_C), lambda i: (i, 0)),
        scratch_shapes=[pltpu.VMEM((tm + 2 * CONV_HALO, CONV_C), F32), pltpu.VMEM((tm, CONV_C), F32)],
        compiler_params=_params("arbitrary"),
    )(of, ob, p, p, p, p, p, p, p, gain.reshape(1, -1), cw, cb.reshape(1, -1),
      lng.reshape(1, -1), lnb.reshape(1, -1))


def _rope(x, cos, sin_signed):
    half = x.shape[-1] // 2
    rot = jnp.concatenate([pltpu.roll(x[:, :half], half // 2, axis=1),
                           pltpu.roll(x[:, half:], half // 2, axis=1)], axis=-1)
    return x * cos + rot * sin_signed


def _ret_kernel(q_ref, k_ref, v_ref, cos_ref, sin_ref, dl_ref, o_ref, s_ref, *, reverse):
    c = RET_CHUNK
    i = pl.program_id(1)

    @pl.when(i == 0)
    def _():
        s_ref[...] = jnp.zeros_like(s_ref)

    dl = dl_ref[...]
    lg_all = -jnp.log1p(jnp.exp(-dl))
    row = lax.broadcasted_iota(jnp.int32, (c, c), 0)
    col = lax.broadcasted_iota(jnp.int32, (c, c), 1)
    idx = lax.broadcasted_iota(jnp.int32, (c, 1), 0).astype(F32)
    if reverse:
        live = col >= row
        dist = (col - row).astype(F32)
        q_pow = c - idx
        k_pow = idx
    else:
        live = row >= col
        dist = (row - col).astype(F32)
        q_pow = idx + 1.0
        k_pow = c - 1.0 - idx
    cos = cos_ref[...]
    sin = sin_ref[...]
    nt = (((1,), (1,)), ((), ()))

    for h in range(RET_HEADS):
        lg = lg_all[h:h + 1]
        dmask = jnp.where(live, jnp.exp(lg * dist), 0.0)
        q = _rope(q_ref[:, h * RET_DK:(h + 1) * RET_DK], cos, sin)
        k = _rope(k_ref[:, h * RET_DK:(h + 1) * RET_DK] * (RET_DK ** -0.5), cos, sin)
        vb = v_ref[:, h * RET_DV:(h + 1) * RET_DV].astype(BF16)
        scores = lax.dot_general(q.astype(BF16), k.astype(BF16), nt, preferred_element_type=F32) * dmask
        intra = jnp.dot(scores.astype(BF16), vb, preferred_element_type=F32)
        s = s_ref[h]
        inter = jnp.dot((q * jnp.exp(lg * q_pow)).astype(BF16), s.astype(BF16),
                        preferred_element_type=F32)
        o_ref[:, h * RET_DV:(h + 1) * RET_DV] = inter + intra
        kdec = (k * jnp.exp(lg * k_pow)).T.astype(BF16)
        s_ref[h] = jnp.exp(lg * c) * s + jnp.dot(kdec, vb, preferred_element_type=F32)


def _ret_scan(p, cos_tab, sin_tab, decay_logit, geo, reverse):
    c = RET_CHUNK
    steps = (geo.Lc + geo.N) // c
    nc = geo.Lc // c
    nl = geo.N // c

    def spec(width, colblk):
        return pl.BlockSpec((c, width), lambda b, i: (_chunk_index(b, i, c, geo, reverse), colblk))

    def tab_index(b, i):
        lat = (nl - 1 - (i - nc)) if reverse else (i - nc)
        return (jnp.where(i < nc, nl, lat), 0)

    tab = pl.BlockSpec((c, RET_DK), tab_index)
    return pl.pallas_call(
        functools.partial(_ret_kernel, reverse=reverse),
        out_shape=jax.ShapeDtypeStruct((geo.T, RET_V), F32),
        grid=(geo.B, steps),
        in_specs=[spec(RET_QK, 0), spec(RET_QK, 1), spec(RET_V, 1), tab, tab,
                  pl.BlockSpec((RET_HEADS, 1), lambda b, i: (0, 0))],
        out_specs=spec(RET_V, 0),
        scratch_shapes=[pltpu.VMEM((RET_HEADS, RET_DK, RET_DV), F32)],
        compiler_params=_params("arbitrary", "arbitrary"),
    )(p, p, p, cos_tab, sin_tab, decay_logit.reshape(RET_HEADS, 1))


def _rope_tables(n):
    t = jnp.arange(n)
    quarter = RET_DK // 4
    inv = 1.0 / (ROPE_BASE ** (jnp.arange(quarter, dtype=F32) / quarter))
    ang_r = (t // GRID_W).astype(F32)[:, None] * inv
    ang_c = (t % GRID_W).astype(F32)[:, None] * inv
    cos = jnp.concatenate([jnp.cos(ang_r), jnp.cos(ang_r), jnp.cos(ang_c), jnp.cos(ang_c)], axis=-1)
    sin = jnp.concatenate([-jnp.sin(ang_r), jnp.sin(ang_r), -jnp.sin(ang_c), jnp.sin(ang_c)], axis=-1)
    cos = jnp.concatenate([cos, jnp.ones((RET_CHUNK, RET_DK), F32)], axis=0)
    sin = jnp.concatenate([sin, jnp.zeros((RET_CHUNK, RET_DK), F32)], axis=0)
    return cos, sin


def _ret_readout_kernel(of_ref, ob_ref, gate_ref, o_ref):
    for h in range(RET_HEADS):
        sl = slice(h * RET_DV, (h + 1) * RET_DV)
        o = of_ref[:, sl] + ob_ref[:, sl]
        r = o * lax.rsqrt(jnp.mean(o * o, axis=-1, keepdims=True) + EPS)
        o_ref[:, sl] = (_silu(gate_ref[:, sl]) * r).astype(BF16)


def _ret_readout(p, of, ob, geo):
    tm = _pick_tile(geo, (128,))
    spec = pl.BlockSpec((tm, RET_V), lambda i: (i, 0))
    return pl.pallas_call(
        _ret_readout_kernel,
        out_shape=jax.ShapeDtypeStruct((geo.T, RET_V), BF16),
        grid=(geo.T // tm,),
        in_specs=[spec, spec, pl.BlockSpec((tm, RET_V), lambda i: (i, 2))],
        out_specs=spec,
        compiler_params=_params("arbitrary"),
    )(of, ob, p)


def _router_kernel(h_ref, rw_ref, rb_ref, eidx_ref, rank_ref, w_ref, cnt_ref, carry_ref, *, tm):
    i = pl.program_id(0)

    @pl.when(i == 0)
    def _():
        carry_ref[...] = jnp.zeros_like(carry_ref)

    logits = jnp.dot(h_ref[...], rw_ref[...], preferred_element_type=F32,
                     precision=lax.Precision.HIGHEST)
    s = _sigmoid(logits)
    sel = s + rb_ref[...]
    lane = lax.broadcasted_iota(jnp.int32, (tm, N_EXPERTS), 1)
    grp = lane // GROUP_SIZE
    ninf = -jnp.inf

    gscore = jnp.zeros((tm, N_EXPERTS), F32)
    gcols = []
    for g in range(N_GROUPS):
        in_g = grp == g
        v1 = jnp.max(jnp.where(in_g, sel, ninf), axis=-1, keepdims=True)
        i1 = jnp.min(jnp.where(in_g & (sel == v1), lane, N_EXPERTS), axis=-1, keepdims=True)
        v2 = jnp.max(jnp.where(in_g & (lane != i1), sel, ninf), axis=-1, keepdims=True)
        gcols.append(v1 + v2)
        gscore = jnp.where(in_g, v1 + v2, gscore)
    beaten = jnp.zeros((tm, N_EXPERTS), jnp.int32)
    for g in range(N_GROUPS):
        wins = (gcols[g] > gscore) | ((gcols[g] == gscore) & (g < grp))
        beaten = beaten + jnp.where(wins, 1, 0)
    cand = jnp.where(beaten < TOPK_GROUPS, sel, ninf)

    lane_k = lax.broadcasted_iota(jnp.int32, (tm, TOP_K), 1)
    eidx = jnp.zeros((tm, TOP_K), jnp.int32)
    wsel = jnp.zeros((tm, TOP_K), F32)
    chosen = jnp.zeros((tm, N_EXPERTS), F32)
    picks = []
    for k in range(TOP_K):
        v = jnp.max(cand, axis=-1, keepdims=True)
        ik = jnp.min(jnp.where(cand == v, lane, N_EXPERTS), axis=-1, keepdims=True)
        hit = lane == ik
        picks.append(ik)
        eidx = jnp.where(lane_k == k, ik, eidx)
        wsel = jnp.where(lane_k == k, jnp.sum(jnp.where(hit, s, 0.0), axis=-1, keepdims=True), wsel)
        chosen = jnp.where(hit, 1.0, chosen)
        cand = jnp.where(hit, ninf, cand)
    w_ref[...] = wsel / jnp.sum(wsel, axis=-1, keepdims=True) * ROUTED_SCALE
    eidx_ref[...] = eidx

    r = lax.broadcasted_iota(jnp.int32, (tm, tm), 0)
    c = lax.broadcasted_iota(jnp.int32, (tm, tm), 1)
    below = jnp.where(c < r, 1.0, 0.0).astype(BF16)
    carry = carry_ref[...]
    pos = jnp.dot(below, chosen.astype(BF16), preferred_element_type=F32) + carry
    rank = jnp.zeros((tm, TOP_K), jnp.int32)
    for k in range(TOP_K):
        rk = jnp.sum(jnp.where(lane == picks[k], pos, 0.0), axis=-1, keepdims=True)
        rank = jnp.where(lane_k == k, rk.astype(jnp.int32), rank)
    rank_ref[...] = rank
    carry = carry + jnp.sum(chosen, axis=0, keepdims=True)
    carry_ref[...] = carry
    cnt_ref[...] = carry


def _router(h_f32, rw, rb, geo):
    tm = 256 if geo.T % 256 == 0 else 128
    tok = pl.BlockSpec((tm, TOP_K), lambda i: (i, 0))
    one = pl.BlockSpec((1, N_EXPERTS), lambda i: (0, 0))
    return pl.pallas_call(
        functools.partial(_router_kernel, tm=tm),
        out_shape=[jax.ShapeDtypeStruct((geo.T, TOP_K), jnp.int32),
                   jax.ShapeDtypeStruct((geo.T, TOP_K), jnp.int32),
                   jax.ShapeDtypeStruct((geo.T, TOP_K), F32),
                   jax.ShapeDtypeStruct((1, N_EXPERTS), F32)],
        grid=(geo.T // tm,),
        in_specs=[pl.BlockSpec((tm, D_MODEL), lambda i: (i, 0)),
                  pl.BlockSpec((D_MODEL, N_EXPERTS), lambda i: (0, 0)), one],
        out_specs=[tok, tok, tok, one],
        scratch_shapes=[pltpu.VMEM((1, N_EXPERTS), F32)],
        compiler_params=_params("arbitrary"),
    )(h_f32, rw, rb.reshape(1, N_EXPERTS))


def _n_blocks(geo):
    return -(-(geo.T * TOP_K) // MOE_BLOCK) + N_EXPERTS


def _dest_kernel(cnt_ref, eidx_ref, rank_ref, dest_ref, blk_ref):
    eidx = eidx_ref[...]
    dest = rank_ref[...]
    blk_row = (lax.broadcasted_iota(jnp.int32, blk_ref.shape, 0) * 128
               + lax.broadcasted_iota(jnp.int32, blk_ref.shape, 1)) * MOE_BLOCK
    blk = jnp.zeros(blk_ref.shape, jnp.int32)
    start = jnp.int32(0)
    for e in range(N_EXPERTS):
        padded = (cnt_ref[e] + (MOE_BLOCK - 1)) // MOE_BLOCK * MOE_BLOCK
        dest = dest + jnp.where(eidx == e, start, 0)
        start = start + padded
        blk = blk + jnp.where(start <= blk_row, 1, 0)
    dest_ref[...] = dest
    blk_ref[...] = jnp.minimum(blk, N_EXPERTS - 1)


def _dest(counts, eidx, rank, geo):
    rows = geo.T * TOP_K // 128
    brow = -(-_n_blocks(geo) // 128)
    full = pl.BlockSpec((rows, 128), lambda: (0, 0))
    dest, blk = pl.pallas_call(
        _dest_kernel,
        out_shape=[jax.ShapeDtypeStruct((rows, 128), jnp.int32),
                   jax.ShapeDtypeStruct((brow, 128), jnp.int32)],
        in_specs=[pl.BlockSpec(memory_space=pltpu.SMEM), full, full],
        out_specs=[full, pl.BlockSpec((brow, 128), lambda: (0, 0))],
    )(counts, eidx.reshape(rows, 128), rank.reshape(rows, 128))
    return dest.reshape(-1), blk.reshape(-1)[:_n_blocks(geo)]


def _row_copy(src_hbm, s, dst_hbm, d, sem):
    return pltpu.make_async_copy(src_hbm.at[pl.ds(s, 1)], dst_hbm.at[pl.ds(d, 1)], sem)


def _zero_fill(cnt_ref, xs_hbm, zero_ref, zsem, n_rows, wait):
    def piece(pos, size):
        if size >= 8:
            copies = [(pl.multiple_of(pos, 8), size)]
        else:
            copies = [(pos + r, 1) for r in range(size)]
        for p, s in copies:
            cp = pltpu.make_async_copy(zero_ref.at[pl.ds(0, s)], xs_hbm.at[pl.ds(p, s)], zsem)
            cp.wait() if wait else cp.start()

    def per_expert(e, start):
        cnt = cnt_ref[e]
        padded = (cnt + (MOE_BLOCK - 1)) // MOE_BLOCK * MOE_BLOCK
        pad = padded - cnt
        pos = start + cnt
        size = 1
        while size < MOE_BLOCK:
            take = (pad & size) != 0
            pl.when(take)(functools.partial(piece, pos, size))
            pos = pos + jnp.where(take, size, 0)
            size *= 2
        return start + padded

    end = lax.fori_loop(0, N_EXPERTS, per_expert, jnp.int32(0))

    def per_block(j, carry):
        piece(end + j * MOE_BLOCK, MOE_BLOCK)
        return carry

    lax.fori_loop(0, (n_rows - end) // MOE_BLOCK, per_block, 0)


def _dispatch_kernel(cnt_ref, dest_ref, h_hbm, xs_hbm, zero_ref, sem, zsem, *, tt, n_rows):
    i = pl.program_id(0)

    @pl.when(i == 0)
    def _():
        zero_ref[...] = jnp.zeros_like(zero_ref)
        _zero_fill(cnt_ref, xs_hbm, zero_ref, zsem, n_rows, wait=False)
        _zero_fill(cnt_ref, xs_hbm, zero_ref, zsem, n_rows, wait=True)

    base = i * tt

    def issue(t, carry):
        for k in range(TOP_K):
            _row_copy(h_hbm, base + t, xs_hbm, dest_ref[t * TOP_K + k], sem).start()
        return carry

    lax.fori_loop(0, tt, issue, 0)

    def drain(t, carry):
        for k in range(TOP_K):
            _row_copy(h_hbm, 0, xs_hbm, 0, sem).wait()
        return carry

    lax.fori_loop(0, tt, drain, 0)


def _dispatch(counts, dest, h_f32, geo):
    tt = 512 if geo.T % 512 == 0 else 128
    n_rows = _n_blocks(geo) * MOE_BLOCK
    return pl.pallas_call(
        functools.partial(_dispatch_kernel, tt=tt, n_rows=n_rows),
        out_shape=jax.ShapeDtypeStruct((n_rows, D_MODEL), F32),
        grid=(geo.T // tt,),
        in_specs=[pl.BlockSpec(memory_space=pltpu.SMEM),
                  pl.BlockSpec((tt * TOP_K,), lambda i: (i,), memory_space=pltpu.SMEM),
                  pl.BlockSpec(memory_space=pl.ANY)],
        out_specs=pl.BlockSpec(memory_space=pl.ANY),
        scratch_shapes=[pltpu.VMEM((MOE_BLOCK, D_MODEL), F32), pltpu.SemaphoreType.DMA,
                        pltpu.SemaphoreType.DMA],
        compiler_params=_params("arbitrary"),
    )(counts, dest, h_f32)


def _swiglu_block(x, wg_ref, wu_ref, wd_ref):
    g = jnp.dot(x, wg_ref[...], preferred_element_type=F32)
    u = jnp.dot(x, wu_ref[...], preferred_element_type=F32)
    return jnp.dot((_silu(g) * u).astype(BF16), wd_ref[...], preferred_element_type=F32)


def _expert_kernel(blk_ref, x_ref, wg_ref, wu_ref, wd_ref, o_ref):
    del blk_ref
    o_ref[...] = _swiglu_block(x_ref[...].astype(BF16), wg_ref, wu_ref, wd_ref)


def _experts(blk_e, xs, wg, wu, wd, geo):
    nb = _n_blocks(geo)
    rows = pl.BlockSpec((MOE_BLOCK, D_MODEL), lambda j, be: (j, 0))
    return pl.pallas_call(
        _expert_kernel,
        out_shape=jax.ShapeDtypeStruct((nb * MOE_BLOCK, D_MODEL), F32),
        grid_spec=pltpu.PrefetchScalarGridSpec(
            num_scalar_prefetch=1,
            grid=(nb,),
            in_specs=[rows,
                      pl.BlockSpec((None, D_MODEL, EXPERT_FF), lambda j, be: (be[j], 0, 0)),
                      pl.BlockSpec((None, D_MODEL, EXPERT_FF), lambda j, be: (be[j], 0, 0)),
                      pl.BlockSpec((None, EXPERT_FF, D_MODEL), lambda j, be: (be[j], 0, 0))],
            out_specs=rows),
        compiler_params=_params("arbitrary"),
    )(blk_e, xs, wg, wu, wd)


def _shared_kernel(x_ref, wg_ref, wu_ref, wd_ref, o_ref):
    o_ref[...] = _swiglu_block(x_ref[...], wg_ref, wu_ref, wd_ref)


def _shared_expert(h_bf, wg, wu, wd, geo):
    tm = _pick_tile(geo, (512, 256, 128))
    rows = pl.BlockSpec((tm, D_MODEL), lambda i: (i, 0))
    ff = wg.shape[1]
    return pl.pallas_call(
        _shared_kernel,
        out_shape=jax.ShapeDtypeStruct((geo.T, D_MODEL), F32),
        grid=(geo.T // tm,),
        in_specs=[rows,
                  pl.BlockSpec((D_MODEL, ff), lambda i: (0, 0)),
                  pl.BlockSpec((D_MODEL, ff), lambda i: (0, 0)),
                  pl.BlockSpec((ff, D_MODEL), lambda i: (0, 0))],
        out_specs=rows,
        compiler_params=_params("arbitrary"),
    )(h_bf, wg, wu, wd)


def _combine_kernel(dest_ref, w_ref, x_ref, sh_ref, m_ref, ys_hbm, o_ref, g_ref, sem, *, tt):
    def issue(t, carry):
        for k in range(TOP_K):
            pltpu.make_async_copy(ys_hbm.at[pl.ds(dest_ref[t * TOP_K + k], 1)],
                                  g_ref.at[k, pl.ds(t, 1)], sem).start()
        return carry

    lax.fori_loop(0, tt, issue, 0)

    def drain(t, carry):
        for k in range(TOP_K):
            pltpu.make_async_copy(ys_hbm.at[pl.ds(0, 1)], g_ref.at[k, pl.ds(0, 1)], sem).wait()
        return carry

    lax.fori_loop(0, tt, drain, 0)

    w = w_ref[...]
    acc = sh_ref[...]
    for k in range(TOP_K):
        acc = acc + w[:, k:k + 1] * g_ref[k]
    o_ref[...] = x_ref[...] + m_ref[0] * acc


def _combine(dest, w, x, sh, ys, mod3, m_gate, geo):
    tt = 128
    rows = pl.BlockSpec((tt, D_MODEL), lambda i: (i, 0))
    return pl.pallas_call(
        functools.partial(_combine_kernel, tt=tt),
        out_shape=jax.ShapeDtypeStruct((geo.T, D_MODEL), F32),
        grid=(geo.T // tt,),
        in_specs=[pl.BlockSpec((tt * TOP_K,), lambda i: (i,), memory_space=pltpu.SMEM),
                  pl.BlockSpec((tt, TOP_K), lambda i: (i, 0)),
                  rows, rows, _mod_spec(m_gate, tt, geo),
                  pl.BlockSpec(memory_space=pl.ANY)],
        out_specs=rows,
        scratch_shapes=[pltpu.VMEM((TOP_K, tt, D_MODEL), F32), pltpu.SemaphoreType.DMA],
        compiler_params=_params("arbitrary"),
    )(dest, w, x, sh, mod3, ys)


def _moe(x, h_bf, h_f32, mod3, rw, rb, wg, wu, wd, sg, su, sd, geo):
    eidx, rank, w, counts = _router(h_f32, rw, rb, geo)
    counts = counts.reshape(N_EXPERTS).astype(jnp.int32)
    dest, blk_e = _dest(counts, eidx, rank, geo)
    xs = _dispatch(counts, dest, h_f32, geo)
    ys = _experts(blk_e, xs, wg.astype(BF16), wu.astype(BF16), wd.astype(BF16), geo)
    sh = _shared_expert(h_bf, sg.astype(BF16), su.astype(BF16), sd.astype(BF16), geo)
    return _combine(dest, w, x, sh, ys, mod3, 5, geo)


def kernel(x, c, ctx, c_ctx, ada_w, ada_b, norm_mix, norm_ffn, norm_final, ev_w_in, ev_w_out, hgrn_lb, hgrn_norm, conv_w, conv_b, conv_norm_g, conv_norm_b, ret_w_in, ret_w_out, ret_decay, router_w, router_b, exp_gate, exp_up, exp_down, sh_gate, sh_up, sh_down):
    b, n, d = x.shape
    lc = ctx.shape[1]
    depth = ada_w.shape[0]
    geo = _geo(b, n, lc)
    assert d == D_MODEL and b < MOD_ROWS
    assert n % RET_CHUNK == 0 and lc % RET_CHUNK == 0

    xs = jnp.concatenate([x.reshape(geo.BN, d), ctx.reshape(geo.BL, d)], axis=0)
    cond = jnp.zeros((MOD_ROWS, d), F32).at[:b].set(c).at[b].set(c_ctx)
    cos_tab, sin_tab = _rope_tables(n)

    for l in range(depth):
        j = l // 2
        mod3 = _adaln(cond, ada_w[l], ada_b[l]).reshape(MOD_ROWS * N_MOD, 1, d)
        (h,) = _normmod(xs, norm_mix[l], mod3, 0, 1, geo, want_f32=False)
        if l % 2 == 0:
            p = _matmul(h, ev_w_in[j].astype(BF16), geo)
            of = _hgrn_scan(p, hgrn_lb, l, geo, reverse=False)
            ob = _hgrn_scan(p, hgrn_lb, l, geo, reverse=True)
            mix = _ev_readout(p, of, ob, hgrn_norm[j], conv_w[j], conv_b[j],
                              conv_norm_g[j], conv_norm_b[j], geo)
            xs = _matmul_resid(mix, ev_w_out[j].astype(BF16), xs, mod3, 2, geo)
        else:
            p = _matmul(h, ret_w_in[j].astype(BF16), geo)
            of = _ret_scan(p, cos_tab, sin_tab, ret_decay[j, 0], geo, reverse=False)
            ob = _ret_scan(p, cos_tab, sin_tab, ret_decay[j, 1], geo, reverse=True)
            mix = _ret_readout(p, of, ob, geo)
            xs = _matmul_resid(mix, ret_w_out[j].astype(BF16), xs, mod3, 2, geo)
        h_bf, h_f32 = _normmod(xs, norm_ffn[l], mod3, 3, 4, geo, want_f32=True)
        xs = _moe(xs, h_bf, h_f32, mod3, router_w[l], router_b[l], exp_gate[l], exp_up[l], exp_down[l],
                  sh_gate[l], sh_up[l], sh_down[l], geo)
    return _final_norm(xs, norm_final, geo).reshape(b, n, d)
```

```python
import collections
import functools

import jax
import jax.numpy as jnp
from jax import lax
from jax.experimental import pallas as pl
from jax.experimental.pallas import tpu as pltpu

F32 = jnp.float32
BF16 = jnp.bfloat16

D_MODEL = 2048
N_MOD = 6
EPS = 1e-6
GRID_W = 64
ROPE_BASE = 10000.0

HG_HEADS = 8
HG_DK = 128
HG_DV = 128
HG_F = HG_HEADS * HG_DK
HG_V = HG_HEADS * HG_DV
CONV_C = D_MODEL // 2
CONV_W = 31
CONV_HALO = 16
HG_CHUNK = 64

RET_HEADS = 8
RET_DK = D_MODEL // RET_HEADS
RET_DV = 2 * RET_DK
RET_QK = RET_HEADS * RET_DK
RET_V = RET_HEADS * RET_DV
RET_CHUNK = 128

N_EXPERTS = 64
EXPERT_FF = D_MODEL // 4
TOP_K = 8
N_GROUPS = 8
GROUP_SIZE = N_EXPERTS // N_GROUPS
TOPK_GROUPS = 4
ROUTED_SCALE = 2.5
MOE_BLOCK = 256

MOD_ROWS = 16
VMEM_LIMIT = 56 * 1024 * 1024

Geo = collections.namedtuple("Geo", "B N Lc BN BL T")


def _geo(b, n, lc):
    return Geo(b, n, lc, b * n, b * lc, b * n + b * lc)


def _pick_tile(geo, cands):
    for t in cands:
        if geo.N % t == 0 and geo.BL % t == 0:
            return t
    raise ValueError("no row tile fits the sequence lengths")


def _mod_row(i, tm, geo):
    return jnp.where(i < geo.BN // tm, i // (geo.N // tm), geo.B)


def _mod_spec(m, tm, geo, ngrid=1):
    if ngrid == 1:
        return pl.BlockSpec((1, 1, D_MODEL), lambda i: (_mod_row(i, tm, geo) * N_MOD + m, 0, 0))
    return pl.BlockSpec((1, 1, D_MODEL), lambda i, j: (_mod_row(i, tm, geo) * N_MOD + m, 0, j))


def _params(*sem):
    return pltpu.CompilerParams(dimension_semantics=sem, vmem_limit_bytes=VMEM_LIMIT)


def _sigmoid(x):
    return jax.nn.sigmoid(x)


def _silu(x):
    return x * jax.nn.sigmoid(x)


def _adaln_kernel(c_ref, w_ref, b_ref, o_ref):
    a = _silu(c_ref[...]).astype(BF16)
    o_ref[...] = jnp.dot(a, w_ref[...].astype(BF16), preferred_element_type=F32) + b_ref[...]


def _adaln(cond, w, b):
    k, n = w.shape
    tn = 1024
    return pl.pallas_call(
        _adaln_kernel,
        name="adaln",
        out_shape=jax.ShapeDtypeStruct((MOD_ROWS, n), F32),
        grid=(n // tn,),
        in_specs=[pl.BlockSpec((MOD_ROWS, k), lambda j: (0, 0)),
                  pl.BlockSpec((k, tn), lambda j: (0, j)),
                  pl.BlockSpec((1, tn), lambda j: (0, j))],
        out_specs=pl.BlockSpec((MOD_ROWS, tn), lambda j: (0, j)),
        compiler_params=_params("arbitrary"),
    )(cond, w, b.reshape(1, n))


def _normmod_kernel(x_ref, g_ref, sh_ref, sc_ref, *o_refs):
    x = x_ref[...]
    y = x * lax.rsqrt(jnp.mean(x * x, axis=-1, keepdims=True) + EPS) * g_ref[...]
    h = y * (1.0 + sc_ref[0]) + sh_ref[0]
    o_refs[0][...] = h.astype(BF16)
    if len(o_refs) > 1:
        o_refs[1][...] = h


def _normmod(x, g, mod3, m_shift, m_scale, geo, want_f32):
    tm = _pick_tile(geo, (256, 128))
    spec = pl.BlockSpec((tm, D_MODEL), lambda i: (i, 0))
    out_shape = [jax.ShapeDtypeStruct((geo.T, D_MODEL), BF16)]
    out_specs = [spec]
    if want_f32:
        out_shape.append(jax.ShapeDtypeStruct((geo.T, D_MODEL), F32))
        out_specs.append(spec)
    return pl.pallas_call(
        _normmod_kernel,
        name="normmod",
        out_shape=out_shape,
        grid=(geo.T // tm,),
        in_specs=[spec, pl.BlockSpec((1, D_MODEL), lambda i: (0, 0)),
                  _mod_spec(m_shift, tm, geo), _mod_spec(m_scale, tm, geo)],
        out_specs=out_specs,
        compiler_params=_params("arbitrary"),
    )(x, g.reshape(1, D_MODEL), mod3, mod3)


def _final_norm_kernel(x_ref, g_ref, o_ref):
    x = x_ref[...]
    o_ref[...] = x * lax.rsqrt(jnp.mean(x * x, axis=-1, keepdims=True) + EPS) * g_ref[...]


def _final_norm(x, g, geo):
    tm = _pick_tile(geo, (256, 128))
    spec = pl.BlockSpec((tm, D_MODEL), lambda i: (i, 0))
    return pl.pallas_call(
        _final_norm_kernel,
        name="final_norm",
        out_shape=jax.ShapeDtypeStruct((geo.BN, D_MODEL), F32),
        grid=(geo.BN // tm,),
        in_specs=[spec, pl.BlockSpec((1, D_MODEL), lambda i: (0, 0))],
        out_specs=spec,
        compiler_params=_params("arbitrary"),
    )(x, g.reshape(1, D_MODEL))


def _mm_kernel(a_ref, w_ref, o_ref):
    o_ref[...] = jnp.dot(a_ref[...], w_ref[...], preferred_element_type=F32)


def _mm_resid_kernel(a_ref, w_ref, x_ref, m_ref, o_ref):
    y = jnp.dot(a_ref[...], w_ref[...], preferred_element_type=F32)
    o_ref[...] = x_ref[...] + m_ref[0] * y


def _matmul(a, w, geo):
    k, n = w.shape
    tm = _pick_tile(geo, (1024, 512, 256, 128))
    tn = 512
    return pl.pallas_call(
        _mm_kernel,
        name="matmul",
        out_shape=jax.ShapeDtypeStruct((geo.T, n), F32),
        grid=(geo.T // tm, n // tn),
        in_specs=[pl.BlockSpec((tm, k), lambda i, j: (i, 0)),
                  pl.BlockSpec((k, tn), lambda i, j: (0, j))],
        out_specs=pl.BlockSpec((tm, tn), lambda i, j: (i, j)),
        compiler_params=_params("arbitrary", "arbitrary"),
    )(a, w)


def _matmul_resid(a, w, x, mod3, m_gate, geo):
    k, n = w.shape
    tm = _pick_tile(geo, (1024, 512, 256, 128))
    tn = 512
    return pl.pallas_call(
        _mm_resid_kernel,
        name="matmul_resid",
        out_shape=jax.ShapeDtypeStruct((geo.T, n), F32),
        grid=(geo.T // tm, n // tn),
        in_specs=[pl.BlockSpec((tm, k), lambda i, j: (i, 0)),
                  pl.BlockSpec((k, tn), lambda i, j: (0, j)),
                  pl.BlockSpec((tm, tn), lambda i, j: (i, j)),
                  pl.BlockSpec((1, 1, tn), lambda i, j: (_mod_row(i, tm, geo) * N_MOD + m_gate, 0, j))],
        out_specs=pl.BlockSpec((tm, tn), lambda i, j: (i, j)),
        compiler_params=_params("arbitrary", "arbitrary"),
    )(a, w, x, mod3)


def _chunk_index(b, i, chunk, geo, reverse):
    nc = geo.Lc // chunk
    nl = geo.N // chunk
    ctx0 = (geo.BN + b * geo.Lc) // chunk
    lat0 = (b * geo.N) // chunk
    if reverse:
        return jnp.where(i < nc, ctx0 + (nc - 1 - i), lat0 + (nl - 1 - (i - nc)))
    return jnp.where(i < nc, ctx0 + i, lat0 + (i - nc))


def _split_dot(tri_bf, x):
    hi = x.astype(BF16)
    r1 = x - hi.astype(F32)
    mid = r1.astype(BF16)
    lo = (r1 - mid.astype(F32)).astype(BF16)
    return (jnp.dot(tri_bf, hi, preferred_element_type=F32)
            + jnp.dot(tri_bf, mid, preferred_element_type=F32)
            + jnp.dot(tri_bf, lo, preferred_element_type=F32))


def _hgrn_kernel(q_ref, f_ref, v_ref, lbp_ref, o_ref, st_ref, *, layer, reverse):
    c = HG_CHUNK
    i = pl.program_id(1)

    @pl.when(i == 0)
    def _():
        st_ref[...] = jnp.zeros_like(st_ref)

    lbp = lbp_ref[...]
    e = jnp.exp(lbp - jnp.max(lbp, axis=0, keepdims=True))
    sm = e / jnp.sum(e, axis=0, keepdims=True)
    lb = sm[0:1]
    for r in range(1, layer + 1):
        lb = lb + sm[r:r + 1]

    row = lax.broadcasted_iota(jnp.int32, (c, c), 0)
    col = lax.broadcasted_iota(jnp.int32, (c, c), 1)
    tri = (col >= row) if reverse else (col <= row)
    tri_bf = jnp.where(tri, 1.0, 0.0).astype(BF16)
    nt = (((1,), (1,)), ((), ()))

    for h in range(HG_HEADS):
        sl = slice(h * HG_DK, (h + 1) * HG_DK)
        qh = _silu(q_ref[:, sl])
        lbh = lb[:, sl]
        fg = lbh + (1.0 - lbh) * _sigmoid(f_ref[:, sl])
        kh = 1.0 - fg
        bcum = _split_dot(tri_bf, jnp.log(fg))
        bmid = bcum[c // 2:c // 2 + 1]
        bend = bcum[0:1] if reverse else bcum[c - 1:c]
        vh = v_ref[:, sl]
        vb = vh.astype(BF16)
        a = (qh * jnp.exp(bcum - bmid)).astype(BF16)
        kd = (kh * jnp.exp(bmid - bcum)).astype(BF16)
        s = lax.dot_general(a, kd, nt, preferred_element_type=F32)
        s = jnp.where(tri, s, 0.0)
        intra = jnp.dot(s.astype(BF16), vb, preferred_element_type=F32)
        st = st_ref[h]
        inter = lax.dot_general((qh * jnp.exp(bcum)).astype(BF16), st.astype(BF16), nt,
                                preferred_element_type=F32)
        o_ref[:, sl] = inter + intra
        kd2 = (kh * jnp.exp(bend - bcum)).astype(BF16)
        st_ref[h] = st * jnp.exp(bend) + jnp.dot(vh.T.astype(BF16), kd2, preferred_element_type=F32)


def _hgrn_scan(p, lb_param, layer, geo, reverse):
    c = HG_CHUNK
    steps = (geo.Lc + geo.N) // c
    fcol = 2 if reverse else 1

    def spec(colblk):
        return pl.BlockSpec((c, HG_F), lambda b, i: (_chunk_index(b, i, c, geo, reverse), colblk))

    return pl.pallas_call(
        functools.partial(_hgrn_kernel, layer=layer, reverse=reverse),
        name="hgrn_scan_bwd" if reverse else "hgrn_scan_fwd",
        out_shape=jax.ShapeDtypeStruct((geo.T, HG_V), F32),
        grid=(geo.B, steps),
        in_specs=[spec(0), spec(fcol), spec(3),
                  pl.BlockSpec(lb_param.shape, lambda b, i: (0, 0))],
        out_specs=spec(0),
        scratch_shapes=[pltpu.VMEM((HG_HEADS, HG_DV, HG_DK), F32)],
        compiler_params=_params("arbitrary", "arbitrary"),
    )(p, p, p, lb_param)


def _ev_readout_kernel(of_ref, ob_ref, gate_ref, a_ref, b_ref, ap_ref, bp_ref, an_ref, bn_ref,
                       gain_ref, cw_ref, cb_ref, lng_ref, lnb_ref, o_ref, ext_ref, conv_ref, *, tm, geo):
    i = pl.program_id(0)
    n_lat = geo.BN // tm
    tpl = geo.N // tm
    tpc = geo.Lc // tm
    j = jnp.where(i < n_lat, i % tpl, (i - n_lat) % tpc)
    per = jnp.where(i < n_lat, tpl, tpc)
    keep_prev = jnp.where(j == 0, 0.0, 1.0)
    keep_next = jnp.where(j == per - 1, 0.0, 1.0)

    o = of_ref[...] + ob_ref[...]
    r = o * lax.rsqrt(jnp.mean(o * o, axis=-1, keepdims=True) + EPS) * gain_ref[...]
    o_ref[:, :HG_V] = (r * _silu(gate_ref[...])).astype(BF16)

    ext_ref[0:CONV_HALO] = ap_ref[...] * _sigmoid(bp_ref[...]) * keep_prev
    ext_ref[CONV_HALO:CONV_HALO + tm] = a_ref[...] * _sigmoid(b_ref[...])
    ext_ref[CONV_HALO + tm:2 * CONV_HALO + tm] = an_ref[...] * _sigmoid(bn_ref[...]) * keep_next
    off = CONV_HALO - CONV_W // 2
    for cj in range(CONV_C // 128):
        cs = slice(cj * 128, (cj + 1) * 128)
        acc = jnp.broadcast_to(cb_ref[:, cs], (tm, 128))
        for k in range(CONV_W):
            acc = acc + cw_ref[k:k + 1, cs] * ext_ref[off + k:off + k + tm, cs]
        conv_ref[:, cs] = acc
    acc = conv_ref[...]
    mu = jnp.mean(acc, axis=-1, keepdims=True)
    xc = acc - mu
    var = jnp.mean(xc * xc, axis=-1, keepdims=True)
    u = xc * lax.rsqrt(var + EPS) * lng_ref[...] + lnb_ref[...]
    o_ref[:, HG_V:] = _silu(u).astype(BF16)


def _ev_readout(p, of, ob, gain, cw, cb, lng, lnb, geo):
    tm = _pick_tile(geo, (128,))
    hb = tm // CONV_HALO
    nhalo = geo.T // CONV_HALO

    def row(colblk):
        return pl.BlockSpec((tm, HG_V), lambda i: (i, colblk))

    def prev(colblk):
        return pl.BlockSpec((CONV_HALO, CONV_C), lambda i: (jnp.maximum(i * hb - 1, 0), colblk))

    def nxt(colblk):
        return pl.BlockSpec((CONV_HALO, CONV_C), lambda i: (jnp.minimum((i + 1) * hb, nhalo - 1), colblk))

    def vec(n):
        return pl.BlockSpec((n, CONV_C), lambda i: (0, 0))

    return pl.pallas_call(
        functools.partial(_ev_readout_kernel, tm=tm, geo=geo),
        name="ev_readout",
        out_shape=jax.ShapeDtypeStruct((geo.T, HG_V + CONV_C), BF16),
        grid=(geo.T // tm,),
        in_specs=[row(0), row(0), row(4), row(5), row(6), prev(5), prev(6), nxt(5), nxt(6),
                  vec(1), vec(CONV_W), vec(1), vec(1), vec(1)],
        out_specs=pl.BlockSpec((tm, HG_V + CONV_C), lambda i: (i, 0)),
        scratch_shapes=[pltpu.VMEM((tm + 2 * CONV_HALO, CONV_C), F32), pltpu.VMEM((tm, CONV_C), F32)],
        compiler_params=_params("arbitrary"),
    )(of, ob, p, p, p, p, p, p, p, gain.reshape(1, -1), cw, cb.reshape(1, -1),
      lng.reshape(1, -1), lnb.reshape(1, -1))


def _rope(x, cos, sin_signed):
    half = x.shape[-1] // 2
    rot = jnp.concatenate([pltpu.roll(x[:, :half], half // 2, axis=1),
                           pltpu.roll(x[:, half:], half // 2, axis=1)], axis=-1)
    return x * cos + rot * sin_signed


def _ret_kernel(q_ref, k_ref, v_ref, cos_ref, sin_ref, dl_ref, o_ref, s_ref, *, reverse):
    c = RET_CHUNK
    i = pl.program_id(1)

    @pl.when(i == 0)
    def _():
        s_ref[...] = jnp.zeros_like(s_ref)

    dl = dl_ref[...]
    lg_all = -jnp.log1p(jnp.exp(-dl))
    row = lax.broadcasted_iota(jnp.int32, (c, c), 0)
    col = lax.broadcasted_iota(jnp.int32, (c, c), 1)
    idx = lax.broadcasted_iota(jnp.int32, (c, 1), 0).astype(F32)
    if reverse:
        live = col >= row
        dist = (col - row).astype(F32)
        q_pow = c - idx
        k_pow = idx
    else:
        live = row >= col
        dist = (row - col).astype(F32)
        q_pow = idx + 1.0
        k_pow = c - 1.0 - idx
    cos = cos_ref[...]
    sin = sin_ref[...]
    nt = (((1,), (1,)), ((), ()))

    for h in range(RET_HEADS):
        lg = lg_all[h:h + 1]
        dmask = jnp.where(live, jnp.exp(lg * dist), 0.0)
        q = _rope(q_ref[:, h * RET_DK:(h + 1) * RET_DK], cos, sin)
        k = _rope(k_ref[:, h * RET_DK:(h + 1) * RET_DK] * (RET_DK ** -0.5), cos, sin)
        vb = v_ref[:, h * RET_DV:(h + 1) * RET_DV].astype(BF16)
        scores = lax.dot_general(q.astype(BF16), k.astype(BF16), nt, preferred_element_type=F32) * dmask
        intra = jnp.dot(scores.astype(BF16), vb, preferred_element_type=F32)
        s = s_ref[h]
        inter = jnp.dot((q * jnp.exp(lg * q_pow)).astype(BF16), s.astype(BF16),
                        preferred_element_type=F32)
        o_ref[:, h * RET_DV:(h + 1) * RET_DV] = inter + intra
        kdec = (k * jnp.exp(lg * k_pow)).T.astype(BF16)
        s_ref[h] = jnp.exp(lg * c) * s + jnp.dot(kdec, vb, preferred_element_type=F32)


def _ret_scan(p, cos_tab, sin_tab, decay_logit, geo, reverse):
    c = RET_CHUNK
    steps = (geo.Lc + geo.N) // c
    nc = geo.Lc // c
    nl = geo.N // c

    def spec(width, colblk):
        return pl.BlockSpec((c, width), lambda b, i: (_chunk_index(b, i, c, geo, reverse), colblk))

    def tab_index(b, i):
        lat = (nl - 1 - (i - nc)) if reverse else (i - nc)
        return (jnp.where(i < nc, nl, lat), 0)

    tab = pl.BlockSpec((c, RET_DK), tab_index)
    return pl.pallas_call(
        functools.partial(_ret_kernel, reverse=reverse),
        name="ret_scan_bwd" if reverse else "ret_scan_fwd",
        out_shape=jax.ShapeDtypeStruct((geo.T, RET_V), F32),
        grid=(geo.B, steps),
        in_specs=[spec(RET_QK, 0), spec(RET_QK, 1), spec(RET_V, 1), tab, tab,
                  pl.BlockSpec((RET_HEADS, 1), lambda b, i: (0, 0))],
        out_specs=spec(RET_V, 0),
        scratch_shapes=[pltpu.VMEM((RET_HEADS, RET_DK, RET_DV), F32)],
        compiler_params=_params("arbitrary", "arbitrary"),
    )(p, p, p, cos_tab, sin_tab, decay_logit.reshape(RET_HEADS, 1))


def _rope_tables(n):
    t = jnp.arange(n)
    quarter = RET_DK // 4
    inv = 1.0 / (ROPE_BASE ** (jnp.arange(quarter, dtype=F32) / quarter))
    ang_r = (t // GRID_W).astype(F32)[:, None] * inv
    ang_c = (t % GRID_W).astype(F32)[:, None] * inv
    cos = jnp.concatenate([jnp.cos(ang_r), jnp.cos(ang_r), jnp.cos(ang_c), jnp.cos(ang_c)], axis=-1)
    sin = jnp.concatenate([-jnp.sin(ang_r), jnp.sin(ang_r), -jnp.sin(ang_c), jnp.sin(ang_c)], axis=-1)
    cos = jnp.concatenate([cos, jnp.ones((RET_CHUNK, RET_DK), F32)], axis=0)
    sin = jnp.concatenate([sin, jnp.zeros((RET_CHUNK, RET_DK), F32)], axis=0)
    return cos, sin


def _ret_readout_kernel(of_ref, ob_ref, gate_ref, o_ref):
    for h in range(RET_HEADS):
        sl = slice(h * RET_DV, (h + 1) * RET_DV)
        o = of_ref[:, sl] + ob_ref[:, sl]
        r = o * lax.rsqrt(jnp.mean(o * o, axis=-1, keepdims=True) + EPS)
        o_ref[:, sl] = (_silu(gate_ref[:, sl]) * r).astype(BF16)


def _ret_readout(p, of, ob, geo):
    tm = _pick_tile(geo, (128,))
    spec = pl.BlockSpec((tm, RET_V), lambda i: (i, 0))
    return pl.pallas_call(
        _ret_readout_kernel,
        name="ret_readout",
        out_shape=jax.ShapeDtypeStruct((geo.T, RET_V), BF16),
        grid=(geo.T // tm,),
        in_specs=[spec, spec, pl.BlockSpec((tm, RET_V), lambda i: (i, 2))],
        out_specs=spec,
        compiler_params=_params("arbitrary"),
    )(of, ob, p)


def _router_kernel(h_ref, rw_ref, rb_ref, eidx_ref, rank_ref, w_ref, cnt_ref, carry_ref, *, tm):
    i = pl.program_id(0)

    @pl.when(i == 0)
    def _():
        carry_ref[...] = jnp.zeros_like(carry_ref)

    logits = jnp.dot(h_ref[...], rw_ref[...], preferred_element_type=F32,
                     precision=lax.Precision.HIGHEST)
    s = _sigmoid(logits)
    sel = s + rb_ref[...]
    lane = lax.broadcasted_iota(jnp.int32, (tm, N_EXPERTS), 1)
    grp = lane // GROUP_SIZE
    ninf = -jnp.inf

    gscore = jnp.zeros((tm, N_EXPERTS), F32)
    gcols = []
    for g in range(N_GROUPS):
        in_g = grp == g
        v1 = jnp.max(jnp.where(in_g, sel, ninf), axis=-1, keepdims=True)
        i1 = jnp.min(jnp.where(in_g & (sel == v1), lane, N_EXPERTS), axis=-1, keepdims=True)
        v2 = jnp.max(jnp.where(in_g & (lane != i1), sel, ninf), axis=-1, keepdims=True)
        gcols.append(v1 + v2)
        gscore = jnp.where(in_g, v1 + v2, gscore)
    beaten = jnp.zeros((tm, N_EXPERTS), jnp.int32)
    for g in range(N_GROUPS):
        wins = (gcols[g] > gscore) | ((gcols[g] == gscore) & (g < grp))
        beaten = beaten + jnp.where(wins, 1, 0)
    cand = jnp.where(beaten < TOPK_GROUPS, sel, ninf)

    lane_k = lax.broadcasted_iota(jnp.int32, (tm, TOP_K), 1)
    eidx = jnp.zeros((tm, TOP_K), jnp.int32)
    wsel = jnp.zeros((tm, TOP_K), F32)
    chosen = jnp.zeros((tm, N_EXPERTS), F32)
    picks = []
    for k in range(TOP_K):
        v = jnp.max(cand, axis=-1, keepdims=True)
        ik = jnp.min(jnp.where(cand == v, lane, N_EXPERTS), axis=-1, keepdims=True)
        hit = lane == ik
        picks.append(ik)
        eidx = jnp.where(lane_k == k, ik, eidx)
        wsel = jnp.where(lane_k == k, jnp.sum(jnp.where(hit, s, 0.0), axis=-1, keepdims=True), wsel)
        chosen = jnp.where(hit, 1.0, chosen)
        cand = jnp.where(hit, ninf, cand)
    w_ref[...] = wsel / jnp.sum(wsel, axis=-1, keepdims=True) * ROUTED_SCALE
    eidx_ref[...] = eidx

    r = lax.broadcasted_iota(jnp.int32, (tm, tm), 0)
    c = lax.broadcasted_iota(jnp.int32, (tm, tm), 1)
    below = jnp.where(c < r, 1.0, 0.0).astype(BF16)
    carry = carry_ref[...]
    pos = jnp.dot(below, chosen.astype(BF16), preferred_element_type=F32) + carry
    rank = jnp.zeros((tm, TOP_K), jnp.int32)
    for k in range(TOP_K):
        rk = jnp.sum(jnp.where(lane == picks[k], pos, 0.0), axis=-1, keepdims=True)
        rank = jnp.where(lane_k == k, rk.astype(jnp.int32), rank)
    rank_ref[...] = rank
    carry = carry + jnp.sum(chosen, axis=0, keepdims=True)
    carry_ref[...] = carry
    cnt_ref[...] = carry


def _router(h_f32, rw, rb, geo):
    tm = 256 if geo.T % 256 == 0 else 128
    tok = pl.BlockSpec((tm, TOP_K), lambda i: (i, 0))
    one = pl.BlockSpec((1, N_EXPERTS), lambda i: (0, 0))
    return pl.pallas_call(
        functools.partial(_router_kernel, tm=tm),
        name="router",
        out_shape=[jax.ShapeDtypeStruct((geo.T, TOP_K), jnp.int32),
                   jax.ShapeDtypeStruct((geo.T, TOP_K), jnp.int32),
                   jax.ShapeDtypeStruct((geo.T, TOP_K), F32),
                   jax.ShapeDtypeStruct((1, N_EXPERTS), F32)],
        grid=(geo.T // tm,),
        in_specs=[pl.BlockSpec((tm, D_MODEL), lambda i: (i, 0)),
                  pl.BlockSpec((D_MODEL, N_EXPERTS), lambda i: (0, 0)), one],
        out_specs=[tok, tok, tok, one],
        scratch_shapes=[pltpu.VMEM((1, N_EXPERTS), F32)],
        compiler_params=_params("arbitrary"),
    )(h_f32, rw, rb.reshape(1, N_EXPERTS))


def _n_blocks(geo):
    return -(-(geo.T * TOP_K) // MOE_BLOCK) + N_EXPERTS


def _dest_kernel(cnt_ref, eidx_ref, rank_ref, dest_ref, blk_ref):
    eidx = eidx_ref[...]
    dest = rank_ref[...]
    blk_row = (lax.broadcasted_iota(jnp.int32, blk_ref.shape, 0) * 128
               + lax.broadcasted_iota(jnp.int32, blk_ref.shape, 1)) * MOE_BLOCK
    blk = jnp.zeros(blk_ref.shape, jnp.int32)
    start = jnp.int32(0)
    for e in range(N_EXPERTS):
        padded = (cnt_ref[e] + (MOE_BLOCK - 1)) // MOE_BLOCK * MOE_BLOCK
        dest = dest + jnp.where(eidx == e, start, 0)
        start = start + padded
        blk = blk + jnp.where(start <= blk_row, 1, 0)
    dest_ref[...] = dest
    blk_ref[...] = jnp.minimum(blk, N_EXPERTS - 1)


def _dest(counts, eidx, rank, geo):
    rows = geo.T * TOP_K // 128
    brow = -(-_n_blocks(geo) // 128)
    full = pl.BlockSpec((rows, 128), lambda: (0, 0))
    dest, blk = pl.pallas_call(
        _dest_kernel,
        name="dest",
        out_shape=[jax.ShapeDtypeStruct((rows, 128), jnp.int32),
                   jax.ShapeDtypeStruct((brow, 128), jnp.int32)],
        in_specs=[pl.BlockSpec(memory_space=pltpu.SMEM), full, full],
        out_specs=[full, pl.BlockSpec((brow, 128), lambda: (0, 0))],
    )(counts, eidx.reshape(rows, 128), rank.reshape(rows, 128))
    return dest.reshape(-1), blk.reshape(-1)[:_n_blocks(geo)]


def _row_copy(src_hbm, s, dst_hbm, d, sem):
    return pltpu.make_async_copy(src_hbm.at[pl.ds(s, 1)], dst_hbm.at[pl.ds(d, 1)], sem)


def _zero_fill(cnt_ref, xs_hbm, zero_ref, zsem, n_rows, wait):
    def piece(pos, size):
        if size >= 8:
            copies = [(pl.multiple_of(pos, 8), size)]
        else:
            copies = [(pos + r, 1) for r in range(size)]
        for p, s in copies:
            cp = pltpu.make_async_copy(zero_ref.at[pl.ds(0, s)], xs_hbm.at[pl.ds(p, s)], zsem)
            cp.wait() if wait else cp.start()

    def per_expert(e, start):
        cnt = cnt_ref[e]
        padded = (cnt + (MOE_BLOCK - 1)) // MOE_BLOCK * MOE_BLOCK
        pad = padded - cnt
        pos = start + cnt
        size = 1
        while size < MOE_BLOCK:
            take = (pad & size) != 0
            pl.when(take)(functools.partial(piece, pos, size))
            pos = pos + jnp.where(take, size, 0)
            size *= 2
        return start + padded

    end = lax.fori_loop(0, N_EXPERTS, per_expert, jnp.int32(0))

    def per_block(j, carry):
        piece(end + j * MOE_BLOCK, MOE_BLOCK)
        return carry

    lax.fori_loop(0, (n_rows - end) // MOE_BLOCK, per_block, 0)


def _dispatch_kernel(cnt_ref, dest_ref, h_ref, xs_hbm, zero_ref, sem, zsem, *, tt, n_rows):
    i = pl.program_id(0)

    @pl.when(i == 0)
    def _():
        zero_ref[...] = jnp.zeros_like(zero_ref)
        _zero_fill(cnt_ref, xs_hbm, zero_ref, zsem, n_rows, wait=False)
        _zero_fill(cnt_ref, xs_hbm, zero_ref, zsem, n_rows, wait=True)

    def issue(t, carry):
        for k in range(TOP_K):
            _row_copy(h_ref, t, xs_hbm, dest_ref[t * TOP_K + k], sem).start()
        return carry

    lax.fori_loop(0, tt, issue, 0)

    def drain(t, carry):
        for k in range(TOP_K):
            _row_copy(h_ref, 0, xs_hbm, 0, sem).wait()
        return carry

    lax.fori_loop(0, tt, drain, 0)


def _dispatch(counts, dest, h_f32, geo):
    tt = 512 if geo.T % 512 == 0 else 128
    n_rows = _n_blocks(geo) * MOE_BLOCK
    return pl.pallas_call(
        functools.partial(_dispatch_kernel, tt=tt, n_rows=n_rows),
        name="dispatch",
        out_shape=jax.ShapeDtypeStruct((n_rows, D_MODEL), F32),
        grid=(geo.T // tt,),
        in_specs=[pl.BlockSpec(memory_space=pltpu.SMEM),
                  pl.BlockSpec((tt * TOP_K,), lambda i: (i,), memory_space=pltpu.SMEM),
                  pl.BlockSpec((tt, D_MODEL), lambda i: (i, 0))],
        out_specs=pl.BlockSpec(memory_space=pl.ANY),
        scratch_shapes=[pltpu.VMEM((MOE_BLOCK, D_MODEL), F32), pltpu.SemaphoreType.DMA,
                        pltpu.SemaphoreType.DMA],
        compiler_params=_params("arbitrary"),
    )(counts, dest, h_f32)


def _swiglu_block(x, wg_ref, wu_ref, wd_ref):
    g = jnp.dot(x, wg_ref[...], preferred_element_type=F32)
    u = jnp.dot(x, wu_ref[...], preferred_element_type=F32)
    return jnp.dot((_silu(g) * u).astype(BF16), wd_ref[...], preferred_element_type=F32)


def _expert_kernel(blk_ref, x_ref, wg_ref, wu_ref, wd_ref, o_ref):
    del blk_ref
    o_ref[...] = _swiglu_block(x_ref[...].astype(BF16), wg_ref, wu_ref, wd_ref)


def _experts(blk_e, xs, wg, wu, wd, geo):
    nb = _n_blocks(geo)
    rows = pl.BlockSpec((MOE_BLOCK, D_MODEL), lambda j, be: (j, 0))
    return pl.pallas_call(
        _expert_kernel,
        name="experts",
        out_shape=jax.ShapeDtypeStruct((nb * MOE_BLOCK, D_MODEL), F32),
        grid_spec=pltpu.PrefetchScalarGridSpec(
            num_scalar_prefetch=1,
            grid=(nb,),
            in_specs=[rows,
                      pl.BlockSpec((None, D_MODEL, EXPERT_FF), lambda j, be: (be[j], 0, 0)),
                      pl.BlockSpec((None, D_MODEL, EXPERT_FF), lambda j, be: (be[j], 0, 0)),
                      pl.BlockSpec((None, EXPERT_FF, D_MODEL), lambda j, be: (be[j], 0, 0))],
            out_specs=rows),
        compiler_params=_params("arbitrary"),
    )(blk_e, xs, wg, wu, wd)


def _shared_kernel(x_ref, wg_ref, wu_ref, wd_ref, o_ref):
    o_ref[...] = _swiglu_block(x_ref[...], wg_ref, wu_ref, wd_ref)


def _shared_expert(h_bf, wg, wu, wd, geo):
    tm = _pick_tile(geo, (512, 256, 128))
    rows = pl.BlockSpec((tm, D_MODEL), lambda i: (i, 0))
    ff = wg.shape[1]
    return pl.pallas_call(
        _shared_kernel,
        name="shared_expert",
        out_shape=jax.ShapeDtypeStruct((geo.T, D_MODEL), F32),
        grid=(geo.T // tm,),
        in_specs=[rows,
                  pl.BlockSpec((D_MODEL, ff), lambda i: (0, 0)),
                  pl.BlockSpec((D_MODEL, ff), lambda i: (0, 0)),
                  pl.BlockSpec((ff, D_MODEL), lambda i: (0, 0))],
        out_specs=rows,
        compiler_params=_params("arbitrary"),
    )(h_bf, wg, wu, wd)


def _combine_kernel(dest_ref, w_ref, x_ref, sh_ref, m_ref, ys_hbm, o_ref, g_ref, sem, *, tt):
    def issue(t, carry):
        for k in range(TOP_K):
            pltpu.make_async_copy(ys_hbm.at[pl.ds(dest_ref[t * TOP_K + k], 1)],
                                  g_ref.at[k, pl.ds(t, 1)], sem).start()
        return carry

    lax.fori_loop(0, tt, issue, 0)

    def drain(t, carry):
        for k in range(TOP_K):
            pltpu.make_async_copy(ys_hbm.at[pl.ds(0, 1)], g_ref.at[k, pl.ds(0, 1)], sem).wait()
        return carry

    lax.fori_loop(0, tt, drain, 0)

    w = w_ref[...]
    acc = sh_ref[...]
    for k in range(TOP_K):
        acc = acc + w[:, k:k + 1] * g_ref[k]
    o_ref[...] = x_ref[...] + m_ref[0] * acc


def _combine(dest, w, x, sh, ys, mod3, m_gate, geo):
    tt = 128
    rows = pl.BlockSpec((tt, D_MODEL), lambda i: (i, 0))
    return pl.pallas_call(
        functools.partial(_combine_kernel, tt=tt),
        name="combine",
        out_shape=jax.ShapeDtypeStruct((geo.T, D_MODEL), F32),
        grid=(geo.T // tt,),
        in_specs=[pl.BlockSpec((tt * TOP_K,), lambda i: (i,), memory_space=pltpu.SMEM),
                  pl.BlockSpec((tt, TOP_K), lambda i: (i, 0)),
                  rows, rows, _mod_spec(m_gate, tt, geo),
                  pl.BlockSpec(memory_space=pl.ANY)],
        out_specs=rows,
        scratch_shapes=[pltpu.VMEM((TOP_K, tt, D_MODEL), F32), pltpu.SemaphoreType.DMA],
        compiler_params=_params("arbitrary"),
    )(dest, w, x, sh, mod3, ys)


def _moe(x, h_bf, h_f32, mod3, rw, rb, wg, wu, wd, sg, su, sd, geo):
    eidx, rank, w, counts = _router(h_f32, rw, rb, geo)
    counts = counts.reshape(N_EXPERTS).astype(jnp.int32)
    dest, blk_e = _dest(counts, eidx, rank, geo)
    xs = _dispatch(counts, dest, h_f32, geo)
    ys = _experts(blk_e, xs, wg.astype(BF16), wu.astype(BF16), wd.astype(BF16), geo)
    sh = _shared_expert(h_bf, sg.astype(BF16), su.astype(BF16), sd.astype(BF16), geo)
    return _combine(dest, w, x, sh, ys, mod3, 5, geo)


def kernel(x, c, ctx, c_ctx, ada_w, ada_b, norm_mix, norm_ffn, norm_final, ev_w_in, ev_w_out, hgrn_lb, hgrn_norm, conv_w, conv_b, conv_norm_g, conv_norm_b, ret_w_in, ret_w_out, ret_decay, router_w, router_b, exp_gate, exp_up, exp_down, sh_gate, sh_up, sh_down):
    b, n, d = x.shape
    lc = ctx.shape[1]
    depth = ada_w.shape[0]
    geo = _geo(b, n, lc)
    assert d == D_MODEL and b < MOD_ROWS
    assert n % RET_CHUNK == 0 and lc % RET_CHUNK == 0

    xs = jnp.concatenate([x.reshape(geo.BN, d), ctx.reshape(geo.BL, d)], axis=0)
    cond = jnp.zeros((MOD_ROWS, d), F32).at[:b].set(c).at[b].set(c_ctx)
    cos_tab, sin_tab = _rope_tables(n)

    for l in range(depth):
        j = l // 2
        mod3 = _adaln(cond, ada_w[l], ada_b[l]).reshape(MOD_ROWS * N_MOD, 1, d)
        (h,) = _normmod(xs, norm_mix[l], mod3, 0, 1, geo, want_f32=False)
        if l % 2 == 0:
            p = _matmul(h, ev_w_in[j].astype(BF16), geo)
            of = _hgrn_scan(p, hgrn_lb, l, geo, reverse=False)
            ob = _hgrn_scan(p, hgrn_lb, l, geo, reverse=True)
            mix = _ev_readout(p, of, ob, hgrn_norm[j], conv_w[j], conv_b[j],
                              conv_norm_g[j], conv_norm_b[j], geo)
            xs = _matmul_resid(mix, ev_w_out[j].astype(BF16), xs, mod3, 2, geo)
        else:
            p = _matmul(h, ret_w_in[j].astype(BF16), geo)
            of = _ret_scan(p, cos_tab, sin_tab, ret_decay[j, 0], geo, reverse=False)
            ob = _ret_scan(p, cos_tab, sin_tab, ret_decay[j, 1], geo, reverse=True)
            mix = _ret_readout(p, of, ob, geo)
            xs = _matmul_resid(mix, ret_w_out[j].astype(BF16), xs, mod3, 2, geo)
        h_bf, h_f32 = _normmod(xs, norm_ffn[l], mod3, 3, 4, geo, want_f32=True)
        xs = _moe(xs, h_bf, h_f32, mod3, router_w[l], router_b[l], exp_gate[l], exp_up[l], exp_down[l],
                  sh_gate[l], sh_up[l], sh_down[l], geo)
    return _final_norm(xs, norm_final, geo).reshape(b, n, d)
```

```python
import collections
import functools

import jax
import jax.numpy as jnp
from jax import lax
from jax.experimental import pallas as pl
from jax.experimental.pallas import tpu as pltpu

F32 = jnp.float32
BF16 = jnp.bfloat16

D_MODEL = 2048
N_MOD = 6
EPS = 1e-6
GRID_W = 64
ROPE_BASE = 10000.0

HG_HEADS = 8
HG_DK = 128
HG_DV = 128
HG_F = HG_HEADS * HG_DK
HG_V = HG_HEADS * HG_DV
CONV_C = D_MODEL // 2
CONV_W = 31
CONV_HALO = 16
HG_CHUNK = 64

RET_HEADS = 8
RET_DK = D_MODEL // RET_HEADS
RET_DV = 2 * RET_DK
RET_QK = RET_HEADS * RET_DK
RET_V = RET_HEADS * RET_DV
RET_CHUNK = 128

N_EXPERTS = 64
EXPERT_FF = D_MODEL // 4
TOP_K = 8
N_GROUPS = 8
GROUP_SIZE = N_EXPERTS // N_GROUPS
TOPK_GROUPS = 4
ROUTED_SCALE = 2.5
MOE_BLOCK = 512

MOD_ROWS = 16
VMEM_LIMIT = 56 * 1024 * 1024

Geo = collections.namedtuple("Geo", "B N Lc BN BL T")


def _geo(b, n, lc):
    return Geo(b, n, lc, b * n, b * lc, b * n + b * lc)


def _pick_tile(geo, cands):
    for t in cands:
        if geo.N % t == 0 and geo.BL % t == 0:
            return t
    raise ValueError("no row tile fits the sequence lengths")


def _mod_row(i, tm, geo):
    return jnp.where(i < geo.BN // tm, i // (geo.N // tm), geo.B)


def _mod_spec(m, tm, geo, ngrid=1):
    if ngrid == 1:
        return pl.BlockSpec((1, 1, D_MODEL), lambda i: (_mod_row(i, tm, geo) * N_MOD + m, 0, 0))
    return pl.BlockSpec((1, 1, D_MODEL), lambda i, j: (_mod_row(i, tm, geo) * N_MOD + m, 0, j))


def _params(*sem):
    return pltpu.CompilerParams(dimension_semantics=sem, vmem_limit_bytes=VMEM_LIMIT)


def _sigmoid(x):
    return jax.nn.sigmoid(x)


def _silu(x):
    return x * jax.nn.sigmoid(x)


def _adaln_kernel(c_ref, w_ref, b_ref, o_ref):
    a = _silu(c_ref[...]).astype(BF16)
    o_ref[...] = jnp.dot(a, w_ref[...].astype(BF16), preferred_element_type=F32) + b_ref[...]


def _adaln(cond, w, b, layer):
    depth, k, n = w.shape
    tn = 1024
    return pl.pallas_call(
        _adaln_kernel,
        name="adaln",
        out_shape=jax.ShapeDtypeStruct((MOD_ROWS, n), F32),
        grid=(n // tn,),
        in_specs=[pl.BlockSpec((MOD_ROWS, k), lambda j: (0, 0)),
                  pl.BlockSpec((None, k, tn), lambda j: (layer, 0, j)),
                  pl.BlockSpec((None, 1, tn), lambda j: (layer, 0, j))],
        out_specs=pl.BlockSpec((MOD_ROWS, tn), lambda j: (0, j)),
        compiler_params=_params("arbitrary"),
    )(cond, w, b.reshape(depth, 1, n))


def _normmod_kernel(x_ref, g_ref, sh_ref, sc_ref, *o_refs):
    x = x_ref[...]
    y = x * lax.rsqrt(jnp.mean(x * x, axis=-1, keepdims=True) + EPS) * g_ref[...]
    h = y * (1.0 + sc_ref[0]) + sh_ref[0]
    o_refs[0][...] = h.astype(BF16)
    if len(o_refs) > 1:
        o_refs[1][...] = h


def _normmod(x, g, mod3, m_shift, m_scale, geo, want_f32):
    tm = _pick_tile(geo, (256, 128))
    spec = pl.BlockSpec((tm, D_MODEL), lambda i: (i, 0))
    out_shape = [jax.ShapeDtypeStruct((geo.T, D_MODEL), BF16)]
    out_specs = [spec]
    if want_f32:
        out_shape.append(jax.ShapeDtypeStruct((geo.T, D_MODEL), F32))
        out_specs.append(spec)
    return pl.pallas_call(
        _normmod_kernel,
        name="normmod",
        out_shape=out_shape,
        grid=(geo.T // tm,),
        in_specs=[spec, pl.BlockSpec((1, D_MODEL), lambda i: (0, 0)),
                  _mod_spec(m_shift, tm, geo), _mod_spec(m_scale, tm, geo)],
        out_specs=out_specs,
        compiler_params=_params("arbitrary"),
    )(x, g.reshape(1, D_MODEL), mod3, mod3)


def _final_norm_kernel(x_ref, g_ref, o_ref):
    x = x_ref[...]
    o_ref[...] = x * lax.rsqrt(jnp.mean(x * x, axis=-1, keepdims=True) + EPS) * g_ref[...]


def _final_norm(x, g, geo):
    tm = _pick_tile(geo, (256, 128))
    spec = pl.BlockSpec((tm, D_MODEL), lambda i: (i, 0))
    return pl.pallas_call(
        _final_norm_kernel,
        name="final_norm",
        out_shape=jax.ShapeDtypeStruct((geo.BN, D_MODEL), F32),
        grid=(geo.BN // tm,),
        in_specs=[spec, pl.BlockSpec((1, D_MODEL), lambda i: (0, 0))],
        out_specs=spec,
        compiler_params=_params("arbitrary"),
    )(x, g.reshape(1, D_MODEL))


def _mm_kernel(a_ref, w_ref, o_ref):
    o_ref[...] = jnp.dot(a_ref[...], w_ref[...], preferred_element_type=F32).astype(o_ref.dtype)


def _mm_resid_kernel(a_ref, w_ref, x_ref, m_ref, o_ref):
    y = jnp.dot(a_ref[...], w_ref[...], preferred_element_type=F32)
    o_ref[...] = x_ref[...] + m_ref[0] * y


def _matmul(a, w, geo):
    k, n = w.shape
    tm = _pick_tile(geo, (1024, 512, 256, 128))
    tn = 512
    return pl.pallas_call(
        _mm_kernel,
        name="matmul",
        out_shape=jax.ShapeDtypeStruct((geo.T, n), BF16),
        grid=(geo.T // tm, n // tn),
        in_specs=[pl.BlockSpec((tm, k), lambda i, j: (i, 0)),
                  pl.BlockSpec((k, tn), lambda i, j: (0, j))],
        out_specs=pl.BlockSpec((tm, tn), lambda i, j: (i, j)),
        compiler_params=_params("arbitrary", "arbitrary"),
    )(a, w)


def _matmul_resid(a, w, x, mod3, m_gate, geo):
    k, n = w.shape
    tm = _pick_tile(geo, (1024, 512, 256, 128))
    tn = 512
    return pl.pallas_call(
        _mm_resid_kernel,
        name="matmul_resid",
        out_shape=jax.ShapeDtypeStruct((geo.T, n), F32),
        grid=(geo.T // tm, n // tn),
        in_specs=[pl.BlockSpec((tm, k), lambda i, j: (i, 0)),
                  pl.BlockSpec((k, tn), lambda i, j: (0, j)),
                  pl.BlockSpec((tm, tn), lambda i, j: (i, j)),
                  pl.BlockSpec((1, 1, tn), lambda i, j: (_mod_row(i, tm, geo) * N_MOD + m_gate, 0, j))],
        out_specs=pl.BlockSpec((tm, tn), lambda i, j: (i, j)),
        compiler_params=_params("arbitrary", "arbitrary"),
    )(a, w, x, mod3)


def _chunk_index(b, i, chunk, geo, reverse):
    nc = geo.Lc // chunk
    nl = geo.N // chunk
    ctx0 = (geo.BN + b * geo.Lc) // chunk
    lat0 = (b * geo.N) // chunk
    if reverse:
        return jnp.where(i < nc, ctx0 + (nc - 1 - i), lat0 + (nl - 1 - (i - nc)))
    return jnp.where(i < nc, ctx0 + i, lat0 + (i - nc))


def _split_dot(tri_bf, x):
    hi = x.astype(BF16)
    r1 = x - hi.astype(F32)
    mid = r1.astype(BF16)
    lo = (r1 - mid.astype(F32)).astype(BF16)
    return (jnp.dot(tri_bf, hi, preferred_element_type=F32)
            + jnp.dot(tri_bf, mid, preferred_element_type=F32)
            + jnp.dot(tri_bf, lo, preferred_element_type=F32))


def _hgrn_kernel(q_ref, f_ref, v_ref, lbp_ref, o_ref, st_ref, *, layer, reverse):
    c = HG_CHUNK
    i = pl.program_id(1)

    @pl.when(i == 0)
    def _():
        st_ref[...] = jnp.zeros_like(st_ref)

    lbp = lbp_ref[...]
    e = jnp.exp(lbp - jnp.max(lbp, axis=0, keepdims=True))
    sm = e / jnp.sum(e, axis=0, keepdims=True)
    lb = sm[0:1]
    for r in range(1, layer + 1):
        lb = lb + sm[r:r + 1]

    row = lax.broadcasted_iota(jnp.int32, (c, c), 0)
    col = lax.broadcasted_iota(jnp.int32, (c, c), 1)
    tri = (col >= row) if reverse else (col <= row)
    tri_bf = jnp.where(tri, 1.0, 0.0).astype(BF16)
    nt = (((1,), (1,)), ((), ()))

    for h in range(HG_HEADS):
        sl = slice(h * HG_DK, (h + 1) * HG_DK)
        qh = _silu(q_ref[:, sl].astype(F32))
        lbh = lb[:, sl]
        fg = lbh + (1.0 - lbh) * _sigmoid(f_ref[:, sl].astype(F32))
        kh = 1.0 - fg
        bcum = _split_dot(tri_bf, jnp.log(fg))
        bmid = bcum[c // 2:c // 2 + 1]
        bend = bcum[0:1] if reverse else bcum[c - 1:c]
        vb = v_ref[:, sl]
        vh = vb.astype(F32)
        a = (qh * jnp.exp(bcum - bmid)).astype(BF16)
        kd = (kh * jnp.exp(bmid - bcum)).astype(BF16)
        s = lax.dot_general(a, kd, nt, preferred_element_type=F32)
        s = jnp.where(tri, s, 0.0)
        intra = jnp.dot(s.astype(BF16), vb, preferred_element_type=F32)
        st = st_ref[h]
        inter = lax.dot_general((qh * jnp.exp(bcum)).astype(BF16), st.astype(BF16), nt,
                                preferred_element_type=F32)
        o_ref[:, sl] = (inter + intra).astype(BF16)
        kd2 = (kh * jnp.exp(bend - bcum)).astype(BF16)
        st_ref[h] = st * jnp.exp(bend) + jnp.dot(vh.T.astype(BF16), kd2, preferred_element_type=F32)


def _hgrn_scan(p, lb_param, layer, geo, reverse):
    c = HG_CHUNK
    steps = (geo.Lc + geo.N) // c
    fcol = 2 if reverse else 1

    def spec(colblk):
        return pl.BlockSpec((c, HG_F), lambda b, i: (_chunk_index(b, i, c, geo, reverse), colblk))

    return pl.pallas_call(
        functools.partial(_hgrn_kernel, layer=layer, reverse=reverse),
        name="hgrn_scan_bwd" if reverse else "hgrn_scan_fwd",
        out_shape=jax.ShapeDtypeStruct((geo.T, HG_V), BF16),
        grid=(geo.B, steps),
        in_specs=[spec(0), spec(fcol), spec(3),
                  pl.BlockSpec(lb_param.shape, lambda b, i: (0, 0))],
        out_specs=spec(0),
        scratch_shapes=[pltpu.VMEM((HG_HEADS, HG_DV, HG_DK), F32)],
        compiler_params=_params("arbitrary", "arbitrary"),
    )(p, p, p, lb_param)


def _ev_readout_kernel(of_ref, ob_ref, gate_ref, a_ref, b_ref, ap_ref, bp_ref, an_ref, bn_ref,
                       gain_ref, cw_ref, cb_ref, lng_ref, lnb_ref, o_ref, ext_ref, conv_ref, *, tm, geo):
    i = pl.program_id(0)
    n_lat = geo.BN // tm
    tpl = geo.N // tm
    tpc = geo.Lc // tm
    j = jnp.where(i < n_lat, i % tpl, (i - n_lat) % tpc)
    per = jnp.where(i < n_lat, tpl, tpc)
    keep_prev = jnp.where(j == 0, 0.0, 1.0)
    keep_next = jnp.where(j == per - 1, 0.0, 1.0)

    o = of_ref[...].astype(F32) + ob_ref[...].astype(F32)
    r = o * lax.rsqrt(jnp.mean(o * o, axis=-1, keepdims=True) + EPS) * gain_ref[...]
    o_ref[:, :HG_V] = (r * _silu(gate_ref[...].astype(F32))).astype(BF16)

    def glu(x_ref, y_ref):
        return x_ref[...].astype(F32) * _sigmoid(y_ref[...].astype(F32))

    ext_ref[0:CONV_HALO] = glu(ap_ref, bp_ref) * keep_prev
    ext_ref[CONV_HALO:CONV_HALO + tm] = glu(a_ref, b_ref)
    ext_ref[CONV_HALO + tm:2 * CONV_HALO + tm] = glu(an_ref, bn_ref) * keep_next
    off = CONV_HALO - CONV_W // 2
    for cj in range(CONV_C // 128):
        cs = slice(cj * 128, (cj + 1) * 128)
        acc = jnp.broadcast_to(cb_ref[:, cs], (tm, 128))
        for k in range(CONV_W):
            acc = acc + cw_ref[k:k + 1, cs] * ext_ref[off + k:off + k + tm, cs]
        conv_ref[:, cs] = acc
    acc = conv_ref[...]
    mu = jnp.mean(acc, axis=-1, keepdims=True)
    xc = acc - mu
    var = jnp.mean(xc * xc, axis=-1, keepdims=True)
    u = xc * lax.rsqrt(var + EPS) * lng_ref[...] + lnb_ref[...]
    o_ref[:, HG_V:] = _silu(u).astype(BF16)


def _ev_readout(p, of, ob, gain, cw, cb, lng, lnb, geo):
    tm = _pick_tile(geo, (128,))
    hb = tm // CONV_HALO
    nhalo = geo.T // CONV_HALO

    def row(colblk):
        return pl.BlockSpec((tm, HG_V), lambda i: (i, colblk))

    def prev(colblk):
        return pl.BlockSpec((CONV_HALO, CONV_C), lambda i: (jnp.maximum(i * hb - 1, 0), colblk))

    def nxt(colblk):
        return pl.BlockSpec((CONV_HALO, CONV_C), lambda i: (jnp.minimum((i + 1) * hb, nhalo - 1), colblk))

    def vec(n):
        return pl.BlockSpec((n, CONV_C), lambda i: (0, 0))

    return pl.pallas_call(
        functools.partial(_ev_readout_kernel, tm=tm, geo=geo),
        name="ev_readout",
        out_shape=jax.ShapeDtypeStruct((geo.T, HG_V + CONV_C), BF16),
        grid=(geo.T // tm,),
        in_specs=[row(0), row(0), row(4), row(5), row(6), prev(5), prev(6), nxt(5), nxt(6),
                  vec(1), vec(CONV_W), vec(1), vec(1), vec(1)],
        out_specs=pl.BlockSpec((tm, HG_V + CONV_C), lambda i: (i, 0)),
        scratch_shapes=[pltpu.VMEM((tm + 2 * CONV_HALO, CONV_C), F32), pltpu.VMEM((tm, CONV_C), F32)],
        compiler_params=_params("arbitrary"),
    )(of, ob, p, p, p, p, p, p, p, gain.reshape(1, -1), cw, cb.reshape(1, -1),
      lng.reshape(1, -1), lnb.reshape(1, -1))


def _rope(x, cos, sin_signed):
    half = x.shape[-1] // 2
    rot = jnp.concatenate([pltpu.roll(x[:, :half], half // 2, axis=1),
                           pltpu.roll(x[:, half:], half // 2, axis=1)], axis=-1)
    return x * cos + rot * sin_signed


def _ret_kernel(q_ref, k_ref, v_ref, cos_ref, sin_ref, dl_ref, o_ref, s_ref, *, reverse):
    c = RET_CHUNK
    i = pl.program_id(1)

    @pl.when(i == 0)
    def _():
        s_ref[...] = jnp.zeros_like(s_ref)

    dl = dl_ref[...]
    lg_all = -jnp.log1p(jnp.exp(-dl))
    row = lax.broadcasted_iota(jnp.int32, (c, c), 0)
    col = lax.broadcasted_iota(jnp.int32, (c, c), 1)
    idx = lax.broadcasted_iota(jnp.int32, (c, 1), 0).astype(F32)
    if reverse:
        live = col >= row
        dist = (col - row).astype(F32)
        q_pow = c - idx
        k_pow = idx
    else:
        live = row >= col
        dist = (row - col).astype(F32)
        q_pow = idx + 1.0
        k_pow = c - 1.0 - idx
    cos = cos_ref[...]
    sin = sin_ref[...]
    nt = (((1,), (1,)), ((), ()))

    for h in range(RET_HEADS):
        lg = lg_all[h:h + 1]
        dmask = jnp.where(live, jnp.exp(lg * dist), 0.0)
        q = _rope(q_ref[:, h * RET_DK:(h + 1) * RET_DK].astype(F32), cos, sin)
        k = _rope(k_ref[:, h * RET_DK:(h + 1) * RET_DK].astype(F32) * (RET_DK ** -0.5), cos, sin)
        vb = v_ref[:, h * RET_DV:(h + 1) * RET_DV]
        scores = lax.dot_general(q.astype(BF16), k.astype(BF16), nt, preferred_element_type=F32) * dmask
        intra = jnp.dot(scores.astype(BF16), vb, preferred_element_type=F32)
        s = s_ref[h]
        inter = jnp.dot((q * jnp.exp(lg * q_pow)).astype(BF16), s.astype(BF16),
                        preferred_element_type=F32)
        o_ref[:, h * RET_DV:(h + 1) * RET_DV] = (inter + intra).astype(BF16)
        kdec = (k * jnp.exp(lg * k_pow)).T.astype(BF16)
        s_ref[h] = jnp.exp(lg * c) * s + jnp.dot(kdec, vb, preferred_element_type=F32)


def _ret_scan(p, cos_tab, sin_tab, decay_logit, geo, reverse):
    c = RET_CHUNK
    steps = (geo.Lc + geo.N) // c
    nc = geo.Lc // c
    nl = geo.N // c

    def spec(width, colblk):
        return pl.BlockSpec((c, width), lambda b, i: (_chunk_index(b, i, c, geo, reverse), colblk))

    def tab_index(b, i):
        lat = (nl - 1 - (i - nc)) if reverse else (i - nc)
        return (jnp.where(i < nc, nl, lat), 0)

    tab = pl.BlockSpec((c, RET_DK), tab_index)
    return pl.pallas_call(
        functools.partial(_ret_kernel, reverse=reverse),
        name="ret_scan_bwd" if reverse else "ret_scan_fwd",
        out_shape=jax.ShapeDtypeStruct((geo.T, RET_V), BF16),
        grid=(geo.B, steps),
        in_specs=[spec(RET_QK, 0), spec(RET_QK, 1), spec(RET_V, 1), tab, tab,
                  pl.BlockSpec((RET_HEADS, 1), lambda b, i: (0, 0))],
        out_specs=spec(RET_V, 0),
        scratch_shapes=[pltpu.VMEM((RET_HEADS, RET_DK, RET_DV), F32)],
        compiler_params=_params("arbitrary", "arbitrary"),
    )(p, p, p, cos_tab, sin_tab, decay_logit.reshape(RET_HEADS, 1))


def _rope_tables(n):
    t = jnp.arange(n)
    quarter = RET_DK // 4
    inv = 1.0 / (ROPE_BASE ** (jnp.arange(quarter, dtype=F32) / quarter))
    ang_r = (t // GRID_W).astype(F32)[:, None] * inv
    ang_c = (t % GRID_W).astype(F32)[:, None] * inv
    cos = jnp.concatenate([jnp.cos(ang_r), jnp.cos(ang_r), jnp.cos(ang_c), jnp.cos(ang_c)], axis=-1)
    sin = jnp.concatenate([-jnp.sin(ang_r), jnp.sin(ang_r), -jnp.sin(ang_c), jnp.sin(ang_c)], axis=-1)
    cos = jnp.concatenate([cos, jnp.ones((RET_CHUNK, RET_DK), F32)], axis=0)
    sin = jnp.concatenate([sin, jnp.zeros((RET_CHUNK, RET_DK), F32)], axis=0)
    return cos, sin


def _ret_readout_kernel(of_ref, ob_ref, gate_ref, o_ref):
    for h in range(RET_HEADS):
        sl = slice(h * RET_DV, (h + 1) * RET_DV)
        o = of_ref[:, sl].astype(F32) + ob_ref[:, sl].astype(F32)
        r = o * lax.rsqrt(jnp.mean(o * o, axis=-1, keepdims=True) + EPS)
        o_ref[:, sl] = (_silu(gate_ref[:, sl].astype(F32)) * r).astype(BF16)


def _ret_readout(p, of, ob, geo):
    tm = _pick_tile(geo, (128,))
    spec = pl.BlockSpec((tm, RET_V), lambda i: (i, 0))
    return pl.pallas_call(
        _ret_readout_kernel,
        name="ret_readout",
        out_shape=jax.ShapeDtypeStruct((geo.T, RET_V), BF16),
        grid=(geo.T // tm,),
        in_specs=[spec, spec, pl.BlockSpec((tm, RET_V), lambda i: (i, 2))],
        out_specs=spec,
        compiler_params=_params("arbitrary"),
    )(of, ob, p)


def _router_kernel(h_ref, rw_ref, rb_ref, eidx_ref, rank_ref, w_ref, cnt_ref, carry_ref, *, tm):
    i = pl.program_id(0)

    @pl.when(i == 0)
    def _():
        carry_ref[...] = jnp.zeros_like(carry_ref)

    logits = jnp.dot(h_ref[...], rw_ref[...], preferred_element_type=F32,
                     precision=lax.Precision.HIGHEST)
    s = _sigmoid(logits)
    sel = s + rb_ref[...]
    lane = lax.broadcasted_iota(jnp.int32, (tm, N_EXPERTS), 1)
    grp = lane // GROUP_SIZE
    ninf = -jnp.inf

    gscore = jnp.zeros((tm, N_EXPERTS), F32)
    gcols = []
    for g in range(N_GROUPS):
        in_g = grp == g
        v1 = jnp.max(jnp.where(in_g, sel, ninf), axis=-1, keepdims=True)
        i1 = jnp.min(jnp.where(in_g & (sel == v1), lane, N_EXPERTS), axis=-1, keepdims=True)
        v2 = jnp.max(jnp.where(in_g & (lane != i1), sel, ninf), axis=-1, keepdims=True)
        gcols.append(v1 + v2)
        gscore = jnp.where(in_g, v1 + v2, gscore)
    beaten = jnp.zeros((tm, N_EXPERTS), jnp.int32)
    for g in range(N_GROUPS):
        wins = (gcols[g] > gscore) | ((gcols[g] == gscore) & (g < grp))
        beaten = beaten + jnp.where(wins, 1, 0)
    cand = jnp.where(beaten < TOPK_GROUPS, sel, ninf)

    lane_k = lax.broadcasted_iota(jnp.int32, (tm, TOP_K), 1)
    eidx = jnp.zeros((tm, TOP_K), jnp.int32)
    wsel = jnp.zeros((tm, TOP_K), F32)
    chosen = jnp.zeros((tm, N_EXPERTS), F32)
    picks = []
    for k in range(TOP_K):
        v = jnp.max(cand, axis=-1, keepdims=True)
        ik = jnp.min(jnp.where(cand == v, lane, N_EXPERTS), axis=-1, keepdims=True)
        hit = lane == ik
        picks.append(ik)
        eidx = jnp.where(lane_k == k, ik, eidx)
        wsel = jnp.where(lane_k == k, jnp.sum(jnp.where(hit, s, 0.0), axis=-1, keepdims=True), wsel)
        chosen = jnp.where(hit, 1.0, chosen)
        cand = jnp.where(hit, ninf, cand)
    w_ref[...] = wsel / jnp.sum(wsel, axis=-1, keepdims=True) * ROUTED_SCALE
    eidx_ref[...] = eidx

    r = lax.broadcasted_iota(jnp.int32, (tm, tm), 0)
    c = lax.broadcasted_iota(jnp.int32, (tm, tm), 1)
    below = jnp.where(c < r, 1.0, 0.0).astype(BF16)
    carry = carry_ref[...]
    pos = jnp.dot(below, chosen.astype(BF16), preferred_element_type=F32) + carry
    rank = jnp.zeros((tm, TOP_K), jnp.int32)
    for k in range(TOP_K):
        rk = jnp.sum(jnp.where(lane == picks[k], pos, 0.0), axis=-1, keepdims=True)
        rank = jnp.where(lane_k == k, rk.astype(jnp.int32), rank)
    rank_ref[...] = rank
    carry = carry + jnp.sum(chosen, axis=0, keepdims=True)
    carry_ref[...] = carry
    cnt_ref[...] = carry


def _router(h_f32, rw, rb, geo):
    tm = 256 if geo.T % 256 == 0 else 128
    tok = pl.BlockSpec((tm, TOP_K), lambda i: (i, 0))
    one = pl.BlockSpec((1, N_EXPERTS), lambda i: (0, 0))
    return pl.pallas_call(
        functools.partial(_router_kernel, tm=tm),
        name="router",
        out_shape=[jax.ShapeDtypeStruct((geo.T, TOP_K), jnp.int32),
                   jax.ShapeDtypeStruct((geo.T, TOP_K), jnp.int32),
                   jax.ShapeDtypeStruct((geo.T, TOP_K), F32),
                   jax.ShapeDtypeStruct((1, N_EXPERTS), F32)],
        grid=(geo.T // tm,),
        in_specs=[pl.BlockSpec((tm, D_MODEL), lambda i: (i, 0)),
                  pl.BlockSpec((D_MODEL, N_EXPERTS), lambda i: (0, 0)), one],
        out_specs=[tok, tok, tok, one],
        scratch_shapes=[pltpu.VMEM((1, N_EXPERTS), F32)],
        compiler_params=_params("arbitrary"),
    )(h_f32, rw, rb.reshape(1, N_EXPERTS))


def _n_blocks(geo):
    return -(-(geo.T * TOP_K) // MOE_BLOCK) + N_EXPERTS


def _dest_kernel(cnt_ref, eidx_ref, rank_ref, dest_ref, blk_ref, *, n_blocks):
    eidx = eidx_ref[...]
    dest = rank_ref[...]
    blk_row = (lax.broadcasted_iota(jnp.int32, blk_ref.shape, 0) * 128
               + lax.broadcasted_iota(jnp.int32, blk_ref.shape, 1)) * MOE_BLOCK
    blk = jnp.zeros(blk_ref.shape, jnp.int32)
    start = jnp.int32(0)
    for e in range(N_EXPERTS):
        padded = (cnt_ref[e] + (MOE_BLOCK - 1)) // MOE_BLOCK * MOE_BLOCK
        dest = dest + jnp.where(eidx == e, start, 0)
        start = start + padded
        blk = blk + jnp.where(start <= blk_row, 1, 0)
    dest_ref[...] = dest
    blk_ref[...] = jnp.where(blk_row == n_blocks * MOE_BLOCK, start // MOE_BLOCK,
                             jnp.minimum(blk, N_EXPERTS - 1))


def _dest(counts, eidx, rank, geo):
    rows = geo.T * TOP_K // 128
    brow = -(-(_n_blocks(geo) + 1) // 128)
    full = pl.BlockSpec((rows, 128), lambda: (0, 0))
    dest, blk = pl.pallas_call(
        functools.partial(_dest_kernel, n_blocks=_n_blocks(geo)),
        name="dest",
        out_shape=[jax.ShapeDtypeStruct((rows, 128), jnp.int32),
                   jax.ShapeDtypeStruct((brow, 128), jnp.int32)],
        in_specs=[pl.BlockSpec(memory_space=pltpu.SMEM), full, full],
        out_specs=[full, pl.BlockSpec((brow, 128), lambda: (0, 0))],
    )(counts, eidx.reshape(rows, 128), rank.reshape(rows, 128))
    return dest.reshape(-1), blk.reshape(-1)[:_n_blocks(geo) + 1]


def _row_copy(src_hbm, s, dst_hbm, d, sem):
    return pltpu.make_async_copy(src_hbm.at[pl.ds(s, 1)], dst_hbm.at[pl.ds(d, 1)], sem)


def _zero_fill(cnt_ref, xs_hbm, zero_ref, zsem, n_rows, wait):
    def piece(pos, size):
        if size >= 8:
            copies = [(pl.multiple_of(pos, 8), size)]
        else:
            copies = [(pos + r, 1) for r in range(size)]
        for p, s in copies:
            cp = pltpu.make_async_copy(zero_ref.at[pl.ds(0, s)], xs_hbm.at[pl.ds(p, s)], zsem)
            cp.wait() if wait else cp.start()

    def per_expert(e, start):
        cnt = cnt_ref[e]
        padded = (cnt + (MOE_BLOCK - 1)) // MOE_BLOCK * MOE_BLOCK
        pad = padded - cnt
        pos = start + cnt
        size = 1
        while size < MOE_BLOCK:
            take = (pad & size) != 0
            pl.when(take)(functools.partial(piece, pos, size))
            pos = pos + jnp.where(take, size, 0)
            size *= 2
        return start + padded

    end = lax.fori_loop(0, N_EXPERTS, per_expert, jnp.int32(0))

    def per_block(j, carry):
        piece(end + j * MOE_BLOCK, MOE_BLOCK)
        return carry

    lax.fori_loop(0, (n_rows - end) // MOE_BLOCK, per_block, 0)


def _dispatch_kernel(cnt_ref, dest_ref, h_ref, xs_hbm, zero_ref, sem, zsem, *, tt, n_rows):
    i = pl.program_id(0)

    @pl.when(i == 0)
    def _():
        zero_ref[...] = jnp.zeros_like(zero_ref)
        _zero_fill(cnt_ref, xs_hbm, zero_ref, zsem, n_rows, wait=False)
        _zero_fill(cnt_ref, xs_hbm, zero_ref, zsem, n_rows, wait=True)

    def issue(t, carry):
        for k in range(TOP_K):
            _row_copy(h_ref, t, xs_hbm, dest_ref[t * TOP_K + k], sem).start()
        return carry

    lax.fori_loop(0, tt, issue, 0)

    def drain(t, carry):
        for k in range(TOP_K):
            _row_copy(h_ref, 0, xs_hbm, 0, sem).wait()
        return carry

    lax.fori_loop(0, tt, drain, 0)


def _dispatch(counts, dest, h_f32, geo):
    tt = 512 if geo.T % 512 == 0 else 128
    n_rows = _n_blocks(geo) * MOE_BLOCK
    return pl.pallas_call(
        functools.partial(_dispatch_kernel, tt=tt, n_rows=n_rows),
        name="dispatch",
        out_shape=jax.ShapeDtypeStruct((n_rows, D_MODEL), F32),
        grid=(geo.T // tt,),
        in_specs=[pl.BlockSpec(memory_space=pltpu.SMEM),
                  pl.BlockSpec((tt * TOP_K,), lambda i: (i,), memory_space=pltpu.SMEM),
                  pl.BlockSpec((tt, D_MODEL), lambda i: (i, 0))],
        out_specs=pl.BlockSpec(memory_space=pl.ANY),
        scratch_shapes=[pltpu.VMEM((MOE_BLOCK, D_MODEL), F32), pltpu.SemaphoreType.DMA,
                        pltpu.SemaphoreType.DMA],
        compiler_params=_params("arbitrary"),
    )(counts, dest, h_f32)


def _swiglu_block(x, wg_ref, wu_ref, wd_ref):
    g = jnp.dot(x, wg_ref[...], preferred_element_type=F32)
    u = jnp.dot(x, wu_ref[...], preferred_element_type=F32)
    return jnp.dot((_silu(g) * u).astype(BF16), wd_ref[...], preferred_element_type=F32)


def _expert_kernel(blk_ref, x_ref, wg_ref, wu_ref, wd_ref, o_ref, wgb_ref, wub_ref, wdb_ref, *, nb):
    j = pl.program_id(0)
    n_used = blk_ref[nb]

    @pl.when((j == 0) | (blk_ref[j] != blk_ref[jnp.maximum(j - 1, 0)]))
    def _():
        wgb_ref[...] = wg_ref[...].astype(BF16)
        wub_ref[...] = wu_ref[...].astype(BF16)
        wdb_ref[...] = wd_ref[...].astype(BF16)

    @pl.when(j < n_used)
    def _():
        o_ref[...] = _swiglu_block(x_ref[...].astype(BF16), wgb_ref, wub_ref, wdb_ref)

    @pl.when(j >= n_used)
    def _():
        o_ref[...] = jnp.zeros_like(o_ref)


def _experts(blk_e, xs, wg, wu, wd, layer, geo):
    nb = _n_blocks(geo)
    rows = pl.BlockSpec((MOE_BLOCK, D_MODEL), lambda j, be: (j, 0))

    def wspec(r, c):
        return pl.BlockSpec((None, None, r, c), lambda j, be: (layer, be[j], 0, 0))

    return pl.pallas_call(
        functools.partial(_expert_kernel, nb=nb),
        name="experts",
        out_shape=jax.ShapeDtypeStruct((nb * MOE_BLOCK, D_MODEL), F32),
        grid_spec=pltpu.PrefetchScalarGridSpec(
            num_scalar_prefetch=1,
            grid=(nb,),
            in_specs=[rows, wspec(D_MODEL, EXPERT_FF), wspec(D_MODEL, EXPERT_FF),
                      wspec(EXPERT_FF, D_MODEL)],
            out_specs=rows,
            scratch_shapes=[pltpu.VMEM((D_MODEL, EXPERT_FF), BF16),
                            pltpu.VMEM((D_MODEL, EXPERT_FF), BF16),
                            pltpu.VMEM((EXPERT_FF, D_MODEL), BF16)]),
        compiler_params=_params("arbitrary"),
    )(blk_e, xs, wg, wu, wd)


def _shared_kernel(x_ref, wg_ref, wu_ref, wd_ref, o_ref):
    o_ref[...] = _swiglu_block(x_ref[...], wg_ref, wu_ref, wd_ref)


def _shared_expert(h_bf, wg, wu, wd, geo):
    tm = _pick_tile(geo, (512, 256, 128))
    rows = pl.BlockSpec((tm, D_MODEL), lambda i: (i, 0))
    ff = wg.shape[1]
    return pl.pallas_call(
        _shared_kernel,
        name="shared_expert",
        out_shape=jax.ShapeDtypeStruct((geo.T, D_MODEL), F32),
        grid=(geo.T // tm,),
        in_specs=[rows,
                  pl.BlockSpec((D_MODEL, ff), lambda i: (0, 0)),
                  pl.BlockSpec((D_MODEL, ff), lambda i: (0, 0)),
                  pl.BlockSpec((ff, D_MODEL), lambda i: (0, 0))],
        out_specs=rows,
        compiler_params=_params("arbitrary"),
    )(h_bf, wg, wu, wd)


def _combine_kernel(dest_ref, w_ref, x_ref, sh_ref, m_ref, ys_hbm, o_ref, g_ref, sem, *, tt):
    def issue(t, carry):
        for k in range(TOP_K):
            pltpu.make_async_copy(ys_hbm.at[pl.ds(dest_ref[t * TOP_K + k], 1)],
                                  g_ref.at[k, pl.ds(t, 1)], sem).start()
        return carry

    lax.fori_loop(0, tt, issue, 0)

    def drain(t, carry):
        for k in range(TOP_K):
            pltpu.make_async_copy(ys_hbm.at[pl.ds(0, 1)], g_ref.at[k, pl.ds(0, 1)], sem).wait()
        return carry

    lax.fori_loop(0, tt, drain, 0)

    w = w_ref[...]
    acc = sh_ref[...]
    for k in range(TOP_K):
        acc = acc + w[:, k:k + 1] * g_ref[k]
    o_ref[...] = x_ref[...] + m_ref[0] * acc


def _combine(dest, w, x, sh, ys, mod3, m_gate, geo):
    tt = 128
    rows = pl.BlockSpec((tt, D_MODEL), lambda i: (i, 0))
    return pl.pallas_call(
        functools.partial(_combine_kernel, tt=tt),
        name="combine",
        out_shape=jax.ShapeDtypeStruct((geo.T, D_MODEL), F32),
        grid=(geo.T // tt,),
        in_specs=[pl.BlockSpec((tt * TOP_K,), lambda i: (i,), memory_space=pltpu.SMEM),
                  pl.BlockSpec((tt, TOP_K), lambda i: (i, 0)),
                  rows, rows, _mod_spec(m_gate, tt, geo),
                  pl.BlockSpec(memory_space=pl.ANY)],
        out_specs=rows,
        scratch_shapes=[pltpu.VMEM((TOP_K, tt, D_MODEL), F32), pltpu.SemaphoreType.DMA],
        compiler_params=_params("arbitrary"),
    )(dest, w, x, sh, mod3, ys)


def _moe(x, h_bf, h_f32, mod3, rw, rb, wg, wu, wd, layer, sg, su, sd, geo):
    eidx, rank, w, counts = _router(h_f32, rw, rb, geo)
    counts = counts.reshape(N_EXPERTS).astype(jnp.int32)
    dest, blk_e = _dest(counts, eidx, rank, geo)
    xs = _dispatch(counts, dest, h_f32, geo)
    ys = _experts(blk_e, xs, wg, wu, wd, layer, geo)
    sh = _shared_expert(h_bf, sg.astype(BF16), su.astype(BF16), sd.astype(BF16), geo)
    return _combine(dest, w, x, sh, ys, mod3, 5, geo)


def kernel(x, c, ctx, c_ctx, ada_w, ada_b, norm_mix, norm_ffn, norm_final, ev_w_in, ev_w_out, hgrn_lb, hgrn_norm, conv_w, conv_b, conv_norm_g, conv_norm_b, ret_w_in, ret_w_out, ret_decay, router_w, router_b, exp_gate, exp_up, exp_down, sh_gate, sh_up, sh_down):
    b, n, d = x.shape
    lc = ctx.shape[1]
    depth = ada_w.shape[0]
    geo = _geo(b, n, lc)
    assert d == D_MODEL and b < MOD_ROWS
    assert n % RET_CHUNK == 0 and lc % RET_CHUNK == 0

    xs = jnp.concatenate([x.reshape(geo.BN, d), ctx.reshape(geo.BL, d)], axis=0)
    cond = jnp.zeros((MOD_ROWS, d), F32).at[:b].set(c).at[b].set(c_ctx)
    cos_tab, sin_tab = _rope_tables(n)

    for l in range(depth):
        j = l // 2
        mod3 = _adaln(cond, ada_w, ada_b, l).reshape(MOD_ROWS * N_MOD, 1, d)
        (h,) = _normmod(xs, norm_mix[l], mod3, 0, 1, geo, want_f32=False)
        if l % 2 == 0:
            p = _matmul(h, ev_w_in[j].astype(BF16), geo)
            of = _hgrn_scan(p, hgrn_lb, l, geo, reverse=False)
            ob = _hgrn_scan(p, hgrn_lb, l, geo, reverse=True)
            mix = _ev_readout(p, of, ob, hgrn_norm[j], conv_w[j], conv_b[j],
                              conv_norm_g[j], conv_norm_b[j], geo)
            xs = _matmul_resid(mix, ev_w_out[j].astype(BF16), xs, mod3, 2, geo)
        else:
            p = _matmul(h, ret_w_in[j].astype(BF16), geo)
            of = _ret_scan(p, cos_tab, sin_tab, ret_decay[j, 0], geo, reverse=False)
            ob = _ret_scan(p, cos_tab, sin_tab, ret_decay[j, 1], geo, reverse=True)
            mix = _ret_readout(p, of, ob, geo)
            xs = _matmul_resid(mix, ret_w_out[j].astype(BF16), xs, mod3, 2, geo)
        h_bf, h_f32 = _normmod(xs, norm_ffn[l], mod3, 3, 4, geo, want_f32=True)
        xs = _moe(xs, h_bf, h_f32, mod3, router_w[l], router_b[l], exp_gate, exp_up, exp_down, l,
                  sh_gate[l], sh_up[l], sh_down[l], geo)
    return _final_norm(xs, norm_final, geo).reshape(b, n, d)
```

```python
import collections
import functools

import jax
import jax.numpy as jnp
from jax import lax
from jax.experimental import pallas as pl
from jax.experimental.pallas import tpu as pltpu

F32 = jnp.float32
BF16 = jnp.bfloat16

D_MODEL = 2048
N_MOD = 6
EPS = 1e-6
GRID_W = 64
ROPE_BASE = 10000.0

HG_HEADS = 8
HG_DK = 128
HG_DV = 128
HG_F = HG_HEADS * HG_DK
HG_V = HG_HEADS * HG_DV
CONV_C = D_MODEL // 2
CONV_W = 31
CONV_HALO = 16
HG_CHUNK = 64

RET_HEADS = 8
RET_DK = D_MODEL // RET_HEADS
RET_DV = 2 * RET_DK
RET_QK = RET_HEADS * RET_DK
RET_V = RET_HEADS * RET_DV
RET_CHUNK = 128

N_EXPERTS = 64
EXPERT_FF = D_MODEL // 4
TOP_K = 8
N_GROUPS = 8
GROUP_SIZE = N_EXPERTS // N_GROUPS
TOPK_GROUPS = 4
ROUTED_SCALE = 2.5
MOE_BLOCK = 512

MOD_ROWS = 16
VMEM_LIMIT = 56 * 1024 * 1024

Geo = collections.namedtuple("Geo", "B N Lc BN BL T")


def _geo(b, n, lc):
    return Geo(b, n, lc, b * n, b * lc, b * n + b * lc)


def _pick_tile(geo, cands):
    for t in cands:
        if geo.N % t == 0 and geo.BL % t == 0:
            return t
    raise ValueError("no row tile fits the sequence lengths")


def _mod_row(i, tm, geo):
    return jnp.where(i < geo.BN // tm, i // (geo.N // tm), geo.B)


def _mod_spec(m, tm, geo, ngrid=1):
    if ngrid == 1:
        return pl.BlockSpec((1, 1, D_MODEL), lambda i: (_mod_row(i, tm, geo) * N_MOD + m, 0, 0))
    return pl.BlockSpec((1, 1, D_MODEL), lambda i, j: (_mod_row(i, tm, geo) * N_MOD + m, 0, j))


def _params(*sem):
    return pltpu.CompilerParams(dimension_semantics=sem, vmem_limit_bytes=VMEM_LIMIT)


def _sigmoid(x):
    return jax.nn.sigmoid(x)


def _silu(x):
    return x * jax.nn.sigmoid(x)


def _adaln_kernel(c_ref, w_ref, b_ref, o_ref):
    a = _silu(c_ref[...]).astype(BF16)
    o_ref[...] = jnp.dot(a, w_ref[...].astype(BF16), preferred_element_type=F32) + b_ref[...]


def _adaln(cond, w, b, layer):
    depth, k, n = w.shape
    tn = 1024
    return pl.pallas_call(
        _adaln_kernel,
        name="adaln",
        out_shape=jax.ShapeDtypeStruct((MOD_ROWS, n), F32),
        grid=(n // tn,),
        in_specs=[pl.BlockSpec((MOD_ROWS, k), lambda j: (0, 0)),
                  pl.BlockSpec((None, k, tn), lambda j: (layer, 0, j)),
                  pl.BlockSpec((None, 1, tn), lambda j: (layer, 0, j))],
        out_specs=pl.BlockSpec((MOD_ROWS, tn), lambda j: (0, j)),
        compiler_params=_params("arbitrary"),
    )(cond, w, b.reshape(depth, 1, n))


def _pack_rows(x):
    n = x.shape[-1] // 2
    bits = lax.bitcast_convert_type(x.astype(BF16).astype(F32), jnp.uint32)
    return (bits[:, n:] & jnp.uint32(0xFFFF0000)) | (bits[:, :n] >> 16)


def _unpack_rows(u):
    lo = lax.bitcast_convert_type(u << 16, F32)
    hi = lax.bitcast_convert_type(u & jnp.uint32(0xFFFF0000), F32)
    return lo, hi


def _normmod_kernel(x_ref, g_ref, sh_ref, sc_ref, o_ref, *, packed):
    x = x_ref[...]
    y = x * lax.rsqrt(jnp.mean(x * x, axis=-1, keepdims=True) + EPS) * g_ref[...]
    h = y * (1.0 + sc_ref[0]) + sh_ref[0]
    o_ref[...] = _pack_rows(h) if packed else h.astype(BF16)


def _normmod(x, g, mod3, m_shift, m_scale, geo, packed):
    tm = _pick_tile(geo, (256, 128))
    spec = pl.BlockSpec((tm, D_MODEL), lambda i: (i, 0))
    if packed:
        out_shape = jax.ShapeDtypeStruct((geo.T, D_MODEL // 2), jnp.uint32)
        out_spec = pl.BlockSpec((tm, D_MODEL // 2), lambda i: (i, 0))
    else:
        out_shape = jax.ShapeDtypeStruct((geo.T, D_MODEL), BF16)
        out_spec = spec
    return pl.pallas_call(
        functools.partial(_normmod_kernel, packed=packed),
        name="normmod",
        out_shape=out_shape,
        grid=(geo.T // tm,),
        in_specs=[spec, pl.BlockSpec((1, D_MODEL), lambda i: (0, 0)),
                  _mod_spec(m_shift, tm, geo), _mod_spec(m_scale, tm, geo)],
        out_specs=out_spec,
        compiler_params=_params("arbitrary"),
    )(x, g.reshape(1, D_MODEL), mod3, mod3)


def _final_norm_kernel(x_ref, g_ref, o_ref):
    x = x_ref[...]
    o_ref[...] = x * lax.rsqrt(jnp.mean(x * x, axis=-1, keepdims=True) + EPS) * g_ref[...]


def _final_norm(x, g, geo):
    tm = _pick_tile(geo, (256, 128))
    spec = pl.BlockSpec((tm, D_MODEL), lambda i: (i, 0))
    return pl.pallas_call(
        _final_norm_kernel,
        name="final_norm",
        out_shape=jax.ShapeDtypeStruct((geo.BN, D_MODEL), F32),
        grid=(geo.BN // tm,),
        in_specs=[spec, pl.BlockSpec((1, D_MODEL), lambda i: (0, 0))],
        out_specs=spec,
        compiler_params=_params("arbitrary"),
    )(x, g.reshape(1, D_MODEL))


def _mm_kernel(a_ref, w_ref, o_ref):
    o_ref[...] = jnp.dot(a_ref[...], w_ref[...], preferred_element_type=F32).astype(o_ref.dtype)


def _mm_resid_kernel(a_ref, w_ref, x_ref, m_ref, o_ref):
    y = jnp.dot(a_ref[...], w_ref[...], preferred_element_type=F32)
    o_ref[...] = x_ref[...] + m_ref[0] * y


def _matmul(a, w, geo):
    k, n = w.shape
    tm = _pick_tile(geo, (1024, 512, 256, 128))
    tn = 512
    return pl.pallas_call(
        _mm_kernel,
        name="matmul",
        out_shape=jax.ShapeDtypeStruct((geo.T, n), BF16),
        grid=(geo.T // tm, n // tn),
        in_specs=[pl.BlockSpec((tm, k), lambda i, j: (i, 0)),
                  pl.BlockSpec((k, tn), lambda i, j: (0, j))],
        out_specs=pl.BlockSpec((tm, tn), lambda i, j: (i, j)),
        compiler_params=_params("arbitrary", "arbitrary"),
    )(a, w)


def _matmul_resid(a, w, x, mod3, m_gate, geo):
    k, n = w.shape
    tm = _pick_tile(geo, (1024, 512, 256, 128))
    tn = 512
    return pl.pallas_call(
        _mm_resid_kernel,
        name="matmul_resid",
        out_shape=jax.ShapeDtypeStruct((geo.T, n), F32),
        grid=(geo.T // tm, n // tn),
        in_specs=[pl.BlockSpec((tm, k), lambda i, j: (i, 0)),
                  pl.BlockSpec((k, tn), lambda i, j: (0, j)),
                  pl.BlockSpec((tm, tn), lambda i, j: (i, j)),
                  pl.BlockSpec((1, 1, tn), lambda i, j: (_mod_row(i, tm, geo) * N_MOD + m_gate, 0, j))],
        out_specs=pl.BlockSpec((tm, tn), lambda i, j: (i, j)),
        compiler_params=_params("arbitrary", "arbitrary"),
    )(a, w, x, mod3)


def _chunk_index(b, i, chunk, geo, reverse):
    nc = geo.Lc // chunk
    nl = geo.N // chunk
    ctx0 = (geo.BN + b * geo.Lc) // chunk
    lat0 = (b * geo.N) // chunk
    if reverse:
        return jnp.where(i < nc, ctx0 + (nc - 1 - i), lat0 + (nl - 1 - (i - nc)))
    return jnp.where(i < nc, ctx0 + i, lat0 + (i - nc))


def _split_dot(tri_bf, x):
    hi = x.astype(BF16)
    r1 = x - hi.astype(F32)
    mid = r1.astype(BF16)
    lo = (r1 - mid.astype(F32)).astype(BF16)
    return (jnp.dot(tri_bf, hi, preferred_element_type=F32)
            + jnp.dot(tri_bf, mid, preferred_element_type=F32)
            + jnp.dot(tri_bf, lo, preferred_element_type=F32))


def _hgrn_kernel(qf_ref, ff_ref, vf_ref, qb_ref, fb_ref, vb_ref, lbp_ref, of_ref, ob_ref,
                 stf_ref, stb_ref, *, layer):
    i = pl.program_id(1)

    @pl.when(i == 0)
    def _():
        stf_ref[...] = jnp.zeros_like(stf_ref)
        stb_ref[...] = jnp.zeros_like(stb_ref)

    lbp = lbp_ref[...]
    e = jnp.exp(lbp - jnp.max(lbp, axis=0, keepdims=True))
    sm = e / jnp.sum(e, axis=0, keepdims=True)
    lb = sm[0:1]
    for r in range(1, layer + 1):
        lb = lb + sm[r:r + 1]

    _hgrn_chunk(qf_ref, ff_ref, vf_ref, of_ref, stf_ref, lb, reverse=False)
    _hgrn_chunk(qb_ref, fb_ref, vb_ref, ob_ref, stb_ref, lb, reverse=True)


def _hgrn_chunk(q_ref, f_ref, v_ref, o_ref, st_ref, lb, *, reverse):
    c = HG_CHUNK
    row = lax.broadcasted_iota(jnp.int32, (c, c), 0)
    col = lax.broadcasted_iota(jnp.int32, (c, c), 1)
    tri = (col >= row) if reverse else (col <= row)
    tri_bf = jnp.where(tri, 1.0, 0.0).astype(BF16)
    nt = (((1,), (1,)), ((), ()))

    for h in range(HG_HEADS):
        sl = slice(h * HG_DK, (h + 1) * HG_DK)
        qh = _silu(q_ref[:, sl].astype(F32))
        lbh = lb[:, sl]
        fg = lbh + (1.0 - lbh) * _sigmoid(f_ref[:, sl].astype(F32))
        kh = 1.0 - fg
        bcum = _split_dot(tri_bf, jnp.log(fg))
        bmid = bcum[c // 2:c // 2 + 1]
        bend = bcum[0:1] if reverse else bcum[c - 1:c]
        vb = v_ref[:, sl]
        vh = vb.astype(F32)
        a = (qh * jnp.exp(bcum - bmid)).astype(BF16)
        kd = (kh * jnp.exp(bmid - bcum)).astype(BF16)
        s = lax.dot_general(a, kd, nt, preferred_element_type=F32)
        s = jnp.where(tri, s, 0.0)
        intra = jnp.dot(s.astype(BF16), vb, preferred_element_type=F32)
        st = st_ref[h]
        inter = lax.dot_general((qh * jnp.exp(bcum)).astype(BF16), st.astype(BF16), nt,
                                preferred_element_type=F32)
        o_ref[:, sl] = (inter + intra).astype(BF16)
        kd2 = (kh * jnp.exp(bend - bcum)).astype(BF16)
        st_ref[h] = st * jnp.exp(bend) + jnp.dot(vh.T.astype(BF16), kd2, preferred_element_type=F32)


def _hgrn_scan(p, lb_param, layer, geo):
    c = HG_CHUNK
    steps = (geo.Lc + geo.N) // c

    def spec(colblk, reverse):
        return pl.BlockSpec((c, HG_F), lambda b, i: (_chunk_index(b, i, c, geo, reverse), colblk))

    out = jax.ShapeDtypeStruct((geo.T, HG_V), BF16)
    state = pltpu.VMEM((HG_HEADS, HG_DV, HG_DK), F32)
    return pl.pallas_call(
        functools.partial(_hgrn_kernel, layer=layer),
        name="hgrn_scan",
        out_shape=[out, out],
        grid=(geo.B, steps),
        in_specs=[spec(0, False), spec(1, False), spec(3, False),
                  spec(0, True), spec(2, True), spec(3, True),
                  pl.BlockSpec(lb_param.shape, lambda b, i: (0, 0))],
        out_specs=[spec(0, False), spec(0, True)],
        scratch_shapes=[state, state],
        compiler_params=_params("arbitrary", "arbitrary"),
    )(p, p, p, p, p, p, lb_param)


def _ev_readout_kernel(of_ref, ob_ref, gate_ref, a_ref, b_ref, ap_ref, bp_ref, an_ref, bn_ref,
                       gain_ref, cw_ref, cb_ref, lng_ref, lnb_ref, o_ref, ext_ref, conv_ref, *, tm, geo):
    i = pl.program_id(0)
    n_lat = geo.BN // tm
    tpl = geo.N // tm
    tpc = geo.Lc // tm
    j = jnp.where(i < n_lat, i % tpl, (i - n_lat) % tpc)
    per = jnp.where(i < n_lat, tpl, tpc)
    keep_prev = jnp.where(j == 0, 0.0, 1.0)
    keep_next = jnp.where(j == per - 1, 0.0, 1.0)

    o = of_ref[...].astype(F32) + ob_ref[...].astype(F32)
    r = o * lax.rsqrt(jnp.mean(o * o, axis=-1, keepdims=True) + EPS) * gain_ref[...]
    o_ref[:, :HG_V] = (r * _silu(gate_ref[...].astype(F32))).astype(BF16)

    def glu(x_ref, y_ref):
        return x_ref[...].astype(F32) * _sigmoid(y_ref[...].astype(F32))

    ext_ref[0:CONV_HALO] = glu(ap_ref, bp_ref) * keep_prev
    ext_ref[CONV_HALO:CONV_HALO + tm] = glu(a_ref, b_ref)
    ext_ref[CONV_HALO + tm:2 * CONV_HALO + tm] = glu(an_ref, bn_ref) * keep_next
    off = CONV_HALO - CONV_W // 2
    for cj in range(CONV_C // 128):
        cs = slice(cj * 128, (cj + 1) * 128)
        acc = jnp.broadcast_to(cb_ref[:, cs], (tm, 128))
        for k in range(CONV_W):
            acc = acc + cw_ref[k:k + 1, cs] * ext_ref[off + k:off + k + tm, cs]
        conv_ref[:, cs] = acc
    acc = conv_ref[...]
    mu = jnp.mean(acc, axis=-1, keepdims=True)
    xc = acc - mu
    var = jnp.mean(xc * xc, axis=-1, keepdims=True)
    u = xc * lax.rsqrt(var + EPS) * lng_ref[...] + lnb_ref[...]
    o_ref[:, HG_V:] = _silu(u).astype(BF16)


def _ev_readout(p, of, ob, gain, cw, cb, lng, lnb, geo):
    tm = _pick_tile(geo, (128,))
    hb = tm // CONV_HALO
    nhalo = geo.T // CONV_HALO

    def row(colblk):
        return pl.BlockSpec((tm, HG_V), lambda i: (i, colblk))

    def prev(colblk):
        return pl.BlockSpec((CONV_HALO, CONV_C), lambda i: (jnp.maximum(i * hb - 1, 0), colblk))

    def nxt(colblk):
        return pl.BlockSpec((CONV_HALO, CONV_C), lambda i: (jnp.minimum((i + 1) * hb, nhalo - 1), colblk))

    def vec(n):
        return pl.BlockSpec((n, CONV_C), lambda i: (0, 0))

    return pl.pallas_call(
        functools.partial(_ev_readout_kernel, tm=tm, geo=geo),
        name="ev_readout",
        out_shape=jax.ShapeDtypeStruct((geo.T, HG_V + CONV_C), BF16),
        grid=(geo.T // tm,),
        in_specs=[row(0), row(0), row(4), row(5), row(6), prev(5), prev(6), nxt(5), nxt(6),
                  vec(1), vec(CONV_W), vec(1), vec(1), vec(1)],
        out_specs=pl.BlockSpec((tm, HG_V + CONV_C), lambda i: (i, 0)),
        scratch_shapes=[pltpu.VMEM((tm + 2 * CONV_HALO, CONV_C), F32), pltpu.VMEM((tm, CONV_C), F32)],
        compiler_params=_params("arbitrary"),
    )(of, ob, p, p, p, p, p, p, p, gain.reshape(1, -1), cw, cb.reshape(1, -1),
      lng.reshape(1, -1), lnb.reshape(1, -1))


def _rope(x, cos, sin_signed):
    half = x.shape[-1] // 2
    rot = jnp.concatenate([pltpu.roll(x[:, :half], half // 2, axis=1),
                           pltpu.roll(x[:, half:], half // 2, axis=1)], axis=-1)
    return x * cos + rot * sin_signed


def _ret_kernel(qf_ref, kf_ref, vf_ref, cosf_ref, sinf_ref, qb_ref, kb_ref, vb_ref, cosb_ref, sinb_ref,
                dl_ref, of_ref, ob_ref, sf_ref, sb_ref):
    i = pl.program_id(1)

    @pl.when(i == 0)
    def _():
        sf_ref[...] = jnp.zeros_like(sf_ref)
        sb_ref[...] = jnp.zeros_like(sb_ref)

    _ret_chunk(qf_ref, kf_ref, vf_ref, cosf_ref, sinf_ref, dl_ref[0], of_ref, sf_ref, reverse=False)
    _ret_chunk(qb_ref, kb_ref, vb_ref, cosb_ref, sinb_ref, dl_ref[1], ob_ref, sb_ref, reverse=True)


def _ret_chunk(q_ref, k_ref, v_ref, cos_ref, sin_ref, dl, o_ref, s_ref, *, reverse):
    c = RET_CHUNK
    lg_all = -jnp.log1p(jnp.exp(-dl))
    row = lax.broadcasted_iota(jnp.int32, (c, c), 0)
    col = lax.broadcasted_iota(jnp.int32, (c, c), 1)
    idx = lax.broadcasted_iota(jnp.int32, (c, 1), 0).astype(F32)
    if reverse:
        live = col >= row
        dist = (col - row).astype(F32)
        q_pow = c - idx
        k_pow = idx
    else:
        live = row >= col
        dist = (row - col).astype(F32)
        q_pow = idx + 1.0
        k_pow = c - 1.0 - idx
    cos = cos_ref[...]
    sin = sin_ref[...]
    nt = (((1,), (1,)), ((), ()))

    for h in range(RET_HEADS):
        lg = lg_all[h:h + 1]
        dmask = jnp.where(live, jnp.exp(lg * dist), 0.0)
        q = _rope(q_ref[:, h * RET_DK:(h + 1) * RET_DK].astype(F32), cos, sin)
        k = _rope(k_ref[:, h * RET_DK:(h + 1) * RET_DK].astype(F32) * (RET_DK ** -0.5), cos, sin)
        vb = v_ref[:, h * RET_DV:(h + 1) * RET_DV]
        scores = lax.dot_general(q.astype(BF16), k.astype(BF16), nt, preferred_element_type=F32) * dmask
        intra = jnp.dot(scores.astype(BF16), vb, preferred_element_type=F32)
        s = s_ref[h]
        inter = jnp.dot((q * jnp.exp(lg * q_pow)).astype(BF16), s.astype(BF16),
                        preferred_element_type=F32)
        o_ref[:, h * RET_DV:(h + 1) * RET_DV] = (inter + intra).astype(BF16)
        kdec = (k * jnp.exp(lg * k_pow)).T.astype(BF16)
        s_ref[h] = jnp.exp(lg * c) * s + jnp.dot(kdec, vb, preferred_element_type=F32)


def _ret_scan(p, cos_tab, sin_tab, decay_logit, geo):
    c = RET_CHUNK
    steps = (geo.Lc + geo.N) // c
    nc = geo.Lc // c
    nl = geo.N // c

    def spec(width, colblk, reverse):
        return pl.BlockSpec((c, width), lambda b, i: (_chunk_index(b, i, c, geo, reverse), colblk))

    def tab(reverse):
        def index(b, i):
            lat = (nl - 1 - (i - nc)) if reverse else (i - nc)
            return (jnp.where(i < nc, nl, lat), 0)
        return pl.BlockSpec((c, RET_DK), index)

    def direction(reverse):
        return [spec(RET_QK, 0, reverse), spec(RET_QK, 1, reverse), spec(RET_V, 1, reverse),
                tab(reverse), tab(reverse)]

    out = jax.ShapeDtypeStruct((geo.T, RET_V), BF16)
    state = pltpu.VMEM((RET_HEADS, RET_DK, RET_DV), F32)
    return pl.pallas_call(
        _ret_kernel,
        name="ret_scan",
        out_shape=[out, out],
        grid=(geo.B, steps),
        in_specs=direction(False) + direction(True)
        + [pl.BlockSpec((2, RET_HEADS, 1), lambda b, i: (0, 0, 0))],
        out_specs=[spec(RET_V, 0, False), spec(RET_V, 0, True)],
        scratch_shapes=[state, state],
        compiler_params=_params("arbitrary", "arbitrary"),
    )(p, p, p, cos_tab, sin_tab, p, p, p, cos_tab, sin_tab, decay_logit.reshape(2, RET_HEADS, 1))


def _rope_tables(n):
    t = jnp.arange(n)
    quarter = RET_DK // 4
    inv = 1.0 / (ROPE_BASE ** (jnp.arange(quarter, dtype=F32) / quarter))
    ang_r = (t // GRID_W).astype(F32)[:, None] * inv
    ang_c = (t % GRID_W).astype(F32)[:, None] * inv
    cos = jnp.concatenate([jnp.cos(ang_r), jnp.cos(ang_r), jnp.cos(ang_c), jnp.cos(ang_c)], axis=-1)
    sin = jnp.concatenate([-jnp.sin(ang_r), jnp.sin(ang_r), -jnp.sin(ang_c), jnp.sin(ang_c)], axis=-1)
    cos = jnp.concatenate([cos, jnp.ones((RET_CHUNK, RET_DK), F32)], axis=0)
    sin = jnp.concatenate([sin, jnp.zeros((RET_CHUNK, RET_DK), F32)], axis=0)
    return cos, sin


def _ret_readout_kernel(of_ref, ob_ref, gate_ref, o_ref):
    for h in range(RET_HEADS):
        sl = slice(h * RET_DV, (h + 1) * RET_DV)
        o = of_ref[:, sl].astype(F32) + ob_ref[:, sl].astype(F32)
        r = o * lax.rsqrt(jnp.mean(o * o, axis=-1, keepdims=True) + EPS)
        o_ref[:, sl] = (_silu(gate_ref[:, sl].astype(F32)) * r).astype(BF16)


def _ret_readout(p, of, ob, geo):
    tm = _pick_tile(geo, (128,))
    spec = pl.BlockSpec((tm, RET_V), lambda i: (i, 0))
    return pl.pallas_call(
        _ret_readout_kernel,
        name="ret_readout",
        out_shape=jax.ShapeDtypeStruct((geo.T, RET_V), BF16),
        grid=(geo.T // tm,),
        in_specs=[spec, spec, pl.BlockSpec((tm, RET_V), lambda i: (i, 2))],
        out_specs=spec,
        compiler_params=_params("arbitrary"),
    )(of, ob, p)


def _router_kernel(h_ref, rw_ref, rb_ref, eidx_ref, rank_ref, w_ref, cnt_ref, carry_ref, *, tm):
    i = pl.program_id(0)

    @pl.when(i == 0)
    def _():
        carry_ref[...] = jnp.zeros_like(carry_ref)

    half = D_MODEL // 2
    h_lo, h_hi = _unpack_rows(h_ref[...])
    h_lo = h_lo.astype(BF16)
    h_hi = h_hi.astype(BF16)
    rest = rw_ref[...]
    logits = jnp.zeros((tm, N_EXPERTS), F32)
    for _ in range(3):
        part = rest.astype(BF16)
        rest = rest - part.astype(F32)
        logits = (logits + jnp.dot(h_lo, part[:half], preferred_element_type=F32)
                  + jnp.dot(h_hi, part[half:], preferred_element_type=F32))
    s = _sigmoid(logits)
    sel = s + rb_ref[...]
    lane = lax.broadcasted_iota(jnp.int32, (tm, N_EXPERTS), 1).astype(F32)
    grp = jnp.floor(lane * (1.0 / GROUP_SIZE))
    ninf = -jnp.inf
    none = float(N_EXPERTS)

    gscore = jnp.zeros((tm, N_EXPERTS), F32)
    gcols = []
    for g in range(N_GROUPS):
        in_g = grp == float(g)
        v1 = jnp.max(jnp.where(in_g, sel, ninf), axis=-1, keepdims=True)
        i1 = jnp.min(jnp.where(in_g & (sel == v1), lane, none), axis=-1, keepdims=True)
        v2 = jnp.max(jnp.where(in_g & (lane != i1), sel, ninf), axis=-1, keepdims=True)
        gcols.append(v1 + v2)
        gscore = jnp.where(in_g, v1 + v2, gscore)
    beaten = jnp.zeros((tm, N_EXPERTS), F32)
    for g in range(N_GROUPS):
        wins = (gcols[g] > gscore) | ((gcols[g] == gscore) & (float(g) < grp))
        beaten = beaten + jnp.where(wins, 1.0, 0.0)
    cand = jnp.where(beaten < float(TOPK_GROUPS), sel, ninf)

    lane_k = lax.broadcasted_iota(jnp.int32, (tm, TOP_K), 1)
    eidx = jnp.zeros((tm, TOP_K), F32)
    wsel = jnp.zeros((tm, TOP_K), F32)
    chosen = jnp.zeros((tm, N_EXPERTS), F32)
    picks = []
    for k in range(TOP_K):
        v = jnp.max(cand, axis=-1, keepdims=True)
        ik = jnp.min(jnp.where(cand == v, lane, none), axis=-1, keepdims=True)
        hit = lane == ik
        picks.append(ik)
        eidx = jnp.where(lane_k == k, ik, eidx)
        wsel = jnp.where(lane_k == k, jnp.sum(jnp.where(hit, s, 0.0), axis=-1, keepdims=True), wsel)
        chosen = jnp.where(hit, 1.0, chosen)
        cand = jnp.where(hit, ninf, cand)
    w_ref[...] = wsel / jnp.sum(wsel, axis=-1, keepdims=True) * ROUTED_SCALE
    eidx_ref[...] = eidx.astype(jnp.int32)

    r = lax.broadcasted_iota(jnp.int32, (tm, tm), 0)
    c = lax.broadcasted_iota(jnp.int32, (tm, tm), 1)
    below = jnp.where(c < r, 1.0, 0.0).astype(BF16)
    carry = carry_ref[...]
    pos = jnp.dot(below, chosen.astype(BF16), preferred_element_type=F32) + carry
    rank = jnp.zeros((tm, TOP_K), jnp.int32)
    for k in range(TOP_K):
        rk = jnp.sum(jnp.where(lane == picks[k], pos, 0.0), axis=-1, keepdims=True)
        rank = jnp.where(lane_k == k, rk.astype(jnp.int32), rank)
    rank_ref[...] = rank
    carry = carry + jnp.sum(chosen, axis=0, keepdims=True)
    carry_ref[...] = carry
    cnt_ref[...] = carry


def _router(h_packed, rw, rb, geo):
    tm = 256 if geo.T % 256 == 0 else 128
    tok = pl.BlockSpec((tm, TOP_K), lambda i: (i, 0))
    one = pl.BlockSpec((1, N_EXPERTS), lambda i: (0, 0))
    return pl.pallas_call(
        functools.partial(_router_kernel, tm=tm),
        name="router",
        out_shape=[jax.ShapeDtypeStruct((geo.T, TOP_K), jnp.int32),
                   jax.ShapeDtypeStruct((geo.T, TOP_K), jnp.int32),
                   jax.ShapeDtypeStruct((geo.T, TOP_K), F32),
                   jax.ShapeDtypeStruct((1, N_EXPERTS), F32)],
        grid=(geo.T // tm,),
        in_specs=[pl.BlockSpec((tm, D_MODEL // 2), lambda i: (i, 0)),
                  pl.BlockSpec((D_MODEL, N_EXPERTS), lambda i: (0, 0)), one],
        out_specs=[tok, tok, tok, one],
        scratch_shapes=[pltpu.VMEM((1, N_EXPERTS), F32)],
        compiler_params=_params("arbitrary"),
    )(h_packed, rw, rb.reshape(1, N_EXPERTS))


def _n_blocks(geo):
    return -(-(geo.T * TOP_K) // MOE_BLOCK) + N_EXPERTS


def _dest_kernel(cnt_ref, eidx_ref, rank_ref, dest_ref, blk_ref, *, n_blocks):
    eidx = eidx_ref[...]
    dest = rank_ref[...]
    blk_row = (lax.broadcasted_iota(jnp.int32, blk_ref.shape, 0) * 128
               + lax.broadcasted_iota(jnp.int32, blk_ref.shape, 1)) * MOE_BLOCK
    blk = jnp.zeros(blk_ref.shape, jnp.int32)
    start = jnp.int32(0)
    for e in range(N_EXPERTS):
        padded = (cnt_ref[e] + (MOE_BLOCK - 1)) // MOE_BLOCK * MOE_BLOCK
        dest = dest + jnp.where(eidx == e, start, 0)
        start = start + padded
        blk = blk + jnp.where(start <= blk_row, 1, 0)
    dest_ref[...] = dest
    blk_ref[...] = jnp.where(blk_row == n_blocks * MOE_BLOCK, start // MOE_BLOCK,
                             jnp.minimum(blk, N_EXPERTS - 1))


def _dest(counts, eidx, rank, geo):
    rows = geo.T * TOP_K // 128
    brow = -(-(_n_blocks(geo) + 1) // 128)
    full = pl.BlockSpec((rows, 128), lambda: (0, 0))
    dest, blk = pl.pallas_call(
        functools.partial(_dest_kernel, n_blocks=_n_blocks(geo)),
        name="dest",
        out_shape=[jax.ShapeDtypeStruct((rows, 128), jnp.int32),
                   jax.ShapeDtypeStruct((brow, 128), jnp.int32)],
        in_specs=[pl.BlockSpec(memory_space=pltpu.SMEM), full, full],
        out_specs=[full, pl.BlockSpec((brow, 128), lambda: (0, 0))],
    )(counts, eidx.reshape(rows, 128), rank.reshape(rows, 128))
    return dest.reshape(-1), blk.reshape(-1)[:_n_blocks(geo) + 1]


def _row_copy(src_hbm, s, dst_hbm, d, sem):
    return pltpu.make_async_copy(src_hbm.at[pl.ds(s, 1)], dst_hbm.at[pl.ds(d, 1)], sem)


def _zero_fill(cnt_ref, xs_hbm, zero_ref, zsem, n_rows, wait):
    def piece(pos, size):
        if size >= 8:
            copies = [(pl.multiple_of(pos, 8), size)]
        else:
            copies = [(pos + r, 1) for r in range(size)]
        for p, s in copies:
            cp = pltpu.make_async_copy(zero_ref.at[pl.ds(0, s)], xs_hbm.at[pl.ds(p, s)], zsem)
            cp.wait() if wait else cp.start()

    def per_expert(e, start):
        cnt = cnt_ref[e]
        padded = (cnt + (MOE_BLOCK - 1)) // MOE_BLOCK * MOE_BLOCK
        pad = padded - cnt
        pos = start + cnt
        size = 1
        while size < MOE_BLOCK:
            take = (pad & size) != 0
            pl.when(take)(functools.partial(piece, pos, size))
            pos = pos + jnp.where(take, size, 0)
            size *= 2
        return start + padded

    end = lax.fori_loop(0, N_EXPERTS, per_expert, jnp.int32(0))

    def per_block(j, carry):
        piece(end + j * MOE_BLOCK, MOE_BLOCK)
        return carry

    lax.fori_loop(0, (n_rows - end) // MOE_BLOCK, per_block, 0)


def _dispatch_kernel(cnt_ref, dest_ref, h_ref, xs_hbm, zero_ref, sem, zsem, *, tt, n_rows):
    i = pl.program_id(0)

    @pl.when(i == 0)
    def _():
        zero_ref[...] = jnp.zeros_like(zero_ref)
        _zero_fill(cnt_ref, xs_hbm, zero_ref, zsem, n_rows, wait=False)
        _zero_fill(cnt_ref, xs_hbm, zero_ref, zsem, n_rows, wait=True)

    def issue(t, carry):
        for k in range(TOP_K):
            _row_copy(h_ref, t, xs_hbm, dest_ref[t * TOP_K + k], sem).start()
        return carry

    lax.fori_loop(0, tt, issue, 0)

    def drain(t, carry):
        for k in range(TOP_K):
            _row_copy(h_ref, 0, xs_hbm, 0, sem).wait()
        return carry

    lax.fori_loop(0, tt, drain, 0)


def _dispatch(counts, dest, h_packed, geo):
    tt = 512 if geo.T % 512 == 0 else 128
    n_rows = _n_blocks(geo) * MOE_BLOCK
    width = D_MODEL // 2
    return pl.pallas_call(
        functools.partial(_dispatch_kernel, tt=tt, n_rows=n_rows),
        name="dispatch",
        out_shape=jax.ShapeDtypeStruct((n_rows, width), jnp.uint32),
        grid=(geo.T // tt,),
        in_specs=[pl.BlockSpec(memory_space=pltpu.SMEM),
                  pl.BlockSpec((tt * TOP_K,), lambda i: (i,), memory_space=pltpu.SMEM),
                  pl.BlockSpec((tt, width), lambda i: (i, 0))],
        out_specs=pl.BlockSpec(memory_space=pl.ANY),
        scratch_shapes=[pltpu.VMEM((MOE_BLOCK, width), jnp.uint32), pltpu.SemaphoreType.DMA,
                        pltpu.SemaphoreType.DMA],
        compiler_params=_params("arbitrary"),
    )(counts, dest, h_packed)


def _swiglu_packed(x_ref, wg_ref, wu_ref, wd_ref):
    half = D_MODEL // 2
    lo, hi = _unpack_rows(x_ref[...])
    lo = lo.astype(BF16)
    hi = hi.astype(BF16)

    def proj(w_ref):
        return (jnp.dot(lo, w_ref[:half], preferred_element_type=F32)
                + jnp.dot(hi, w_ref[half:], preferred_element_type=F32))

    g = proj(wg_ref)
    u = proj(wu_ref)
    return jnp.dot((_silu(g) * u).astype(BF16), wd_ref[...], preferred_element_type=F32)


def _expert_kernel(blk_ref, x_ref, wg_ref, wu_ref, wd_ref, o_ref, wgb_ref, wub_ref, wdb_ref, *, nb):
    j = pl.program_id(0)
    n_used = blk_ref[nb]

    @pl.when((j == 0) | (blk_ref[j] != blk_ref[jnp.maximum(j - 1, 0)]))
    def _():
        wgb_ref[...] = wg_ref[...].astype(BF16)
        wub_ref[...] = wu_ref[...].astype(BF16)
        wdb_ref[...] = wd_ref[...].astype(BF16)

    @pl.when(j < n_used)
    def _():
        o_ref[...] = _pack_rows(_swiglu_packed(x_ref, wgb_ref, wub_ref, wdb_ref))

    @pl.when(j >= n_used)
    def _():
        o_ref[...] = jnp.zeros_like(o_ref)


def _experts(blk_e, xs, wg, wu, wd, layer, geo):
    nb = _n_blocks(geo)
    rows = pl.BlockSpec((MOE_BLOCK, D_MODEL // 2), lambda j, be: (j, 0))

    def wspec(r, c):
        return pl.BlockSpec((None, None, r, c), lambda j, be: (layer, be[j], 0, 0))

    return pl.pallas_call(
        functools.partial(_expert_kernel, nb=nb),
        name="experts",
        out_shape=jax.ShapeDtypeStruct((nb * MOE_BLOCK, D_MODEL // 2), jnp.uint32),
        grid_spec=pltpu.PrefetchScalarGridSpec(
            num_scalar_prefetch=1,
            grid=(nb,),
            in_specs=[rows, wspec(D_MODEL, EXPERT_FF), wspec(D_MODEL, EXPERT_FF),
                      wspec(EXPERT_FF, D_MODEL)],
            out_specs=rows,
            scratch_shapes=[pltpu.VMEM((D_MODEL, EXPERT_FF), BF16),
                            pltpu.VMEM((D_MODEL, EXPERT_FF), BF16),
                            pltpu.VMEM((EXPERT_FF, D_MODEL), BF16)]),
        compiler_params=_params("arbitrary"),
    )(blk_e, xs, wg, wu, wd)


def _shared_kernel(x_ref, wg_ref, wu_ref, wd_ref, o_ref):
    o_ref[...] = _swiglu_packed(x_ref, wg_ref, wu_ref, wd_ref)


def _shared_expert(h_packed, wg, wu, wd, geo):
    tm = _pick_tile(geo, (512, 256, 128))
    ff = wg.shape[1]
    return pl.pallas_call(
        _shared_kernel,
        name="shared_expert",
        out_shape=jax.ShapeDtypeStruct((geo.T, D_MODEL), F32),
        grid=(geo.T // tm,),
        in_specs=[pl.BlockSpec((tm, D_MODEL // 2), lambda i: (i, 0)),
                  pl.BlockSpec((D_MODEL, ff), lambda i: (0, 0)),
                  pl.BlockSpec((D_MODEL, ff), lambda i: (0, 0)),
                  pl.BlockSpec((ff, D_MODEL), lambda i: (0, 0))],
        out_specs=pl.BlockSpec((tm, D_MODEL), lambda i: (i, 0)),
        compiler_params=_params("arbitrary"),
    )(h_packed, wg, wu, wd)


def _combine_kernel(dest_ref, dnext_ref, w_ref, x_ref, sh_ref, m_ref, ys_hbm, o_ref, g_ref, sems, *, tt):
    i = pl.program_id(0)
    n = pl.num_programs(0)
    slot = i % 2

    def gather(idx_ref, s):
        def issue(t, carry):
            for k in range(TOP_K):
                pltpu.make_async_copy(ys_hbm.at[pl.ds(idx_ref[t * TOP_K + k], 1)],
                                      g_ref.at[s, k, pl.ds(t, 1)], sems.at[s]).start()
            return carry
        lax.fori_loop(0, tt, issue, 0)

    @pl.when(i == 0)
    def _():
        gather(dest_ref, 0)

    @pl.when(i + 1 < n)
    def _():
        gather(dnext_ref, 1 - slot)

    def drain(t, carry):
        for k in range(TOP_K):
            pltpu.make_async_copy(ys_hbm.at[pl.ds(0, 1)], g_ref.at[slot, k, pl.ds(0, 1)],
                                  sems.at[slot]).wait()
        return carry

    lax.fori_loop(0, tt, drain, 0)

    half = D_MODEL // 2
    w = w_ref[...]
    acc_lo = sh_ref[:, :half]
    acc_hi = sh_ref[:, half:]
    for k in range(TOP_K):
        lo, hi = _unpack_rows(g_ref[slot, k])
        acc_lo = acc_lo + w[:, k:k + 1] * lo
        acc_hi = acc_hi + w[:, k:k + 1] * hi
    gate = m_ref[0]
    o_ref[:, :half] = x_ref[:, :half] + gate[:, :half] * acc_lo
    o_ref[:, half:] = x_ref[:, half:] + gate[:, half:] * acc_hi


def _combine(dest, w, x, sh, ys, mod3, m_gate, geo):
    tt = _pick_tile(geo, (256, 128))
    n = geo.T // tt
    rows = pl.BlockSpec((tt, D_MODEL), lambda i: (i, 0))
    return pl.pallas_call(
        functools.partial(_combine_kernel, tt=tt),
        name="combine",
        out_shape=jax.ShapeDtypeStruct((geo.T, D_MODEL), F32),
        grid=(n,),
        in_specs=[pl.BlockSpec((tt * TOP_K,), lambda i: (i,), memory_space=pltpu.SMEM),
                  pl.BlockSpec((tt * TOP_K,), lambda i: (jnp.minimum(i + 1, n - 1),),
                               memory_space=pltpu.SMEM),
                  pl.BlockSpec((tt, TOP_K), lambda i: (i, 0)),
                  rows, rows, _mod_spec(m_gate, tt, geo),
                  pl.BlockSpec(memory_space=pl.ANY)],
        out_specs=rows,
        scratch_shapes=[pltpu.VMEM((2, TOP_K, tt, D_MODEL // 2), jnp.uint32),
                        pltpu.SemaphoreType.DMA((2,))],
        compiler_params=_params("arbitrary"),
    )(dest, dest, w, x, sh, mod3, ys)


def _moe(x, h_packed, mod3, rw, rb, wg, wu, wd, layer, sg, su, sd, geo):
    eidx, rank, w, counts = _router(h_packed, rw, rb, geo)
    counts = counts.reshape(N_EXPERTS).astype(jnp.int32)
    dest, blk_e = _dest(counts, eidx, rank, geo)
    xs = _dispatch(counts, dest, h_packed, geo)
    ys = _experts(blk_e, xs, wg, wu, wd, layer, geo)
    sh = _shared_expert(h_packed, sg.astype(BF16), su.astype(BF16), sd.astype(BF16), geo)
    return _combine(dest, w, x, sh, ys, mod3, 5, geo)


def kernel(x, c, ctx, c_ctx, ada_w, ada_b, norm_mix, norm_ffn, norm_final, ev_w_in, ev_w_out, hgrn_lb, hgrn_norm, conv_w, conv_b, conv_norm_g, conv_norm_b, ret_w_in, ret_w_out, ret_decay, router_w, router_b, exp_gate, exp_up, exp_down, sh_gate, sh_up, sh_down):
    b, n, d = x.shape
    lc = ctx.shape[1]
    depth = ada_w.shape[0]
    geo = _geo(b, n, lc)
    assert d == D_MODEL and b < MOD_ROWS
    assert n % RET_CHUNK == 0 and lc % RET_CHUNK == 0

    xs = jnp.concatenate([x.reshape(geo.BN, d), ctx.reshape(geo.BL, d)], axis=0)
    cond = jnp.zeros((MOD_ROWS, d), F32).at[:b].set(c).at[b].set(c_ctx)
    cos_tab, sin_tab = _rope_tables(n)

    for l in range(depth):
        j = l // 2
        mod3 = _adaln(cond, ada_w, ada_b, l).reshape(MOD_ROWS * N_MOD, 1, d)
        h = _normmod(xs, norm_mix[l], mod3, 0, 1, geo, packed=False)
        if l % 2 == 0:
            p = _matmul(h, ev_w_in[j].astype(BF16), geo)
            of, ob = _hgrn_scan(p, hgrn_lb, l, geo)
            mix = _ev_readout(p, of, ob, hgrn_norm[j], conv_w[j], conv_b[j],
                              conv_norm_g[j], conv_norm_b[j], geo)
            xs = _matmul_resid(mix, ev_w_out[j].astype(BF16), xs, mod3, 2, geo)
        else:
            p = _matmul(h, ret_w_in[j].astype(BF16), geo)
            of, ob = _ret_scan(p, cos_tab, sin_tab, ret_decay[j], geo)
            mix = _ret_readout(p, of, ob, geo)
            xs = _matmul_resid(mix, ret_w_out[j].astype(BF16), xs, mod3, 2, geo)
        h_packed = _normmod(xs, norm_ffn[l], mod3, 3, 4, geo, packed=True)
        xs = _moe(xs, h_packed, mod3, router_w[l], router_b[l], exp_gate, exp_up, exp_down, l,
                  sh_gate[l], sh_up[l], sh_down[l], geo)
    return _final_norm(xs, norm_final, geo).reshape(b, n, d)
```

```python
import collections
import functools

import jax
import jax.numpy as jnp
from jax import lax
from jax.experimental import pallas as pl
from jax.experimental.pallas import tpu as pltpu

F32 = jnp.float32
BF16 = jnp.bfloat16

D_MODEL = 2048
N_MOD = 6
EPS = 1e-6
GRID_W = 64
ROPE_BASE = 10000.0

HG_HEADS = 8
HG_DK = 128
HG_DV = 128
HG_F = HG_HEADS * HG_DK
HG_V = HG_HEADS * HG_DV
CONV_C = D_MODEL // 2
CONV_W = 31
CONV_HALO = 16
HG_CHUNK = 128

RET_HEADS = 8
RET_DK = D_MODEL // RET_HEADS
RET_DV = 2 * RET_DK
RET_QK = RET_HEADS * RET_DK
RET_V = RET_HEADS * RET_DV
RET_CHUNK = 256

N_EXPERTS = 64
EXPERT_FF = D_MODEL // 4
TOP_K = 8
N_GROUPS = 8
GROUP_SIZE = N_EXPERTS // N_GROUPS
TOPK_GROUPS = 4
ROUTED_SCALE = 2.5
MOE_BLOCK = 512

MOD_ROWS = 16
VMEM_LIMIT = 56 * 1024 * 1024

Geo = collections.namedtuple("Geo", "B N Lc BN BL T")


def _geo(b, n, lc):
    return Geo(b, n, lc, b * n, b * lc, b * n + b * lc)


def _pick_tile(geo, cands):
    for t in cands:
        if geo.N % t == 0 and geo.BL % t == 0:
            return t
    raise ValueError("no row tile fits the sequence lengths")


def _mod_row(i, tm, geo):
    return jnp.where(i < geo.BN // tm, i // (geo.N // tm), geo.B)


def _mod_spec(m, tm, geo, ngrid=1):
    if ngrid == 1:
        return pl.BlockSpec((1, 1, D_MODEL), lambda i: (_mod_row(i, tm, geo) * N_MOD + m, 0, 0))
    return pl.BlockSpec((1, 1, D_MODEL), lambda i, j: (_mod_row(i, tm, geo) * N_MOD + m, 0, j))


def _params(*sem):
    return pltpu.CompilerParams(dimension_semantics=sem, vmem_limit_bytes=VMEM_LIMIT)


def _sigmoid(x):
    return jax.nn.sigmoid(x)


def _silu(x):
    return x * jax.nn.sigmoid(x)


def _adaln_kernel(c_ref, w_ref, b_ref, o_ref):
    a = _silu(c_ref[...]).astype(BF16)
    o_ref[...] = jnp.dot(a, w_ref[...].astype(BF16), preferred_element_type=F32) + b_ref[...]


def _adaln(cond, w, b, layer):
    depth, k, n = w.shape
    tn = 1024
    return pl.pallas_call(
        _adaln_kernel,
        name="adaln",
        out_shape=jax.ShapeDtypeStruct((MOD_ROWS, n), F32),
        grid=(n // tn,),
        in_specs=[pl.BlockSpec((MOD_ROWS, k), lambda j: (0, 0)),
                  pl.BlockSpec((None, k, tn), lambda j: (layer, 0, j)),
                  pl.BlockSpec((None, 1, tn), lambda j: (layer, 0, j))],
        out_specs=pl.BlockSpec((MOD_ROWS, tn), lambda j: (0, j)),
        compiler_params=_params("arbitrary"),
    )(cond, w, b.reshape(depth, 1, n))


def _pack_rows(x):
    n = x.shape[-1] // 2
    bits = lax.bitcast_convert_type(x.astype(BF16).astype(F32), jnp.uint32)
    return (bits[:, n:] & jnp.uint32(0xFFFF0000)) | (bits[:, :n] >> 16)


def _unpack_rows(u):
    lo = lax.bitcast_convert_type(u << 16, F32)
    hi = lax.bitcast_convert_type(u & jnp.uint32(0xFFFF0000), F32)
    return lo, hi


def _normmod_kernel(x_ref, g_ref, sh_ref, sc_ref, o_ref, *, packed):
    x = x_ref[...]
    y = x * lax.rsqrt(jnp.mean(x * x, axis=-1, keepdims=True) + EPS) * g_ref[...]
    h = y * (1.0 + sc_ref[0]) + sh_ref[0]
    o_ref[...] = _pack_rows(h) if packed else h.astype(BF16)


def _normmod(x, g, mod3, m_shift, m_scale, geo, packed):
    tm = _pick_tile(geo, (256, 128))
    spec = pl.BlockSpec((tm, D_MODEL), lambda i: (i, 0))
    if packed:
        out_shape = jax.ShapeDtypeStruct((geo.T, D_MODEL // 2), jnp.uint32)
        out_spec = pl.BlockSpec((tm, D_MODEL // 2), lambda i: (i, 0))
    else:
        out_shape = jax.ShapeDtypeStruct((geo.T, D_MODEL), BF16)
        out_spec = spec
    return pl.pallas_call(
        functools.partial(_normmod_kernel, packed=packed),
        name="normmod",
        out_shape=out_shape,
        grid=(geo.T // tm,),
        in_specs=[spec, pl.BlockSpec((1, D_MODEL), lambda i: (0, 0)),
                  _mod_spec(m_shift, tm, geo), _mod_spec(m_scale, tm, geo)],
        out_specs=out_spec,
        compiler_params=_params("arbitrary"),
    )(x, g.reshape(1, D_MODEL), mod3, mod3)


def _final_norm_kernel(x_ref, g_ref, o_ref):
    x = x_ref[...]
    o_ref[...] = x * lax.rsqrt(jnp.mean(x * x, axis=-1, keepdims=True) + EPS) * g_ref[...]


def _final_norm(x, g, geo):
    tm = _pick_tile(geo, (256, 128))
    spec = pl.BlockSpec((tm, D_MODEL), lambda i: (i, 0))
    return pl.pallas_call(
        _final_norm_kernel,
        name="final_norm",
        out_shape=jax.ShapeDtypeStruct((geo.BN, D_MODEL), F32),
        grid=(geo.BN // tm,),
        in_specs=[spec, pl.BlockSpec((1, D_MODEL), lambda i: (0, 0))],
        out_specs=spec,
        compiler_params=_params("arbitrary"),
    )(x, g.reshape(1, D_MODEL))


def _mm_kernel(a_ref, w_ref, o_ref):
    o_ref[...] = jnp.dot(a_ref[...], w_ref[...], preferred_element_type=F32).astype(o_ref.dtype)


def _mm_resid_kernel(a_ref, w_ref, x_ref, m_ref, o_ref):
    y = jnp.dot(a_ref[...], w_ref[...], preferred_element_type=F32)
    o_ref[...] = x_ref[...] + m_ref[0] * y


def _matmul(a, w, geo):
    k, n = w.shape
    tm = _pick_tile(geo, (1024, 512, 256, 128))
    tn = 512
    return pl.pallas_call(
        _mm_kernel,
        name="matmul",
        out_shape=jax.ShapeDtypeStruct((geo.T, n), BF16),
        grid=(geo.T // tm, n // tn),
        in_specs=[pl.BlockSpec((tm, k), lambda i, j: (i, 0)),
                  pl.BlockSpec((k, tn), lambda i, j: (0, j))],
        out_specs=pl.BlockSpec((tm, tn), lambda i, j: (i, j)),
        compiler_params=_params("arbitrary", "arbitrary"),
    )(a, w)


def _matmul_resid(a, w, x, mod3, m_gate, geo):
    k, n = w.shape
    tm = _pick_tile(geo, (1024, 512, 256, 128))
    tn = 512
    return pl.pallas_call(
        _mm_resid_kernel,
        name="matmul_resid",
        out_shape=jax.ShapeDtypeStruct((geo.T, n), F32),
        grid=(geo.T // tm, n // tn),
        in_specs=[pl.BlockSpec((tm, k), lambda i, j: (i, 0)),
                  pl.BlockSpec((k, tn), lambda i, j: (0, j)),
                  pl.BlockSpec((tm, tn), lambda i, j: (i, j)),
                  pl.BlockSpec((1, 1, tn), lambda i, j: (_mod_row(i, tm, geo) * N_MOD + m_gate, 0, j))],
        out_specs=pl.BlockSpec((tm, tn), lambda i, j: (i, j)),
        compiler_params=_params("arbitrary", "arbitrary"),
    )(a, w, x, mod3)


def _chunk_index(b, i, chunk, geo, reverse):
    nc = geo.Lc // chunk
    nl = geo.N // chunk
    ctx0 = (geo.BN + b * geo.Lc) // chunk
    lat0 = (b * geo.N) // chunk
    if reverse:
        return jnp.where(i < nc, ctx0 + (nc - 1 - i), lat0 + (nl - 1 - (i - nc)))
    return jnp.where(i < nc, ctx0 + i, lat0 + (i - nc))


def _split_dot(tri_bf, x):
    hi = x.astype(BF16)
    r1 = x - hi.astype(F32)
    mid = r1.astype(BF16)
    lo = (r1 - mid.astype(F32)).astype(BF16)
    return (jnp.dot(tri_bf, hi, preferred_element_type=F32)
            + jnp.dot(tri_bf, mid, preferred_element_type=F32)
            + jnp.dot(tri_bf, lo, preferred_element_type=F32))


def _hgrn_kernel(qf_ref, ff_ref, vf_ref, qb_ref, fb_ref, vb_ref, lbp_ref, of_ref, ob_ref,
                 stf_ref, stb_ref, *, layer):
    i = pl.program_id(1)

    @pl.when(i == 0)
    def _():
        stf_ref[...] = jnp.zeros_like(stf_ref)
        stb_ref[...] = jnp.zeros_like(stb_ref)

    lbp = lbp_ref[...]
    e = jnp.exp(lbp - jnp.max(lbp, axis=0, keepdims=True))
    sm = e / jnp.sum(e, axis=0, keepdims=True)
    lb = sm[0:1]
    for r in range(1, layer + 1):
        lb = lb + sm[r:r + 1]

    _hgrn_chunk(qf_ref, ff_ref, vf_ref, of_ref, stf_ref, lb, reverse=False)
    _hgrn_chunk(qb_ref, fb_ref, vb_ref, ob_ref, stb_ref, lb, reverse=True)


def _hgrn_chunk(q_ref, f_ref, v_ref, o_ref, st_ref, lb, *, reverse):
    c = HG_CHUNK
    row = lax.broadcasted_iota(jnp.int32, (c, c), 0)
    col = lax.broadcasted_iota(jnp.int32, (c, c), 1)
    tri = (col >= row) if reverse else (col <= row)
    tri_bf = jnp.where(tri, 1.0, 0.0).astype(BF16)
    nt = (((1,), (1,)), ((), ()))

    for h in range(HG_HEADS):
        sl = slice(h * HG_DK, (h + 1) * HG_DK)
        qh = _silu(q_ref[:, sl].astype(F32))
        lbh = lb[:, sl]
        fg = lbh + (1.0 - lbh) * _sigmoid(f_ref[:, sl].astype(F32))
        kh = 1.0 - fg
        bcum = _split_dot(tri_bf, jnp.log(fg))
        bmid = bcum[c // 2:c // 2 + 1]
        bend = bcum[0:1] if reverse else bcum[c - 1:c]
        vb = v_ref[:, sl]
        vh = vb.astype(F32)
        a = (qh * jnp.exp(bcum - bmid)).astype(BF16)
        kd = (kh * jnp.exp(bmid - bcum)).astype(BF16)
        s = lax.dot_general(a, kd, nt, preferred_element_type=F32)
        s = jnp.where(tri, s, 0.0)
        intra = jnp.dot(s.astype(BF16), vb, preferred_element_type=F32)
        st = st_ref[h]
        inter = lax.dot_general((qh * jnp.exp(bcum)).astype(BF16), st.astype(BF16), nt,
                                preferred_element_type=F32)
        o_ref[:, sl] = (inter + intra).astype(BF16)
        kd2 = (kh * jnp.exp(bend - bcum)).astype(BF16)
        st_ref[h] = st * jnp.exp(bend) + jnp.dot(vh.T.astype(BF16), kd2, preferred_element_type=F32)


def _hgrn_scan(p, lb_param, layer, geo):
    c = HG_CHUNK
    steps = (geo.Lc + geo.N) // c

    def spec(colblk, reverse):
        return pl.BlockSpec((c, HG_F), lambda b, i: (_chunk_index(b, i, c, geo, reverse), colblk))

    out = jax.ShapeDtypeStruct((geo.T, HG_V), BF16)
    state = pltpu.VMEM((HG_HEADS, HG_DV, HG_DK), F32)
    return pl.pallas_call(
        functools.partial(_hgrn_kernel, layer=layer),
        name="hgrn_scan",
        out_shape=[out, out],
        grid=(geo.B, steps),
        in_specs=[spec(0, False), spec(1, False), spec(3, False),
                  spec(0, True), spec(2, True), spec(3, True),
                  pl.BlockSpec(lb_param.shape, lambda b, i: (0, 0))],
        out_specs=[spec(0, False), spec(0, True)],
        scratch_shapes=[state, state],
        compiler_params=_params("arbitrary", "arbitrary"),
    )(p, p, p, p, p, p, lb_param)


def _ev_readout_kernel(of_ref, ob_ref, gate_ref, a_ref, b_ref, ap_ref, bp_ref, an_ref, bn_ref,
                       gain_ref, cw_ref, cb_ref, lng_ref, lnb_ref, o_ref, ext_ref, conv_ref, *, tm, geo):
    i = pl.program_id(0)
    n_lat = geo.BN // tm
    tpl = geo.N // tm
    tpc = geo.Lc // tm
    j = jnp.where(i < n_lat, i % tpl, (i - n_lat) % tpc)
    per = jnp.where(i < n_lat, tpl, tpc)
    keep_prev = jnp.where(j == 0, 0.0, 1.0)
    keep_next = jnp.where(j == per - 1, 0.0, 1.0)

    o = of_ref[...].astype(F32) + ob_ref[...].astype(F32)
    r = o * lax.rsqrt(jnp.mean(o * o, axis=-1, keepdims=True) + EPS) * gain_ref[...]
    o_ref[:, :HG_V] = (r * _silu(gate_ref[...].astype(F32))).astype(BF16)

    def glu(x_ref, y_ref):
        return x_ref[...].astype(F32) * _sigmoid(y_ref[...].astype(F32))

    ext_ref[0:CONV_HALO] = glu(ap_ref, bp_ref) * keep_prev
    ext_ref[CONV_HALO:CONV_HALO + tm] = glu(a_ref, b_ref)
    ext_ref[CONV_HALO + tm:2 * CONV_HALO + tm] = glu(an_ref, bn_ref) * keep_next
    off = CONV_HALO - CONV_W // 2
    for cj in range(CONV_C // 128):
        cs = slice(cj * 128, (cj + 1) * 128)
        acc = jnp.broadcast_to(cb_ref[:, cs], (tm, 128))
        for k in range(CONV_W):
            acc = acc + cw_ref[k:k + 1, cs] * ext_ref[off + k:off + k + tm, cs]
        conv_ref[:, cs] = acc
    acc = conv_ref[...]
    mu = jnp.mean(acc, axis=-1, keepdims=True)
    xc = acc - mu
    var = jnp.mean(xc * xc, axis=-1, keepdims=True)
    u = xc * lax.rsqrt(var + EPS) * lng_ref[...] + lnb_ref[...]
    o_ref[:, HG_V:] = _silu(u).astype(BF16)


def _ev_readout(p, of, ob, gain, cw, cb, lng, lnb, geo):
    tm = _pick_tile(geo, (128,))
    hb = tm // CONV_HALO
    nhalo = geo.T // CONV_HALO

    def row(colblk):
        return pl.BlockSpec((tm, HG_V), lambda i: (i, colblk))

    def prev(colblk):
        return pl.BlockSpec((CONV_HALO, CONV_C), lambda i: (jnp.maximum(i * hb - 1, 0), colblk))

    def nxt(colblk):
        return pl.BlockSpec((CONV_HALO, CONV_C), lambda i: (jnp.minimum((i + 1) * hb, nhalo - 1), colblk))

    def vec(n):
        return pl.BlockSpec((n, CONV_C), lambda i: (0, 0))

    return pl.pallas_call(
        functools.partial(_ev_readout_kernel, tm=tm, geo=geo),
        name="ev_readout",
        out_shape=jax.ShapeDtypeStruct((geo.T, HG_V + CONV_C), BF16),
        grid=(geo.T // tm,),
        in_specs=[row(0), row(0), row(4), row(5), row(6), prev(5), prev(6), nxt(5), nxt(6),
                  vec(1), vec(CONV_W), vec(1), vec(1), vec(1)],
        out_specs=pl.BlockSpec((tm, HG_V + CONV_C), lambda i: (i, 0)),
        scratch_shapes=[pltpu.VMEM((tm + 2 * CONV_HALO, CONV_C), F32), pltpu.VMEM((tm, CONV_C), F32)],
        compiler_params=_params("arbitrary"),
    )(of, ob, p, p, p, p, p, p, p, gain.reshape(1, -1), cw, cb.reshape(1, -1),
      lng.reshape(1, -1), lnb.reshape(1, -1))


def _rope(x, cos, sin_signed):
    half = x.shape[-1] // 2
    rot = jnp.concatenate([pltpu.roll(x[:, :half], half // 2, axis=1),
                           pltpu.roll(x[:, half:], half // 2, axis=1)], axis=-1)
    return x * cos + rot * sin_signed


def _ret_kernel(qf_ref, kf_ref, vf_ref, cosf_ref, sinf_ref, qb_ref, kb_ref, vb_ref, cosb_ref, sinb_ref,
                dl_ref, of_ref, ob_ref, sf_ref, sb_ref):
    i = pl.program_id(1)

    @pl.when(i == 0)
    def _():
        sf_ref[...] = jnp.zeros_like(sf_ref)
        sb_ref[...] = jnp.zeros_like(sb_ref)

    _ret_chunk(qf_ref, kf_ref, vf_ref, cosf_ref, sinf_ref, dl_ref[0], of_ref, sf_ref, reverse=False)
    _ret_chunk(qb_ref, kb_ref, vb_ref, cosb_ref, sinb_ref, dl_ref[1], ob_ref, sb_ref, reverse=True)


def _ret_chunk(q_ref, k_ref, v_ref, cos_ref, sin_ref, dl, o_ref, s_ref, *, reverse):
    c = RET_CHUNK
    lg_all = -jnp.log1p(jnp.exp(-dl))
    row = lax.broadcasted_iota(jnp.int32, (c, c), 0)
    col = lax.broadcasted_iota(jnp.int32, (c, c), 1)
    idx = lax.broadcasted_iota(jnp.int32, (c, 1), 0).astype(F32)
    if reverse:
        live = col >= row
        dist = (col - row).astype(F32)
        q_pow = c - idx
        k_pow = idx
    else:
        live = row >= col
        dist = (row - col).astype(F32)
        q_pow = idx + 1.0
        k_pow = c - 1.0 - idx
    cos = cos_ref[...]
    sin = sin_ref[...]
    nt = (((1,), (1,)), ((), ()))

    for h in range(RET_HEADS):
        lg = lg_all[h:h + 1]
        dmask = jnp.where(live, jnp.exp(lg * dist), 0.0)
        q = _rope(q_ref[:, h * RET_DK:(h + 1) * RET_DK].astype(F32), cos, sin)
        k = _rope(k_ref[:, h * RET_DK:(h + 1) * RET_DK].astype(F32) * (RET_DK ** -0.5), cos, sin)
        vb = v_ref[:, h * RET_DV:(h + 1) * RET_DV]
        scores = lax.dot_general(q.astype(BF16), k.astype(BF16), nt, preferred_element_type=F32) * dmask
        intra = jnp.dot(scores.astype(BF16), vb, preferred_element_type=F32)
        s = s_ref[h]
        inter = jnp.dot((q * jnp.exp(lg * q_pow)).astype(BF16), s.astype(BF16),
                        preferred_element_type=F32)
        o_ref[:, h * RET_DV:(h + 1) * RET_DV] = (inter + intra).astype(BF16)
        kdec = (k * jnp.exp(lg * k_pow)).T.astype(BF16)
        s_ref[h] = jnp.exp(lg * c) * s + jnp.dot(kdec, vb, preferred_element_type=F32)


def _ret_scan(p, cos_tab, sin_tab, decay_logit, geo):
    c = RET_CHUNK
    steps = (geo.Lc + geo.N) // c
    nc = geo.Lc // c
    nl = geo.N // c

    def spec(width, colblk, reverse):
        return pl.BlockSpec((c, width), lambda b, i: (_chunk_index(b, i, c, geo, reverse), colblk))

    def tab(reverse):
        def index(b, i):
            lat = (nl - 1 - (i - nc)) if reverse else (i - nc)
            return (jnp.where(i < nc, nl, lat), 0)
        return pl.BlockSpec((c, RET_DK), index)

    def direction(reverse):
        return [spec(RET_QK, 0, reverse), spec(RET_QK, 1, reverse), spec(RET_V, 1, reverse),
                tab(reverse), tab(reverse)]

    out = jax.ShapeDtypeStruct((geo.T, RET_V), BF16)
    state = pltpu.VMEM((RET_HEADS, RET_DK, RET_DV), F32)
    return pl.pallas_call(
        _ret_kernel,
        name="ret_scan",
        out_shape=[out, out],
        grid=(geo.B, steps),
        in_specs=direction(False) + direction(True)
        + [pl.BlockSpec((2, RET_HEADS, 1), lambda b, i: (0, 0, 0))],
        out_specs=[spec(RET_V, 0, False), spec(RET_V, 0, True)],
        scratch_shapes=[state, state],
        compiler_params=_params("arbitrary", "arbitrary"),
    )(p, p, p, cos_tab, sin_tab, p, p, p, cos_tab, sin_tab, decay_logit.reshape(2, RET_HEADS, 1))


def _rope_tables(n):
    t = jnp.arange(n)
    quarter = RET_DK // 4
    inv = 1.0 / (ROPE_BASE ** (jnp.arange(quarter, dtype=F32) / quarter))
    ang_r = (t // GRID_W).astype(F32)[:, None] * inv
    ang_c = (t % GRID_W).astype(F32)[:, None] * inv
    cos = jnp.concatenate([jnp.cos(ang_r), jnp.cos(ang_r), jnp.cos(ang_c), jnp.cos(ang_c)], axis=-1)
    sin = jnp.concatenate([-jnp.sin(ang_r), jnp.sin(ang_r), -jnp.sin(ang_c), jnp.sin(ang_c)], axis=-1)
    cos = jnp.concatenate([cos, jnp.ones((RET_CHUNK, RET_DK), F32)], axis=0)
    sin = jnp.concatenate([sin, jnp.zeros((RET_CHUNK, RET_DK), F32)], axis=0)
    return cos, sin


def _ret_readout_kernel(of_ref, ob_ref, gate_ref, o_ref):
    for h in range(RET_HEADS):
        sl = slice(h * RET_DV, (h + 1) * RET_DV)
        o = of_ref[:, sl].astype(F32) + ob_ref[:, sl].astype(F32)
        r = o * lax.rsqrt(jnp.mean(o * o, axis=-1, keepdims=True) + EPS)
        o_ref[:, sl] = (_silu(gate_ref[:, sl].astype(F32)) * r).astype(BF16)


def _ret_readout(p, of, ob, geo):
    tm = _pick_tile(geo, (128,))
    spec = pl.BlockSpec((tm, RET_V), lambda i: (i, 0))
    return pl.pallas_call(
        _ret_readout_kernel,
        name="ret_readout",
        out_shape=jax.ShapeDtypeStruct((geo.T, RET_V), BF16),
        grid=(geo.T // tm,),
        in_specs=[spec, spec, pl.BlockSpec((tm, RET_V), lambda i: (i, 2))],
        out_specs=spec,
        compiler_params=_params("arbitrary"),
    )(of, ob, p)


def _router_kernel(h_ref, rw_ref, rb_ref, eidx_ref, rank_ref, w_ref, cnt_ref, carry_ref, *, tm):
    i = pl.program_id(0)

    @pl.when(i == 0)
    def _():
        carry_ref[...] = jnp.zeros_like(carry_ref)

    half = D_MODEL // 2
    h_lo, h_hi = _unpack_rows(h_ref[...])
    h_lo = h_lo.astype(BF16)
    h_hi = h_hi.astype(BF16)
    rest = rw_ref[...]
    logits = jnp.zeros((tm, N_EXPERTS), F32)
    for _ in range(3):
        part = rest.astype(BF16)
        rest = rest - part.astype(F32)
        logits = (logits + jnp.dot(h_lo, part[:half], preferred_element_type=F32)
                  + jnp.dot(h_hi, part[half:], preferred_element_type=F32))
    s = _sigmoid(logits)
    sel = s + rb_ref[...]
    lane = lax.broadcasted_iota(jnp.int32, (tm, N_EXPERTS), 1).astype(F32)
    grp = jnp.floor(lane * (1.0 / GROUP_SIZE))
    ninf = -jnp.inf
    none = float(N_EXPERTS)

    gscore = jnp.zeros((tm, N_EXPERTS), F32)
    gcols = []
    for g in range(N_GROUPS):
        in_g = grp == float(g)
        v1 = jnp.max(jnp.where(in_g, sel, ninf), axis=-1, keepdims=True)
        i1 = jnp.min(jnp.where(in_g & (sel == v1), lane, none), axis=-1, keepdims=True)
        v2 = jnp.max(jnp.where(in_g & (lane != i1), sel, ninf), axis=-1, keepdims=True)
        gcols.append(v1 + v2)
        gscore = jnp.where(in_g, v1 + v2, gscore)
    beaten = jnp.zeros((tm, N_EXPERTS), F32)
    for g in range(N_GROUPS):
        wins = (gcols[g] > gscore) | ((gcols[g] == gscore) & (float(g) < grp))
        beaten = beaten + jnp.where(wins, 1.0, 0.0)
    cand = jnp.where(beaten < float(TOPK_GROUPS), sel, ninf)

    lane_k = lax.broadcasted_iota(jnp.int32, (tm, TOP_K), 1)
    eidx = jnp.zeros((tm, TOP_K), F32)
    wsel = jnp.zeros((tm, TOP_K), F32)
    chosen = jnp.zeros((tm, N_EXPERTS), F32)
    picks = []
    for k in range(TOP_K):
        v = jnp.max(cand, axis=-1, keepdims=True)
        ik = jnp.min(jnp.where(cand == v, lane, none), axis=-1, keepdims=True)
        hit = lane == ik
        picks.append(ik)
        eidx = jnp.where(lane_k == k, ik, eidx)
        wsel = jnp.where(lane_k == k, jnp.sum(jnp.where(hit, s, 0.0), axis=-1, keepdims=True), wsel)
        chosen = jnp.where(hit, 1.0, chosen)
        cand = jnp.where(hit, ninf, cand)
    w_ref[...] = wsel / jnp.sum(wsel, axis=-1, keepdims=True) * ROUTED_SCALE
    eidx_ref[...] = eidx.astype(jnp.int32)

    r = lax.broadcasted_iota(jnp.int32, (tm, tm), 0)
    c = lax.broadcasted_iota(jnp.int32, (tm, tm), 1)
    below = jnp.where(c < r, 1.0, 0.0).astype(BF16)
    carry = carry_ref[...]
    pos = jnp.dot(below, chosen.astype(BF16), preferred_element_type=F32) + carry
    rank = jnp.zeros((tm, TOP_K), jnp.int32)
    for k in range(TOP_K):
        rk = jnp.sum(jnp.where(lane == picks[k], pos, 0.0), axis=-1, keepdims=True)
        rank = jnp.where(lane_k == k, rk.astype(jnp.int32), rank)
    rank_ref[...] = rank
    carry = carry + jnp.sum(chosen, axis=0, keepdims=True)
    carry_ref[...] = carry
    cnt_ref[...] = carry


def _router(h_packed, rw, rb, geo):
    tm = 256 if geo.T % 256 == 0 else 128
    tok = pl.BlockSpec((tm, TOP_K), lambda i: (i, 0))
    one = pl.BlockSpec((1, N_EXPERTS), lambda i: (0, 0))
    return pl.pallas_call(
        functools.partial(_router_kernel, tm=tm),
        name="router",
        out_shape=[jax.ShapeDtypeStruct((geo.T, TOP_K), jnp.int32),
                   jax.ShapeDtypeStruct((geo.T, TOP_K), jnp.int32),
                   jax.ShapeDtypeStruct((geo.T, TOP_K), F32),
                   jax.ShapeDtypeStruct((1, N_EXPERTS), F32)],
        grid=(geo.T // tm,),
        in_specs=[pl.BlockSpec((tm, D_MODEL // 2), lambda i: (i, 0)),
                  pl.BlockSpec((D_MODEL, N_EXPERTS), lambda i: (0, 0)), one],
        out_specs=[tok, tok, tok, one],
        scratch_shapes=[pltpu.VMEM((1, N_EXPERTS), F32)],
        compiler_params=_params("arbitrary"),
    )(h_packed, rw, rb.reshape(1, N_EXPERTS))


def _n_blocks(geo):
    return -(-(geo.T * TOP_K) // MOE_BLOCK) + N_EXPERTS


def _dest_kernel(cnt_ref, eidx_ref, rank_ref, dest_ref, blk_ref, *, n_blocks):
    eidx = eidx_ref[...]
    dest = rank_ref[...]
    blk_row = (lax.broadcasted_iota(jnp.int32, blk_ref.shape, 0) * 128
               + lax.broadcasted_iota(jnp.int32, blk_ref.shape, 1)) * MOE_BLOCK
    blk = jnp.zeros(blk_ref.shape, jnp.int32)
    start = jnp.int32(0)
    for e in range(N_EXPERTS):
        padded = (cnt_ref[e] + (MOE_BLOCK - 1)) // MOE_BLOCK * MOE_BLOCK
        dest = dest + jnp.where(eidx == e, start, 0)
        start = start + padded
        blk = blk + jnp.where(start <= blk_row, 1, 0)
    dest_ref[...] = dest
    blk_ref[...] = jnp.where(blk_row == n_blocks * MOE_BLOCK, start // MOE_BLOCK,
                             jnp.minimum(blk, N_EXPERTS - 1))


def _dest(counts, eidx, rank, geo):
    rows = geo.T * TOP_K // 128
    brow = -(-(_n_blocks(geo) + 1) // 128)
    full = pl.BlockSpec((rows, 128), lambda: (0, 0))
    dest, blk = pl.pallas_call(
        functools.partial(_dest_kernel, n_blocks=_n_blocks(geo)),
        name="dest",
        out_shape=[jax.ShapeDtypeStruct((rows, 128), jnp.int32),
                   jax.ShapeDtypeStruct((brow, 128), jnp.int32)],
        in_specs=[pl.BlockSpec(memory_space=pltpu.SMEM), full, full],
        out_specs=[full, pl.BlockSpec((brow, 128), lambda: (0, 0))],
    )(counts, eidx.reshape(rows, 128), rank.reshape(rows, 128))
    return dest.reshape(-1), blk.reshape(-1)[:_n_blocks(geo) + 1]


def _row_copy(src_hbm, s, dst_hbm, d, sem):
    return pltpu.make_async_copy(src_hbm.at[pl.ds(s, 1)], dst_hbm.at[pl.ds(d, 1)], sem)


def _zero_fill(cnt_ref, xs_hbm, zero_ref, zsem, n_rows, wait):
    def piece(pos, size):
        if size >= 8:
            copies = [(pl.multiple_of(pos, 8), size)]
        else:
            copies = [(pos + r, 1) for r in range(size)]
        for p, s in copies:
            cp = pltpu.make_async_copy(zero_ref.at[pl.ds(0, s)], xs_hbm.at[pl.ds(p, s)], zsem)
            cp.wait() if wait else cp.start()

    def per_expert(e, start):
        cnt = cnt_ref[e]
        padded = (cnt + (MOE_BLOCK - 1)) // MOE_BLOCK * MOE_BLOCK
        pad = padded - cnt
        pos = start + cnt
        size = 1
        while size < MOE_BLOCK:
            take = (pad & size) != 0
            pl.when(take)(functools.partial(piece, pos, size))
            pos = pos + jnp.where(take, size, 0)
            size *= 2
        return start + padded

    end = lax.fori_loop(0, N_EXPERTS, per_expert, jnp.int32(0))

    def per_block(j, carry):
        piece(end + j * MOE_BLOCK, MOE_BLOCK)
        return carry

    lax.fori_loop(0, (n_rows - end) // MOE_BLOCK, per_block, 0)


def _dispatch_kernel(cnt_ref, dest_ref, h_ref, xs_hbm, zero_ref, sem, zsem, *, tt, n_rows):
    i = pl.program_id(0)

    @pl.when(i == 0)
    def _():
        zero_ref[...] = jnp.zeros_like(zero_ref)
        _zero_fill(cnt_ref, xs_hbm, zero_ref, zsem, n_rows, wait=False)
        _zero_fill(cnt_ref, xs_hbm, zero_ref, zsem, n_rows, wait=True)

    def issue(t, carry):
        for k in range(TOP_K):
            _row_copy(h_ref, t, xs_hbm, dest_ref[t * TOP_K + k], sem).start(priority=k % 2)
        return carry

    lax.fori_loop(0, tt, issue, 0)

    def drain(t, carry):
        for k in range(TOP_K):
            _row_copy(h_ref, 0, xs_hbm, 0, sem).wait()
        return carry

    lax.fori_loop(0, tt, drain, 0)


def _dispatch(counts, dest, h_packed, geo):
    tt = 512 if geo.T % 512 == 0 else 128
    n_rows = _n_blocks(geo) * MOE_BLOCK
    width = D_MODEL // 2
    return pl.pallas_call(
        functools.partial(_dispatch_kernel, tt=tt, n_rows=n_rows),
        name="dispatch",
        out_shape=jax.ShapeDtypeStruct((n_rows, width), jnp.uint32),
        grid=(geo.T // tt,),
        in_specs=[pl.BlockSpec(memory_space=pltpu.SMEM),
                  pl.BlockSpec((tt * TOP_K,), lambda i: (i,), memory_space=pltpu.SMEM),
                  pl.BlockSpec((tt, width), lambda i: (i, 0))],
        out_specs=pl.BlockSpec(memory_space=pl.ANY),
        scratch_shapes=[pltpu.VMEM((MOE_BLOCK, width), jnp.uint32), pltpu.SemaphoreType.DMA,
                        pltpu.SemaphoreType.DMA],
        compiler_params=_params("arbitrary"),
    )(counts, dest, h_packed)


def _swiglu_packed(x_ref, wg_ref, wu_ref, wd_ref):
    half = D_MODEL // 2
    lo, hi = _unpack_rows(x_ref[...])
    lo = lo.astype(BF16)
    hi = hi.astype(BF16)

    def proj(w_ref):
        return (jnp.dot(lo, w_ref[:half], preferred_element_type=F32)
                + jnp.dot(hi, w_ref[half:], preferred_element_type=F32))

    g = proj(wg_ref)
    u = proj(wu_ref)
    return jnp.dot((_silu(g) * u).astype(BF16), wd_ref[...], preferred_element_type=F32)


def _expert_kernel(blk_ref, x_ref, wg_ref, wu_ref, wd_ref, o_ref, wgb_ref, wub_ref, wdb_ref, *, nb):
    j = pl.program_id(0)
    n_used = blk_ref[nb]

    @pl.when((j == 0) | (blk_ref[j] != blk_ref[jnp.maximum(j - 1, 0)]))
    def _():
        wgb_ref[...] = wg_ref[...].astype(BF16)
        wub_ref[...] = wu_ref[...].astype(BF16)
        wdb_ref[...] = wd_ref[...].astype(BF16)

    @pl.when(j < n_used)
    def _():
        o_ref[...] = _pack_rows(_swiglu_packed(x_ref, wgb_ref, wub_ref, wdb_ref))

    @pl.when(j >= n_used)
    def _():
        o_ref[...] = jnp.zeros_like(o_ref)


def _experts(blk_e, xs, wg, wu, wd, layer, geo):
    nb = _n_blocks(geo)
    rows = pl.BlockSpec((MOE_BLOCK, D_MODEL // 2), lambda j, be: (j, 0))

    def wspec(r, c):
        return pl.BlockSpec((None, None, r, c), lambda j, be: (layer, be[j], 0, 0))

    return pl.pallas_call(
        functools.partial(_expert_kernel, nb=nb),
        name="experts",
        out_shape=jax.ShapeDtypeStruct((nb * MOE_BLOCK, D_MODEL // 2), jnp.uint32),
        grid_spec=pltpu.PrefetchScalarGridSpec(
            num_scalar_prefetch=1,
            grid=(nb,),
            in_specs=[rows, wspec(D_MODEL, EXPERT_FF), wspec(D_MODEL, EXPERT_FF),
                      wspec(EXPERT_FF, D_MODEL)],
            out_specs=rows,
            scratch_shapes=[pltpu.VMEM((D_MODEL, EXPERT_FF), BF16),
                            pltpu.VMEM((D_MODEL, EXPERT_FF), BF16),
                            pltpu.VMEM((EXPERT_FF, D_MODEL), BF16)]),
        compiler_params=_params("arbitrary"),
    )(blk_e, xs, wg, wu, wd)


def _shared_kernel(x_ref, wg_ref, wu_ref, wd_ref, o_ref):
    o_ref[...] = _swiglu_packed(x_ref, wg_ref, wu_ref, wd_ref)


def _shared_expert(h_packed, wg, wu, wd, geo):
    tm = _pick_tile(geo, (512, 256, 128))
    ff = wg.shape[1]
    return pl.pallas_call(
        _shared_kernel,
        name="shared_expert",
        out_shape=jax.ShapeDtypeStruct((geo.T, D_MODEL), F32),
        grid=(geo.T // tm,),
        in_specs=[pl.BlockSpec((tm, D_MODEL // 2), lambda i: (i, 0)),
                  pl.BlockSpec((D_MODEL, ff), lambda i: (0, 0)),
                  pl.BlockSpec((D_MODEL, ff), lambda i: (0, 0)),
                  pl.BlockSpec((ff, D_MODEL), lambda i: (0, 0))],
        out_specs=pl.BlockSpec((tm, D_MODEL), lambda i: (i, 0)),
        compiler_params=_params("arbitrary"),
    )(h_packed, wg, wu, wd)


def _combine_kernel(dest_ref, dnext_ref, w_ref, x_ref, sh_ref, m_ref, ys_hbm, o_ref, g_ref, sems, *, tt):
    i = pl.program_id(0)
    n = pl.num_programs(0)
    slot = i % 2

    def gather(idx_ref, s):
        def issue(t, carry):
            for k in range(TOP_K):
                pltpu.make_async_copy(ys_hbm.at[pl.ds(idx_ref[t * TOP_K + k], 1)],
                                      g_ref.at[s, k, pl.ds(t, 1)], sems.at[s]).start(priority=k % 2)
            return carry
        lax.fori_loop(0, tt, issue, 0)

    @pl.when(i == 0)
    def _():
        gather(dest_ref, 0)

    @pl.when(i + 1 < n)
    def _():
        gather(dnext_ref, 1 - slot)

    def drain(t, carry):
        for k in range(TOP_K):
            pltpu.make_async_copy(ys_hbm.at[pl.ds(0, 1)], g_ref.at[slot, k, pl.ds(0, 1)],
                                  sems.at[slot]).wait()
        return carry

    lax.fori_loop(0, tt, drain, 0)

    half = D_MODEL // 2
    w = w_ref[...]
    acc_lo = sh_ref[:, :half]
    acc_hi = sh_ref[:, half:]
    for k in range(TOP_K):
        lo, hi = _unpack_rows(g_ref[slot, k])
        acc_lo = acc_lo + w[:, k:k + 1] * lo
        acc_hi = acc_hi + w[:, k:k + 1] * hi
    gate = m_ref[0]
    o_ref[:, :half] = x_ref[:, :half] + gate[:, :half] * acc_lo
    o_ref[:, half:] = x_ref[:, half:] + gate[:, half:] * acc_hi


def _combine(dest, w, x, sh, ys, mod3, m_gate, geo):
    tt = _pick_tile(geo, (256, 128))
    n = geo.T // tt
    rows = pl.BlockSpec((tt, D_MODEL), lambda i: (i, 0))
    return pl.pallas_call(
        functools.partial(_combine_kernel, tt=tt),
        name="combine",
        out_shape=jax.ShapeDtypeStruct((geo.T, D_MODEL), F32),
        grid=(n,),
        in_specs=[pl.BlockSpec((tt * TOP_K,), lambda i: (i,), memory_space=pltpu.SMEM),
                  pl.BlockSpec((tt * TOP_K,), lambda i: (jnp.minimum(i + 1, n - 1),),
                               memory_space=pltpu.SMEM),
                  pl.BlockSpec((tt, TOP_K), lambda i: (i, 0)),
                  rows, rows, _mod_spec(m_gate, tt, geo),
                  pl.BlockSpec(memory_space=pl.ANY)],
        out_specs=rows,
        scratch_shapes=[pltpu.VMEM((2, TOP_K, tt, D_MODEL // 2), jnp.uint32),
                        pltpu.SemaphoreType.DMA((2,))],
        compiler_params=_params("arbitrary"),
    )(dest, dest, w, x, sh, mod3, ys)


def _moe(x, h_packed, mod3, rw, rb, wg, wu, wd, layer, sg, su, sd, geo):
    eidx, rank, w, counts = _router(h_packed, rw, rb, geo)
    counts = counts.reshape(N_EXPERTS).astype(jnp.int32)
    dest, blk_e = _dest(counts, eidx, rank, geo)
    xs = _dispatch(counts, dest, h_packed, geo)
    ys = _experts(blk_e, xs, wg, wu, wd, layer, geo)
    sh = _shared_expert(h_packed, sg.astype(BF16), su.astype(BF16), sd.astype(BF16), geo)
    return _combine(dest, w, x, sh, ys, mod3, 5, geo)


def kernel(x, c, ctx, c_ctx, ada_w, ada_b, norm_mix, norm_ffn, norm_final, ev_w_in, ev_w_out, hgrn_lb, hgrn_norm, conv_w, conv_b, conv_norm_g, conv_norm_b, ret_w_in, ret_w_out, ret_decay, router_w, router_b, exp_gate, exp_up, exp_down, sh_gate, sh_up, sh_down):
    b, n, d = x.shape
    lc = ctx.shape[1]
    depth = ada_w.shape[0]
    geo = _geo(b, n, lc)
    assert d == D_MODEL and b < MOD_ROWS
    assert n % RET_CHUNK == 0 and lc % RET_CHUNK == 0

    xs = jnp.concatenate([x.reshape(geo.BN, d), ctx.reshape(geo.BL, d)], axis=0)
    cond = jnp.zeros((MOD_ROWS, d), F32).at[:b].set(c).at[b].set(c_ctx)
    cos_tab, sin_tab = _rope_tables(n)

    for l in range(depth):
        j = l // 2
        mod3 = _adaln(cond, ada_w, ada_b, l).reshape(MOD_ROWS * N_MOD, 1, d)
        h = _normmod(xs, norm_mix[l], mod3, 0, 1, geo, packed=False)
        if l % 2 == 0:
            p = _matmul(h, ev_w_in[j].astype(BF16), geo)
            of, ob = _hgrn_scan(p, hgrn_lb, l, geo)
            mix = _ev_readout(p, of, ob, hgrn_norm[j], conv_w[j], conv_b[j],
                              conv_norm_g[j], conv_norm_b[j], geo)
            xs = _matmul_resid(mix, ev_w_out[j].astype(BF16), xs, mod3, 2, geo)
        else:
            p = _matmul(h, ret_w_in[j].astype(BF16), geo)
            of, ob = _ret_scan(p, cos_tab, sin_tab, ret_decay[j], geo)
            mix = _ret_readout(p, of, ob, geo)
            xs = _matmul_resid(mix, ret_w_out[j].astype(BF16), xs, mod3, 2, geo)
        h_packed = _normmod(xs, norm_ffn[l], mod3, 3, 4, geo, packed=True)
        xs = _moe(xs, h_packed, mod3, router_w[l], router_b[l], exp_gate, exp_up, exp_down, l,
                  sh_gate[l], sh_up[l], sh_down[l], geo)
    return _final_norm(xs, norm_final, geo).reshape(b, n, d)
```

```python
import collections
import functools

import jax
import jax.numpy as jnp
from jax import lax
from jax.experimental import pallas as pl
from jax.experimental.pallas import tpu as pltpu

F32 = jnp.float32
BF16 = jnp.bfloat16

D_MODEL = 2048
N_MOD = 6
EPS = 1e-6
GRID_W = 64
ROPE_BASE = 10000.0

HG_HEADS = 8
HG_DK = 128
HG_DV = 128
HG_F = HG_HEADS * HG_DK
HG_V = HG_HEADS * HG_DV
CONV_C = D_MODEL // 2
CONV_W = 31
CONV_HALO = 16
HG_CHUNK = 128

RET_HEADS = 8
RET_DK = D_MODEL // RET_HEADS
RET_DV = 2 * RET_DK
RET_QK = RET_HEADS * RET_DK
RET_V = RET_HEADS * RET_DV
RET_CHUNK = 256

N_EXPERTS = 64
EXPERT_FF = D_MODEL // 4
TOP_K = 8
N_GROUPS = 8
GROUP_SIZE = N_EXPERTS // N_GROUPS
TOPK_GROUPS = 4
ROUTED_SCALE = 2.5
MOE_BLOCK = 512

MOD_ROWS = 16
VMEM_LIMIT = 56 * 1024 * 1024

Geo = collections.namedtuple("Geo", "B N Lc BN BL T")


def _geo(b, n, lc):
    return Geo(b, n, lc, b * n, b * lc, b * n + b * lc)


def _pick_tile(geo, cands):
    for t in cands:
        if geo.N % t == 0 and geo.BL % t == 0:
            return t
    raise ValueError("no row tile fits the sequence lengths")


def _mod_row(i, tm, geo):
    return jnp.where(i < geo.BN // tm, i // (geo.N // tm), geo.B)


def _mod_spec(m, tm, geo, ngrid=1):
    if ngrid == 1:
        return pl.BlockSpec((1, 1, D_MODEL), lambda i: (_mod_row(i, tm, geo) * N_MOD + m, 0, 0))
    return pl.BlockSpec((1, 1, D_MODEL), lambda i, j: (_mod_row(i, tm, geo) * N_MOD + m, 0, j))


def _params(*sem):
    return pltpu.CompilerParams(dimension_semantics=sem, vmem_limit_bytes=VMEM_LIMIT)


def _sigmoid(x):
    return jax.nn.sigmoid(x)


def _silu(x):
    return x * jax.nn.sigmoid(x)


def _adaln_kernel(c_ref, w_ref, b_ref, o_ref):
    a = _silu(c_ref[...]).astype(BF16)
    o_ref[...] = jnp.dot(a, w_ref[...].astype(BF16), preferred_element_type=F32) + b_ref[...]


def _adaln(cond, w, b, layer):
    depth, k, n = w.shape
    tn = 1024
    return pl.pallas_call(
        _adaln_kernel,
        name="adaln",
        out_shape=jax.ShapeDtypeStruct((MOD_ROWS, n), F32),
        grid=(n // tn,),
        in_specs=[pl.BlockSpec((MOD_ROWS, k), lambda j: (0, 0)),
                  pl.BlockSpec((None, k, tn), lambda j: (layer, 0, j)),
                  pl.BlockSpec((None, 1, tn), lambda j: (layer, 0, j))],
        out_specs=pl.BlockSpec((MOD_ROWS, tn), lambda j: (0, j)),
        compiler_params=_params("arbitrary"),
    )(cond, w, b.reshape(depth, 1, n))


def _pack_rows(x):
    n = x.shape[-1] // 2
    bits = lax.bitcast_convert_type(x.astype(BF16).astype(F32), jnp.uint32)
    return (bits[:, n:] & jnp.uint32(0xFFFF0000)) | (bits[:, :n] >> 16)


def _unpack_rows(u):
    lo = lax.bitcast_convert_type(u << 16, F32)
    hi = lax.bitcast_convert_type(u & jnp.uint32(0xFFFF0000), F32)
    return lo, hi


def _normmod_kernel(x_ref, g_ref, sh_ref, sc_ref, o_ref, *, packed):
    x = x_ref[...]
    y = x * lax.rsqrt(jnp.mean(x * x, axis=-1, keepdims=True) + EPS) * g_ref[...]
    h = y * (1.0 + sc_ref[0]) + sh_ref[0]
    o_ref[...] = _pack_rows(h) if packed else h.astype(BF16)


def _normmod(x, g, mod3, m_shift, m_scale, geo, packed):
    tm = _pick_tile(geo, (256, 128))
    spec = pl.BlockSpec((tm, D_MODEL), lambda i: (i, 0))
    if packed:
        out_shape = jax.ShapeDtypeStruct((geo.T, D_MODEL // 2), jnp.uint32)
        out_spec = pl.BlockSpec((tm, D_MODEL // 2), lambda i: (i, 0))
    else:
        out_shape = jax.ShapeDtypeStruct((geo.T, D_MODEL), BF16)
        out_spec = spec
    return pl.pallas_call(
        functools.partial(_normmod_kernel, packed=packed),
        name="normmod",
        out_shape=out_shape,
        grid=(geo.T // tm,),
        in_specs=[spec, pl.BlockSpec((1, D_MODEL), lambda i: (0, 0)),
                  _mod_spec(m_shift, tm, geo), _mod_spec(m_scale, tm, geo)],
        out_specs=out_spec,
        compiler_params=_params("arbitrary"),
    )(x, g.reshape(1, D_MODEL), mod3, mod3)


def _mm_kernel(a_ref, w_ref, o_ref):
    o_ref[...] = jnp.dot(a_ref[...], w_ref[...], preferred_element_type=F32).astype(o_ref.dtype)


def _mm_resid_kernel(a_ref, w_ref, x_ref, m_ref, o_ref):
    y = jnp.dot(a_ref[...], w_ref[...], preferred_element_type=F32)
    o_ref[...] = x_ref[...] + m_ref[0] * y


def _matmul(a, w, geo):
    k, n = w.shape
    tm = _pick_tile(geo, (1024, 512, 256, 128))
    tn = 512
    return pl.pallas_call(
        _mm_kernel,
        name="matmul",
        out_shape=jax.ShapeDtypeStruct((geo.T, n), BF16),
        grid=(geo.T // tm, n // tn),
        in_specs=[pl.BlockSpec((tm, k), lambda i, j: (i, 0)),
                  pl.BlockSpec((k, tn), lambda i, j: (0, j))],
        out_specs=pl.BlockSpec((tm, tn), lambda i, j: (i, j)),
        compiler_params=_params("arbitrary", "arbitrary"),
    )(a, w)


def _matmul_resid(a, w, x, mod3, m_gate, geo):
    k, n = w.shape
    tm = _pick_tile(geo, (1024, 512, 256, 128))
    tn = 512
    return pl.pallas_call(
        _mm_resid_kernel,
        name="matmul_resid",
        out_shape=jax.ShapeDtypeStruct((geo.T, n), F32),
        grid=(geo.T // tm, n // tn),
        in_specs=[pl.BlockSpec((tm, k), lambda i, j: (i, 0)),
                  pl.BlockSpec((k, tn), lambda i, j: (0, j)),
                  pl.BlockSpec((tm, tn), lambda i, j: (i, j)),
                  pl.BlockSpec((1, 1, tn), lambda i, j: (_mod_row(i, tm, geo) * N_MOD + m_gate, 0, j))],
        out_specs=pl.BlockSpec((tm, tn), lambda i, j: (i, j)),
        compiler_params=_params("arbitrary", "arbitrary"),
    )(a, w, x, mod3)


def _chunk_index(b, i, chunk, geo, reverse):
    nc = geo.Lc // chunk
    nl = geo.N // chunk
    ctx0 = (geo.BN + b * geo.Lc) // chunk
    lat0 = (b * geo.N) // chunk
    if reverse:
        return jnp.where(i < nc, ctx0 + (nc - 1 - i), lat0 + (nl - 1 - (i - nc)))
    return jnp.where(i < nc, ctx0 + i, lat0 + (i - nc))


def _split_dot(tri_bf, x):
    hi = x.astype(BF16)
    r1 = x - hi.astype(F32)
    mid = r1.astype(BF16)
    lo = (r1 - mid.astype(F32)).astype(BF16)
    return (jnp.dot(tri_bf, hi, preferred_element_type=F32)
            + jnp.dot(tri_bf, mid, preferred_element_type=F32)
            + jnp.dot(tri_bf, lo, preferred_element_type=F32))


def _hgrn_kernel(qf_ref, ff_ref, vf_ref, qb_ref, fb_ref, vb_ref, lbp_ref, of_ref, ob_ref,
                 stf_ref, stb_ref, *, layer):
    i = pl.program_id(1)

    @pl.when(i == 0)
    def _():
        stf_ref[...] = jnp.zeros_like(stf_ref)
        stb_ref[...] = jnp.zeros_like(stb_ref)

    lbp = lbp_ref[...]
    e = jnp.exp(lbp - jnp.max(lbp, axis=0, keepdims=True))
    sm = e / jnp.sum(e, axis=0, keepdims=True)
    lb = sm[0:1]
    for r in range(1, layer + 1):
        lb = lb + sm[r:r + 1]

    _hgrn_chunk(qf_ref, ff_ref, vf_ref, of_ref, stf_ref, lb, reverse=False)
    _hgrn_chunk(qb_ref, fb_ref, vb_ref, ob_ref, stb_ref, lb, reverse=True)


def _hgrn_chunk(q_ref, f_ref, v_ref, o_ref, st_ref, lb, *, reverse):
    c = HG_CHUNK
    row = lax.broadcasted_iota(jnp.int32, (c, c), 0)
    col = lax.broadcasted_iota(jnp.int32, (c, c), 1)
    tri = (col >= row) if reverse else (col <= row)
    tri_bf = jnp.where(tri, 1.0, 0.0).astype(BF16)
    nt = (((1,), (1,)), ((), ()))

    for h in range(HG_HEADS):
        sl = slice(h * HG_DK, (h + 1) * HG_DK)
        qh = _silu(q_ref[:, sl].astype(F32))
        lbh = lb[:, sl]
        fg = lbh + (1.0 - lbh) * _sigmoid(f_ref[:, sl].astype(F32))
        kh = 1.0 - fg
        bcum = _split_dot(tri_bf, jnp.log(fg))
        bmid = bcum[c // 2:c // 2 + 1]
        bend = bcum[0:1] if reverse else bcum[c - 1:c]
        vb = v_ref[:, sl]
        vh = vb.astype(F32)
        a = (qh * jnp.exp(bcum - bmid)).astype(BF16)
        kd = (kh * jnp.exp(bmid - bcum)).astype(BF16)
        s = lax.dot_general(a, kd, nt, preferred_element_type=F32)
        s = jnp.where(tri, s, 0.0)
        intra = jnp.dot(s.astype(BF16), vb, preferred_element_type=F32)
        st = st_ref[h]
        inter = lax.dot_general((qh * jnp.exp(bcum)).astype(BF16), st.astype(BF16), nt,
                                preferred_element_type=F32)
        o_ref[:, sl] = (inter + intra).astype(BF16)
        kd2 = (kh * jnp.exp(bend - bcum)).astype(BF16)
        st_ref[h] = st * jnp.exp(bend) + jnp.dot(vh.T.astype(BF16), kd2, preferred_element_type=F32)


def _hgrn_scan(p, lb_param, layer, geo):
    c = HG_CHUNK
    steps = (geo.Lc + geo.N) // c

    def spec(colblk, reverse):
        return pl.BlockSpec((c, HG_F), lambda b, i: (_chunk_index(b, i, c, geo, reverse), colblk))

    out = jax.ShapeDtypeStruct((geo.T, HG_V), BF16)
    state = pltpu.VMEM((HG_HEADS, HG_DV, HG_DK), F32)
    return pl.pallas_call(
        functools.partial(_hgrn_kernel, layer=layer),
        name="hgrn_scan",
        out_shape=[out, out],
        grid=(geo.B, steps),
        in_specs=[spec(0, False), spec(1, False), spec(3, False),
                  spec(0, True), spec(2, True), spec(3, True),
                  pl.BlockSpec(lb_param.shape, lambda b, i: (0, 0))],
        out_specs=[spec(0, False), spec(0, True)],
        scratch_shapes=[state, state],
        compiler_params=_params("arbitrary", "arbitrary"),
    )(p, p, p, p, p, p, lb_param)


def _ev_readout_kernel(of_ref, ob_ref, gate_ref, a_ref, b_ref, ap_ref, bp_ref, an_ref, bn_ref,
                       gain_ref, cw_ref, cb_ref, lng_ref, lnb_ref, o_ref, ext_ref, conv_ref, *, tm, geo):
    i = pl.program_id(0)
    n_lat = geo.BN // tm
    tpl = geo.N // tm
    tpc = geo.Lc // tm
    j = jnp.where(i < n_lat, i % tpl, (i - n_lat) % tpc)
    per = jnp.where(i < n_lat, tpl, tpc)
    keep_prev = jnp.where(j == 0, 0.0, 1.0)
    keep_next = jnp.where(j == per - 1, 0.0, 1.0)

    o = of_ref[...].astype(F32) + ob_ref[...].astype(F32)
    r = o * lax.rsqrt(jnp.mean(o * o, axis=-1, keepdims=True) + EPS) * gain_ref[...]
    o_ref[:, :HG_V] = (r * _silu(gate_ref[...].astype(F32))).astype(BF16)

    def glu(x_ref, y_ref):
        return x_ref[...].astype(F32) * _sigmoid(y_ref[...].astype(F32))

    ext_ref[0:CONV_HALO] = glu(ap_ref, bp_ref) * keep_prev
    ext_ref[CONV_HALO:CONV_HALO + tm] = glu(a_ref, b_ref)
    ext_ref[CONV_HALO + tm:2 * CONV_HALO + tm] = glu(an_ref, bn_ref) * keep_next
    off = CONV_HALO - CONV_W // 2
    for cj in range(CONV_C // 128):
        cs = slice(cj * 128, (cj + 1) * 128)
        acc = jnp.broadcast_to(cb_ref[:, cs], (tm, 128))
        for k in range(CONV_W):
            acc = acc + cw_ref[k:k + 1, cs] * ext_ref[off + k:off + k + tm, cs]
        conv_ref[:, cs] = acc
    acc = conv_ref[...]
    mu = jnp.mean(acc, axis=-1, keepdims=True)
    xc = acc - mu
    var = jnp.mean(xc * xc, axis=-1, keepdims=True)
    u = xc * lax.rsqrt(var + EPS) * lng_ref[...] + lnb_ref[...]
    o_ref[:, HG_V:] = _silu(u).astype(BF16)


def _ev_readout(p, of, ob, gain, cw, cb, lng, lnb, geo):
    tm = _pick_tile(geo, (128,))
    hb = tm // CONV_HALO
    nhalo = geo.T // CONV_HALO

    def row(colblk):
        return pl.BlockSpec((tm, HG_V), lambda i: (i, colblk))

    def prev(colblk):
        return pl.BlockSpec((CONV_HALO, CONV_C), lambda i: (jnp.maximum(i * hb - 1, 0), colblk))

    def nxt(colblk):
        return pl.BlockSpec((CONV_HALO, CONV_C), lambda i: (jnp.minimum((i + 1) * hb, nhalo - 1), colblk))

    def vec(n):
        return pl.BlockSpec((n, CONV_C), lambda i: (0, 0))

    return pl.pallas_call(
        functools.partial(_ev_readout_kernel, tm=tm, geo=geo),
        name="ev_readout",
        out_shape=jax.ShapeDtypeStruct((geo.T, HG_V + CONV_C), BF16),
        grid=(geo.T // tm,),
        in_specs=[row(0), row(0), row(4), row(5), row(6), prev(5), prev(6), nxt(5), nxt(6),
                  vec(1), vec(CONV_W), vec(1), vec(1), vec(1)],
        out_specs=pl.BlockSpec((tm, HG_V + CONV_C), lambda i: (i, 0)),
        scratch_shapes=[pltpu.VMEM((tm + 2 * CONV_HALO, CONV_C), F32), pltpu.VMEM((tm, CONV_C), F32)],
        compiler_params=_params("arbitrary"),
    )(of, ob, p, p, p, p, p, p, p, gain.reshape(1, -1), cw, cb.reshape(1, -1),
      lng.reshape(1, -1), lnb.reshape(1, -1))


def _rope(x, cos, sin_signed):
    half = x.shape[-1] // 2
    rot = jnp.concatenate([pltpu.roll(x[:, :half], half // 2, axis=1),
                           pltpu.roll(x[:, half:], half // 2, axis=1)], axis=-1)
    return x * cos + rot * sin_signed


def _ret_kernel(qf_ref, kf_ref, vf_ref, cosf_ref, sinf_ref, qb_ref, kb_ref, vb_ref, cosb_ref, sinb_ref,
                dl_ref, of_ref, ob_ref, sf_ref, sb_ref):
    i = pl.program_id(1)

    @pl.when(i == 0)
    def _():
        sf_ref[...] = jnp.zeros_like(sf_ref)
        sb_ref[...] = jnp.zeros_like(sb_ref)

    _ret_chunk(qf_ref, kf_ref, vf_ref, cosf_ref, sinf_ref, dl_ref[0], of_ref, sf_ref, reverse=False)
    _ret_chunk(qb_ref, kb_ref, vb_ref, cosb_ref, sinb_ref, dl_ref[1], ob_ref, sb_ref, reverse=True)


def _ret_chunk(q_ref, k_ref, v_ref, cos_ref, sin_ref, dl, o_ref, s_ref, *, reverse):
    c = RET_CHUNK
    lg_all = -jnp.log1p(jnp.exp(-dl))
    row = lax.broadcasted_iota(jnp.int32, (c, c), 0)
    col = lax.broadcasted_iota(jnp.int32, (c, c), 1)
    idx = lax.broadcasted_iota(jnp.int32, (c, 1), 0).astype(F32)
    if reverse:
        live = col >= row
        dist = (col - row).astype(F32)
        q_pow = c - idx
        k_pow = idx
    else:
        live = row >= col
        dist = (row - col).astype(F32)
        q_pow = idx + 1.0
        k_pow = c - 1.0 - idx
    cos = cos_ref[...]
    sin = sin_ref[...]
    nt = (((1,), (1,)), ((), ()))

    for h in range(RET_HEADS):
        lg = lg_all[h:h + 1]
        dmask = jnp.where(live, jnp.exp(lg * dist), 0.0)
        q = _rope(q_ref[:, h * RET_DK:(h + 1) * RET_DK].astype(F32), cos, sin)
        k = _rope(k_ref[:, h * RET_DK:(h + 1) * RET_DK].astype(F32) * (RET_DK ** -0.5), cos, sin)
        vb = v_ref[:, h * RET_DV:(h + 1) * RET_DV]
        scores = lax.dot_general(q.astype(BF16), k.astype(BF16), nt, preferred_element_type=F32) * dmask
        intra = jnp.dot(scores.astype(BF16), vb, preferred_element_type=F32)
        s = s_ref[h]
        inter = jnp.dot((q * jnp.exp(lg * q_pow)).astype(BF16), s.astype(BF16),
                        preferred_element_type=F32)
        o_ref[:, h * RET_DV:(h + 1) * RET_DV] = (inter + intra).astype(BF16)
        kdec = (k * jnp.exp(lg * k_pow)).T.astype(BF16)
        s_ref[h] = jnp.exp(lg * c) * s + jnp.dot(kdec, vb, preferred_element_type=F32)


def _ret_scan(p, cos_tab, sin_tab, decay_logit, geo):
    c = RET_CHUNK
    steps = (geo.Lc + geo.N) // c
    nc = geo.Lc // c
    nl = geo.N // c

    def spec(width, colblk, reverse):
        return pl.BlockSpec((c, width), lambda b, i: (_chunk_index(b, i, c, geo, reverse), colblk))

    def tab(reverse):
        def index(b, i):
            lat = (nl - 1 - (i - nc)) if reverse else (i - nc)
            return (jnp.where(i < nc, nl, lat), 0)
        return pl.BlockSpec((c, RET_DK), index)

    def direction(reverse):
        return [spec(RET_QK, 0, reverse), spec(RET_QK, 1, reverse), spec(RET_V, 1, reverse),
                tab(reverse), tab(reverse)]

    out = jax.ShapeDtypeStruct((geo.T, RET_V), BF16)
    state = pltpu.VMEM((RET_HEADS, RET_DK, RET_DV), F32)
    return pl.pallas_call(
        _ret_kernel,
        name="ret_scan",
        out_shape=[out, out],
        grid=(geo.B, steps),
        in_specs=direction(False) + direction(True)
        + [pl.BlockSpec((2, RET_HEADS, 1), lambda b, i: (0, 0, 0))],
        out_specs=[spec(RET_V, 0, False), spec(RET_V, 0, True)],
        scratch_shapes=[state, state],
        compiler_params=_params("arbitrary", "arbitrary"),
    )(p, p, p, cos_tab, sin_tab, p, p, p, cos_tab, sin_tab, decay_logit.reshape(2, RET_HEADS, 1))


def _rope_tables(n):
    t = jnp.arange(n)
    quarter = RET_DK // 4
    inv = 1.0 / (ROPE_BASE ** (jnp.arange(quarter, dtype=F32) / quarter))
    ang_r = (t // GRID_W).astype(F32)[:, None] * inv
    ang_c = (t % GRID_W).astype(F32)[:, None] * inv
    cos = jnp.concatenate([jnp.cos(ang_r), jnp.cos(ang_r), jnp.cos(ang_c), jnp.cos(ang_c)], axis=-1)
    sin = jnp.concatenate([-jnp.sin(ang_r), jnp.sin(ang_r), -jnp.sin(ang_c), jnp.sin(ang_c)], axis=-1)
    cos = jnp.concatenate([cos, jnp.ones((RET_CHUNK, RET_DK), F32)], axis=0)
    sin = jnp.concatenate([sin, jnp.zeros((RET_CHUNK, RET_DK), F32)], axis=0)
    return cos, sin


def _ret_readout_kernel(of_ref, ob_ref, gate_ref, o_ref):
    for h in range(RET_HEADS):
        sl = slice(h * RET_DV, (h + 1) * RET_DV)
        o = of_ref[:, sl].astype(F32) + ob_ref[:, sl].astype(F32)
        r = o * lax.rsqrt(jnp.mean(o * o, axis=-1, keepdims=True) + EPS)
        o_ref[:, sl] = (_silu(gate_ref[:, sl].astype(F32)) * r).astype(BF16)


def _ret_readout(p, of, ob, geo):
    tm = _pick_tile(geo, (128,))
    spec = pl.BlockSpec((tm, RET_V), lambda i: (i, 0))
    return pl.pallas_call(
        _ret_readout_kernel,
        name="ret_readout",
        out_shape=jax.ShapeDtypeStruct((geo.T, RET_V), BF16),
        grid=(geo.T // tm,),
        in_specs=[spec, spec, pl.BlockSpec((tm, RET_V), lambda i: (i, 2))],
        out_specs=spec,
        compiler_params=_params("arbitrary"),
    )(of, ob, p)


def _router_kernel(h_ref, rw_ref, rb_ref, eidx_ref, rank_ref, w_ref, cnt_ref, carry_ref, *, tm):
    i = pl.program_id(0)

    @pl.when(i == 0)
    def _():
        carry_ref[...] = jnp.zeros_like(carry_ref)

    half = D_MODEL // 2
    h_lo, h_hi = _unpack_rows(h_ref[...])
    h_lo = h_lo.astype(BF16)
    h_hi = h_hi.astype(BF16)
    rest = rw_ref[...]
    logits = jnp.zeros((tm, N_EXPERTS), F32)
    for _ in range(3):
        part = rest.astype(BF16)
        rest = rest - part.astype(F32)
        logits = (logits + jnp.dot(h_lo, part[:half], preferred_element_type=F32)
                  + jnp.dot(h_hi, part[half:], preferred_element_type=F32))
    s = _sigmoid(logits)
    sel = s + rb_ref[...]
    lane = lax.broadcasted_iota(jnp.int32, (tm, N_EXPERTS), 1).astype(F32)
    grp = jnp.floor(lane * (1.0 / GROUP_SIZE))
    ninf = -jnp.inf
    none = float(N_EXPERTS)

    gscore = jnp.zeros((tm, N_EXPERTS), F32)
    gcols = []
    for g in range(N_GROUPS):
        in_g = grp == float(g)
        v1 = jnp.max(jnp.where(in_g, sel, ninf), axis=-1, keepdims=True)
        i1 = jnp.min(jnp.where(in_g & (sel == v1), lane, none), axis=-1, keepdims=True)
        v2 = jnp.max(jnp.where(in_g & (lane != i1), sel, ninf), axis=-1, keepdims=True)
        gcols.append(v1 + v2)
        gscore = jnp.where(in_g, v1 + v2, gscore)
    beaten = jnp.zeros((tm, N_EXPERTS), F32)
    for g in range(N_GROUPS):
        wins = (gcols[g] > gscore) | ((gcols[g] == gscore) & (float(g) < grp))
        beaten = beaten + jnp.where(wins, 1.0, 0.0)
    cand = jnp.where(beaten < float(TOPK_GROUPS), sel, ninf)

    lane_k = lax.broadcasted_iota(jnp.int32, (tm, TOP_K), 1)
    eidx = jnp.zeros((tm, TOP_K), F32)
    wsel = jnp.zeros((tm, TOP_K), F32)
    chosen = jnp.zeros((tm, N_EXPERTS), F32)
    picks = []
    for k in range(TOP_K):
        v = jnp.max(cand, axis=-1, keepdims=True)
        ik = jnp.min(jnp.where(cand == v, lane, none), axis=-1, keepdims=True)
        hit = lane == ik
        picks.append(ik)
        eidx = jnp.where(lane_k == k, ik, eidx)
        wsel = jnp.where(lane_k == k, jnp.sum(jnp.where(hit, s, 0.0), axis=-1, keepdims=True), wsel)
        chosen = jnp.where(hit, 1.0, chosen)
        cand = jnp.where(hit, ninf, cand)
    w_ref[...] = wsel / jnp.sum(wsel, axis=-1, keepdims=True) * ROUTED_SCALE
    eidx_ref[...] = eidx.astype(jnp.int32)

    r = lax.broadcasted_iota(jnp.int32, (tm, tm), 0)
    c = lax.broadcasted_iota(jnp.int32, (tm, tm), 1)
    below = jnp.where(c < r, 1.0, 0.0).astype(BF16)
    carry = carry_ref[...]
    pos = jnp.dot(below, chosen.astype(BF16), preferred_element_type=F32) + carry
    rank = jnp.zeros((tm, TOP_K), jnp.int32)
    for k in range(TOP_K):
        rk = jnp.sum(jnp.where(lane == picks[k], pos, 0.0), axis=-1, keepdims=True)
        rank = jnp.where(lane_k == k, rk.astype(jnp.int32), rank)
    rank_ref[...] = rank
    carry = carry + jnp.sum(chosen, axis=0, keepdims=True)
    carry_ref[...] = carry
    cnt_ref[...] = carry


def _router(h_packed, rw, rb, geo):
    tm = 256 if geo.T % 256 == 0 else 128
    tok = pl.BlockSpec((tm, TOP_K), lambda i: (i, 0))
    one = pl.BlockSpec((1, N_EXPERTS), lambda i: (0, 0))
    return pl.pallas_call(
        functools.partial(_router_kernel, tm=tm),
        name="router",
        out_shape=[jax.ShapeDtypeStruct((geo.T, TOP_K), jnp.int32),
                   jax.ShapeDtypeStruct((geo.T, TOP_K), jnp.int32),
                   jax.ShapeDtypeStruct((geo.T, TOP_K), F32),
                   jax.ShapeDtypeStruct((1, N_EXPERTS), F32)],
        grid=(geo.T // tm,),
        in_specs=[pl.BlockSpec((tm, D_MODEL // 2), lambda i: (i, 0)),
                  pl.BlockSpec((D_MODEL, N_EXPERTS), lambda i: (0, 0)), one],
        out_specs=[tok, tok, tok, one],
        scratch_shapes=[pltpu.VMEM((1, N_EXPERTS), F32)],
        compiler_params=_params("arbitrary"),
    )(h_packed, rw, rb.reshape(1, N_EXPERTS))


def _n_blocks(geo):
    return -(-(geo.T * TOP_K) // MOE_BLOCK) + N_EXPERTS


def _dest_kernel(cnt_ref, eidx_ref, rank_ref, dest_ref, blk_ref, *, n_blocks):
    eidx = eidx_ref[...]
    dest = rank_ref[...]
    blk_row = (lax.broadcasted_iota(jnp.int32, blk_ref.shape, 0) * 128
               + lax.broadcasted_iota(jnp.int32, blk_ref.shape, 1)) * MOE_BLOCK
    blk = jnp.zeros(blk_ref.shape, jnp.int32)
    start = jnp.int32(0)
    for e in range(N_EXPERTS):
        padded = (cnt_ref[e] + (MOE_BLOCK - 1)) // MOE_BLOCK * MOE_BLOCK
        dest = dest + jnp.where(eidx == e, start, 0)
        start = start + padded
        blk = blk + jnp.where(start <= blk_row, 1, 0)
    dest_ref[...] = dest
    blk_ref[...] = jnp.where(blk_row == n_blocks * MOE_BLOCK, start // MOE_BLOCK,
                             jnp.minimum(blk, N_EXPERTS - 1))


def _dest(counts, eidx, rank, geo):
    rows = geo.T * TOP_K // 128
    brow = -(-(_n_blocks(geo) + 1) // 128)
    full = pl.BlockSpec((rows, 128), lambda: (0, 0))
    dest, blk = pl.pallas_call(
        functools.partial(_dest_kernel, n_blocks=_n_blocks(geo)),
        name="dest",
        out_shape=[jax.ShapeDtypeStruct((rows, 128), jnp.int32),
                   jax.ShapeDtypeStruct((brow, 128), jnp.int32)],
        in_specs=[pl.BlockSpec(memory_space=pltpu.SMEM), full, full],
        out_specs=[full, pl.BlockSpec((brow, 128), lambda: (0, 0))],
    )(counts, eidx.reshape(rows, 128), rank.reshape(rows, 128))
    return dest.reshape(-1), blk.reshape(-1)[:_n_blocks(geo) + 1]


def _row_copy(src_hbm, s, dst_hbm, d, sem):
    return pltpu.make_async_copy(src_hbm.at[pl.ds(s, 1)], dst_hbm.at[pl.ds(d, 1)], sem)


def _zero_fill(cnt_ref, xs_hbm, zero_ref, zsem, n_rows, wait):
    def piece(pos, size):
        if size >= 8:
            copies = [(pl.multiple_of(pos, 8), size)]
        else:
            copies = [(pos + r, 1) for r in range(size)]
        for p, s in copies:
            cp = pltpu.make_async_copy(zero_ref.at[pl.ds(0, s)], xs_hbm.at[pl.ds(p, s)], zsem)
            cp.wait() if wait else cp.start()

    def per_expert(e, start):
        cnt = cnt_ref[e]
        padded = (cnt + (MOE_BLOCK - 1)) // MOE_BLOCK * MOE_BLOCK
        pad = padded - cnt
        pos = start + cnt
        size = 1
        while size < MOE_BLOCK:
            take = (pad & size) != 0
            pl.when(take)(functools.partial(piece, pos, size))
            pos = pos + jnp.where(take, size, 0)
            size *= 2
        return start + padded

    end = lax.fori_loop(0, N_EXPERTS, per_expert, jnp.int32(0))

    def per_block(j, carry):
        piece(end + j * MOE_BLOCK, MOE_BLOCK)
        return carry

    lax.fori_loop(0, (n_rows - end) // MOE_BLOCK, per_block, 0)


def _dispatch_kernel(cnt_ref, dest_ref, h_ref, xs_hbm, zero_ref, sem, zsem, *, tt, n_rows):
    i = pl.program_id(0)

    @pl.when(i == 0)
    def _():
        zero_ref[...] = jnp.zeros_like(zero_ref)
        _zero_fill(cnt_ref, xs_hbm, zero_ref, zsem, n_rows, wait=False)
        _zero_fill(cnt_ref, xs_hbm, zero_ref, zsem, n_rows, wait=True)

    def issue(t, carry):
        for k in range(TOP_K):
            _row_copy(h_ref, t, xs_hbm, dest_ref[t * TOP_K + k], sem).start(priority=k % 2)
        return carry

    lax.fori_loop(0, tt, issue, 0)

    def drain(t, carry):
        for k in range(TOP_K):
            _row_copy(h_ref, 0, xs_hbm, 0, sem).wait()
        return carry

    lax.fori_loop(0, tt, drain, 0)


def _dispatch(counts, dest, h_packed, geo):
    tt = 512 if geo.T % 512 == 0 else 128
    n_rows = _n_blocks(geo) * MOE_BLOCK
    width = D_MODEL // 2
    return pl.pallas_call(
        functools.partial(_dispatch_kernel, tt=tt, n_rows=n_rows),
        name="dispatch",
        out_shape=jax.ShapeDtypeStruct((n_rows, width), jnp.uint32),
        grid=(geo.T // tt,),
        in_specs=[pl.BlockSpec(memory_space=pltpu.SMEM),
                  pl.BlockSpec((tt * TOP_K,), lambda i: (i,), memory_space=pltpu.SMEM),
                  pl.BlockSpec((tt, width), lambda i: (i, 0))],
        out_specs=pl.BlockSpec(memory_space=pl.ANY),
        scratch_shapes=[pltpu.VMEM((MOE_BLOCK, width), jnp.uint32), pltpu.SemaphoreType.DMA,
                        pltpu.SemaphoreType.DMA],
        compiler_params=_params("arbitrary"),
    )(counts, dest, h_packed)


def _swiglu_packed(x_ref, wg_ref, wu_ref, wd_ref):
    half = D_MODEL // 2
    lo, hi = _unpack_rows(x_ref[...])
    lo = lo.astype(BF16)
    hi = hi.astype(BF16)

    def proj(w_ref):
        return (jnp.dot(lo, w_ref[:half], preferred_element_type=F32)
                + jnp.dot(hi, w_ref[half:], preferred_element_type=F32))

    g = proj(wg_ref)
    u = proj(wu_ref)
    return jnp.dot((_silu(g) * u).astype(BF16), wd_ref[...], preferred_element_type=F32)


def _expert_kernel(blk_ref, x_ref, wg_ref, wu_ref, wd_ref, o_ref, wgb_ref, wub_ref, wdb_ref, *, nb):
    j = pl.program_id(0)
    n_used = blk_ref[nb]

    @pl.when((j == 0) | (blk_ref[j] != blk_ref[jnp.maximum(j - 1, 0)]))
    def _():
        wgb_ref[...] = wg_ref[...].astype(BF16)
        wub_ref[...] = wu_ref[...].astype(BF16)
        wdb_ref[...] = wd_ref[...].astype(BF16)

    @pl.when(j < n_used)
    def _():
        o_ref[...] = _pack_rows(_swiglu_packed(x_ref, wgb_ref, wub_ref, wdb_ref))

    @pl.when(j >= n_used)
    def _():
        o_ref[...] = jnp.zeros_like(o_ref)


def _experts(blk_e, xs, wg, wu, wd, layer, geo):
    nb = _n_blocks(geo)
    rows = pl.BlockSpec((MOE_BLOCK, D_MODEL // 2), lambda j, be: (j, 0))

    def wspec(r, c):
        return pl.BlockSpec((None, None, r, c), lambda j, be: (layer, be[j], 0, 0))

    return pl.pallas_call(
        functools.partial(_expert_kernel, nb=nb),
        name="experts",
        out_shape=jax.ShapeDtypeStruct((nb * MOE_BLOCK, D_MODEL // 2), jnp.uint32),
        grid_spec=pltpu.PrefetchScalarGridSpec(
            num_scalar_prefetch=1,
            grid=(nb,),
            in_specs=[rows, wspec(D_MODEL, EXPERT_FF), wspec(D_MODEL, EXPERT_FF),
                      wspec(EXPERT_FF, D_MODEL)],
            out_specs=rows,
            scratch_shapes=[pltpu.VMEM((D_MODEL, EXPERT_FF), BF16),
                            pltpu.VMEM((D_MODEL, EXPERT_FF), BF16),
                            pltpu.VMEM((EXPERT_FF, D_MODEL), BF16)]),
        compiler_params=_params("arbitrary"),
    )(blk_e, xs, wg, wu, wd)


def _combine_kernel(dest_ref, dnext_ref, w_ref, x_ref, h_ref, sg_ref, su_ref, sd_ref, m_ref, *rest,
                    tt, post):
    if post == "next":
        ng_ref, nsh_ref, nsc_ref, ys_hbm, o_ref, hn_ref, g_ref, sems = rest
    else:
        ng_ref, ys_hbm, o_ref, g_ref, sems = rest
    i = pl.program_id(0)
    n = pl.num_programs(0)
    slot = i % 2

    def gather(idx_ref, s):
        def issue(t, carry):
            for k in range(TOP_K):
                pltpu.make_async_copy(ys_hbm.at[pl.ds(idx_ref[t * TOP_K + k], 1)],
                                      g_ref.at[s, k, pl.ds(t, 1)], sems.at[s]).start(priority=k % 2)
            return carry
        lax.fori_loop(0, tt, issue, 0)

    @pl.when(i == 0)
    def _():
        gather(dest_ref, 0)

    @pl.when(i + 1 < n)
    def _():
        gather(dnext_ref, 1 - slot)

    shared = _swiglu_packed(h_ref, sg_ref, su_ref, sd_ref)

    def drain(t, carry):
        for k in range(TOP_K):
            pltpu.make_async_copy(ys_hbm.at[pl.ds(0, 1)], g_ref.at[slot, k, pl.ds(0, 1)],
                                  sems.at[slot]).wait()
        return carry

    lax.fori_loop(0, tt, drain, 0)

    half = D_MODEL // 2
    w = w_ref[...]
    acc_lo = shared[:, :half]
    acc_hi = shared[:, half:]
    for k in range(TOP_K):
        lo, hi = _unpack_rows(g_ref[slot, k])
        acc_lo = acc_lo + w[:, k:k + 1] * lo
        acc_hi = acc_hi + w[:, k:k + 1] * hi
    gate = m_ref[0]
    y_lo = x_ref[:, :half] + gate[:, :half] * acc_lo
    y_hi = x_ref[:, half:] + gate[:, half:] * acc_hi
    ms = (jnp.sum(y_lo * y_lo, axis=-1, keepdims=True)
          + jnp.sum(y_hi * y_hi, axis=-1, keepdims=True)) * (1.0 / D_MODEL)
    inv = lax.rsqrt(ms + EPS)
    ng = ng_ref[...]
    n_lo = y_lo * inv * ng[:, :half]
    n_hi = y_hi * inv * ng[:, half:]
    if post == "next":
        o_ref[:, :half] = y_lo
        o_ref[:, half:] = y_hi
        sc = nsc_ref[0]
        sft = nsh_ref[0]
        hn_ref[:, :half] = (n_lo * (1.0 + sc[:, :half]) + sft[:, :half]).astype(BF16)
        hn_ref[:, half:] = (n_hi * (1.0 + sc[:, half:]) + sft[:, half:]).astype(BF16)
    else:
        o_ref[:, :half] = n_lo
        o_ref[:, half:] = n_hi


def _combine(dest, w, x, h_packed, sg, su, sd, ys, mod3, m_gate, post, norm_g, next_mod3, geo):
    tt = _pick_tile(geo, (256, 128))
    n = geo.T // tt
    ff = sg.shape[1]
    rows = pl.BlockSpec((tt, D_MODEL), lambda i: (i, 0))
    vec = pl.BlockSpec((1, D_MODEL), lambda i: (0, 0))
    in_specs = [pl.BlockSpec((tt * TOP_K,), lambda i: (i,), memory_space=pltpu.SMEM),
                pl.BlockSpec((tt * TOP_K,), lambda i: (jnp.minimum(i + 1, n - 1),),
                             memory_space=pltpu.SMEM),
                pl.BlockSpec((tt, TOP_K), lambda i: (i, 0)),
                rows,
                pl.BlockSpec((tt, D_MODEL // 2), lambda i: (i, 0)),
                pl.BlockSpec((D_MODEL, ff), lambda i: (0, 0)),
                pl.BlockSpec((D_MODEL, ff), lambda i: (0, 0)),
                pl.BlockSpec((ff, D_MODEL), lambda i: (0, 0)),
                _mod_spec(m_gate, tt, geo), vec]
    args = [dest, dest, w, x, h_packed, sg, su, sd, mod3, norm_g.reshape(1, D_MODEL)]
    out_shape = [jax.ShapeDtypeStruct((geo.T, D_MODEL), F32)]
    out_specs = [rows]
    if post == "next":
        in_specs += [_mod_spec(0, tt, geo), _mod_spec(1, tt, geo)]
        args += [next_mod3, next_mod3]
        out_shape.append(jax.ShapeDtypeStruct((geo.T, D_MODEL), BF16))
        out_specs.append(rows)
    in_specs.append(pl.BlockSpec(memory_space=pl.ANY))
    args.append(ys)
    return pl.pallas_call(
        functools.partial(_combine_kernel, tt=tt, post=post),
        name="combine",
        out_shape=out_shape,
        grid=(n,),
        in_specs=in_specs,
        out_specs=out_specs,
        scratch_shapes=[pltpu.VMEM((2, TOP_K, tt, D_MODEL // 2), jnp.uint32),
                        pltpu.SemaphoreType.DMA((2,))],
        compiler_params=_params("arbitrary"),
    )(*args)


def _moe(x, h_packed, mod3, rw, rb, wg, wu, wd, layer, sg, su, sd, post, norm_g, next_mod3, geo):
    eidx, rank, w, counts = _router(h_packed, rw, rb, geo)
    counts = counts.reshape(N_EXPERTS).astype(jnp.int32)
    dest, blk_e = _dest(counts, eidx, rank, geo)
    xs = _dispatch(counts, dest, h_packed, geo)
    ys = _experts(blk_e, xs, wg, wu, wd, layer, geo)
    return _combine(dest, w, x, h_packed, sg.astype(BF16), su.astype(BF16), sd.astype(BF16), ys,
                    mod3, 5, post, norm_g, next_mod3, geo)


def kernel(x, c, ctx, c_ctx, ada_w, ada_b, norm_mix, norm_ffn, norm_final, ev_w_in, ev_w_out, hgrn_lb, hgrn_norm, conv_w, conv_b, conv_norm_g, conv_norm_b, ret_w_in, ret_w_out, ret_decay, router_w, router_b, exp_gate, exp_up, exp_down, sh_gate, sh_up, sh_down):
    b, n, d = x.shape
    lc = ctx.shape[1]
    depth = ada_w.shape[0]
    geo = _geo(b, n, lc)
    assert d == D_MODEL and b < MOD_ROWS
    assert n % RET_CHUNK == 0 and lc % RET_CHUNK == 0

    xs = jnp.concatenate([x.reshape(geo.BN, d), ctx.reshape(geo.BL, d)], axis=0)
    cond = jnp.zeros((MOD_ROWS, d), F32).at[:b].set(c).at[b].set(c_ctx)
    cos_tab, sin_tab = _rope_tables(n)

    mods = [_adaln(cond, ada_w, ada_b, l).reshape(MOD_ROWS * N_MOD, 1, d) for l in range(depth)]
    h = _normmod(xs, norm_mix[0], mods[0], 0, 1, geo, packed=False)
    for l in range(depth):
        j = l // 2
        last = l == depth - 1
        mod3 = mods[l]
        tail = geo._replace(T=geo.BN, BL=0, Lc=0) if last else geo
        if l % 2 == 0:
            p = _matmul(h, ev_w_in[j].astype(BF16), geo)
            of, ob = _hgrn_scan(p, hgrn_lb, l, geo)
            mix = _ev_readout(p, of, ob, hgrn_norm[j], conv_w[j], conv_b[j],
                              conv_norm_g[j], conv_norm_b[j], geo)
            xs = _matmul_resid(mix, ev_w_out[j].astype(BF16), xs, mod3, 2, tail)
        else:
            p = _matmul(h, ret_w_in[j].astype(BF16), geo)
            of, ob = _ret_scan(p, cos_tab, sin_tab, ret_decay[j], geo)
            mix = _ret_readout(p, of, ob, geo)
            xs = _matmul_resid(mix, ret_w_out[j].astype(BF16), xs, mod3, 2, tail)
        h_packed = _normmod(xs, norm_ffn[l], mod3, 3, 4, tail, packed=True)
        moe_w = (router_w[l], router_b[l], exp_gate, exp_up, exp_down, l, sh_gate[l], sh_up[l], sh_down[l])
        if last:
            (out,) = _moe(xs, h_packed, mod3, *moe_w, "final", norm_final, None, tail)
        else:
            xs, h = _moe(xs, h_packed, mod3, *moe_w, "next", norm_mix[l + 1], mods[l + 1], tail)
    return out.reshape(b, n, d)
```

```python
import collections
import functools

import jax
import jax.numpy as jnp
from jax import lax
from jax.experimental import pallas as pl
from jax.experimental.pallas import tpu as pltpu
from jax.experimental.pallas import tpu_sc as plsc

F32 = jnp.float32
BF16 = jnp.bfloat16

D_MODEL = 2048
N_MOD = 6
EPS = 1e-6
GRID_W = 64
ROPE_BASE = 10000.0

HG_HEADS = 8
HG_DK = 128
HG_DV = 128
HG_F = HG_HEADS * HG_DK
HG_V = HG_HEADS * HG_DV
CONV_C = D_MODEL // 2
CONV_W = 31
CONV_HALO = 16
HG_CHUNK = 128

RET_HEADS = 8
RET_DK = D_MODEL // RET_HEADS
RET_DV = 2 * RET_DK
RET_QK = RET_HEADS * RET_DK
RET_V = RET_HEADS * RET_DV
RET_CHUNK = 256

N_EXPERTS = 64
EXPERT_FF = D_MODEL // 4
TOP_K = 8
N_GROUPS = 8
GROUP_SIZE = N_EXPERTS // N_GROUPS
TOPK_GROUPS = 4
ROUTED_SCALE = 2.5
MOE_BLOCK = 512

V7X_SC_CORES = 2
V7X_SC_SUBCORES = 16
SC_ROWS = 64
MOD_ROWS = 16
VMEM_LIMIT = 56 * 1024 * 1024

Geo = collections.namedtuple("Geo", "B N Lc BN BL T")


def _geo(b, n, lc):
    return Geo(b, n, lc, b * n, b * lc, b * n + b * lc)


def _pick_tile(geo, cands):
    for t in cands:
        if geo.N % t == 0 and geo.BL % t == 0:
            return t
    raise ValueError("no row tile fits the sequence lengths")


def _mod_row(i, tm, geo):
    return jnp.where(i < geo.BN // tm, i // (geo.N // tm), geo.B)


def _mod_spec(m, tm, geo, ngrid=1):
    if ngrid == 1:
        return pl.BlockSpec((1, 1, D_MODEL), lambda i: (_mod_row(i, tm, geo) * N_MOD + m, 0, 0))
    return pl.BlockSpec((1, 1, D_MODEL), lambda i, j: (_mod_row(i, tm, geo) * N_MOD + m, 0, j))


def _params(*sem):
    return pltpu.CompilerParams(dimension_semantics=sem, vmem_limit_bytes=VMEM_LIMIT)


def _sigmoid(x):
    return jax.nn.sigmoid(x)


def _silu(x):
    return x * jax.nn.sigmoid(x)


def _adaln_kernel(c_ref, w_ref, b_ref, o_ref):
    a = _silu(c_ref[...]).astype(BF16)
    o_ref[...] = jnp.dot(a, w_ref[...].astype(BF16), preferred_element_type=F32) + b_ref[...]


def _adaln(cond, w, b, layer):
    depth, k, n = w.shape
    tn = 1024
    return pl.pallas_call(
        _adaln_kernel,
        name="adaln",
        out_shape=jax.ShapeDtypeStruct((MOD_ROWS, n), F32),
        grid=(n // tn,),
        in_specs=[pl.BlockSpec((MOD_ROWS, k), lambda j: (0, 0)),
                  pl.BlockSpec((None, k, tn), lambda j: (layer, 0, j)),
                  pl.BlockSpec((None, 1, tn), lambda j: (layer, 0, j))],
        out_specs=pl.BlockSpec((MOD_ROWS, tn), lambda j: (0, j)),
        compiler_params=_params("arbitrary"),
    )(cond, w, b.reshape(depth, 1, n))


def _pack_rows(x):
    n = x.shape[-1] // 2
    bits = lax.bitcast_convert_type(x.astype(BF16).astype(F32), jnp.uint32)
    return (bits[:, n:] & jnp.uint32(0xFFFF0000)) | (bits[:, :n] >> 16)


def _unpack_rows(u):
    lo = lax.bitcast_convert_type(u << 16, F32)
    hi = lax.bitcast_convert_type(u & jnp.uint32(0xFFFF0000), F32)
    return lo, hi


def _normmod_kernel(x_ref, g_ref, sh_ref, sc_ref, o_ref, *, packed):
    x = x_ref[...]
    y = x * lax.rsqrt(jnp.mean(x * x, axis=-1, keepdims=True) + EPS) * g_ref[...]
    h = y * (1.0 + sc_ref[0]) + sh_ref[0]
    o_ref[...] = _pack_rows(h) if packed else h.astype(BF16)


def _normmod(x, g, mod3, m_shift, m_scale, geo, packed):
    tm = _pick_tile(geo, (256, 128))
    spec = pl.BlockSpec((tm, D_MODEL), lambda i: (i, 0))
    if packed:
        out_shape = jax.ShapeDtypeStruct((geo.T, D_MODEL // 2), jnp.uint32)
        out_spec = pl.BlockSpec((tm, D_MODEL // 2), lambda i: (i, 0))
    else:
        out_shape = jax.ShapeDtypeStruct((geo.T, D_MODEL), BF16)
        out_spec = spec
    return pl.pallas_call(
        functools.partial(_normmod_kernel, packed=packed),
        name="normmod",
        out_shape=out_shape,
        grid=(geo.T // tm,),
        in_specs=[spec, pl.BlockSpec((1, D_MODEL), lambda i: (0, 0)),
                  _mod_spec(m_shift, tm, geo), _mod_spec(m_scale, tm, geo)],
        out_specs=out_spec,
        compiler_params=_params("arbitrary"),
    )(x, g.reshape(1, D_MODEL), mod3, mod3)


def _mm_kernel(a_ref, w_ref, o_ref):
    o_ref[...] = jnp.dot(a_ref[...], w_ref[...], preferred_element_type=F32).astype(o_ref.dtype)


def _mm_resid_kernel(a_ref, w_ref, x_ref, m_ref, o_ref):
    y = jnp.dot(a_ref[...], w_ref[...], preferred_element_type=F32)
    o_ref[...] = x_ref[...] + m_ref[0] * y


def _matmul(a, w, geo):
    k, n = w.shape
    tm = _pick_tile(geo, (1024, 512, 256, 128))
    tn = 512
    return pl.pallas_call(
        _mm_kernel,
        name="matmul",
        out_shape=jax.ShapeDtypeStruct((geo.T, n), BF16),
        grid=(geo.T // tm, n // tn),
        in_specs=[pl.BlockSpec((tm, k), lambda i, j: (i, 0)),
                  pl.BlockSpec((k, tn), lambda i, j: (0, j))],
        out_specs=pl.BlockSpec((tm, tn), lambda i, j: (i, j)),
        compiler_params=_params("arbitrary", "arbitrary"),
    )(a, w)


def _matmul_resid(a, w, x, mod3, m_gate, geo):
    k, n = w.shape
    tm = _pick_tile(geo, (1024, 512, 256, 128))
    tn = 512
    return pl.pallas_call(
        _mm_resid_kernel,
        name="matmul_resid",
        out_shape=jax.ShapeDtypeStruct((geo.T, n), F32),
        grid=(geo.T // tm, n // tn),
        in_specs=[pl.BlockSpec((tm, k), lambda i, j: (i, 0)),
                  pl.BlockSpec((k, tn), lambda i, j: (0, j)),
                  pl.BlockSpec((tm, tn), lambda i, j: (i, j)),
                  pl.BlockSpec((1, 1, tn), lambda i, j: (_mod_row(i, tm, geo) * N_MOD + m_gate, 0, j))],
        out_specs=pl.BlockSpec((tm, tn), lambda i, j: (i, j)),
        compiler_params=_params("arbitrary", "arbitrary"),
    )(a, w, x, mod3)


def _chunk_index(b, i, chunk, geo, reverse):
    nc = geo.Lc // chunk
    nl = geo.N // chunk
    ctx0 = (geo.BN + b * geo.Lc) // chunk
    lat0 = (b * geo.N) // chunk
    if reverse:
        return jnp.where(i < nc, ctx0 + (nc - 1 - i), lat0 + (nl - 1 - (i - nc)))
    return jnp.where(i < nc, ctx0 + i, lat0 + (i - nc))


def _split_dot(tri_bf, x):
    hi = x.astype(BF16)
    r1 = x - hi.astype(F32)
    mid = r1.astype(BF16)
    lo = (r1 - mid.astype(F32)).astype(BF16)
    return (jnp.dot(tri_bf, hi, preferred_element_type=F32)
            + jnp.dot(tri_bf, mid, preferred_element_type=F32)
            + jnp.dot(tri_bf, lo, preferred_element_type=F32))


def _hgrn_kernel(qf_ref, ff_ref, vf_ref, qb_ref, fb_ref, vb_ref, lbp_ref, of_ref, ob_ref,
                 stf_ref, stb_ref, *, layer):
    i = pl.program_id(1)

    @pl.when(i == 0)
    def _():
        stf_ref[...] = jnp.zeros_like(stf_ref)
        stb_ref[...] = jnp.zeros_like(stb_ref)

    lbp = lbp_ref[...]
    e = jnp.exp(lbp - jnp.max(lbp, axis=0, keepdims=True))
    sm = e / jnp.sum(e, axis=0, keepdims=True)
    lb = sm[0:1]
    for r in range(1, layer + 1):
        lb = lb + sm[r:r + 1]

    _hgrn_chunk(qf_ref, ff_ref, vf_ref, of_ref, stf_ref, lb, reverse=False)
    _hgrn_chunk(qb_ref, fb_ref, vb_ref, ob_ref, stb_ref, lb, reverse=True)


def _hgrn_chunk(q_ref, f_ref, v_ref, o_ref, st_ref, lb, *, reverse):
    c = HG_CHUNK
    row = lax.broadcasted_iota(jnp.int32, (c, c), 0)
    col = lax.broadcasted_iota(jnp.int32, (c, c), 1)
    tri = (col >= row) if reverse else (col <= row)
    tri_bf = jnp.where(tri, 1.0, 0.0).astype(BF16)
    nt = (((1,), (1,)), ((), ()))

    for h in range(HG_HEADS):
        sl = slice(h * HG_DK, (h + 1) * HG_DK)
        qh = _silu(q_ref[:, sl].astype(F32))
        lbh = lb[:, sl]
        fg = lbh + (1.0 - lbh) * _sigmoid(f_ref[:, sl].astype(F32))
        kh = 1.0 - fg
        bcum = _split_dot(tri_bf, jnp.log(fg))
        bmid = bcum[c // 2:c // 2 + 1]
        bend = bcum[0:1] if reverse else bcum[c - 1:c]
        vb = v_ref[:, sl]
        vh = vb.astype(F32)
        a = (qh * jnp.exp(bcum - bmid)).astype(BF16)
        kd = (kh * jnp.exp(bmid - bcum)).astype(BF16)
        s = lax.dot_general(a, kd, nt, preferred_element_type=F32)
        s = jnp.where(tri, s, 0.0)
        intra = jnp.dot(s.astype(BF16), vb, preferred_element_type=F32)
        st = st_ref[h]
        inter = lax.dot_general((qh * jnp.exp(bcum)).astype(BF16), st.astype(BF16), nt,
                                preferred_element_type=F32)
        o_ref[:, sl] = (inter + intra).astype(BF16)
        kd2 = (kh * jnp.exp(bend - bcum)).astype(BF16)
        st_ref[h] = st * jnp.exp(bend) + jnp.dot(vh.T.astype(BF16), kd2, preferred_element_type=F32)


def _hgrn_scan(p, lb_param, layer, geo):
    c = HG_CHUNK
    steps = (geo.Lc + geo.N) // c

    def spec(colblk, reverse):
        return pl.BlockSpec((c, HG_F), lambda b, i: (_chunk_index(b, i, c, geo, reverse), colblk))

    out = jax.ShapeDtypeStruct((geo.T, HG_V), BF16)
    state = pltpu.VMEM((HG_HEADS, HG_DV, HG_DK), F32)
    return pl.pallas_call(
        functools.partial(_hgrn_kernel, layer=layer),
        name="hgrn_scan",
        out_shape=[out, out],
        grid=(geo.B, steps),
        in_specs=[spec(0, False), spec(1, False), spec(3, False),
                  spec(0, True), spec(2, True), spec(3, True),
                  pl.BlockSpec(lb_param.shape, lambda b, i: (0, 0))],
        out_specs=[spec(0, False), spec(0, True)],
        scratch_shapes=[state, state],
        compiler_params=_params("arbitrary", "arbitrary"),
    )(p, p, p, p, p, p, lb_param)


def _ev_readout_kernel(of_ref, ob_ref, gate_ref, a_ref, b_ref, ap_ref, bp_ref, an_ref, bn_ref,
                       gain_ref, cw_ref, cb_ref, lng_ref, lnb_ref, o_ref, ext_ref, conv_ref, *, tm, geo):
    i = pl.program_id(0)
    n_lat = geo.BN // tm
    tpl = geo.N // tm
    tpc = geo.Lc // tm
    j = jnp.where(i < n_lat, i % tpl, (i - n_lat) % tpc)
    per = jnp.where(i < n_lat, tpl, tpc)
    keep_prev = jnp.where(j == 0, 0.0, 1.0)
    keep_next = jnp.where(j == per - 1, 0.0, 1.0)

    o = of_ref[...].astype(F32) + ob_ref[...].astype(F32)
    r = o * lax.rsqrt(jnp.mean(o * o, axis=-1, keepdims=True) + EPS) * gain_ref[...]
    o_ref[:, :HG_V] = (r * _silu(gate_ref[...].astype(F32))).astype(BF16)

    def glu(x_ref, y_ref):
        return x_ref[...].astype(F32) * _sigmoid(y_ref[...].astype(F32))

    ext_ref[0:CONV_HALO] = glu(ap_ref, bp_ref) * keep_prev
    ext_ref[CONV_HALO:CONV_HALO + tm] = glu(a_ref, b_ref)
    ext_ref[CONV_HALO + tm:2 * CONV_HALO + tm] = glu(an_ref, bn_ref) * keep_next
    off = CONV_HALO - CONV_W // 2
    for cj in range(CONV_C // 128):
        cs = slice(cj * 128, (cj + 1) * 128)
        acc = jnp.broadcast_to(cb_ref[:, cs], (tm, 128))
        for k in range(CONV_W):
            acc = acc + cw_ref[k:k + 1, cs] * ext_ref[off + k:off + k + tm, cs]
        conv_ref[:, cs] = acc
    acc = conv_ref[...]
    mu = jnp.mean(acc, axis=-1, keepdims=True)
    xc = acc - mu
    var = jnp.mean(xc * xc, axis=-1, keepdims=True)
    u = xc * lax.rsqrt(var + EPS) * lng_ref[...] + lnb_ref[...]
    o_ref[:, HG_V:] = _silu(u).astype(BF16)


def _ev_readout(p, of, ob, gain, cw, cb, lng, lnb, geo):
    tm = _pick_tile(geo, (128,))
    hb = tm // CONV_HALO
    nhalo = geo.T // CONV_HALO

    def row(colblk):
        return pl.BlockSpec((tm, HG_V), lambda i: (i, colblk))

    def prev(colblk):
        return pl.BlockSpec((CONV_HALO, CONV_C), lambda i: (jnp.maximum(i * hb - 1, 0), colblk))

    def nxt(colblk):
        return pl.BlockSpec((CONV_HALO, CONV_C), lambda i: (jnp.minimum((i + 1) * hb, nhalo - 1), colblk))

    def vec(n):
        return pl.BlockSpec((n, CONV_C), lambda i: (0, 0))

    return pl.pallas_call(
        functools.partial(_ev_readout_kernel, tm=tm, geo=geo),
        name="ev_readout",
        out_shape=jax.ShapeDtypeStruct((geo.T, HG_V + CONV_C), BF16),
        grid=(geo.T // tm,),
        in_specs=[row(0), row(0), row(4), row(5), row(6), prev(5), prev(6), nxt(5), nxt(6),
                  vec(1), vec(CONV_W), vec(1), vec(1), vec(1)],
        out_specs=pl.BlockSpec((tm, HG_V + CONV_C), lambda i: (i, 0)),
        scratch_shapes=[pltpu.VMEM((tm + 2 * CONV_HALO, CONV_C), F32), pltpu.VMEM((tm, CONV_C), F32)],
        compiler_params=_params("arbitrary"),
    )(of, ob, p, p, p, p, p, p, p, gain.reshape(1, -1), cw, cb.reshape(1, -1),
      lng.reshape(1, -1), lnb.reshape(1, -1))


def _rope(x, cos, sin_signed):
    half = x.shape[-1] // 2
    rot = jnp.concatenate([pltpu.roll(x[:, :half], half // 2, axis=1),
                           pltpu.roll(x[:, half:], half // 2, axis=1)], axis=-1)
    return x * cos + rot * sin_signed


def _ret_kernel(qf_ref, kf_ref, vf_ref, cosf_ref, sinf_ref, qb_ref, kb_ref, vb_ref, cosb_ref, sinb_ref,
                dl_ref, of_ref, ob_ref, sf_ref, sb_ref):
    i = pl.program_id(1)

    @pl.when(i == 0)
    def _():
        sf_ref[...] = jnp.zeros_like(sf_ref)
        sb_ref[...] = jnp.zeros_like(sb_ref)

    _ret_chunk(qf_ref, kf_ref, vf_ref, cosf_ref, sinf_ref, dl_ref[0], of_ref, sf_ref, reverse=False)
    _ret_chunk(qb_ref, kb_ref, vb_ref, cosb_ref, sinb_ref, dl_ref[1], ob_ref, sb_ref, reverse=True)


def _ret_chunk(q_ref, k_ref, v_ref, cos_ref, sin_ref, dl, o_ref, s_ref, *, reverse):
    c = RET_CHUNK
    lg_all = -jnp.log1p(jnp.exp(-dl))
    row = lax.broadcasted_iota(jnp.int32, (c, c), 0)
    col = lax.broadcasted_iota(jnp.int32, (c, c), 1)
    idx = lax.broadcasted_iota(jnp.int32, (c, 1), 0).astype(F32)
    if reverse:
        live = col >= row
        dist = (col - row).astype(F32)
        q_pow = c - idx
        k_pow = idx
    else:
        live = row >= col
        dist = (row - col).astype(F32)
        q_pow = idx + 1.0
        k_pow = c - 1.0 - idx
    cos = cos_ref[...]
    sin = sin_ref[...]
    nt = (((1,), (1,)), ((), ()))

    for h in range(RET_HEADS):
        lg = lg_all[h:h + 1]
        dmask = jnp.where(live, jnp.exp(lg * dist), 0.0)
        q = _rope(q_ref[:, h * RET_DK:(h + 1) * RET_DK].astype(F32), cos, sin)
        k = _rope(k_ref[:, h * RET_DK:(h + 1) * RET_DK].astype(F32) * (RET_DK ** -0.5), cos, sin)
        vb = v_ref[:, h * RET_DV:(h + 1) * RET_DV]
        scores = lax.dot_general(q.astype(BF16), k.astype(BF16), nt, preferred_element_type=F32) * dmask
        intra = jnp.dot(scores.astype(BF16), vb, preferred_element_type=F32)
        s = s_ref[h]
        inter = jnp.dot((q * jnp.exp(lg * q_pow)).astype(BF16), s.astype(BF16),
                        preferred_element_type=F32)
        o_ref[:, h * RET_DV:(h + 1) * RET_DV] = (inter + intra).astype(BF16)
        kdec = (k * jnp.exp(lg * k_pow)).T.astype(BF16)
        s_ref[h] = jnp.exp(lg * c) * s + jnp.dot(kdec, vb, preferred_element_type=F32)


def _ret_scan(p, cos_tab, sin_tab, decay_logit, geo):
    c = RET_CHUNK
    steps = (geo.Lc + geo.N) // c
    nc = geo.Lc // c
    nl = geo.N // c

    def spec(width, colblk, reverse):
        return pl.BlockSpec((c, width), lambda b, i: (_chunk_index(b, i, c, geo, reverse), colblk))

    def tab(reverse):
        def index(b, i):
            lat = (nl - 1 - (i - nc)) if reverse else (i - nc)
            return (jnp.where(i < nc, nl, lat), 0)
        return pl.BlockSpec((c, RET_DK), index)

    def direction(reverse):
        return [spec(RET_QK, 0, reverse), spec(RET_QK, 1, reverse), spec(RET_V, 1, reverse),
                tab(reverse), tab(reverse)]

    out = jax.ShapeDtypeStruct((geo.T, RET_V), BF16)
    state = pltpu.VMEM((RET_HEADS, RET_DK, RET_DV), F32)
    return pl.pallas_call(
        _ret_kernel,
        name="ret_scan",
        out_shape=[out, out],
        grid=(geo.B, steps),
        in_specs=direction(False) + direction(True)
        + [pl.BlockSpec((2, RET_HEADS, 1), lambda b, i: (0, 0, 0))],
        out_specs=[spec(RET_V, 0, False), spec(RET_V, 0, True)],
        scratch_shapes=[state, state],
        compiler_params=_params("arbitrary", "arbitrary"),
    )(p, p, p, cos_tab, sin_tab, p, p, p, cos_tab, sin_tab, decay_logit.reshape(2, RET_HEADS, 1))


def _rope_tables(n):
    t = jnp.arange(n)
    quarter = RET_DK // 4
    inv = 1.0 / (ROPE_BASE ** (jnp.arange(quarter, dtype=F32) / quarter))
    ang_r = (t // GRID_W).astype(F32)[:, None] * inv
    ang_c = (t % GRID_W).astype(F32)[:, None] * inv
    cos = jnp.concatenate([jnp.cos(ang_r), jnp.cos(ang_r), jnp.cos(ang_c), jnp.cos(ang_c)], axis=-1)
    sin = jnp.concatenate([-jnp.sin(ang_r), jnp.sin(ang_r), -jnp.sin(ang_c), jnp.sin(ang_c)], axis=-1)
    cos = jnp.concatenate([cos, jnp.ones((RET_CHUNK, RET_DK), F32)], axis=0)
    sin = jnp.concatenate([sin, jnp.zeros((RET_CHUNK, RET_DK), F32)], axis=0)
    return cos, sin


def _ret_readout_kernel(of_ref, ob_ref, gate_ref, o_ref):
    for h in range(RET_HEADS):
        sl = slice(h * RET_DV, (h + 1) * RET_DV)
        o = of_ref[:, sl].astype(F32) + ob_ref[:, sl].astype(F32)
        r = o * lax.rsqrt(jnp.mean(o * o, axis=-1, keepdims=True) + EPS)
        o_ref[:, sl] = (_silu(gate_ref[:, sl].astype(F32)) * r).astype(BF16)


def _ret_readout(p, of, ob, geo):
    tm = _pick_tile(geo, (128,))
    spec = pl.BlockSpec((tm, RET_V), lambda i: (i, 0))
    return pl.pallas_call(
        _ret_readout_kernel,
        name="ret_readout",
        out_shape=jax.ShapeDtypeStruct((geo.T, RET_V), BF16),
        grid=(geo.T // tm,),
        in_specs=[spec, spec, pl.BlockSpec((tm, RET_V), lambda i: (i, 2))],
        out_specs=spec,
        compiler_params=_params("arbitrary"),
    )(of, ob, p)


def _router_kernel(h_ref, rw_ref, rb_ref, eidx_ref, rank_ref, w_ref, cnt_ref, carry_ref, *, tm):
    i = pl.program_id(0)

    @pl.when(i == 0)
    def _():
        carry_ref[...] = jnp.zeros_like(carry_ref)

    half = D_MODEL // 2
    h_lo, h_hi = _unpack_rows(h_ref[...])
    h_lo = h_lo.astype(BF16)
    h_hi = h_hi.astype(BF16)
    rest = rw_ref[...]
    logits = jnp.zeros((tm, N_EXPERTS), F32)
    for _ in range(3):
        part = rest.astype(BF16)
        rest = rest - part.astype(F32)
        logits = (logits + jnp.dot(h_lo, part[:half], preferred_element_type=F32)
                  + jnp.dot(h_hi, part[half:], preferred_element_type=F32))
    s = _sigmoid(logits)
    sel = s + rb_ref[...]
    lane = lax.broadcasted_iota(jnp.int32, (tm, N_EXPERTS), 1).astype(F32)
    grp = jnp.floor(lane * (1.0 / GROUP_SIZE))
    ninf = -jnp.inf
    none = float(N_EXPERTS)

    gscore = jnp.zeros((tm, N_EXPERTS), F32)
    gcols = []
    for g in range(N_GROUPS):
        in_g = grp == float(g)
        v1 = jnp.max(jnp.where(in_g, sel, ninf), axis=-1, keepdims=True)
        i1 = jnp.min(jnp.where(in_g & (sel == v1), lane, none), axis=-1, keepdims=True)
        v2 = jnp.max(jnp.where(in_g & (lane != i1), sel, ninf), axis=-1, keepdims=True)
        gcols.append(v1 + v2)
        gscore = jnp.where(in_g, v1 + v2, gscore)
    beaten = jnp.zeros((tm, N_EXPERTS), F32)
    for g in range(N_GROUPS):
        wins = (gcols[g] > gscore) | ((gcols[g] == gscore) & (float(g) < grp))
        beaten = beaten + jnp.where(wins, 1.0, 0.0)
    cand = jnp.where(beaten < float(TOPK_GROUPS), sel, ninf)

    lane_k = lax.broadcasted_iota(jnp.int32, (tm, TOP_K), 1)
    eidx = jnp.zeros((tm, TOP_K), F32)
    wsel = jnp.zeros((tm, TOP_K), F32)
    chosen = jnp.zeros((tm, N_EXPERTS), F32)
    picks = []
    for k in range(TOP_K):
        v = jnp.max(cand, axis=-1, keepdims=True)
        ik = jnp.min(jnp.where(cand == v, lane, none), axis=-1, keepdims=True)
        hit = lane == ik
        picks.append(ik)
        eidx = jnp.where(lane_k == k, ik, eidx)
        wsel = jnp.where(lane_k == k, jnp.sum(jnp.where(hit, s, 0.0), axis=-1, keepdims=True), wsel)
        chosen = jnp.where(hit, 1.0, chosen)
        cand = jnp.where(hit, ninf, cand)
    w_ref[...] = wsel / jnp.sum(wsel, axis=-1, keepdims=True) * ROUTED_SCALE
    eidx_ref[...] = eidx.astype(jnp.int32)

    r = lax.broadcasted_iota(jnp.int32, (tm, tm), 0)
    c = lax.broadcasted_iota(jnp.int32, (tm, tm), 1)
    below = jnp.where(c < r, 1.0, 0.0).astype(BF16)
    carry = carry_ref[...]
    pos = jnp.dot(below, chosen.astype(BF16), preferred_element_type=F32) + carry
    rank = jnp.zeros((tm, TOP_K), jnp.int32)
    for k in range(TOP_K):
        rk = jnp.sum(jnp.where(lane == picks[k], pos, 0.0), axis=-1, keepdims=True)
        rank = jnp.where(lane_k == k, rk.astype(jnp.int32), rank)
    rank_ref[...] = rank
    carry = carry + jnp.sum(chosen, axis=0, keepdims=True)
    carry_ref[...] = carry
    cnt_ref[...] = carry


def _router(h_packed, rw, rb, geo):
    tm = 256 if geo.T % 256 == 0 else 128
    tok = pl.BlockSpec((tm, TOP_K), lambda i: (i, 0))
    one = pl.BlockSpec((1, N_EXPERTS), lambda i: (0, 0))
    return pl.pallas_call(
        functools.partial(_router_kernel, tm=tm),
        name="router",
        out_shape=[jax.ShapeDtypeStruct((geo.T, TOP_K), jnp.int32),
                   jax.ShapeDtypeStruct((geo.T, TOP_K), jnp.int32),
                   jax.ShapeDtypeStruct((geo.T, TOP_K), F32),
                   jax.ShapeDtypeStruct((1, N_EXPERTS), F32)],
        grid=(geo.T // tm,),
        in_specs=[pl.BlockSpec((tm, D_MODEL // 2), lambda i: (i, 0)),
                  pl.BlockSpec((D_MODEL, N_EXPERTS), lambda i: (0, 0)), one],
        out_specs=[tok, tok, tok, one],
        scratch_shapes=[pltpu.VMEM((1, N_EXPERTS), F32)],
        compiler_params=_params("arbitrary"),
    )(h_packed, rw, rb.reshape(1, N_EXPERTS))


def _n_blocks(geo):
    return -(-(geo.T * TOP_K) // MOE_BLOCK) + N_EXPERTS


def _dest_kernel(cnt_ref, eidx_ref, rank_ref, dest_ref, blk_ref, *, n_blocks):
    eidx = eidx_ref[...]
    dest = rank_ref[...]
    blk_row = (lax.broadcasted_iota(jnp.int32, blk_ref.shape, 0) * 128
               + lax.broadcasted_iota(jnp.int32, blk_ref.shape, 1)) * MOE_BLOCK
    blk = jnp.zeros(blk_ref.shape, jnp.int32)
    start = jnp.int32(0)
    for e in range(N_EXPERTS):
        padded = (cnt_ref[e] + (MOE_BLOCK - 1)) // MOE_BLOCK * MOE_BLOCK
        dest = dest + jnp.where(eidx == e, start, 0)
        start = start + padded
        blk = blk + jnp.where(start <= blk_row, 1, 0)
    dest_ref[...] = dest
    blk_ref[...] = jnp.where(blk_row == n_blocks * MOE_BLOCK, start // MOE_BLOCK,
                             jnp.minimum(blk, N_EXPERTS - 1))


def _dest(counts, eidx, rank, geo):
    rows = geo.T * TOP_K // 128
    brow = -(-(_n_blocks(geo) + 1) // 128)
    full = pl.BlockSpec((rows, 128), lambda: (0, 0))
    dest, blk = pl.pallas_call(
        functools.partial(_dest_kernel, n_blocks=_n_blocks(geo)),
        name="dest",
        out_shape=[jax.ShapeDtypeStruct((rows, 128), jnp.int32),
                   jax.ShapeDtypeStruct((brow, 128), jnp.int32)],
        in_specs=[pl.BlockSpec(memory_space=pltpu.SMEM), full, full],
        out_specs=[full, pl.BlockSpec((brow, 128), lambda: (0, 0))],
    )(counts, eidx.reshape(rows, 128), rank.reshape(rows, 128))
    return dest.reshape(-1), blk.reshape(-1)[:_n_blocks(geo) + 1]


def _row_copy(src_hbm, s, dst_hbm, d, sem):
    return pltpu.make_async_copy(src_hbm.at[pl.ds(s, 1)], dst_hbm.at[pl.ds(d, 1)], sem)


def _zero_fill(cnt_ref, xs_hbm, zero_ref, zsem, n_rows, wait):
    def piece(pos, size):
        if size >= 8:
            copies = [(pl.multiple_of(pos, 8), size)]
        else:
            copies = [(pos + r, 1) for r in range(size)]
        for p, s in copies:
            cp = pltpu.make_async_copy(zero_ref.at[pl.ds(0, s)], xs_hbm.at[pl.ds(p, s)], zsem)
            cp.wait() if wait else cp.start()

    def per_expert(e, start):
        cnt = cnt_ref[e]
        padded = (cnt + (MOE_BLOCK - 1)) // MOE_BLOCK * MOE_BLOCK
        pad = padded - cnt
        pos = start + cnt
        size = 1
        while size < MOE_BLOCK:
            take = (pad & size) != 0
            pl.when(take)(functools.partial(piece, pos, size))
            pos = pos + jnp.where(take, size, 0)
            size *= 2
        return start + padded

    end = lax.fori_loop(0, N_EXPERTS, per_expert, jnp.int32(0))

    def per_block(j, carry):
        piece(end + j * MOE_BLOCK, MOE_BLOCK)
        return carry

    lax.fori_loop(0, (n_rows - end) // MOE_BLOCK, per_block, 0)


def _dispatch_kernel(cnt_ref, dest_ref, h_ref, xs_hbm, zero_ref, sem, zsem, *, tt, n_rows):
    i = pl.program_id(0)

    @pl.when(i == 0)
    def _():
        zero_ref[...] = jnp.zeros_like(zero_ref)
        _zero_fill(cnt_ref, xs_hbm, zero_ref, zsem, n_rows, wait=False)
        _zero_fill(cnt_ref, xs_hbm, zero_ref, zsem, n_rows, wait=True)

    def issue(t, carry):
        for k in range(TOP_K):
            _row_copy(h_ref, t, xs_hbm, dest_ref[t * TOP_K + k], sem).start(priority=k % 2)
        return carry

    lax.fori_loop(0, tt, issue, 0)

    def drain(t, carry):
        for k in range(TOP_K):
            _row_copy(h_ref, 0, xs_hbm, 0, sem).wait()
        return carry

    lax.fori_loop(0, tt, drain, 0)


def _dispatch(counts, dest, h_packed, geo):
    tt = 512 if geo.T % 512 == 0 else 128
    n_rows = _n_blocks(geo) * MOE_BLOCK
    width = D_MODEL // 2
    return pl.pallas_call(
        functools.partial(_dispatch_kernel, tt=tt, n_rows=n_rows),
        name="dispatch",
        out_shape=jax.ShapeDtypeStruct((n_rows, width), jnp.uint32),
        grid=(geo.T // tt,),
        in_specs=[pl.BlockSpec(memory_space=pltpu.SMEM),
                  pl.BlockSpec((tt * TOP_K,), lambda i: (i,), memory_space=pltpu.SMEM),
                  pl.BlockSpec((tt, width), lambda i: (i, 0))],
        out_specs=pl.BlockSpec(memory_space=pl.ANY),
        scratch_shapes=[pltpu.VMEM((MOE_BLOCK, width), jnp.uint32), pltpu.SemaphoreType.DMA,
                        pltpu.SemaphoreType.DMA],
        compiler_params=_params("arbitrary"),
    )(counts, dest, h_packed)


def _swiglu_packed(x_ref, wg_ref, wu_ref, wd_ref):
    half = D_MODEL // 2
    lo, hi = _unpack_rows(x_ref[...])
    lo = lo.astype(BF16)
    hi = hi.astype(BF16)

    def proj(w_ref):
        return (jnp.dot(lo, w_ref[:half], preferred_element_type=F32)
                + jnp.dot(hi, w_ref[half:], preferred_element_type=F32))

    g = proj(wg_ref)
    u = proj(wu_ref)
    return jnp.dot((_silu(g) * u).astype(BF16), wd_ref[...], preferred_element_type=F32)


def _expert_kernel(blk_ref, x_ref, wg_ref, wu_ref, wd_ref, o_ref, wgb_ref, wub_ref, wdb_ref, *, nb):
    j = pl.program_id(0)
    n_used = blk_ref[nb]

    @pl.when((j == 0) | (blk_ref[j] != blk_ref[jnp.maximum(j - 1, 0)]))
    def _():
        wgb_ref[...] = wg_ref[...].astype(BF16)
        wub_ref[...] = wu_ref[...].astype(BF16)
        wdb_ref[...] = wd_ref[...].astype(BF16)

    @pl.when(j < n_used)
    def _():
        o_ref[...] = _pack_rows(_swiglu_packed(x_ref, wgb_ref, wub_ref, wdb_ref))

    @pl.when(j >= n_used)
    def _():
        o_ref[...] = jnp.zeros_like(o_ref)


def _experts(blk_e, xs, wg, wu, wd, layer, geo):
    nb = _n_blocks(geo)
    rows = pl.BlockSpec((MOE_BLOCK, D_MODEL // 2), lambda j, be: (j, 0))

    def wspec(r, c):
        return pl.BlockSpec((None, None, r, c), lambda j, be: (layer, be[j], 0, 0))

    return pl.pallas_call(
        functools.partial(_expert_kernel, nb=nb),
        name="experts",
        out_shape=jax.ShapeDtypeStruct((nb * MOE_BLOCK, D_MODEL // 2), jnp.uint32),
        grid_spec=pltpu.PrefetchScalarGridSpec(
            num_scalar_prefetch=1,
            grid=(nb,),
            in_specs=[rows, wspec(D_MODEL, EXPERT_FF), wspec(D_MODEL, EXPERT_FF),
                      wspec(EXPERT_FF, D_MODEL)],
            out_specs=rows,
            scratch_shapes=[pltpu.VMEM((D_MODEL, EXPERT_FF), BF16),
                            pltpu.VMEM((D_MODEL, EXPERT_FF), BF16),
                            pltpu.VMEM((EXPERT_FF, D_MODEL), BF16)]),
        compiler_params=_params("arbitrary"),
    )(blk_e, xs, wg, wu, wd)


def _sc_gather_rows(table, idx):
    nw = V7X_SC_CORES * V7X_SC_SUBCORES
    m = idx.shape[0]
    w = table.shape[1]
    per = m // nw
    assert m % (nw * SC_ROWS) == 0
    mesh = plsc.VectorSubcoreMesh(core_axis_name="c", subcore_axis_name="s")

    @functools.partial(
        pl.kernel, mesh=mesh,
        out_type=jax.ShapeDtypeStruct((m, w), table.dtype),
        scratch_types=[pltpu.VMEM((SC_ROWS,), jnp.int32),
                       pltpu.VMEM((SC_ROWS, w), table.dtype),
                       pltpu.SemaphoreType.DMA],
    )
    def gather(table_hbm, idx_hbm, out_hbm, idx_v, rows_v, sem):
        wid = lax.axis_index("s") * V7X_SC_CORES + lax.axis_index("c")
        base = wid * per

        @pl.loop(0, per // SC_ROWS)
        def _(j):
            off = pl.multiple_of(base + j * SC_ROWS, 8)
            pltpu.sync_copy(idx_hbm.at[pl.ds(off, SC_ROWS)], idx_v)
            pltpu.async_copy(table_hbm.at[idx_v], rows_v, sem).wait()
            pltpu.sync_copy(rows_v, out_hbm.at[pl.ds(off, SC_ROWS)])

    return gather(table, idx)


def _combine_kernel(w_ref, x_ref, h_ref, sg_ref, su_ref, sd_ref, m_ref, *rest, post):
    if post == "next":
        ng_ref, nsh_ref, nsc_ref, g_ref, o_ref, hn_ref = rest
    else:
        ng_ref, g_ref, o_ref = rest
    half = D_MODEL // 2
    shared = _swiglu_packed(h_ref, sg_ref, su_ref, sd_ref)
    w = w_ref[...]
    acc_lo = shared[:, :half]
    acc_hi = shared[:, half:]
    for k in range(TOP_K):
        rows = lax.bitcast_convert_type(g_ref[:, k * half:(k + 1) * half], jnp.uint32)
        lo, hi = _unpack_rows(rows)
        acc_lo = acc_lo + w[:, k:k + 1] * lo
        acc_hi = acc_hi + w[:, k:k + 1] * hi
    gate = m_ref[0]
    y_lo = x_ref[:, :half] + gate[:, :half] * acc_lo
    y_hi = x_ref[:, half:] + gate[:, half:] * acc_hi
    ms = (jnp.sum(y_lo * y_lo, axis=-1, keepdims=True)
          + jnp.sum(y_hi * y_hi, axis=-1, keepdims=True)) * (1.0 / D_MODEL)
    inv = lax.rsqrt(ms + EPS)
    ng = ng_ref[...]
    n_lo = y_lo * inv * ng[:, :half]
    n_hi = y_hi * inv * ng[:, half:]
    if post == "next":
        o_ref[:, :half] = y_lo
        o_ref[:, half:] = y_hi
        sc = nsc_ref[0]
        sft = nsh_ref[0]
        hn_ref[:, :half] = (n_lo * (1.0 + sc[:, :half]) + sft[:, :half]).astype(BF16)
        hn_ref[:, half:] = (n_hi * (1.0 + sc[:, half:]) + sft[:, half:]).astype(BF16)
    else:
        o_ref[:, :half] = n_lo
        o_ref[:, half:] = n_hi


def _combine(dest, w, x, h_packed, sg, su, sd, ys, mod3, m_gate, post, norm_g, next_mod3, geo):
    half = D_MODEL // 2
    gathered = _sc_gather_rows(lax.bitcast_convert_type(ys, jnp.int32), dest)
    gathered = gathered.reshape(geo.T, TOP_K * half)
    tt = _pick_tile(geo, (256, 128))
    n = geo.T // tt
    ff = sg.shape[1]
    rows = pl.BlockSpec((tt, D_MODEL), lambda i: (i, 0))
    vec = pl.BlockSpec((1, D_MODEL), lambda i: (0, 0))
    in_specs = [pl.BlockSpec((tt, TOP_K), lambda i: (i, 0)),
                rows,
                pl.BlockSpec((tt, half), lambda i: (i, 0)),
                pl.BlockSpec((D_MODEL, ff), lambda i: (0, 0)),
                pl.BlockSpec((D_MODEL, ff), lambda i: (0, 0)),
                pl.BlockSpec((ff, D_MODEL), lambda i: (0, 0)),
                _mod_spec(m_gate, tt, geo), vec]
    args = [w, x, h_packed, sg, su, sd, mod3, norm_g.reshape(1, D_MODEL)]
    out_shape = [jax.ShapeDtypeStruct((geo.T, D_MODEL), F32)]
    out_specs = [rows]
    if post == "next":
        in_specs += [_mod_spec(0, tt, geo), _mod_spec(1, tt, geo)]
        args += [next_mod3, next_mod3]
        out_shape.append(jax.ShapeDtypeStruct((geo.T, D_MODEL), BF16))
        out_specs.append(rows)
    in_specs.append(pl.BlockSpec((tt, TOP_K * half), lambda i: (i, 0)))
    args.append(gathered)
    return pl.pallas_call(
        functools.partial(_combine_kernel, post=post),
        name="combine",
        out_shape=out_shape,
        grid=(n,),
        in_specs=in_specs,
        out_specs=out_specs,
        compiler_params=_params("arbitrary"),
    )(*args)


def _moe(x, h_packed, mod3, rw, rb, wg, wu, wd, layer, sg, su, sd, post, norm_g, next_mod3, geo):
    eidx, rank, w, counts = _router(h_packed, rw, rb, geo)
    counts = counts.reshape(N_EXPERTS).astype(jnp.int32)
    dest, blk_e = _dest(counts, eidx, rank, geo)
    xs = _dispatch(counts, dest, h_packed, geo)
    ys = _experts(blk_e, xs, wg, wu, wd, layer, geo)
    return _combine(dest, w, x, h_packed, sg.astype(BF16), su.astype(BF16), sd.astype(BF16), ys,
                    mod3, 5, post, norm_g, next_mod3, geo)


def kernel(x, c, ctx, c_ctx, ada_w, ada_b, norm_mix, norm_ffn, norm_final, ev_w_in, ev_w_out, hgrn_lb, hgrn_norm, conv_w, conv_b, conv_norm_g, conv_norm_b, ret_w_in, ret_w_out, ret_decay, router_w, router_b, exp_gate, exp_up, exp_down, sh_gate, sh_up, sh_down):
    b, n, d = x.shape
    lc = ctx.shape[1]
    depth = ada_w.shape[0]
    geo = _geo(b, n, lc)
    assert d == D_MODEL and b < MOD_ROWS
    assert n % RET_CHUNK == 0 and lc % RET_CHUNK == 0

    xs = jnp.concatenate([x.reshape(geo.BN, d), ctx.reshape(geo.BL, d)], axis=0)
    cond = jnp.zeros((MOD_ROWS, d), F32).at[:b].set(c).at[b].set(c_ctx)
    cos_tab, sin_tab = _rope_tables(n)

    mods = [_adaln(cond, ada_w, ada_b, l).reshape(MOD_ROWS * N_MOD, 1, d) for l in range(depth)]
    h = _normmod(xs, norm_mix[0], mods[0], 0, 1, geo, packed=False)
    for l in range(depth):
        j = l // 2
        last = l == depth - 1
        mod3 = mods[l]
        tail = geo._replace(T=geo.BN, BL=0, Lc=0) if last else geo
        if l % 2 == 0:
            p = _matmul(h, ev_w_in[j].astype(BF16), geo)
            of, ob = _hgrn_scan(p, hgrn_lb, l, geo)
            mix = _ev_readout(p, of, ob, hgrn_norm[j], conv_w[j], conv_b[j],
                              conv_norm_g[j], conv_norm_b[j], geo)
            xs = _matmul_resid(mix, ev_w_out[j].astype(BF16), xs, mod3, 2, tail)
        else:
            p = _matmul(h, ret_w_in[j].astype(BF16), geo)
            of, ob = _ret_scan(p, cos_tab, sin_tab, ret_decay[j], geo)
            mix = _ret_readout(p, of, ob, geo)
            xs = _matmul_resid(mix, ret_w_out[j].astype(BF16), xs, mod3, 2, tail)
        h_packed = _normmod(xs, norm_ffn[l], mod3, 3, 4, tail, packed=True)
        moe_w = (router_w[l], router_b[l], exp_gate, exp_up, exp_down, l, sh_gate[l], sh_up[l], sh_down[l])
        if last:
            (out,) = _moe(xs, h_packed, mod3, *moe_w, "final", norm_final, None, tail)
        else:
            xs, h = _moe(xs, h_packed, mod3, *moe_w, "next", norm_mix[l + 1], mods[l + 1], tail)
    return out.reshape(b, n, d)
```

```python
import collections
import functools

import jax
import jax.numpy as jnp
from jax import lax
from jax.experimental import pallas as pl
from jax.experimental.pallas import tpu as pltpu
from jax.experimental.pallas import tpu_sc as plsc

F32 = jnp.float32
BF16 = jnp.bfloat16

D_MODEL = 2048
N_MOD = 6
EPS = 1e-6
GRID_W = 64
ROPE_BASE = 10000.0

HG_HEADS = 8
HG_DK = 128
HG_DV = 128
HG_F = HG_HEADS * HG_DK
HG_V = HG_HEADS * HG_DV
CONV_C = D_MODEL // 2
CONV_W = 31
CONV_HALO = 16
HG_CHUNK = 128

RET_HEADS = 8
RET_DK = D_MODEL // RET_HEADS
RET_DV = 2 * RET_DK
RET_QK = RET_HEADS * RET_DK
RET_V = RET_HEADS * RET_DV
RET_CHUNK = 256

N_EXPERTS = 64
EXPERT_FF = D_MODEL // 4
TOP_K = 8
N_GROUPS = 8
GROUP_SIZE = N_EXPERTS // N_GROUPS
TOPK_GROUPS = 4
ROUTED_SCALE = 2.5
MOE_BLOCK = 512

V7X_SC_CORES = 2
V7X_SC_SUBCORES = 16
SC_ROWS = 64
MOD_ROWS = 16
VMEM_LIMIT = 56 * 1024 * 1024

Geo = collections.namedtuple("Geo", "B N Lc BN BL T")


def _geo(b, n, lc):
    return Geo(b, n, lc, b * n, b * lc, b * n + b * lc)


def _pick_tile(geo, cands):
    for t in cands:
        if geo.N % t == 0 and geo.BL % t == 0:
            return t
    raise ValueError("no row tile fits the sequence lengths")


def _mod_row(i, tm, geo):
    return jnp.where(i < geo.BN // tm, i // (geo.N // tm), geo.B)


def _mod_spec(m, tm, geo, ngrid=1):
    if ngrid == 1:
        return pl.BlockSpec((1, 1, D_MODEL), lambda i: (_mod_row(i, tm, geo) * N_MOD + m, 0, 0))
    return pl.BlockSpec((1, 1, D_MODEL), lambda i, j: (_mod_row(i, tm, geo) * N_MOD + m, 0, j))


def _params(*sem):
    return pltpu.CompilerParams(dimension_semantics=sem, vmem_limit_bytes=VMEM_LIMIT)


def _sigmoid(x):
    return jax.nn.sigmoid(x)


def _silu(x):
    return x * jax.nn.sigmoid(x)


def _adaln_kernel(c_ref, w_ref, b_ref, o_ref):
    a = _silu(c_ref[...]).astype(BF16)
    o_ref[...] = jnp.dot(a, w_ref[...].astype(BF16), preferred_element_type=F32) + b_ref[...]


def _adaln(cond, w, b, layer):
    depth, k, n = w.shape
    tn = 1024
    return pl.pallas_call(
        _adaln_kernel,
        name="adaln",
        out_shape=jax.ShapeDtypeStruct((MOD_ROWS, n), F32),
        grid=(n // tn,),
        in_specs=[pl.BlockSpec((MOD_ROWS, k), lambda j: (0, 0)),
                  pl.BlockSpec((None, k, tn), lambda j: (layer, 0, j)),
                  pl.BlockSpec((None, 1, tn), lambda j: (layer, 0, j))],
        out_specs=pl.BlockSpec((MOD_ROWS, tn), lambda j: (0, j)),
        compiler_params=_params("arbitrary"),
    )(cond, w, b.reshape(depth, 1, n))


PACKED = jnp.int32


def _pack_rows(x):
    n = x.shape[-1] // 2
    bits = lax.bitcast_convert_type(x.astype(BF16).astype(F32), jnp.uint32)
    words = (bits[:, n:] & jnp.uint32(0xFFFF0000)) | (bits[:, :n] >> 16)
    return lax.bitcast_convert_type(words, PACKED)


def _unpack_rows(p):
    u = lax.bitcast_convert_type(p, jnp.uint32)
    lo = lax.bitcast_convert_type(u << 16, F32)
    hi = lax.bitcast_convert_type(u & jnp.uint32(0xFFFF0000), F32)
    return lo, hi


def _normmod_kernel(x_ref, g_ref, sh_ref, sc_ref, o_ref, *, packed):
    x = x_ref[...]
    y = x * lax.rsqrt(jnp.mean(x * x, axis=-1, keepdims=True) + EPS) * g_ref[...]
    h = y * (1.0 + sc_ref[0]) + sh_ref[0]
    o_ref[...] = _pack_rows(h) if packed else h.astype(BF16)


def _normmod(x, g, mod3, m_shift, m_scale, geo, packed):
    tm = _pick_tile(geo, (256, 128))
    spec = pl.BlockSpec((tm, D_MODEL), lambda i: (i, 0))
    if packed:
        out_shape = jax.ShapeDtypeStruct((geo.T, D_MODEL // 2), PACKED)
        out_spec = pl.BlockSpec((tm, D_MODEL // 2), lambda i: (i, 0))
    else:
        out_shape = jax.ShapeDtypeStruct((geo.T, D_MODEL), BF16)
        out_spec = spec
    return pl.pallas_call(
        functools.partial(_normmod_kernel, packed=packed),
        name="normmod",
        out_shape=out_shape,
        grid=(geo.T // tm,),
        in_specs=[spec, pl.BlockSpec((1, D_MODEL), lambda i: (0, 0)),
                  _mod_spec(m_shift, tm, geo), _mod_spec(m_scale, tm, geo)],
        out_specs=out_spec,
        compiler_params=_params("arbitrary"),
    )(x, g.reshape(1, D_MODEL), mod3, mod3)


def _mm_kernel(a_ref, w_ref, o_ref):
    o_ref[...] = jnp.dot(a_ref[...], w_ref[...], preferred_element_type=F32).astype(o_ref.dtype)


def _mm_resid_kernel(a_ref, w_ref, x_ref, m_ref, o_ref):
    y = jnp.dot(a_ref[...], w_ref[...], preferred_element_type=F32)
    o_ref[...] = x_ref[...] + m_ref[0] * y


def _matmul(a, w, geo):
    k, n = w.shape
    tm = _pick_tile(geo, (1024, 512, 256, 128))
    tn = 512
    return pl.pallas_call(
        _mm_kernel,
        name="matmul",
        out_shape=jax.ShapeDtypeStruct((geo.T, n), BF16),
        grid=(geo.T // tm, n // tn),
        in_specs=[pl.BlockSpec((tm, k), lambda i, j: (i, 0)),
                  pl.BlockSpec((k, tn), lambda i, j: (0, j))],
        out_specs=pl.BlockSpec((tm, tn), lambda i, j: (i, j)),
        compiler_params=_params("arbitrary", "arbitrary"),
    )(a, w)


def _matmul_resid(a, w, x, mod3, m_gate, geo):
    k, n = w.shape
    tm = _pick_tile(geo, (1024, 512, 256, 128))
    tn = 512
    return pl.pallas_call(
        _mm_resid_kernel,
        name="matmul_resid",
        out_shape=jax.ShapeDtypeStruct((geo.T, n), F32),
        grid=(geo.T // tm, n // tn),
        in_specs=[pl.BlockSpec((tm, k), lambda i, j: (i, 0)),
                  pl.BlockSpec((k, tn), lambda i, j: (0, j)),
                  pl.BlockSpec((tm, tn), lambda i, j: (i, j)),
                  pl.BlockSpec((1, 1, tn), lambda i, j: (_mod_row(i, tm, geo) * N_MOD + m_gate, 0, j))],
        out_specs=pl.BlockSpec((tm, tn), lambda i, j: (i, j)),
        compiler_params=_params("arbitrary", "arbitrary"),
    )(a, w, x, mod3)


def _chunk_index(b, i, chunk, geo, reverse):
    nc = geo.Lc // chunk
    nl = geo.N // chunk
    ctx0 = (geo.BN + b * geo.Lc) // chunk
    lat0 = (b * geo.N) // chunk
    if reverse:
        return jnp.where(i < nc, ctx0 + (nc - 1 - i), lat0 + (nl - 1 - (i - nc)))
    return jnp.where(i < nc, ctx0 + i, lat0 + (i - nc))


def _split_dot(tri_bf, x):
    hi = x.astype(BF16)
    r1 = x - hi.astype(F32)
    mid = r1.astype(BF16)
    lo = (r1 - mid.astype(F32)).astype(BF16)
    return (jnp.dot(tri_bf, hi, preferred_element_type=F32)
            + jnp.dot(tri_bf, mid, preferred_element_type=F32)
            + jnp.dot(tri_bf, lo, preferred_element_type=F32))


def _hgrn_kernel(qf_ref, ff_ref, vf_ref, qb_ref, fb_ref, vb_ref, lbp_ref, of_ref, ob_ref,
                 stf_ref, stb_ref, *, layer):
    i = pl.program_id(1)

    @pl.when(i == 0)
    def _():
        stf_ref[...] = jnp.zeros_like(stf_ref)
        stb_ref[...] = jnp.zeros_like(stb_ref)

    lbp = lbp_ref[...]
    e = jnp.exp(lbp - jnp.max(lbp, axis=0, keepdims=True))
    sm = e / jnp.sum(e, axis=0, keepdims=True)
    lb = sm[0:1]
    for r in range(1, layer + 1):
        lb = lb + sm[r:r + 1]

    _hgrn_chunk(qf_ref, ff_ref, vf_ref, of_ref, stf_ref, lb, reverse=False)
    _hgrn_chunk(qb_ref, fb_ref, vb_ref, ob_ref, stb_ref, lb, reverse=True)


def _hgrn_chunk(q_ref, f_ref, v_ref, o_ref, st_ref, lb, *, reverse):
    c = HG_CHUNK
    row = lax.broadcasted_iota(jnp.int32, (c, c), 0)
    col = lax.broadcasted_iota(jnp.int32, (c, c), 1)
    tri = (col >= row) if reverse else (col <= row)
    tri_bf = jnp.where(tri, 1.0, 0.0).astype(BF16)
    nt = (((1,), (1,)), ((), ()))

    for h in range(HG_HEADS):
        sl = slice(h * HG_DK, (h + 1) * HG_DK)
        qh = _silu(q_ref[:, sl].astype(F32))
        lbh = lb[:, sl]
        fg = lbh + (1.0 - lbh) * _sigmoid(f_ref[:, sl].astype(F32))
        kh = 1.0 - fg
        bcum = _split_dot(tri_bf, jnp.log(fg))
        bmid = bcum[c // 2:c // 2 + 1]
        bend = bcum[0:1] if reverse else bcum[c - 1:c]
        vb = v_ref[:, sl]
        vh = vb.astype(F32)
        a = (qh * jnp.exp(bcum - bmid)).astype(BF16)
        kd = (kh * jnp.exp(bmid - bcum)).astype(BF16)
        s = lax.dot_general(a, kd, nt, preferred_element_type=F32)
        s = jnp.where(tri, s, 0.0)
        intra = jnp.dot(s.astype(BF16), vb, preferred_element_type=F32)
        st = st_ref[h]
        inter = lax.dot_general((qh * jnp.exp(bcum)).astype(BF16), st.astype(BF16), nt,
                                preferred_element_type=F32)
        o_ref[:, sl] = (inter + intra).astype(BF16)
        kd2 = (kh * jnp.exp(bend - bcum)).astype(BF16)
        st_ref[h] = st * jnp.exp(bend) + jnp.dot(vh.T.astype(BF16), kd2, preferred_element_type=F32)


def _hgrn_scan(p, lb_param, layer, geo):
    c = HG_CHUNK
    steps = (geo.Lc + geo.N) // c

    def spec(colblk, reverse):
        return pl.BlockSpec((c, HG_F), lambda b, i: (_chunk_index(b, i, c, geo, reverse), colblk))

    out = jax.ShapeDtypeStruct((geo.T, HG_V), BF16)
    state = pltpu.VMEM((HG_HEADS, HG_DV, HG_DK), F32)
    return pl.pallas_call(
        functools.partial(_hgrn_kernel, layer=layer),
        name="hgrn_scan",
        out_shape=[out, out],
        grid=(geo.B, steps),
        in_specs=[spec(0, False), spec(1, False), spec(3, False),
                  spec(0, True), spec(2, True), spec(3, True),
                  pl.BlockSpec(lb_param.shape, lambda b, i: (0, 0))],
        out_specs=[spec(0, False), spec(0, True)],
        scratch_shapes=[state, state],
        compiler_params=_params("arbitrary", "arbitrary"),
    )(p, p, p, p, p, p, lb_param)


def _ev_readout_kernel(of_ref, ob_ref, gate_ref, a_ref, b_ref, ap_ref, bp_ref, an_ref, bn_ref,
                       gain_ref, cw_ref, cb_ref, lng_ref, lnb_ref, o_ref, ext_ref, conv_ref, *, tm, geo):
    i = pl.program_id(0)
    n_lat = geo.BN // tm
    tpl = geo.N // tm
    tpc = geo.Lc // tm
    j = jnp.where(i < n_lat, i % tpl, (i - n_lat) % tpc)
    per = jnp.where(i < n_lat, tpl, tpc)
    keep_prev = jnp.where(j == 0, 0.0, 1.0)
    keep_next = jnp.where(j == per - 1, 0.0, 1.0)

    o = of_ref[...].astype(F32) + ob_ref[...].astype(F32)
    r = o * lax.rsqrt(jnp.mean(o * o, axis=-1, keepdims=True) + EPS) * gain_ref[...]
    o_ref[:, :HG_V] = (r * _silu(gate_ref[...].astype(F32))).astype(BF16)

    def glu(x_ref, y_ref):
        return x_ref[...].astype(F32) * _sigmoid(y_ref[...].astype(F32))

    ext_ref[0:CONV_HALO] = glu(ap_ref, bp_ref) * keep_prev
    ext_ref[CONV_HALO:CONV_HALO + tm] = glu(a_ref, b_ref)
    ext_ref[CONV_HALO + tm:2 * CONV_HALO + tm] = glu(an_ref, bn_ref) * keep_next
    off = CONV_HALO - CONV_W // 2
    for cj in range(CONV_C // 128):
        cs = slice(cj * 128, (cj + 1) * 128)
        acc = jnp.broadcast_to(cb_ref[:, cs], (tm, 128))
        for k in range(CONV_W):
            acc = acc + cw_ref[k:k + 1, cs] * ext_ref[off + k:off + k + tm, cs]
        conv_ref[:, cs] = acc
    acc = conv_ref[...]
    mu = jnp.mean(acc, axis=-1, keepdims=True)
    xc = acc - mu
    var = jnp.mean(xc * xc, axis=-1, keepdims=True)
    u = xc * lax.rsqrt(var + EPS) * lng_ref[...] + lnb_ref[...]
    o_ref[:, HG_V:] = _silu(u).astype(BF16)


def _ev_readout(p, of, ob, gain, cw, cb, lng, lnb, geo):
    tm = _pick_tile(geo, (128,))
    hb = tm // CONV_HALO
    nhalo = geo.T // CONV_HALO

    def row(colblk):
        return pl.BlockSpec((tm, HG_V), lambda i: (i, colblk))

    def prev(colblk):
        return pl.BlockSpec((CONV_HALO, CONV_C), lambda i: (jnp.maximum(i * hb - 1, 0), colblk))

    def nxt(colblk):
        return pl.BlockSpec((CONV_HALO, CONV_C), lambda i: (jnp.minimum((i + 1) * hb, nhalo - 1), colblk))

    def vec(n):
        return pl.BlockSpec((n, CONV_C), lambda i: (0, 0))

    return pl.pallas_call(
        functools.partial(_ev_readout_kernel, tm=tm, geo=geo),
        name="ev_readout",
        out_shape=jax.ShapeDtypeStruct((geo.T, HG_V + CONV_C), BF16),
        grid=(geo.T // tm,),
        in_specs=[row(0), row(0), row(4), row(5), row(6), prev(5), prev(6), nxt(5), nxt(6),
                  vec(1), vec(CONV_W), vec(1), vec(1), vec(1)],
        out_specs=pl.BlockSpec((tm, HG_V + CONV_C), lambda i: (i, 0)),
        scratch_shapes=[pltpu.VMEM((tm + 2 * CONV_HALO, CONV_C), F32), pltpu.VMEM((tm, CONV_C), F32)],
        compiler_params=_params("arbitrary"),
    )(of, ob, p, p, p, p, p, p, p, gain.reshape(1, -1), cw, cb.reshape(1, -1),
      lng.reshape(1, -1), lnb.reshape(1, -1))


def _rope(x, cos, sin_signed):
    half = x.shape[-1] // 2
    rot = jnp.concatenate([pltpu.roll(x[:, :half], half // 2, axis=1),
                           pltpu.roll(x[:, half:], half // 2, axis=1)], axis=-1)
    return x * cos + rot * sin_signed


def _ret_kernel(qf_ref, kf_ref, vf_ref, cosf_ref, sinf_ref, qb_ref, kb_ref, vb_ref, cosb_ref, sinb_ref,
                dl_ref, of_ref, ob_ref, sf_ref, sb_ref):
    i = pl.program_id(1)

    @pl.when(i == 0)
    def _():
        sf_ref[...] = jnp.zeros_like(sf_ref)
        sb_ref[...] = jnp.zeros_like(sb_ref)

    _ret_chunk(qf_ref, kf_ref, vf_ref, cosf_ref, sinf_ref, dl_ref[0], of_ref, sf_ref, reverse=False)
    _ret_chunk(qb_ref, kb_ref, vb_ref, cosb_ref, sinb_ref, dl_ref[1], ob_ref, sb_ref, reverse=True)


def _ret_chunk(q_ref, k_ref, v_ref, cos_ref, sin_ref, dl, o_ref, s_ref, *, reverse):
    c = RET_CHUNK
    lg_all = -jnp.log1p(jnp.exp(-dl))
    row = lax.broadcasted_iota(jnp.int32, (c, c), 0)
    col = lax.broadcasted_iota(jnp.int32, (c, c), 1)
    idx = lax.broadcasted_iota(jnp.int32, (c, 1), 0).astype(F32)
    if reverse:
        live = col >= row
        dist = (col - row).astype(F32)
        q_pow = c - idx
        k_pow = idx
    else:
        live = row >= col
        dist = (row - col).astype(F32)
        q_pow = idx + 1.0
        k_pow = c - 1.0 - idx
    cos = cos_ref[...]
    sin = sin_ref[...]
    nt = (((1,), (1,)), ((), ()))

    for h in range(RET_HEADS):
        lg = lg_all[h:h + 1]
        dmask = jnp.where(live, jnp.exp(lg * dist), 0.0)
        q = _rope(q_ref[:, h * RET_DK:(h + 1) * RET_DK].astype(F32), cos, sin)
        k = _rope(k_ref[:, h * RET_DK:(h + 1) * RET_DK].astype(F32) * (RET_DK ** -0.5), cos, sin)
        vb = v_ref[:, h * RET_DV:(h + 1) * RET_DV]
        scores = lax.dot_general(q.astype(BF16), k.astype(BF16), nt, preferred_element_type=F32) * dmask
        intra = jnp.dot(scores.astype(BF16), vb, preferred_element_type=F32)
        s = s_ref[h]
        inter = jnp.dot((q * jnp.exp(lg * q_pow)).astype(BF16), s.astype(BF16),
                        preferred_element_type=F32)
        o_ref[:, h * RET_DV:(h + 1) * RET_DV] = (inter + intra).astype(BF16)
        kdec = (k * jnp.exp(lg * k_pow)).T.astype(BF16)
        s_ref[h] = jnp.exp(lg * c) * s + jnp.dot(kdec, vb, preferred_element_type=F32)


def _ret_scan(p, cos_tab, sin_tab, decay_logit, geo):
    c = RET_CHUNK
    steps = (geo.Lc + geo.N) // c
    nc = geo.Lc // c
    nl = geo.N // c

    def spec(width, colblk, reverse):
        return pl.BlockSpec((c, width), lambda b, i: (_chunk_index(b, i, c, geo, reverse), colblk))

    def tab(reverse):
        def index(b, i):
            lat = (nl - 1 - (i - nc)) if reverse else (i - nc)
            return (jnp.where(i < nc, nl, lat), 0)
        return pl.BlockSpec((c, RET_DK), index)

    def direction(reverse):
        return [spec(RET_QK, 0, reverse), spec(RET_QK, 1, reverse), spec(RET_V, 1, reverse),
                tab(reverse), tab(reverse)]

    out = jax.ShapeDtypeStruct((geo.T, RET_V), BF16)
    state = pltpu.VMEM((RET_HEADS, RET_DK, RET_DV), F32)
    return pl.pallas_call(
        _ret_kernel,
        name="ret_scan",
        out_shape=[out, out],
        grid=(geo.B, steps),
        in_specs=direction(False) + direction(True)
        + [pl.BlockSpec((2, RET_HEADS, 1), lambda b, i: (0, 0, 0))],
        out_specs=[spec(RET_V, 0, False), spec(RET_V, 0, True)],
        scratch_shapes=[state, state],
        compiler_params=_params("arbitrary", "arbitrary"),
    )(p, p, p, cos_tab, sin_tab, p, p, p, cos_tab, sin_tab, decay_logit.reshape(2, RET_HEADS, 1))


def _rope_tables(n):
    t = jnp.arange(n)
    quarter = RET_DK // 4
    inv = 1.0 / (ROPE_BASE ** (jnp.arange(quarter, dtype=F32) / quarter))
    ang_r = (t // GRID_W).astype(F32)[:, None] * inv
    ang_c = (t % GRID_W).astype(F32)[:, None] * inv
    cos = jnp.concatenate([jnp.cos(ang_r), jnp.cos(ang_r), jnp.cos(ang_c), jnp.cos(ang_c)], axis=-1)
    sin = jnp.concatenate([-jnp.sin(ang_r), jnp.sin(ang_r), -jnp.sin(ang_c), jnp.sin(ang_c)], axis=-1)
    cos = jnp.concatenate([cos, jnp.ones((RET_CHUNK, RET_DK), F32)], axis=0)
    sin = jnp.concatenate([sin, jnp.zeros((RET_CHUNK, RET_DK), F32)], axis=0)
    return cos, sin


def _ret_readout_kernel(of_ref, ob_ref, gate_ref, o_ref):
    for h in range(RET_HEADS):
        sl = slice(h * RET_DV, (h + 1) * RET_DV)
        o = of_ref[:, sl].astype(F32) + ob_ref[:, sl].astype(F32)
        r = o * lax.rsqrt(jnp.mean(o * o, axis=-1, keepdims=True) + EPS)
        o_ref[:, sl] = (_silu(gate_ref[:, sl].astype(F32)) * r).astype(BF16)


def _ret_readout(p, of, ob, geo):
    tm = _pick_tile(geo, (128,))
    spec = pl.BlockSpec((tm, RET_V), lambda i: (i, 0))
    return pl.pallas_call(
        _ret_readout_kernel,
        name="ret_readout",
        out_shape=jax.ShapeDtypeStruct((geo.T, RET_V), BF16),
        grid=(geo.T // tm,),
        in_specs=[spec, spec, pl.BlockSpec((tm, RET_V), lambda i: (i, 2))],
        out_specs=spec,
        compiler_params=_params("arbitrary"),
    )(of, ob, p)


def _router_kernel(h_ref, rw_ref, rb_ref, eidx_ref, rank_ref, w_ref, cnt_ref, carry_ref, *, tm):
    i = pl.program_id(0)

    @pl.when(i == 0)
    def _():
        carry_ref[...] = jnp.zeros_like(carry_ref)

    half = D_MODEL // 2
    h_lo, h_hi = _unpack_rows(h_ref[...])
    h_lo = h_lo.astype(BF16)
    h_hi = h_hi.astype(BF16)
    rest = rw_ref[...]
    logits = jnp.zeros((tm, N_EXPERTS), F32)
    for _ in range(3):
        part = rest.astype(BF16)
        rest = rest - part.astype(F32)
        logits = (logits + jnp.dot(h_lo, part[:half], preferred_element_type=F32)
                  + jnp.dot(h_hi, part[half:], preferred_element_type=F32))
    s = _sigmoid(logits)
    sel = s + rb_ref[...]
    lane = lax.broadcasted_iota(jnp.int32, (tm, N_EXPERTS), 1).astype(F32)
    grp = jnp.floor(lane * (1.0 / GROUP_SIZE))
    ninf = -jnp.inf
    none = float(N_EXPERTS)

    gscore = jnp.zeros((tm, N_EXPERTS), F32)
    gcols = []
    for g in range(N_GROUPS):
        in_g = grp == float(g)
        v1 = jnp.max(jnp.where(in_g, sel, ninf), axis=-1, keepdims=True)
        i1 = jnp.min(jnp.where(in_g & (sel == v1), lane, none), axis=-1, keepdims=True)
        v2 = jnp.max(jnp.where(in_g & (lane != i1), sel, ninf), axis=-1, keepdims=True)
        gcols.append(v1 + v2)
        gscore = jnp.where(in_g, v1 + v2, gscore)
    beaten = jnp.zeros((tm, N_EXPERTS), F32)
    for g in range(N_GROUPS):
        wins = (gcols[g] > gscore) | ((gcols[g] == gscore) & (float(g) < grp))
        beaten = beaten + jnp.where(wins, 1.0, 0.0)
    cand = jnp.where(beaten < float(TOPK_GROUPS), sel, ninf)

    lane_k = lax.broadcasted_iota(jnp.int32, (tm, TOP_K), 1)
    eidx = jnp.zeros((tm, TOP_K), F32)
    wsel = jnp.zeros((tm, TOP_K), F32)
    chosen = jnp.zeros((tm, N_EXPERTS), F32)
    picks = []
    for k in range(TOP_K):
        v = jnp.max(cand, axis=-1, keepdims=True)
        ik = jnp.min(jnp.where(cand == v, lane, none), axis=-1, keepdims=True)
        hit = lane == ik
        picks.append(ik)
        eidx = jnp.where(lane_k == k, ik, eidx)
        wsel = jnp.where(lane_k == k, jnp.sum(jnp.where(hit, s, 0.0), axis=-1, keepdims=True), wsel)
        chosen = jnp.where(hit, 1.0, chosen)
        cand = jnp.where(hit, ninf, cand)
    w_ref[...] = wsel / jnp.sum(wsel, axis=-1, keepdims=True) * ROUTED_SCALE
    eidx_ref[...] = eidx.astype(jnp.int32)

    r = lax.broadcasted_iota(jnp.int32, (tm, tm), 0)
    c = lax.broadcasted_iota(jnp.int32, (tm, tm), 1)
    below = jnp.where(c < r, 1.0, 0.0).astype(BF16)
    carry = carry_ref[...]
    pos = jnp.dot(below, chosen.astype(BF16), preferred_element_type=F32) + carry
    rank = jnp.zeros((tm, TOP_K), jnp.int32)
    for k in range(TOP_K):
        rk = jnp.sum(jnp.where(lane == picks[k], pos, 0.0), axis=-1, keepdims=True)
        rank = jnp.where(lane_k == k, rk.astype(jnp.int32), rank)
    rank_ref[...] = rank
    carry = carry + jnp.sum(chosen, axis=0, keepdims=True)
    carry_ref[...] = carry
    cnt_ref[...] = carry


def _router(h_packed, rw, rb, geo):
    tm = 256 if geo.T % 256 == 0 else 128
    tok = pl.BlockSpec((tm, TOP_K), lambda i: (i, 0))
    one = pl.BlockSpec((1, N_EXPERTS), lambda i: (0, 0))
    return pl.pallas_call(
        functools.partial(_router_kernel, tm=tm),
        name="router",
        out_shape=[jax.ShapeDtypeStruct((geo.T, TOP_K), jnp.int32),
                   jax.ShapeDtypeStruct((geo.T, TOP_K), jnp.int32),
                   jax.ShapeDtypeStruct((geo.T, TOP_K), F32),
                   jax.ShapeDtypeStruct((1, N_EXPERTS), F32)],
        grid=(geo.T // tm,),
        in_specs=[pl.BlockSpec((tm, D_MODEL // 2), lambda i: (i, 0)),
                  pl.BlockSpec((D_MODEL, N_EXPERTS), lambda i: (0, 0)), one],
        out_specs=[tok, tok, tok, one],
        scratch_shapes=[pltpu.VMEM((1, N_EXPERTS), F32)],
        compiler_params=_params("arbitrary"),
    )(h_packed, rw, rb.reshape(1, N_EXPERTS))


def _n_blocks(geo):
    return -(-(geo.T * TOP_K) // MOE_BLOCK) + N_EXPERTS


def _dest_kernel(cnt_ref, eidx_ref, rank_ref, dest_ref, blk_ref, *, n_blocks):
    eidx = eidx_ref[...]
    dest = rank_ref[...]
    blk_row = (lax.broadcasted_iota(jnp.int32, blk_ref.shape, 0) * 128
               + lax.broadcasted_iota(jnp.int32, blk_ref.shape, 1)) * MOE_BLOCK
    blk = jnp.zeros(blk_ref.shape, jnp.int32)
    start = jnp.int32(0)
    for e in range(N_EXPERTS):
        padded = (cnt_ref[e] + (MOE_BLOCK - 1)) // MOE_BLOCK * MOE_BLOCK
        dest = dest + jnp.where(eidx == e, start, 0)
        start = start + padded
        blk = blk + jnp.where(start <= blk_row, 1, 0)
    dest_ref[...] = dest
    blk_ref[...] = jnp.where(blk_row == n_blocks * MOE_BLOCK, start // MOE_BLOCK,
                             jnp.minimum(blk, N_EXPERTS - 1))


def _dest(counts, eidx, rank, geo):
    rows = geo.T * TOP_K // 128
    brow = -(-(_n_blocks(geo) + 1) // 128)
    full = pl.BlockSpec((rows, 128), lambda: (0, 0))
    dest, blk = pl.pallas_call(
        functools.partial(_dest_kernel, n_blocks=_n_blocks(geo)),
        name="dest",
        out_shape=[jax.ShapeDtypeStruct((rows, 128), jnp.int32),
                   jax.ShapeDtypeStruct((brow, 128), jnp.int32)],
        in_specs=[pl.BlockSpec(memory_space=pltpu.SMEM), full, full],
        out_specs=[full, pl.BlockSpec((brow, 128), lambda: (0, 0))],
    )(counts, eidx.reshape(rows, 128), rank.reshape(rows, 128))
    return dest.reshape(-1), blk.reshape(-1)[:_n_blocks(geo) + 1]


def _row_copy(src_hbm, s, dst_hbm, d, sem):
    return pltpu.make_async_copy(src_hbm.at[pl.ds(s, 1)], dst_hbm.at[pl.ds(d, 1)], sem)


def _zero_fill(cnt_ref, xs_hbm, zero_ref, zsem, n_rows, wait):
    def piece(pos, size):
        if size >= 8:
            copies = [(pl.multiple_of(pos, 8), size)]
        else:
            copies = [(pos + r, 1) for r in range(size)]
        for p, s in copies:
            cp = pltpu.make_async_copy(zero_ref.at[pl.ds(0, s)], xs_hbm.at[pl.ds(p, s)], zsem)
            cp.wait() if wait else cp.start()

    def per_expert(e, start):
        cnt = cnt_ref[e]
        padded = (cnt + (MOE_BLOCK - 1)) // MOE_BLOCK * MOE_BLOCK
        pad = padded - cnt
        pos = start + cnt
        size = 1
        while size < MOE_BLOCK:
            take = (pad & size) != 0
            pl.when(take)(functools.partial(piece, pos, size))
            pos = pos + jnp.where(take, size, 0)
            size *= 2
        return start + padded

    end = lax.fori_loop(0, N_EXPERTS, per_expert, jnp.int32(0))

    def per_block(j, carry):
        piece(end + j * MOE_BLOCK, MOE_BLOCK)
        return carry

    lax.fori_loop(0, (n_rows - end) // MOE_BLOCK, per_block, 0)


def _dispatch_kernel(cnt_ref, dest_ref, h_ref, xs_hbm, zero_ref, sem, zsem, *, tt, n_rows):
    i = pl.program_id(0)

    @pl.when(i == 0)
    def _():
        zero_ref[...] = jnp.zeros_like(zero_ref)
        _zero_fill(cnt_ref, xs_hbm, zero_ref, zsem, n_rows, wait=False)
        _zero_fill(cnt_ref, xs_hbm, zero_ref, zsem, n_rows, wait=True)

    def issue(t, carry):
        for k in range(TOP_K):
            _row_copy(h_ref, t, xs_hbm, dest_ref[t * TOP_K + k], sem).start(priority=k % 2)
        return carry

    lax.fori_loop(0, tt, issue, 0)

    def drain(t, carry):
        for k in range(TOP_K):
            _row_copy(h_ref, 0, xs_hbm, 0, sem).wait()
        return carry

    lax.fori_loop(0, tt, drain, 0)


def _dispatch(counts, dest, h_packed, geo):
    tt = 512 if geo.T % 512 == 0 else 128
    n_rows = _n_blocks(geo) * MOE_BLOCK
    width = D_MODEL // 2
    return pl.pallas_call(
        functools.partial(_dispatch_kernel, tt=tt, n_rows=n_rows),
        name="dispatch",
        out_shape=jax.ShapeDtypeStruct((n_rows, width), PACKED),
        grid=(geo.T // tt,),
        in_specs=[pl.BlockSpec(memory_space=pltpu.SMEM),
                  pl.BlockSpec((tt * TOP_K,), lambda i: (i,), memory_space=pltpu.SMEM),
                  pl.BlockSpec((tt, width), lambda i: (i, 0))],
        out_specs=pl.BlockSpec(memory_space=pl.ANY),
        scratch_shapes=[pltpu.VMEM((MOE_BLOCK, width), PACKED), pltpu.SemaphoreType.DMA,
                        pltpu.SemaphoreType.DMA],
        compiler_params=_params("arbitrary"),
    )(counts, dest, h_packed)


def _swiglu_packed(x_ref, wg_ref, wu_ref, wd_ref):
    half = D_MODEL // 2
    lo, hi = _unpack_rows(x_ref[...])
    lo = lo.astype(BF16)
    hi = hi.astype(BF16)

    def proj(w_ref):
        return (jnp.dot(lo, w_ref[:half], preferred_element_type=F32)
                + jnp.dot(hi, w_ref[half:], preferred_element_type=F32))

    g = proj(wg_ref)
    u = proj(wu_ref)
    return jnp.dot((_silu(g) * u).astype(BF16), wd_ref[...], preferred_element_type=F32)


def _expert_kernel(blk_ref, x_ref, wg_ref, wu_ref, wd_ref, o_ref, wgb_ref, wub_ref, wdb_ref, *, nb):
    j = pl.program_id(0)
    n_used = blk_ref[nb]

    @pl.when((j == 0) | (blk_ref[j] != blk_ref[jnp.maximum(j - 1, 0)]))
    def _():
        wgb_ref[...] = wg_ref[...].astype(BF16)
        wub_ref[...] = wu_ref[...].astype(BF16)
        wdb_ref[...] = wd_ref[...].astype(BF16)

    @pl.when(j < n_used)
    def _():
        o_ref[...] = _pack_rows(_swiglu_packed(x_ref, wgb_ref, wub_ref, wdb_ref))

    @pl.when(j >= n_used)
    def _():
        o_ref[...] = jnp.zeros_like(o_ref)


def _experts(blk_e, xs, wg, wu, wd, layer, geo):
    nb = _n_blocks(geo)
    rows = pl.BlockSpec((MOE_BLOCK, D_MODEL // 2), lambda j, be: (j, 0))

    def wspec(r, c):
        return pl.BlockSpec((None, None, r, c), lambda j, be: (layer, be[j], 0, 0))

    return pl.pallas_call(
        functools.partial(_expert_kernel, nb=nb),
        name="experts",
        out_shape=jax.ShapeDtypeStruct((nb * MOE_BLOCK, D_MODEL // 2), PACKED),
        grid_spec=pltpu.PrefetchScalarGridSpec(
            num_scalar_prefetch=1,
            grid=(nb,),
            in_specs=[rows, wspec(D_MODEL, EXPERT_FF), wspec(D_MODEL, EXPERT_FF),
                      wspec(EXPERT_FF, D_MODEL)],
            out_specs=rows,
            scratch_shapes=[pltpu.VMEM((D_MODEL, EXPERT_FF), BF16),
                            pltpu.VMEM((D_MODEL, EXPERT_FF), BF16),
                            pltpu.VMEM((EXPERT_FF, D_MODEL), BF16)]),
        compiler_params=_params("arbitrary"),
    )(blk_e, xs, wg, wu, wd)


def _sc_gather_rows(table, idx):
    nw = V7X_SC_CORES * V7X_SC_SUBCORES
    m = idx.shape[0]
    w = table.shape[1]
    per = m // nw
    assert m % (nw * SC_ROWS) == 0
    mesh = plsc.VectorSubcoreMesh(core_axis_name="c", subcore_axis_name="s")

    @functools.partial(
        pl.kernel, mesh=mesh,
        out_type=jax.ShapeDtypeStruct((m, w), table.dtype),
        scratch_types=[pltpu.VMEM((SC_ROWS,), jnp.int32),
                       pltpu.VMEM((SC_ROWS, w), table.dtype),
                       pltpu.SemaphoreType.DMA],
    )
    def gather(table_hbm, idx_hbm, out_hbm, idx_v, rows_v, sem):
        wid = lax.axis_index("s") * V7X_SC_CORES + lax.axis_index("c")
        base = wid * per

        @pl.loop(0, per // SC_ROWS)
        def _(j):
            off = pl.multiple_of(base + j * SC_ROWS, 8)
            pltpu.sync_copy(idx_hbm.at[pl.ds(off, SC_ROWS)], idx_v)
            pltpu.async_copy(table_hbm.at[idx_v], rows_v, sem).wait()
            pltpu.sync_copy(rows_v, out_hbm.at[pl.ds(off, SC_ROWS)])

    return gather(table, idx)


def _combine_kernel(w_ref, x_ref, h_ref, sg_ref, su_ref, sd_ref, m_ref, *rest, post):
    if post == "next":
        ng_ref, nsh_ref, nsc_ref, g_ref, o_ref, hn_ref = rest
    else:
        ng_ref, g_ref, o_ref = rest
    half = D_MODEL // 2
    shared = _swiglu_packed(h_ref, sg_ref, su_ref, sd_ref)
    w = w_ref[...]
    acc_lo = shared[:, :half]
    acc_hi = shared[:, half:]
    for k in range(TOP_K):
        lo, hi = _unpack_rows(g_ref[k])
        acc_lo = acc_lo + w[:, k:k + 1] * lo
        acc_hi = acc_hi + w[:, k:k + 1] * hi
    gate = m_ref[0]
    y_lo = x_ref[:, :half] + gate[:, :half] * acc_lo
    y_hi = x_ref[:, half:] + gate[:, half:] * acc_hi
    ms = (jnp.sum(y_lo * y_lo, axis=-1, keepdims=True)
          + jnp.sum(y_hi * y_hi, axis=-1, keepdims=True)) * (1.0 / D_MODEL)
    inv = lax.rsqrt(ms + EPS)
    ng = ng_ref[...]
    n_lo = y_lo * inv * ng[:, :half]
    n_hi = y_hi * inv * ng[:, half:]
    if post == "next":
        o_ref[:, :half] = y_lo
        o_ref[:, half:] = y_hi
        sc = nsc_ref[0]
        sft = nsh_ref[0]
        hn_ref[:, :half] = (n_lo * (1.0 + sc[:, :half]) + sft[:, :half]).astype(BF16)
        hn_ref[:, half:] = (n_hi * (1.0 + sc[:, half:]) + sft[:, half:]).astype(BF16)
    else:
        o_ref[:, :half] = n_lo
        o_ref[:, half:] = n_hi


def _combine(dest, w, x, h_packed, sg, su, sd, ys, mod3, m_gate, post, norm_g, next_mod3, geo):
    half = D_MODEL // 2
    order = dest.reshape(geo.T, TOP_K).T.reshape(-1)
    gathered = _sc_gather_rows(ys, order).reshape(TOP_K, geo.T, half)
    tt = _pick_tile(geo, (256, 128))
    n = geo.T // tt
    ff = sg.shape[1]
    rows = pl.BlockSpec((tt, D_MODEL), lambda i: (i, 0))
    vec = pl.BlockSpec((1, D_MODEL), lambda i: (0, 0))
    in_specs = [pl.BlockSpec((tt, TOP_K), lambda i: (i, 0)),
                rows,
                pl.BlockSpec((tt, half), lambda i: (i, 0)),
                pl.BlockSpec((D_MODEL, ff), lambda i: (0, 0)),
                pl.BlockSpec((D_MODEL, ff), lambda i: (0, 0)),
                pl.BlockSpec((ff, D_MODEL), lambda i: (0, 0)),
                _mod_spec(m_gate, tt, geo), vec]
    args = [w, x, h_packed, sg, su, sd, mod3, norm_g.reshape(1, D_MODEL)]
    out_shape = [jax.ShapeDtypeStruct((geo.T, D_MODEL), F32)]
    out_specs = [rows]
    if post == "next":
        in_specs += [_mod_spec(0, tt, geo), _mod_spec(1, tt, geo)]
        args += [next_mod3, next_mod3]
        out_shape.append(jax.ShapeDtypeStruct((geo.T, D_MODEL), BF16))
        out_specs.append(rows)
    in_specs.append(pl.BlockSpec((TOP_K, tt, half), lambda i: (0, i, 0)))
    args.append(gathered)
    return pl.pallas_call(
        functools.partial(_combine_kernel, post=post),
        name="combine",
        out_shape=out_shape,
        grid=(n,),
        in_specs=in_specs,
        out_specs=out_specs,
        compiler_params=_params("arbitrary"),
    )(*args)


def _moe(x, h_packed, mod3, rw, rb, wg, wu, wd, layer, sg, su, sd, post, norm_g, next_mod3, geo):
    eidx, rank, w, counts = _router(h_packed, rw, rb, geo)
    counts = counts.reshape(N_EXPERTS).astype(jnp.int32)
    dest, blk_e = _dest(counts, eidx, rank, geo)
    xs = _dispatch(counts, dest, h_packed, geo)
    ys = _experts(blk_e, xs, wg, wu, wd, layer, geo)
    return _combine(dest, w, x, h_packed, sg.astype(BF16), su.astype(BF16), sd.astype(BF16), ys,
                    mod3, 5, post, norm_g, next_mod3, geo)


def kernel(x, c, ctx, c_ctx, ada_w, ada_b, norm_mix, norm_ffn, norm_final, ev_w_in, ev_w_out, hgrn_lb, hgrn_norm, conv_w, conv_b, conv_norm_g, conv_norm_b, ret_w_in, ret_w_out, ret_decay, router_w, router_b, exp_gate, exp_up, exp_down, sh_gate, sh_up, sh_down):
    b, n, d = x.shape
    lc = ctx.shape[1]
    depth = ada_w.shape[0]
    geo = _geo(b, n, lc)
    assert d == D_MODEL and b < MOD_ROWS
    assert n % RET_CHUNK == 0 and lc % RET_CHUNK == 0

    xs = jnp.concatenate([x.reshape(geo.BN, d), ctx.reshape(geo.BL, d)], axis=0)
    cond = jnp.zeros((MOD_ROWS, d), F32).at[:b].set(c).at[b].set(c_ctx)
    cos_tab, sin_tab = _rope_tables(n)

    mods = [_adaln(cond, ada_w, ada_b, l).reshape(MOD_ROWS * N_MOD, 1, d) for l in range(depth)]
    h = _normmod(xs, norm_mix[0], mods[0], 0, 1, geo, packed=False)
    for l in range(depth):
        j = l // 2
        last = l == depth - 1
        mod3 = mods[l]
        tail = geo._replace(T=geo.BN, BL=0, Lc=0) if last else geo
        if l % 2 == 0:
            p = _matmul(h, ev_w_in[j].astype(BF16), geo)
            of, ob = _hgrn_scan(p, hgrn_lb, l, geo)
            mix = _ev_readout(p, of, ob, hgrn_norm[j], conv_w[j], conv_b[j],
                              conv_norm_g[j], conv_norm_b[j], geo)
            xs = _matmul_resid(mix, ev_w_out[j].astype(BF16), xs, mod3, 2, tail)
        else:
            p = _matmul(h, ret_w_in[j].astype(BF16), geo)
            of, ob = _ret_scan(p, cos_tab, sin_tab, ret_decay[j], geo)
            mix = _ret_readout(p, of, ob, geo)
            xs = _matmul_resid(mix, ret_w_out[j].astype(BF16), xs, mod3, 2, tail)
        h_packed = _normmod(xs, norm_ffn[l], mod3, 3, 4, tail, packed=True)
        moe_w = (router_w[l], router_b[l], exp_gate, exp_up, exp_down, l, sh_gate[l], sh_up[l], sh_down[l])
        if last:
            (out,) = _moe(xs, h_packed, mod3, *moe_w, "final", norm_final, None, tail)
        else:
            xs, h = _moe(xs, h_packed, mod3, *moe_w, "next", norm_mix[l + 1], mods[l + 1], tail)
    return out.reshape(b, n, d)
```

```python
import collections
import functools

import jax
import jax.numpy as jnp
from jax import lax
from jax.experimental import pallas as pl
from jax.experimental.pallas import tpu as pltpu
from jax.experimental.pallas import tpu_sc as plsc

F32 = jnp.float32
BF16 = jnp.bfloat16

D_MODEL = 2048
N_MOD = 6
EPS = 1e-6
GRID_W = 64
ROPE_BASE = 10000.0

HG_HEADS = 8
HG_DK = 128
HG_DV = 128
HG_F = HG_HEADS * HG_DK
HG_V = HG_HEADS * HG_DV
CONV_C = D_MODEL // 2
CONV_W = 31
CONV_HALO = 16
HG_CHUNK = 128

RET_HEADS = 8
RET_DK = D_MODEL // RET_HEADS
RET_DV = 2 * RET_DK
RET_QK = RET_HEADS * RET_DK
RET_V = RET_HEADS * RET_DV
RET_CHUNK = 256

N_EXPERTS = 64
EXPERT_FF = D_MODEL // 4
TOP_K = 8
N_GROUPS = 8
GROUP_SIZE = N_EXPERTS // N_GROUPS
TOPK_GROUPS = 4
ROUTED_SCALE = 2.5
MOE_BLOCK = 512

V7X_SC_CORES = 2
V7X_SC_SUBCORES = 16
SC_ROWS = 64
MOD_ROWS = 16
VMEM_LIMIT = 56 * 1024 * 1024

Geo = collections.namedtuple("Geo", "B N Lc BN BL T")


def _geo(b, n, lc):
    return Geo(b, n, lc, b * n, b * lc, b * n + b * lc)


def _pick_tile(geo, cands):
    for t in cands:
        if geo.N % t == 0 and geo.BL % t == 0:
            return t
    raise ValueError("no row tile fits the sequence lengths")


def _mod_row(i, tm, geo):
    return jnp.where(i < geo.BN // tm, i // (geo.N // tm), geo.B)


def _mod_spec(m, tm, geo, ngrid=1):
    if ngrid == 1:
        return pl.BlockSpec((1, 1, D_MODEL), lambda i: (_mod_row(i, tm, geo) * N_MOD + m, 0, 0))
    return pl.BlockSpec((1, 1, D_MODEL), lambda i, j: (_mod_row(i, tm, geo) * N_MOD + m, 0, j))


def _params(*sem):
    return pltpu.CompilerParams(dimension_semantics=sem, vmem_limit_bytes=VMEM_LIMIT)


def _sigmoid(x):
    return jax.nn.sigmoid(x)


def _silu(x):
    return x * jax.nn.sigmoid(x)


def _adaln_kernel(c_ref, w_ref, b_ref, o_ref):
    a = _silu(c_ref[...]).astype(BF16)
    o_ref[...] = jnp.dot(a, w_ref[...].astype(BF16), preferred_element_type=F32) + b_ref[...]


def _adaln(cond, w, b, layer):
    depth, k, n = w.shape
    tn = 1024
    return pl.pallas_call(
        _adaln_kernel,
        name="adaln",
        out_shape=jax.ShapeDtypeStruct((MOD_ROWS, n), F32),
        grid=(n // tn,),
        in_specs=[pl.BlockSpec((MOD_ROWS, k), lambda j: (0, 0)),
                  pl.BlockSpec((None, k, tn), lambda j: (layer, 0, j)),
                  pl.BlockSpec((None, 1, tn), lambda j: (layer, 0, j))],
        out_specs=pl.BlockSpec((MOD_ROWS, tn), lambda j: (0, j)),
        compiler_params=_params("arbitrary"),
    )(cond, w, b.reshape(depth, 1, n))


PACKED = jnp.int32


def _pack_rows(x):
    n = x.shape[-1] // 2
    bits = lax.bitcast_convert_type(x.astype(BF16).astype(F32), jnp.uint32)
    words = (bits[:, n:] & jnp.uint32(0xFFFF0000)) | (bits[:, :n] >> 16)
    return lax.bitcast_convert_type(words, PACKED)


def _unpack_rows(p):
    u = lax.bitcast_convert_type(p, jnp.uint32)
    lo = lax.bitcast_convert_type(u << 16, F32)
    hi = lax.bitcast_convert_type(u & jnp.uint32(0xFFFF0000), F32)
    return lo, hi


def _normmod_kernel(x_ref, g_ref, sh_ref, sc_ref, o_ref, *, packed):
    x = x_ref[...]
    y = x * lax.rsqrt(jnp.mean(x * x, axis=-1, keepdims=True) + EPS) * g_ref[...]
    h = y * (1.0 + sc_ref[0]) + sh_ref[0]
    o_ref[...] = _pack_rows(h) if packed else h.astype(BF16)


def _normmod(x, g, mod3, m_shift, m_scale, geo, packed):
    tm = _pick_tile(geo, (256, 128))
    spec = pl.BlockSpec((tm, D_MODEL), lambda i: (i, 0))
    if packed:
        out_shape = jax.ShapeDtypeStruct((geo.T, D_MODEL // 2), PACKED)
        out_spec = pl.BlockSpec((tm, D_MODEL // 2), lambda i: (i, 0))
    else:
        out_shape = jax.ShapeDtypeStruct((geo.T, D_MODEL), BF16)
        out_spec = spec
    return pl.pallas_call(
        functools.partial(_normmod_kernel, packed=packed),
        name="normmod",
        out_shape=out_shape,
        grid=(geo.T // tm,),
        in_specs=[spec, pl.BlockSpec((1, D_MODEL), lambda i: (0, 0)),
                  _mod_spec(m_shift, tm, geo), _mod_spec(m_scale, tm, geo)],
        out_specs=out_spec,
        compiler_params=_params("arbitrary"),
    )(x, g.reshape(1, D_MODEL), mod3, mod3)


def _mm_kernel(a_ref, w_ref, o_ref):
    o_ref[...] = jnp.dot(a_ref[...], w_ref[...], preferred_element_type=F32).astype(o_ref.dtype)


def _mm_resid_kernel(a_ref, w_ref, x_ref, m_ref, o_ref):
    y = jnp.dot(a_ref[...], w_ref[...], preferred_element_type=F32)
    o_ref[...] = x_ref[...] + m_ref[0] * y


def _matmul(a, w, geo):
    k, n = w.shape
    tm = _pick_tile(geo, (1024, 512, 256, 128))
    tn = 512
    return pl.pallas_call(
        _mm_kernel,
        name="matmul",
        out_shape=jax.ShapeDtypeStruct((geo.T, n), BF16),
        grid=(geo.T // tm, n // tn),
        in_specs=[pl.BlockSpec((tm, k), lambda i, j: (i, 0)),
                  pl.BlockSpec((k, tn), lambda i, j: (0, j))],
        out_specs=pl.BlockSpec((tm, tn), lambda i, j: (i, j)),
        compiler_params=_params("arbitrary", "arbitrary"),
    )(a, w)


def _matmul_resid(a, w, x, mod3, m_gate, geo):
    k, n = w.shape
    tm = _pick_tile(geo, (1024, 512, 256, 128))
    tn = 512
    return pl.pallas_call(
        _mm_resid_kernel,
        name="matmul_resid",
        out_shape=jax.ShapeDtypeStruct((geo.T, n), F32),
        grid=(geo.T // tm, n // tn),
        in_specs=[pl.BlockSpec((tm, k), lambda i, j: (i, 0)),
                  pl.BlockSpec((k, tn), lambda i, j: (0, j)),
                  pl.BlockSpec((tm, tn), lambda i, j: (i, j)),
                  pl.BlockSpec((1, 1, tn), lambda i, j: (_mod_row(i, tm, geo) * N_MOD + m_gate, 0, j))],
        out_specs=pl.BlockSpec((tm, tn), lambda i, j: (i, j)),
        compiler_params=_params("arbitrary", "arbitrary"),
    )(a, w, x, mod3)


def _chunk_index(b, i, chunk, geo, reverse):
    nc = geo.Lc // chunk
    nl = geo.N // chunk
    ctx0 = (geo.BN + b * geo.Lc) // chunk
    lat0 = (b * geo.N) // chunk
    if reverse:
        return jnp.where(i < nc, ctx0 + (nc - 1 - i), lat0 + (nl - 1 - (i - nc)))
    return jnp.where(i < nc, ctx0 + i, lat0 + (i - nc))


def _split_dot(tri_bf, x):
    hi = x.astype(BF16)
    r1 = x - hi.astype(F32)
    mid = r1.astype(BF16)
    lo = (r1 - mid.astype(F32)).astype(BF16)
    return (jnp.dot(tri_bf, hi, preferred_element_type=F32)
            + jnp.dot(tri_bf, mid, preferred_element_type=F32)
            + jnp.dot(tri_bf, lo, preferred_element_type=F32))


def _hgrn_kernel(qf_ref, ff_ref, vf_ref, qb_ref, fb_ref, vb_ref, lbp_ref, of_ref, ob_ref,
                 stf_ref, stb_ref, *, layer):
    i = pl.program_id(1)

    @pl.when(i == 0)
    def _():
        stf_ref[...] = jnp.zeros_like(stf_ref)
        stb_ref[...] = jnp.zeros_like(stb_ref)

    lbp = lbp_ref[...]
    e = jnp.exp(lbp - jnp.max(lbp, axis=0, keepdims=True))
    sm = e / jnp.sum(e, axis=0, keepdims=True)
    lb = sm[0:1]
    for r in range(1, layer + 1):
        lb = lb + sm[r:r + 1]

    _hgrn_chunk(qf_ref, ff_ref, vf_ref, of_ref, stf_ref, lb, reverse=False)
    _hgrn_chunk(qb_ref, fb_ref, vb_ref, ob_ref, stb_ref, lb, reverse=True)


def _hgrn_chunk(q_ref, f_ref, v_ref, o_ref, st_ref, lb, *, reverse):
    c = HG_CHUNK
    row = lax.broadcasted_iota(jnp.int32, (c, c), 0)
    col = lax.broadcasted_iota(jnp.int32, (c, c), 1)
    tri = (col >= row) if reverse else (col <= row)
    tri_bf = jnp.where(tri, 1.0, 0.0).astype(BF16)
    nt = (((1,), (1,)), ((), ()))

    for h in range(HG_HEADS):
        sl = slice(h * HG_DK, (h + 1) * HG_DK)
        qh = _silu(q_ref[:, sl].astype(F32))
        lbh = lb[:, sl]
        fg = lbh + (1.0 - lbh) * _sigmoid(f_ref[:, sl].astype(F32))
        kh = 1.0 - fg
        bcum = _split_dot(tri_bf, jnp.log(fg))
        bmid = bcum[c // 2:c // 2 + 1]
        bend = bcum[0:1] if reverse else bcum[c - 1:c]
        vb = v_ref[:, sl]
        vh = vb.astype(F32)
        a = (qh * jnp.exp(bcum - bmid)).astype(BF16)
        kd = (kh * jnp.exp(bmid - bcum)).astype(BF16)
        s = lax.dot_general(a, kd, nt, preferred_element_type=F32)
        s = jnp.where(tri, s, 0.0)
        intra = jnp.dot(s.astype(BF16), vb, preferred_element_type=F32)
        st = st_ref[h]
        inter = lax.dot_general((qh * jnp.exp(bcum)).astype(BF16), st.astype(BF16), nt,
                                preferred_element_type=F32)
        o_ref[:, sl] = (inter + intra).astype(BF16)
        kd2 = (kh * jnp.exp(bend - bcum)).astype(BF16)
        st_ref[h] = st * jnp.exp(bend) + jnp.dot(vh.T.astype(BF16), kd2, preferred_element_type=F32)


def _hgrn_scan(p, lb_param, layer, geo):
    c = HG_CHUNK
    steps = (geo.Lc + geo.N) // c

    def spec(colblk, reverse):
        return pl.BlockSpec((c, HG_F), lambda b, i: (_chunk_index(b, i, c, geo, reverse), colblk))

    out = jax.ShapeDtypeStruct((geo.T, HG_V), BF16)
    state = pltpu.VMEM((HG_HEADS, HG_DV, HG_DK), F32)
    return pl.pallas_call(
        functools.partial(_hgrn_kernel, layer=layer),
        name="hgrn_scan",
        out_shape=[out, out],
        grid=(geo.B, steps),
        in_specs=[spec(0, False), spec(1, False), spec(3, False),
                  spec(0, True), spec(2, True), spec(3, True),
                  pl.BlockSpec(lb_param.shape, lambda b, i: (0, 0))],
        out_specs=[spec(0, False), spec(0, True)],
        scratch_shapes=[state, state],
        compiler_params=_params("arbitrary", "arbitrary"),
    )(p, p, p, p, p, p, lb_param)


def _ev_readout_kernel(of_ref, ob_ref, gate_ref, a_ref, b_ref, ap_ref, bp_ref, an_ref, bn_ref,
                       gain_ref, cw_ref, cb_ref, lng_ref, lnb_ref, o_ref, ext_ref, conv_ref, *, tm, geo):
    i = pl.program_id(0)
    n_lat = geo.BN // tm
    tpl = geo.N // tm
    tpc = geo.Lc // tm
    j = jnp.where(i < n_lat, i % tpl, (i - n_lat) % tpc)
    per = jnp.where(i < n_lat, tpl, tpc)
    keep_prev = jnp.where(j == 0, 0.0, 1.0)
    keep_next = jnp.where(j == per - 1, 0.0, 1.0)

    o = of_ref[...].astype(F32) + ob_ref[...].astype(F32)
    r = o * lax.rsqrt(jnp.mean(o * o, axis=-1, keepdims=True) + EPS) * gain_ref[...]
    o_ref[:, :HG_V] = (r * _silu(gate_ref[...].astype(F32))).astype(BF16)

    def glu(x_ref, y_ref):
        return x_ref[...].astype(F32) * _sigmoid(y_ref[...].astype(F32))

    ext_ref[0:CONV_HALO] = glu(ap_ref, bp_ref) * keep_prev
    ext_ref[CONV_HALO:CONV_HALO + tm] = glu(a_ref, b_ref)
    ext_ref[CONV_HALO + tm:2 * CONV_HALO + tm] = glu(an_ref, bn_ref) * keep_next
    off = CONV_HALO - CONV_W // 2
    for cj in range(CONV_C // 128):
        cs = slice(cj * 128, (cj + 1) * 128)
        acc = jnp.broadcast_to(cb_ref[:, cs], (tm, 128))
        for k in range(CONV_W):
            acc = acc + cw_ref[k:k + 1, cs] * ext_ref[off + k:off + k + tm, cs]
        conv_ref[:, cs] = acc
    acc = conv_ref[...]
    mu = jnp.mean(acc, axis=-1, keepdims=True)
    xc = acc - mu
    var = jnp.mean(xc * xc, axis=-1, keepdims=True)
    u = xc * lax.rsqrt(var + EPS) * lng_ref[...] + lnb_ref[...]
    o_ref[:, HG_V:] = _silu(u).astype(BF16)


def _ev_readout(p, of, ob, gain, cw, cb, lng, lnb, geo):
    tm = _pick_tile(geo, (128,))
    hb = tm // CONV_HALO
    nhalo = geo.T // CONV_HALO

    def row(colblk):
        return pl.BlockSpec((tm, HG_V), lambda i: (i, colblk))

    def prev(colblk):
        return pl.BlockSpec((CONV_HALO, CONV_C), lambda i: (jnp.maximum(i * hb - 1, 0), colblk))

    def nxt(colblk):
        return pl.BlockSpec((CONV_HALO, CONV_C), lambda i: (jnp.minimum((i + 1) * hb, nhalo - 1), colblk))

    def vec(n):
        return pl.BlockSpec((n, CONV_C), lambda i: (0, 0))

    return pl.pallas_call(
        functools.partial(_ev_readout_kernel, tm=tm, geo=geo),
        name="ev_readout",
        out_shape=jax.ShapeDtypeStruct((geo.T, HG_V + CONV_C), BF16),
        grid=(geo.T // tm,),
        in_specs=[row(0), row(0), row(4), row(5), row(6), prev(5), prev(6), nxt(5), nxt(6),
                  vec(1), vec(CONV_W), vec(1), vec(1), vec(1)],
        out_specs=pl.BlockSpec((tm, HG_V + CONV_C), lambda i: (i, 0)),
        scratch_shapes=[pltpu.VMEM((tm + 2 * CONV_HALO, CONV_C), F32), pltpu.VMEM((tm, CONV_C), F32)],
        compiler_params=_params("arbitrary"),
    )(of, ob, p, p, p, p, p, p, p, gain.reshape(1, -1), cw, cb.reshape(1, -1),
      lng.reshape(1, -1), lnb.reshape(1, -1))


def _rope(x, cos, sin_signed):
    half = x.shape[-1] // 2
    rot = jnp.concatenate([pltpu.roll(x[:, :half], half // 2, axis=1),
                           pltpu.roll(x[:, half:], half // 2, axis=1)], axis=-1)
    return x * cos + rot * sin_signed


def _ret_kernel(qf_ref, kf_ref, vf_ref, cosf_ref, sinf_ref, qb_ref, kb_ref, vb_ref, cosb_ref, sinb_ref,
                dl_ref, of_ref, ob_ref, sf_ref, sb_ref):
    i = pl.program_id(1)

    @pl.when(i == 0)
    def _():
        sf_ref[...] = jnp.zeros_like(sf_ref)
        sb_ref[...] = jnp.zeros_like(sb_ref)

    _ret_chunk(qf_ref, kf_ref, vf_ref, cosf_ref, sinf_ref, dl_ref[0], of_ref, sf_ref, reverse=False)
    _ret_chunk(qb_ref, kb_ref, vb_ref, cosb_ref, sinb_ref, dl_ref[1], ob_ref, sb_ref, reverse=True)


def _ret_chunk(q_ref, k_ref, v_ref, cos_ref, sin_ref, dl, o_ref, s_ref, *, reverse):
    c = RET_CHUNK
    lg_all = -jnp.log1p(jnp.exp(-dl))
    row = lax.broadcasted_iota(jnp.int32, (c, c), 0)
    col = lax.broadcasted_iota(jnp.int32, (c, c), 1)
    idx = lax.broadcasted_iota(jnp.int32, (c, 1), 0).astype(F32)
    if reverse:
        live = col >= row
        dist = (col - row).astype(F32)
        q_pow = c - idx
        k_pow = idx
    else:
        live = row >= col
        dist = (row - col).astype(F32)
        q_pow = idx + 1.0
        k_pow = c - 1.0 - idx
    cos = cos_ref[...]
    sin = sin_ref[...]
    nt = (((1,), (1,)), ((), ()))

    for h in range(RET_HEADS):
        lg = lg_all[h:h + 1]
        dmask = jnp.where(live, jnp.exp(lg * dist), 0.0)
        q = _rope(q_ref[:, h * RET_DK:(h + 1) * RET_DK].astype(F32), cos, sin)
        k = _rope(k_ref[:, h * RET_DK:(h + 1) * RET_DK].astype(F32) * (RET_DK ** -0.5), cos, sin)
        vb = v_ref[:, h * RET_DV:(h + 1) * RET_DV]
        scores = lax.dot_general(q.astype(BF16), k.astype(BF16), nt, preferred_element_type=F32) * dmask
        intra = jnp.dot(scores.astype(BF16), vb, preferred_element_type=F32)
        s = s_ref[h]
        inter = jnp.dot((q * jnp.exp(lg * q_pow)).astype(BF16), s.astype(BF16),
                        preferred_element_type=F32)
        o_ref[:, h * RET_DV:(h + 1) * RET_DV] = (inter + intra).astype(BF16)
        kdec = (k * jnp.exp(lg * k_pow)).T.astype(BF16)
        s_ref[h] = jnp.exp(lg * c) * s + jnp.dot(kdec, vb, preferred_element_type=F32)


def _ret_scan(p, cos_tab, sin_tab, decay_logit, geo):
    c = RET_CHUNK
    steps = (geo.Lc + geo.N) // c
    nc = geo.Lc // c
    nl = geo.N // c

    def spec(width, colblk, reverse):
        return pl.BlockSpec((c, width), lambda b, i: (_chunk_index(b, i, c, geo, reverse), colblk))

    def tab(reverse):
        def index(b, i):
            lat = (nl - 1 - (i - nc)) if reverse else (i - nc)
            return (jnp.where(i < nc, nl, lat), 0)
        return pl.BlockSpec((c, RET_DK), index)

    def direction(reverse):
        return [spec(RET_QK, 0, reverse), spec(RET_QK, 1, reverse), spec(RET_V, 1, reverse),
                tab(reverse), tab(reverse)]

    out = jax.ShapeDtypeStruct((geo.T, RET_V), BF16)
    state = pltpu.VMEM((RET_HEADS, RET_DK, RET_DV), F32)
    return pl.pallas_call(
        _ret_kernel,
        name="ret_scan",
        out_shape=[out, out],
        grid=(geo.B, steps),
        in_specs=direction(False) + direction(True)
        + [pl.BlockSpec((2, RET_HEADS, 1), lambda b, i: (0, 0, 0))],
        out_specs=[spec(RET_V, 0, False), spec(RET_V, 0, True)],
        scratch_shapes=[state, state],
        compiler_params=_params("arbitrary", "arbitrary"),
    )(p, p, p, cos_tab, sin_tab, p, p, p, cos_tab, sin_tab, decay_logit.reshape(2, RET_HEADS, 1))


def _rope_tables(n):
    t = jnp.arange(n)
    quarter = RET_DK // 4
    inv = 1.0 / (ROPE_BASE ** (jnp.arange(quarter, dtype=F32) / quarter))
    ang_r = (t // GRID_W).astype(F32)[:, None] * inv
    ang_c = (t % GRID_W).astype(F32)[:, None] * inv
    cos = jnp.concatenate([jnp.cos(ang_r), jnp.cos(ang_r), jnp.cos(ang_c), jnp.cos(ang_c)], axis=-1)
    sin = jnp.concatenate([-jnp.sin(ang_r), jnp.sin(ang_r), -jnp.sin(ang_c), jnp.sin(ang_c)], axis=-1)
    cos = jnp.concatenate([cos, jnp.ones((RET_CHUNK, RET_DK), F32)], axis=0)
    sin = jnp.concatenate([sin, jnp.zeros((RET_CHUNK, RET_DK), F32)], axis=0)
    return cos, sin


def _ret_readout_kernel(of_ref, ob_ref, gate_ref, o_ref):
    for h in range(RET_HEADS):
        sl = slice(h * RET_DV, (h + 1) * RET_DV)
        o = of_ref[:, sl].astype(F32) + ob_ref[:, sl].astype(F32)
        r = o * lax.rsqrt(jnp.mean(o * o, axis=-1, keepdims=True) + EPS)
        o_ref[:, sl] = (_silu(gate_ref[:, sl].astype(F32)) * r).astype(BF16)


def _ret_readout(p, of, ob, geo):
    tm = _pick_tile(geo, (128,))
    spec = pl.BlockSpec((tm, RET_V), lambda i: (i, 0))
    return pl.pallas_call(
        _ret_readout_kernel,
        name="ret_readout",
        out_shape=jax.ShapeDtypeStruct((geo.T, RET_V), BF16),
        grid=(geo.T // tm,),
        in_specs=[spec, spec, pl.BlockSpec((tm, RET_V), lambda i: (i, 2))],
        out_specs=spec,
        compiler_params=_params("arbitrary"),
    )(of, ob, p)


def _router_kernel(h_ref, rw_ref, rb_ref, eidx_ref, rank_ref, w_ref, cnt_ref, carry_ref, *, tm):
    i = pl.program_id(0)

    @pl.when(i == 0)
    def _():
        carry_ref[...] = jnp.zeros_like(carry_ref)

    half = D_MODEL // 2
    h_lo, h_hi = _unpack_rows(h_ref[...])
    h_lo = h_lo.astype(BF16)
    h_hi = h_hi.astype(BF16)
    rest = rw_ref[...]
    logits = jnp.zeros((tm, N_EXPERTS), F32)
    for _ in range(3):
        part = rest.astype(BF16)
        rest = rest - part.astype(F32)
        logits = (logits + jnp.dot(h_lo, part[:half], preferred_element_type=F32)
                  + jnp.dot(h_hi, part[half:], preferred_element_type=F32))
    s = _sigmoid(logits)
    sel = s + rb_ref[...]
    lane = lax.broadcasted_iota(jnp.int32, (tm, N_EXPERTS), 1).astype(F32)
    grp = jnp.floor(lane * (1.0 / GROUP_SIZE))
    ninf = -jnp.inf
    none = float(N_EXPERTS)

    gscore = jnp.zeros((tm, N_EXPERTS), F32)
    gcols = []
    for g in range(N_GROUPS):
        in_g = grp == float(g)
        v1 = jnp.max(jnp.where(in_g, sel, ninf), axis=-1, keepdims=True)
        i1 = jnp.min(jnp.where(in_g & (sel == v1), lane, none), axis=-1, keepdims=True)
        v2 = jnp.max(jnp.where(in_g & (lane != i1), sel, ninf), axis=-1, keepdims=True)
        gcols.append(v1 + v2)
        gscore = jnp.where(in_g, v1 + v2, gscore)
    beaten = jnp.zeros((tm, N_EXPERTS), F32)
    for g in range(N_GROUPS):
        wins = (gcols[g] > gscore) | ((gcols[g] == gscore) & (float(g) < grp))
        beaten = beaten + jnp.where(wins, 1.0, 0.0)
    cand = jnp.where(beaten < float(TOPK_GROUPS), sel, ninf)

    lane_k = lax.broadcasted_iota(jnp.int32, (tm, TOP_K), 1)
    eidx = jnp.zeros((tm, TOP_K), F32)
    wsel = jnp.zeros((tm, TOP_K), F32)
    chosen = jnp.zeros((tm, N_EXPERTS), F32)
    picks = []
    for k in range(TOP_K):
        v = jnp.max(cand, axis=-1, keepdims=True)
        ik = jnp.min(jnp.where(cand == v, lane, none), axis=-1, keepdims=True)
        hit = lane == ik
        picks.append(ik)
        eidx = jnp.where(lane_k == k, ik, eidx)
        wsel = jnp.where(lane_k == k, jnp.sum(jnp.where(hit, s, 0.0), axis=-1, keepdims=True), wsel)
        chosen = jnp.where(hit, 1.0, chosen)
        cand = jnp.where(hit, ninf, cand)
    w_ref[...] = wsel / jnp.sum(wsel, axis=-1, keepdims=True) * ROUTED_SCALE
    eidx_ref[...] = eidx.astype(jnp.int32)

    r = lax.broadcasted_iota(jnp.int32, (tm, tm), 0)
    c = lax.broadcasted_iota(jnp.int32, (tm, tm), 1)
    below = jnp.where(c < r, 1.0, 0.0).astype(BF16)
    carry = carry_ref[...]
    pos = jnp.dot(below, chosen.astype(BF16), preferred_element_type=F32) + carry
    rank = jnp.zeros((tm, TOP_K), jnp.int32)
    for k in range(TOP_K):
        rk = jnp.sum(jnp.where(lane == picks[k], pos, 0.0), axis=-1, keepdims=True)
        rank = jnp.where(lane_k == k, rk.astype(jnp.int32), rank)
    rank_ref[...] = rank
    carry = carry + jnp.sum(chosen, axis=0, keepdims=True)
    carry_ref[...] = carry
    cnt_ref[...] = carry


def _router(h_packed, rw, rb, geo):
    tm = 256 if geo.T % 256 == 0 else 128
    tok = pl.BlockSpec((tm, TOP_K), lambda i: (i, 0))
    one = pl.BlockSpec((1, N_EXPERTS), lambda i: (0, 0))
    return pl.pallas_call(
        functools.partial(_router_kernel, tm=tm),
        name="router",
        out_shape=[jax.ShapeDtypeStruct((geo.T, TOP_K), jnp.int32),
                   jax.ShapeDtypeStruct((geo.T, TOP_K), jnp.int32),
                   jax.ShapeDtypeStruct((geo.T, TOP_K), F32),
                   jax.ShapeDtypeStruct((1, N_EXPERTS), F32)],
        grid=(geo.T // tm,),
        in_specs=[pl.BlockSpec((tm, D_MODEL // 2), lambda i: (i, 0)),
                  pl.BlockSpec((D_MODEL, N_EXPERTS), lambda i: (0, 0)), one],
        out_specs=[tok, tok, tok, one],
        scratch_shapes=[pltpu.VMEM((1, N_EXPERTS), F32)],
        compiler_params=_params("arbitrary"),
    )(h_packed, rw, rb.reshape(1, N_EXPERTS))


def _n_blocks(geo):
    return -(-(geo.T * TOP_K) // MOE_BLOCK) + N_EXPERTS


def _dest_kernel(cnt_ref, eidx_ref, rank_ref, dest_ref, blk_ref, *, n_blocks):
    eidx = eidx_ref[...]
    dest = rank_ref[...]
    blk_row = (lax.broadcasted_iota(jnp.int32, blk_ref.shape, 0) * 128
               + lax.broadcasted_iota(jnp.int32, blk_ref.shape, 1)) * MOE_BLOCK
    blk = jnp.zeros(blk_ref.shape, jnp.int32)
    start = jnp.int32(0)
    for e in range(N_EXPERTS):
        padded = (cnt_ref[e] + (MOE_BLOCK - 1)) // MOE_BLOCK * MOE_BLOCK
        dest = dest + jnp.where(eidx == e, start, 0)
        start = start + padded
        blk = blk + jnp.where(start <= blk_row, 1, 0)
    dest_ref[...] = dest
    blk_ref[...] = jnp.where(blk_row == n_blocks * MOE_BLOCK, start // MOE_BLOCK,
                             jnp.minimum(blk, N_EXPERTS - 1))


def _dest(counts, eidx, rank, geo):
    rows = geo.T * TOP_K // 128
    brow = -(-(_n_blocks(geo) + 1) // 128)
    full = pl.BlockSpec((rows, 128), lambda: (0, 0))
    dest, blk = pl.pallas_call(
        functools.partial(_dest_kernel, n_blocks=_n_blocks(geo)),
        name="dest",
        out_shape=[jax.ShapeDtypeStruct((rows, 128), jnp.int32),
                   jax.ShapeDtypeStruct((brow, 128), jnp.int32)],
        in_specs=[pl.BlockSpec(memory_space=pltpu.SMEM), full, full],
        out_specs=[full, pl.BlockSpec((brow, 128), lambda: (0, 0))],
    )(counts, eidx.reshape(rows, 128), rank.reshape(rows, 128))
    return dest.reshape(-1), blk.reshape(-1)[:_n_blocks(geo) + 1]


def _zero_fill(cnt_ref, xs_hbm, zero_ref, zsem, n_rows, wait):
    def piece(pos, size):
        if size >= 8:
            copies = [(pl.multiple_of(pos, 8), size)]
        else:
            copies = [(pos + r, 1) for r in range(size)]
        for p, s in copies:
            cp = pltpu.make_async_copy(zero_ref.at[pl.ds(0, s)], xs_hbm.at[pl.ds(p, s)], zsem)
            cp.wait() if wait else cp.start()

    def per_expert(e, start):
        cnt = cnt_ref[e]
        padded = (cnt + (MOE_BLOCK - 1)) // MOE_BLOCK * MOE_BLOCK
        pad = padded - cnt
        pos = start + cnt
        size = 1
        while size < MOE_BLOCK:
            take = (pad & size) != 0
            pl.when(take)(functools.partial(piece, pos, size))
            pos = pos + jnp.where(take, size, 0)
            size *= 2
        return start + padded

    end = lax.fori_loop(0, N_EXPERTS, per_expert, jnp.int32(0))

    def per_block(j, carry):
        piece(end + j * MOE_BLOCK, MOE_BLOCK)
        return carry

    lax.fori_loop(0, (n_rows - end) // MOE_BLOCK, per_block, 0)


def _sc_scatter_rows(src, order, n_rows):
    nw = V7X_SC_CORES * V7X_SC_SUBCORES
    t, w = src.shape
    per = t // nw
    assert t % (nw * SC_ROWS) == 0
    mesh = plsc.VectorSubcoreMesh(core_axis_name="c", subcore_axis_name="s")

    @functools.partial(
        pl.kernel, mesh=mesh,
        out_type=jax.ShapeDtypeStruct((n_rows, w), src.dtype),
        scratch_types=[pltpu.VMEM((SC_ROWS, w), src.dtype)]
        + [pltpu.VMEM((SC_ROWS,), jnp.int32)] * TOP_K + [pltpu.SemaphoreType.DMA],
    )
    def scatter(src_hbm, idx_hbm, out_hbm, rows_v, *rest):
        idx_vs, sem = rest[:TOP_K], rest[TOP_K]
        wid = lax.axis_index("s") * V7X_SC_CORES + lax.axis_index("c")
        base = wid * per

        @pl.loop(0, per // SC_ROWS)
        def _(j):
            off = pl.multiple_of(base + j * SC_ROWS, 8)
            pltpu.sync_copy(src_hbm.at[pl.ds(off, SC_ROWS)], rows_v)
            for k in range(TOP_K):
                pltpu.sync_copy(idx_hbm.at[pl.ds(pl.multiple_of(k * t + off, 8), SC_ROWS)], idx_vs[k])
            copies = [pltpu.async_copy(rows_v, out_hbm.at[idx_vs[k]], sem) for k in range(TOP_K)]
            for cp in copies:
                cp.wait()

    return scatter(src, order)


def _zero_pad_kernel(cnt_ref, xs_in, xs_hbm, zero_ref, zsem, *, n_rows):
    del xs_in
    zero_ref[...] = jnp.zeros_like(zero_ref)
    _zero_fill(cnt_ref, xs_hbm, zero_ref, zsem, n_rows, wait=False)
    _zero_fill(cnt_ref, xs_hbm, zero_ref, zsem, n_rows, wait=True)


def _dispatch(counts, order, h_packed, geo):
    n_rows = _n_blocks(geo) * MOE_BLOCK
    width = D_MODEL // 2
    xs = _sc_scatter_rows(h_packed[:geo.T], order, n_rows)
    return pl.pallas_call(
        functools.partial(_zero_pad_kernel, n_rows=n_rows),
        name="zero_pad",
        out_shape=jax.ShapeDtypeStruct((n_rows, width), PACKED),
        in_specs=[pl.BlockSpec(memory_space=pltpu.SMEM), pl.BlockSpec(memory_space=pl.ANY)],
        out_specs=pl.BlockSpec(memory_space=pl.ANY),
        scratch_shapes=[pltpu.VMEM((MOE_BLOCK, width), PACKED), pltpu.SemaphoreType.DMA],
        input_output_aliases={1: 0},
    )(counts, xs)


def _swiglu_packed(x_ref, wg_ref, wu_ref, wd_ref):
    half = D_MODEL // 2
    lo, hi = _unpack_rows(x_ref[...])
    lo = lo.astype(BF16)
    hi = hi.astype(BF16)

    def proj(w_ref):
        return (jnp.dot(lo, w_ref[:half], preferred_element_type=F32)
                + jnp.dot(hi, w_ref[half:], preferred_element_type=F32))

    g = proj(wg_ref)
    u = proj(wu_ref)
    return jnp.dot((_silu(g) * u).astype(BF16), wd_ref[...], preferred_element_type=F32)


def _expert_kernel(blk_ref, x_ref, wg_ref, wu_ref, wd_ref, o_ref, wgb_ref, wub_ref, wdb_ref, *, nb):
    j = pl.program_id(0)
    n_used = blk_ref[nb]

    @pl.when((j == 0) | (blk_ref[j] != blk_ref[jnp.maximum(j - 1, 0)]))
    def _():
        wgb_ref[...] = wg_ref[...].astype(BF16)
        wub_ref[...] = wu_ref[...].astype(BF16)
        wdb_ref[...] = wd_ref[...].astype(BF16)

    @pl.when(j < n_used)
    def _():
        o_ref[...] = _pack_rows(_swiglu_packed(x_ref, wgb_ref, wub_ref, wdb_ref))

    @pl.when(j >= n_used)
    def _():
        o_ref[...] = jnp.zeros_like(o_ref)


def _experts(blk_e, xs, wg, wu, wd, layer, geo):
    nb = _n_blocks(geo)
    rows = pl.BlockSpec((MOE_BLOCK, D_MODEL // 2), lambda j, be: (j, 0))

    def wspec(r, c):
        return pl.BlockSpec((None, None, r, c), lambda j, be: (layer, be[j], 0, 0))

    return pl.pallas_call(
        functools.partial(_expert_kernel, nb=nb),
        name="experts",
        out_shape=jax.ShapeDtypeStruct((nb * MOE_BLOCK, D_MODEL // 2), PACKED),
        grid_spec=pltpu.PrefetchScalarGridSpec(
            num_scalar_prefetch=1,
            grid=(nb,),
            in_specs=[rows, wspec(D_MODEL, EXPERT_FF), wspec(D_MODEL, EXPERT_FF),
                      wspec(EXPERT_FF, D_MODEL)],
            out_specs=rows,
            scratch_shapes=[pltpu.VMEM((D_MODEL, EXPERT_FF), BF16),
                            pltpu.VMEM((D_MODEL, EXPERT_FF), BF16),
                            pltpu.VMEM((EXPERT_FF, D_MODEL), BF16)]),
        compiler_params=_params("arbitrary"),
    )(blk_e, xs, wg, wu, wd)


def _sc_gather_rows(table, idx):
    nw = V7X_SC_CORES * V7X_SC_SUBCORES
    m = idx.shape[0]
    w = table.shape[1]
    per = m // nw
    assert m % (nw * SC_ROWS) == 0
    mesh = plsc.VectorSubcoreMesh(core_axis_name="c", subcore_axis_name="s")

    @functools.partial(
        pl.kernel, mesh=mesh,
        out_type=jax.ShapeDtypeStruct((m, w), table.dtype),
        scratch_types=[pltpu.VMEM((SC_ROWS,), jnp.int32),
                       pltpu.VMEM((SC_ROWS, w), table.dtype),
                       pltpu.SemaphoreType.DMA],
    )
    def gather(table_hbm, idx_hbm, out_hbm, idx_v, rows_v, sem):
        wid = lax.axis_index("s") * V7X_SC_CORES + lax.axis_index("c")
        base = wid * per

        @pl.loop(0, per // SC_ROWS)
        def _(j):
            off = pl.multiple_of(base + j * SC_ROWS, 8)
            pltpu.sync_copy(idx_hbm.at[pl.ds(off, SC_ROWS)], idx_v)
            pltpu.async_copy(table_hbm.at[idx_v], rows_v, sem).wait()
            pltpu.sync_copy(rows_v, out_hbm.at[pl.ds(off, SC_ROWS)])

    return gather(table, idx)


def _combine_kernel(w_ref, x_ref, h_ref, sg_ref, su_ref, sd_ref, m_ref, *rest, post):
    if post == "next":
        ng_ref, nsh_ref, nsc_ref, g_ref, o_ref, hn_ref = rest
    else:
        ng_ref, g_ref, o_ref = rest
    half = D_MODEL // 2
    shared = _swiglu_packed(h_ref, sg_ref, su_ref, sd_ref)
    w = w_ref[...]
    acc_lo = shared[:, :half]
    acc_hi = shared[:, half:]
    for k in range(TOP_K):
        lo, hi = _unpack_rows(g_ref[k])
        acc_lo = acc_lo + w[:, k:k + 1] * lo
        acc_hi = acc_hi + w[:, k:k + 1] * hi
    gate = m_ref[0]
    y_lo = x_ref[:, :half] + gate[:, :half] * acc_lo
    y_hi = x_ref[:, half:] + gate[:, half:] * acc_hi
    ms = (jnp.sum(y_lo * y_lo, axis=-1, keepdims=True)
          + jnp.sum(y_hi * y_hi, axis=-1, keepdims=True)) * (1.0 / D_MODEL)
    inv = lax.rsqrt(ms + EPS)
    ng = ng_ref[...]
    n_lo = y_lo * inv * ng[:, :half]
    n_hi = y_hi * inv * ng[:, half:]
    if post == "next":
        o_ref[:, :half] = y_lo
        o_ref[:, half:] = y_hi
        sc = nsc_ref[0]
        sft = nsh_ref[0]
        hn_ref[:, :half] = (n_lo * (1.0 + sc[:, :half]) + sft[:, :half]).astype(BF16)
        hn_ref[:, half:] = (n_hi * (1.0 + sc[:, half:]) + sft[:, half:]).astype(BF16)
    else:
        o_ref[:, :half] = n_lo
        o_ref[:, half:] = n_hi


def _combine(order, w, x, h_packed, sg, su, sd, ys, mod3, m_gate, post, norm_g, next_mod3, geo):
    half = D_MODEL // 2
    gathered = _sc_gather_rows(ys, order).reshape(TOP_K, geo.T, half)
    tt = _pick_tile(geo, (256, 128))
    n = geo.T // tt
    ff = sg.shape[1]
    rows = pl.BlockSpec((tt, D_MODEL), lambda i: (i, 0))
    vec = pl.BlockSpec((1, D_MODEL), lambda i: (0, 0))
    in_specs = [pl.BlockSpec((tt, TOP_K), lambda i: (i, 0)),
                rows,
                pl.BlockSpec((tt, half), lambda i: (i, 0)),
                pl.BlockSpec((D_MODEL, ff), lambda i: (0, 0)),
                pl.BlockSpec((D_MODEL, ff), lambda i: (0, 0)),
                pl.BlockSpec((ff, D_MODEL), lambda i: (0, 0)),
                _mod_spec(m_gate, tt, geo), vec]
    args = [w, x, h_packed, sg, su, sd, mod3, norm_g.reshape(1, D_MODEL)]
    out_shape = [jax.ShapeDtypeStruct((geo.T, D_MODEL), F32)]
    out_specs = [rows]
    if post == "next":
        in_specs += [_mod_spec(0, tt, geo), _mod_spec(1, tt, geo)]
        args += [next_mod3, next_mod3]
        out_shape.append(jax.ShapeDtypeStruct((geo.T, D_MODEL), BF16))
        out_specs.append(rows)
    in_specs.append(pl.BlockSpec((TOP_K, tt, half), lambda i: (0, i, 0)))
    args.append(gathered)
    return pl.pallas_call(
        functools.partial(_combine_kernel, post=post),
        name="combine",
        out_shape=out_shape,
        grid=(n,),
        in_specs=in_specs,
        out_specs=out_specs,
        compiler_params=_params("arbitrary"),
    )(*args)


def _moe(x, h_packed, mod3, rw, rb, wg, wu, wd, layer, sg, su, sd, post, norm_g, next_mod3, geo):
    eidx, rank, w, counts = _router(h_packed, rw, rb, geo)
    counts = counts.reshape(N_EXPERTS).astype(jnp.int32)
    dest, blk_e = _dest(counts, eidx, rank, geo)
    order = dest.reshape(geo.T, TOP_K).T.reshape(-1)
    xs = _dispatch(counts, order, h_packed, geo)
    ys = _experts(blk_e, xs, wg, wu, wd, layer, geo)
    return _combine(order, w, x, h_packed, sg.astype(BF16), su.astype(BF16), sd.astype(BF16), ys,
                    mod3, 5, post, norm_g, next_mod3, geo)


def kernel(x, c, ctx, c_ctx, ada_w, ada_b, norm_mix, norm_ffn, norm_final, ev_w_in, ev_w_out, hgrn_lb, hgrn_norm, conv_w, conv_b, conv_norm_g, conv_norm_b, ret_w_in, ret_w_out, ret_decay, router_w, router_b, exp_gate, exp_up, exp_down, sh_gate, sh_up, sh_down):
    b, n, d = x.shape
    lc = ctx.shape[1]
    depth = ada_w.shape[0]
    geo = _geo(b, n, lc)
    assert d == D_MODEL and b < MOD_ROWS
    assert n % RET_CHUNK == 0 and lc % RET_CHUNK == 0

    xs = jnp.concatenate([x.reshape(geo.BN, d), ctx.reshape(geo.BL, d)], axis=0)
    cond = jnp.zeros((MOD_ROWS, d), F32).at[:b].set(c).at[b].set(c_ctx)
    cos_tab, sin_tab = _rope_tables(n)

    mods = [_adaln(cond, ada_w, ada_b, l).reshape(MOD_ROWS * N_MOD, 1, d) for l in range(depth)]
    h = _normmod(xs, norm_mix[0], mods[0], 0, 1, geo, packed=False)
    for l in range(depth):
        j = l // 2
        last = l == depth - 1
        mod3 = mods[l]
        tail = geo._replace(T=geo.BN, BL=0, Lc=0) if last else geo
        if l % 2 == 0:
            p = _matmul(h, ev_w_in[j].astype(BF16), geo)
            of, ob = _hgrn_scan(p, hgrn_lb, l, geo)
            mix = _ev_readout(p, of, ob, hgrn_norm[j], conv_w[j], conv_b[j],
                              conv_norm_g[j], conv_norm_b[j], geo)
            xs = _matmul_resid(mix, ev_w_out[j].astype(BF16), xs, mod3, 2, tail)
        else:
            p = _matmul(h, ret_w_in[j].astype(BF16), geo)
            of, ob = _ret_scan(p, cos_tab, sin_tab, ret_decay[j], geo)
            mix = _ret_readout(p, of, ob, geo)
            xs = _matmul_resid(mix, ret_w_out[j].astype(BF16), xs, mod3, 2, tail)
        h_packed = _normmod(xs, norm_ffn[l], mod3, 3, 4, tail, packed=True)
        moe_w = (router_w[l], router_b[l], exp_gate, exp_up, exp_down, l, sh_gate[l], sh_up[l], sh_down[l])
        if last:
            (out,) = _moe(xs, h_packed, mod3, *moe_w, "final", norm_final, None, tail)
        else:
            xs, h = _moe(xs, h_packed, mod3, *moe_w, "next", norm_mix[l + 1], mods[l + 1], tail)
    return out.reshape(b, n, d)
```

```python
import collections
import functools

import jax
import jax.numpy as jnp
from jax import lax
from jax.experimental import pallas as pl
from jax.experimental.pallas import tpu as pltpu
from jax.experimental.pallas import tpu_sc as plsc

F32 = jnp.float32
BF16 = jnp.bfloat16

D_MODEL = 2048
N_MOD = 6
EPS = 1e-6
GRID_W = 64
ROPE_BASE = 10000.0

HG_HEADS = 8
HG_DK = 128
HG_DV = 128
HG_F = HG_HEADS * HG_DK
HG_V = HG_HEADS * HG_DV
CONV_C = D_MODEL // 2
CONV_W = 31
CONV_HALO = 16
HG_CHUNK = 128

RET_HEADS = 8
RET_DK = D_MODEL // RET_HEADS
RET_DV = 2 * RET_DK
RET_QK = RET_HEADS * RET_DK
RET_V = RET_HEADS * RET_DV
RET_CHUNK = 256

N_EXPERTS = 64
EXPERT_FF = D_MODEL // 4
TOP_K = 8
N_GROUPS = 8
GROUP_SIZE = N_EXPERTS // N_GROUPS
TOPK_GROUPS = 4
ROUTED_SCALE = 2.5
MOE_BLOCK = 512

V7X_SC_CORES = 2
V7X_SC_SUBCORES = 16
SC_ROWS = 32
MOD_ROWS = 16
VMEM_LIMIT = 56 * 1024 * 1024

Geo = collections.namedtuple("Geo", "B N Lc BN BL T")


def _geo(b, n, lc):
    return Geo(b, n, lc, b * n, b * lc, b * n + b * lc)


def _pick_tile(geo, cands):
    for t in cands:
        if geo.N % t == 0 and geo.BL % t == 0:
            return t
    raise ValueError("no row tile fits the sequence lengths")


def _mod_row(i, tm, geo):
    return jnp.where(i < geo.BN // tm, i // (geo.N // tm), geo.B)


def _mod_spec(m, tm, geo, ngrid=1):
    if ngrid == 1:
        return pl.BlockSpec((1, 1, D_MODEL), lambda i: (_mod_row(i, tm, geo) * N_MOD + m, 0, 0))
    return pl.BlockSpec((1, 1, D_MODEL), lambda i, j: (_mod_row(i, tm, geo) * N_MOD + m, 0, j))


def _params(*sem):
    return pltpu.CompilerParams(dimension_semantics=sem, vmem_limit_bytes=VMEM_LIMIT)


def _sigmoid(x):
    return jax.nn.sigmoid(x)


def _silu(x):
    return x * jax.nn.sigmoid(x)


def _adaln_kernel(c_ref, w_ref, b_ref, o_ref):
    a = _silu(c_ref[...]).astype(BF16)
    o_ref[...] = jnp.dot(a, w_ref[...].astype(BF16), preferred_element_type=F32) + b_ref[...]


def _adaln(cond, w, b, layer):
    depth, k, n = w.shape
    tn = 1024
    return pl.pallas_call(
        _adaln_kernel,
        name="adaln",
        out_shape=jax.ShapeDtypeStruct((MOD_ROWS, n), F32),
        grid=(n // tn,),
        in_specs=[pl.BlockSpec((MOD_ROWS, k), lambda j: (0, 0)),
                  pl.BlockSpec((None, k, tn), lambda j: (layer, 0, j)),
                  pl.BlockSpec((None, 1, tn), lambda j: (layer, 0, j))],
        out_specs=pl.BlockSpec((MOD_ROWS, tn), lambda j: (0, j)),
        compiler_params=_params("arbitrary"),
    )(cond, w, b.reshape(depth, 1, n))


PACKED = jnp.int32


def _pack_rows(x):
    n = x.shape[-1] // 2
    bits = lax.bitcast_convert_type(x.astype(BF16).astype(F32), jnp.uint32)
    words = (bits[:, n:] & jnp.uint32(0xFFFF0000)) | (bits[:, :n] >> 16)
    return lax.bitcast_convert_type(words, PACKED)


def _unpack_rows(p):
    u = lax.bitcast_convert_type(p, jnp.uint32)
    lo = lax.bitcast_convert_type(u << 16, F32)
    hi = lax.bitcast_convert_type(u & jnp.uint32(0xFFFF0000), F32)
    return lo, hi


def _normmod_kernel(x_ref, g_ref, sh_ref, sc_ref, o_ref, *, packed):
    x = x_ref[...]
    y = x * lax.rsqrt(jnp.mean(x * x, axis=-1, keepdims=True) + EPS) * g_ref[...]
    h = y * (1.0 + sc_ref[0]) + sh_ref[0]
    o_ref[...] = _pack_rows(h) if packed else h.astype(BF16)


def _normmod(x, g, mod3, m_shift, m_scale, geo, packed):
    tm = _pick_tile(geo, (256, 128))
    spec = pl.BlockSpec((tm, D_MODEL), lambda i: (i, 0))
    if packed:
        out_shape = jax.ShapeDtypeStruct((geo.T, D_MODEL // 2), PACKED)
        out_spec = pl.BlockSpec((tm, D_MODEL // 2), lambda i: (i, 0))
    else:
        out_shape = jax.ShapeDtypeStruct((geo.T, D_MODEL), BF16)
        out_spec = spec
    return pl.pallas_call(
        functools.partial(_normmod_kernel, packed=packed),
        name="normmod",
        out_shape=out_shape,
        grid=(geo.T // tm,),
        in_specs=[spec, pl.BlockSpec((1, D_MODEL), lambda i: (0, 0)),
                  _mod_spec(m_shift, tm, geo), _mod_spec(m_scale, tm, geo)],
        out_specs=out_spec,
        compiler_params=_params("arbitrary"),
    )(x, g.reshape(1, D_MODEL), mod3, mod3)


def _mm_kernel(a_ref, w_ref, o_ref):
    o_ref[...] = jnp.dot(a_ref[...], w_ref[...], preferred_element_type=F32).astype(o_ref.dtype)


def _mm_resid_kernel(a_ref, w_ref, x_ref, m_ref, o_ref):
    y = jnp.dot(a_ref[...], w_ref[...], preferred_element_type=F32)
    o_ref[...] = x_ref[...] + m_ref[0] * y


def _matmul(a, w, geo):
    k, n = w.shape
    tm = _pick_tile(geo, (1024, 512, 256, 128))
    tn = 1024 if n % 1024 == 0 else 512
    return pl.pallas_call(
        _mm_kernel,
        name="matmul",
        out_shape=jax.ShapeDtypeStruct((geo.T, n), BF16),
        grid=(geo.T // tm, n // tn),
        in_specs=[pl.BlockSpec((tm, k), lambda i, j: (i, 0)),
                  pl.BlockSpec((k, tn), lambda i, j: (0, j))],
        out_specs=pl.BlockSpec((tm, tn), lambda i, j: (i, j)),
        compiler_params=_params("arbitrary", "arbitrary"),
    )(a, w)


def _matmul_resid(a, w, x, mod3, m_gate, geo):
    k, n = w.shape
    tm = _pick_tile(geo, (1024, 512, 256, 128))
    tn = 512
    return pl.pallas_call(
        _mm_resid_kernel,
        name="matmul_resid",
        out_shape=jax.ShapeDtypeStruct((geo.T, n), F32),
        grid=(geo.T // tm, n // tn),
        in_specs=[pl.BlockSpec((tm, k), lambda i, j: (i, 0)),
                  pl.BlockSpec((k, tn), lambda i, j: (0, j)),
                  pl.BlockSpec((tm, tn), lambda i, j: (i, j)),
                  pl.BlockSpec((1, 1, tn), lambda i, j: (_mod_row(i, tm, geo) * N_MOD + m_gate, 0, j))],
        out_specs=pl.BlockSpec((tm, tn), lambda i, j: (i, j)),
        compiler_params=_params("arbitrary", "arbitrary"),
    )(a, w, x, mod3)


def _chunk_index(b, i, chunk, geo, reverse):
    nc = geo.Lc // chunk
    nl = geo.N // chunk
    ctx0 = (geo.BN + b * geo.Lc) // chunk
    lat0 = (b * geo.N) // chunk
    if reverse:
        return jnp.where(i < nc, ctx0 + (nc - 1 - i), lat0 + (nl - 1 - (i - nc)))
    return jnp.where(i < nc, ctx0 + i, lat0 + (i - nc))


def _split_dot(tri_bf, x):
    hi = x.astype(BF16)
    r1 = x - hi.astype(F32)
    mid = r1.astype(BF16)
    lo = (r1 - mid.astype(F32)).astype(BF16)
    return (jnp.dot(tri_bf, hi, preferred_element_type=F32)
            + jnp.dot(tri_bf, mid, preferred_element_type=F32)
            + jnp.dot(tri_bf, lo, preferred_element_type=F32))


def _hgrn_kernel(qf_ref, ff_ref, vf_ref, qb_ref, fb_ref, vb_ref, lbp_ref, of_ref, ob_ref,
                 stf_ref, stb_ref, *, layer):
    i = pl.program_id(1)

    @pl.when(i == 0)
    def _():
        stf_ref[...] = jnp.zeros_like(stf_ref)
        stb_ref[...] = jnp.zeros_like(stb_ref)

    lbp = lbp_ref[...]
    e = jnp.exp(lbp - jnp.max(lbp, axis=0, keepdims=True))
    sm = e / jnp.sum(e, axis=0, keepdims=True)
    lb = sm[0:1]
    for r in range(1, layer + 1):
        lb = lb + sm[r:r + 1]

    _hgrn_chunk(qf_ref, ff_ref, vf_ref, of_ref, stf_ref, lb, reverse=False)
    _hgrn_chunk(qb_ref, fb_ref, vb_ref, ob_ref, stb_ref, lb, reverse=True)


def _hgrn_chunk(q_ref, f_ref, v_ref, o_ref, st_ref, lb, *, reverse):
    c = HG_CHUNK
    row = lax.broadcasted_iota(jnp.int32, (c, c), 0)
    col = lax.broadcasted_iota(jnp.int32, (c, c), 1)
    tri = (col >= row) if reverse else (col <= row)
    tri_bf = jnp.where(tri, 1.0, 0.0).astype(BF16)
    nt = (((1,), (1,)), ((), ()))

    for h in range(HG_HEADS):
        sl = slice(h * HG_DK, (h + 1) * HG_DK)
        qh = _silu(q_ref[:, sl].astype(F32))
        lbh = lb[:, sl]
        fg = lbh + (1.0 - lbh) * _sigmoid(f_ref[:, sl].astype(F32))
        kh = 1.0 - fg
        bcum = _split_dot(tri_bf, jnp.log(fg))
        bmid = bcum[c // 2:c // 2 + 1]
        bend = bcum[0:1] if reverse else bcum[c - 1:c]
        vb = v_ref[:, sl]
        vh = vb.astype(F32)
        a = (qh * jnp.exp(bcum - bmid)).astype(BF16)
        kd = (kh * jnp.exp(bmid - bcum)).astype(BF16)
        s = lax.dot_general(a, kd, nt, preferred_element_type=F32)
        s = jnp.where(tri, s, 0.0)
        intra = jnp.dot(s.astype(BF16), vb, preferred_element_type=F32)
        st = st_ref[h]
        inter = lax.dot_general((qh * jnp.exp(bcum)).astype(BF16), st.astype(BF16), nt,
                                preferred_element_type=F32)
        o_ref[:, sl] = (inter + intra).astype(BF16)
        kd2 = (kh * jnp.exp(bend - bcum)).astype(BF16)
        st_ref[h] = st * jnp.exp(bend) + jnp.dot(vh.T.astype(BF16), kd2, preferred_element_type=F32)


def _hgrn_scan(p, lb_param, layer, geo):
    c = HG_CHUNK
    steps = (geo.Lc + geo.N) // c

    def spec(colblk, reverse):
        return pl.BlockSpec((c, HG_F), lambda b, i: (_chunk_index(b, i, c, geo, reverse), colblk))

    out = jax.ShapeDtypeStruct((geo.T, HG_V), BF16)
    state = pltpu.VMEM((HG_HEADS, HG_DV, HG_DK), F32)
    return pl.pallas_call(
        functools.partial(_hgrn_kernel, layer=layer),
        name="hgrn_scan",
        out_shape=[out, out],
        grid=(geo.B, steps),
        in_specs=[spec(0, False), spec(1, False), spec(3, False),
                  spec(0, True), spec(2, True), spec(3, True),
                  pl.BlockSpec(lb_param.shape, lambda b, i: (0, 0))],
        out_specs=[spec(0, False), spec(0, True)],
        scratch_shapes=[state, state],
        compiler_params=_params("arbitrary", "arbitrary"),
    )(p, p, p, p, p, p, lb_param)


def _ev_readout_kernel(of_ref, ob_ref, gate_ref, a_ref, b_ref, ap_ref, bp_ref, an_ref, bn_ref,
                       gain_ref, cw_ref, cb_ref, lng_ref, lnb_ref, o_ref, ext_ref, conv_ref, *, tm, geo):
    i = pl.program_id(0)
    n_lat = geo.BN // tm
    tpl = geo.N // tm
    tpc = geo.Lc // tm
    j = jnp.where(i < n_lat, i % tpl, (i - n_lat) % tpc)
    per = jnp.where(i < n_lat, tpl, tpc)
    keep_prev = jnp.where(j == 0, 0.0, 1.0)
    keep_next = jnp.where(j == per - 1, 0.0, 1.0)

    o = of_ref[...].astype(F32) + ob_ref[...].astype(F32)
    r = o * lax.rsqrt(jnp.mean(o * o, axis=-1, keepdims=True) + EPS) * gain_ref[...]
    o_ref[:, :HG_V] = (r * _silu(gate_ref[...].astype(F32))).astype(BF16)

    def glu(x_ref, y_ref):
        return x_ref[...].astype(F32) * _sigmoid(y_ref[...].astype(F32))

    ext_ref[0:CONV_HALO] = glu(ap_ref, bp_ref) * keep_prev
    ext_ref[CONV_HALO:CONV_HALO + tm] = glu(a_ref, b_ref)
    ext_ref[CONV_HALO + tm:2 * CONV_HALO + tm] = glu(an_ref, bn_ref) * keep_next
    off = CONV_HALO - CONV_W // 2
    for cj in range(CONV_C // 128):
        cs = slice(cj * 128, (cj + 1) * 128)
        acc = jnp.broadcast_to(cb_ref[:, cs], (tm, 128))
        for k in range(CONV_W):
            acc = acc + cw_ref[k:k + 1, cs] * ext_ref[off + k:off + k + tm, cs]
        conv_ref[:, cs] = acc
    acc = conv_ref[...]
    mu = jnp.mean(acc, axis=-1, keepdims=True)
    xc = acc - mu
    var = jnp.mean(xc * xc, axis=-1, keepdims=True)
    u = xc * lax.rsqrt(var + EPS) * lng_ref[...] + lnb_ref[...]
    o_ref[:, HG_V:] = _silu(u).astype(BF16)


def _ev_readout(p, of, ob, gain, cw, cb, lng, lnb, geo):
    tm = _pick_tile(geo, (128,))
    hb = tm // CONV_HALO
    nhalo = geo.T // CONV_HALO

    def row(colblk):
        return pl.BlockSpec((tm, HG_V), lambda i: (i, colblk))

    def prev(colblk):
        return pl.BlockSpec((CONV_HALO, CONV_C), lambda i: (jnp.maximum(i * hb - 1, 0), colblk))

    def nxt(colblk):
        return pl.BlockSpec((CONV_HALO, CONV_C), lambda i: (jnp.minimum((i + 1) * hb, nhalo - 1), colblk))

    def vec(n):
        return pl.BlockSpec((n, CONV_C), lambda i: (0, 0))

    return pl.pallas_call(
        functools.partial(_ev_readout_kernel, tm=tm, geo=geo),
        name="ev_readout",
        out_shape=jax.ShapeDtypeStruct((geo.T, HG_V + CONV_C), BF16),
        grid=(geo.T // tm,),
        in_specs=[row(0), row(0), row(4), row(5), row(6), prev(5), prev(6), nxt(5), nxt(6),
                  vec(1), vec(CONV_W), vec(1), vec(1), vec(1)],
        out_specs=pl.BlockSpec((tm, HG_V + CONV_C), lambda i: (i, 0)),
        scratch_shapes=[pltpu.VMEM((tm + 2 * CONV_HALO, CONV_C), F32), pltpu.VMEM((tm, CONV_C), F32)],
        compiler_params=_params("arbitrary"),
    )(of, ob, p, p, p, p, p, p, p, gain.reshape(1, -1), cw, cb.reshape(1, -1),
      lng.reshape(1, -1), lnb.reshape(1, -1))


def _rope(x, cos, sin_signed):
    half = x.shape[-1] // 2
    rot = jnp.concatenate([pltpu.roll(x[:, :half], half // 2, axis=1),
                           pltpu.roll(x[:, half:], half // 2, axis=1)], axis=-1)
    return x * cos + rot * sin_signed


def _ret_kernel(qf_ref, kf_ref, vf_ref, cosf_ref, sinf_ref, qb_ref, kb_ref, vb_ref, cosb_ref, sinb_ref,
                dl_ref, of_ref, ob_ref, sf_ref, sb_ref):
    i = pl.program_id(1)

    @pl.when(i == 0)
    def _():
        sf_ref[...] = jnp.zeros_like(sf_ref)
        sb_ref[...] = jnp.zeros_like(sb_ref)

    _ret_chunk(qf_ref, kf_ref, vf_ref, cosf_ref, sinf_ref, dl_ref[0], of_ref, sf_ref, reverse=False)
    _ret_chunk(qb_ref, kb_ref, vb_ref, cosb_ref, sinb_ref, dl_ref[1], ob_ref, sb_ref, reverse=True)


def _ret_chunk(q_ref, k_ref, v_ref, cos_ref, sin_ref, dl, o_ref, s_ref, *, reverse):
    c = RET_CHUNK
    lg_all = -jnp.log1p(jnp.exp(-dl))
    row = lax.broadcasted_iota(jnp.int32, (c, c), 0)
    col = lax.broadcasted_iota(jnp.int32, (c, c), 1)
    idx = lax.broadcasted_iota(jnp.int32, (c, 1), 0).astype(F32)
    if reverse:
        live = col >= row
        dist = (col - row).astype(F32)
        q_pow = c - idx
        k_pow = idx
    else:
        live = row >= col
        dist = (row - col).astype(F32)
        q_pow = idx + 1.0
        k_pow = c - 1.0 - idx
    cos = cos_ref[...]
    sin = sin_ref[...]
    nt = (((1,), (1,)), ((), ()))

    for h in range(RET_HEADS):
        lg = lg_all[h:h + 1]
        dmask = jnp.where(live, jnp.exp(lg * dist), 0.0)
        q = _rope(q_ref[:, h * RET_DK:(h + 1) * RET_DK].astype(F32), cos, sin)
        k = _rope(k_ref[:, h * RET_DK:(h + 1) * RET_DK].astype(F32) * (RET_DK ** -0.5), cos, sin)
        vb = v_ref[:, h * RET_DV:(h + 1) * RET_DV]
        scores = lax.dot_general(q.astype(BF16), k.astype(BF16), nt, preferred_element_type=F32) * dmask
        intra = jnp.dot(scores.astype(BF16), vb, preferred_element_type=F32)
        s = s_ref[h]
        inter = jnp.dot((q * jnp.exp(lg * q_pow)).astype(BF16), s.astype(BF16),
                        preferred_element_type=F32)
        o_ref[:, h * RET_DV:(h + 1) * RET_DV] = (inter + intra).astype(BF16)
        kdec = (k * jnp.exp(lg * k_pow)).T.astype(BF16)
        s_ref[h] = jnp.exp(lg * c) * s + jnp.dot(kdec, vb, preferred_element_type=F32)


def _ret_scan(p, cos_tab, sin_tab, decay_logit, geo):
    c = RET_CHUNK
    steps = (geo.Lc + geo.N) // c
    nc = geo.Lc // c
    nl = geo.N // c

    def spec(width, colblk, reverse):
        return pl.BlockSpec((c, width), lambda b, i: (_chunk_index(b, i, c, geo, reverse), colblk))

    def tab(reverse):
        def index(b, i):
            lat = (nl - 1 - (i - nc)) if reverse else (i - nc)
            return (jnp.where(i < nc, nl, lat), 0)
        return pl.BlockSpec((c, RET_DK), index)

    def direction(reverse):
        return [spec(RET_QK, 0, reverse), spec(RET_QK, 1, reverse), spec(RET_V, 1, reverse),
                tab(reverse), tab(reverse)]

    out = jax.ShapeDtypeStruct((geo.T, RET_V), BF16)
    state = pltpu.VMEM((RET_HEADS, RET_DK, RET_DV), F32)
    return pl.pallas_call(
        _ret_kernel,
        name="ret_scan",
        out_shape=[out, out],
        grid=(geo.B, steps),
        in_specs=direction(False) + direction(True)
        + [pl.BlockSpec((2, RET_HEADS, 1), lambda b, i: (0, 0, 0))],
        out_specs=[spec(RET_V, 0, False), spec(RET_V, 0, True)],
        scratch_shapes=[state, state],
        compiler_params=_params("arbitrary", "arbitrary"),
    )(p, p, p, cos_tab, sin_tab, p, p, p, cos_tab, sin_tab, decay_logit.reshape(2, RET_HEADS, 1))


def _rope_tables(n):
    t = jnp.arange(n)
    quarter = RET_DK // 4
    inv = 1.0 / (ROPE_BASE ** (jnp.arange(quarter, dtype=F32) / quarter))
    ang_r = (t // GRID_W).astype(F32)[:, None] * inv
    ang_c = (t % GRID_W).astype(F32)[:, None] * inv
    cos = jnp.concatenate([jnp.cos(ang_r), jnp.cos(ang_r), jnp.cos(ang_c), jnp.cos(ang_c)], axis=-1)
    sin = jnp.concatenate([-jnp.sin(ang_r), jnp.sin(ang_r), -jnp.sin(ang_c), jnp.sin(ang_c)], axis=-1)
    cos = jnp.concatenate([cos, jnp.ones((RET_CHUNK, RET_DK), F32)], axis=0)
    sin = jnp.concatenate([sin, jnp.zeros((RET_CHUNK, RET_DK), F32)], axis=0)
    return cos, sin


def _ret_readout_kernel(of_ref, ob_ref, gate_ref, o_ref):
    for h in range(RET_HEADS):
        sl = slice(h * RET_DV, (h + 1) * RET_DV)
        o = of_ref[:, sl].astype(F32) + ob_ref[:, sl].astype(F32)
        r = o * lax.rsqrt(jnp.mean(o * o, axis=-1, keepdims=True) + EPS)
        o_ref[:, sl] = (_silu(gate_ref[:, sl].astype(F32)) * r).astype(BF16)


def _ret_readout(p, of, ob, geo):
    tm = _pick_tile(geo, (128,))
    spec = pl.BlockSpec((tm, RET_V), lambda i: (i, 0))
    return pl.pallas_call(
        _ret_readout_kernel,
        name="ret_readout",
        out_shape=jax.ShapeDtypeStruct((geo.T, RET_V), BF16),
        grid=(geo.T // tm,),
        in_specs=[spec, spec, pl.BlockSpec((tm, RET_V), lambda i: (i, 2))],
        out_specs=spec,
        compiler_params=_params("arbitrary"),
    )(of, ob, p)


def _router_kernel(h_ref, rw_ref, rb_ref, eidx_ref, rank_ref, w_ref, cnt_ref, carry_ref, *, tm):
    i = pl.program_id(0)

    @pl.when(i == 0)
    def _():
        carry_ref[...] = jnp.zeros_like(carry_ref)

    half = D_MODEL // 2
    h_lo, h_hi = _unpack_rows(h_ref[...])
    h_lo = h_lo.astype(BF16)
    h_hi = h_hi.astype(BF16)
    rest = rw_ref[...]
    logits = jnp.zeros((tm, N_EXPERTS), F32)
    for _ in range(3):
        part = rest.astype(BF16)
        rest = rest - part.astype(F32)
        logits = (logits + jnp.dot(h_lo, part[:half], preferred_element_type=F32)
                  + jnp.dot(h_hi, part[half:], preferred_element_type=F32))
    s = _sigmoid(logits)
    sel = s + rb_ref[...]
    lane = lax.broadcasted_iota(jnp.int32, (tm, N_EXPERTS), 1).astype(F32)
    grp = jnp.floor(lane * (1.0 / GROUP_SIZE))
    ninf = -jnp.inf
    none = float(N_EXPERTS)

    gscore = jnp.zeros((tm, N_EXPERTS), F32)
    gcols = []
    for g in range(N_GROUPS):
        in_g = grp == float(g)
        v1 = jnp.max(jnp.where(in_g, sel, ninf), axis=-1, keepdims=True)
        i1 = jnp.min(jnp.where(in_g & (sel == v1), lane, none), axis=-1, keepdims=True)
        v2 = jnp.max(jnp.where(in_g & (lane != i1), sel, ninf), axis=-1, keepdims=True)
        gcols.append(v1 + v2)
        gscore = jnp.where(in_g, v1 + v2, gscore)
    beaten = jnp.zeros((tm, N_EXPERTS), F32)
    for g in range(N_GROUPS):
        wins = (gcols[g] > gscore) | ((gcols[g] == gscore) & (float(g) < grp))
        beaten = beaten + jnp.where(wins, 1.0, 0.0)
    cand = jnp.where(beaten < float(TOPK_GROUPS), sel, ninf)

    lane_k = lax.broadcasted_iota(jnp.int32, (tm, TOP_K), 1)
    eidx = jnp.zeros((tm, TOP_K), F32)
    wsel = jnp.zeros((tm, TOP_K), F32)
    chosen = jnp.zeros((tm, N_EXPERTS), F32)
    picks = []
    for k in range(TOP_K):
        v = jnp.max(cand, axis=-1, keepdims=True)
        ik = jnp.min(jnp.where(cand == v, lane, none), axis=-1, keepdims=True)
        hit = lane == ik
        picks.append(ik)
        eidx = jnp.where(lane_k == k, ik, eidx)
        wsel = jnp.where(lane_k == k, jnp.sum(jnp.where(hit, s, 0.0), axis=-1, keepdims=True), wsel)
        chosen = jnp.where(hit, 1.0, chosen)
        cand = jnp.where(hit, ninf, cand)
    w_ref[...] = wsel / jnp.sum(wsel, axis=-1, keepdims=True) * ROUTED_SCALE
    eidx_ref[...] = eidx.astype(jnp.int32)

    r = lax.broadcasted_iota(jnp.int32, (tm, tm), 0)
    c = lax.broadcasted_iota(jnp.int32, (tm, tm), 1)
    below = jnp.where(c < r, 1.0, 0.0).astype(BF16)
    carry = carry_ref[...]
    pos = jnp.dot(below, chosen.astype(BF16), preferred_element_type=F32) + carry
    rank = jnp.zeros((tm, TOP_K), jnp.int32)
    for k in range(TOP_K):
        rk = jnp.sum(jnp.where(lane == picks[k], pos, 0.0), axis=-1, keepdims=True)
        rank = jnp.where(lane_k == k, rk.astype(jnp.int32), rank)
    rank_ref[...] = rank
    carry = carry + jnp.sum(chosen, axis=0, keepdims=True)
    carry_ref[...] = carry
    cnt_ref[...] = carry


def _router(h_packed, rw, rb, geo):
    tm = 256 if geo.T % 256 == 0 else 128
    tok = pl.BlockSpec((tm, TOP_K), lambda i: (i, 0))
    one = pl.BlockSpec((1, N_EXPERTS), lambda i: (0, 0))
    return pl.pallas_call(
        functools.partial(_router_kernel, tm=tm),
        name="router",
        out_shape=[jax.ShapeDtypeStruct((geo.T, TOP_K), jnp.int32),
                   jax.ShapeDtypeStruct((geo.T, TOP_K), jnp.int32),
                   jax.ShapeDtypeStruct((geo.T, TOP_K), F32),
                   jax.ShapeDtypeStruct((1, N_EXPERTS), F32)],
        grid=(geo.T // tm,),
        in_specs=[pl.BlockSpec((tm, D_MODEL // 2), lambda i: (i, 0)),
                  pl.BlockSpec((D_MODEL, N_EXPERTS), lambda i: (0, 0)), one],
        out_specs=[tok, tok, tok, one],
        scratch_shapes=[pltpu.VMEM((1, N_EXPERTS), F32)],
        compiler_params=_params("arbitrary"),
    )(h_packed, rw, rb.reshape(1, N_EXPERTS))


def _n_blocks(geo):
    return -(-(geo.T * TOP_K) // MOE_BLOCK) + N_EXPERTS


def _dest_kernel(cnt_ref, eidx_ref, rank_ref, dest_ref, blk_ref, *, n_blocks):
    eidx = eidx_ref[...]
    dest = rank_ref[...]
    blk_row = (lax.broadcasted_iota(jnp.int32, blk_ref.shape, 0) * 128
               + lax.broadcasted_iota(jnp.int32, blk_ref.shape, 1)) * MOE_BLOCK
    blk = jnp.zeros(blk_ref.shape, jnp.int32)
    start = jnp.int32(0)
    for e in range(N_EXPERTS):
        padded = (cnt_ref[e] + (MOE_BLOCK - 1)) // MOE_BLOCK * MOE_BLOCK
        dest = dest + jnp.where(eidx == e, start, 0)
        start = start + padded
        blk = blk + jnp.where(start <= blk_row, 1, 0)
    dest_ref[...] = dest
    blk_ref[...] = jnp.where(blk_row == n_blocks * MOE_BLOCK, start // MOE_BLOCK,
                             jnp.minimum(blk, N_EXPERTS - 1))


def _dest(counts, eidx, rank, geo):
    rows = geo.T * TOP_K // 128
    brow = -(-(_n_blocks(geo) + 1) // 128)
    full = pl.BlockSpec((rows, 128), lambda: (0, 0))
    dest, blk = pl.pallas_call(
        functools.partial(_dest_kernel, n_blocks=_n_blocks(geo)),
        name="dest",
        out_shape=[jax.ShapeDtypeStruct((rows, 128), jnp.int32),
                   jax.ShapeDtypeStruct((brow, 128), jnp.int32)],
        in_specs=[pl.BlockSpec(memory_space=pltpu.SMEM), full, full],
        out_specs=[full, pl.BlockSpec((brow, 128), lambda: (0, 0))],
    )(counts, eidx.reshape(rows, 128), rank.reshape(rows, 128))
    return dest.reshape(-1), blk.reshape(-1)[:_n_blocks(geo) + 1]


def _zero_fill(cnt_ref, xs_hbm, zero_ref, zsem, n_rows, wait):
    def piece(pos, size):
        if size >= 8:
            copies = [(pl.multiple_of(pos, 8), size)]
        else:
            copies = [(pos + r, 1) for r in range(size)]
        for p, s in copies:
            cp = pltpu.make_async_copy(zero_ref.at[pl.ds(0, s)], xs_hbm.at[pl.ds(p, s)], zsem)
            cp.wait() if wait else cp.start()

    def per_expert(e, start):
        cnt = cnt_ref[e]
        padded = (cnt + (MOE_BLOCK - 1)) // MOE_BLOCK * MOE_BLOCK
        pad = padded - cnt
        pos = start + cnt
        size = 1
        while size < MOE_BLOCK:
            take = (pad & size) != 0
            pl.when(take)(functools.partial(piece, pos, size))
            pos = pos + jnp.where(take, size, 0)
            size *= 2
        return start + padded

    end = lax.fori_loop(0, N_EXPERTS, per_expert, jnp.int32(0))

    def per_block(j, carry):
        piece(end + j * MOE_BLOCK, MOE_BLOCK)
        return carry

    lax.fori_loop(0, (n_rows - end) // MOE_BLOCK, per_block, 0)


def _sc_scatter_rows(src, order, n_rows):
    nw = V7X_SC_CORES * V7X_SC_SUBCORES
    t, w = src.shape
    per = t // nw
    n = per // SC_ROWS
    assert t % (nw * SC_ROWS * 2) == 0
    mesh = plsc.VectorSubcoreMesh(core_axis_name="c", subcore_axis_name="s")

    @functools.partial(
        pl.kernel, mesh=mesh,
        out_type=jax.ShapeDtypeStruct((n_rows, w), src.dtype),
        scratch_types=[pltpu.VMEM((SC_ROWS, w), src.dtype)] * 2
        + [pltpu.VMEM((SC_ROWS,), jnp.int32)] * (2 * TOP_K) + [pltpu.SemaphoreType.DMA] * 2,
    )
    def scatter(src_hbm, idx_hbm, out_hbm, *scratch):
        rows_vs = scratch[:2]
        idx_vs = (scratch[2:2 + TOP_K], scratch[2 + TOP_K:2 + 2 * TOP_K])
        sems = scratch[2 + 2 * TOP_K:]
        wid = lax.axis_index("s") * V7X_SC_CORES + lax.axis_index("c")
        base = wid * per

        def load(chunk, b):
            off = pl.multiple_of(base + chunk * SC_ROWS, 8)
            pltpu.sync_copy(src_hbm.at[pl.ds(off, SC_ROWS)], rows_vs[b])
            for k in range(TOP_K):
                pltpu.sync_copy(idx_hbm.at[pl.ds(pl.multiple_of(k * t + off, 8), SC_ROWS)], idx_vs[b][k])

        def copies(b):
            return [pltpu.make_async_copy(rows_vs[b], out_hbm.at[idx_vs[b][k]], sems[b])
                    for k in range(TOP_K)]

        load(0, 0)

        @pl.loop(0, n, step=2)
        def _(j):
            for b in range(2):
                cur = j + b
                for cp in copies(b):
                    cp.start()
                pl.when(cur + 1 < n)(functools.partial(load, cur + 1, 1 - b))
                for cp in copies(b):
                    cp.wait()

    return scatter(src, order)


def _zero_pad_kernel(cnt_ref, xs_in, xs_hbm, zero_ref, zsem, *, n_rows):
    del xs_in
    zero_ref[...] = jnp.zeros_like(zero_ref)
    _zero_fill(cnt_ref, xs_hbm, zero_ref, zsem, n_rows, wait=False)
    _zero_fill(cnt_ref, xs_hbm, zero_ref, zsem, n_rows, wait=True)


def _dispatch(counts, order, h_packed, geo):
    n_rows = _n_blocks(geo) * MOE_BLOCK
    width = D_MODEL // 2
    xs = _sc_scatter_rows(h_packed[:geo.T], order, n_rows)
    return pl.pallas_call(
        functools.partial(_zero_pad_kernel, n_rows=n_rows),
        name="zero_pad",
        out_shape=jax.ShapeDtypeStruct((n_rows, width), PACKED),
        in_specs=[pl.BlockSpec(memory_space=pltpu.SMEM), pl.BlockSpec(memory_space=pl.ANY)],
        out_specs=pl.BlockSpec(memory_space=pl.ANY),
        scratch_shapes=[pltpu.VMEM((MOE_BLOCK, width), PACKED), pltpu.SemaphoreType.DMA],
        input_output_aliases={1: 0},
    )(counts, xs)


def _swiglu_packed(x_ref, wg_ref, wu_ref, wd_ref):
    half = D_MODEL // 2
    lo, hi = _unpack_rows(x_ref[...])
    lo = lo.astype(BF16)
    hi = hi.astype(BF16)

    def proj(w_ref):
        return (jnp.dot(lo, w_ref[:half], preferred_element_type=F32)
                + jnp.dot(hi, w_ref[half:], preferred_element_type=F32))

    g = proj(wg_ref)
    u = proj(wu_ref)
    return jnp.dot((_silu(g) * u).astype(BF16), wd_ref[...], preferred_element_type=F32)


def _expert_kernel(blk_ref, x_ref, wg_ref, wu_ref, wd_ref, o_ref, wgb_ref, wub_ref, wdb_ref, *, nb):
    j = pl.program_id(0)
    n_used = blk_ref[nb]

    @pl.when((j == 0) | (blk_ref[j] != blk_ref[jnp.maximum(j - 1, 0)]))
    def _():
        wgb_ref[...] = wg_ref[...].astype(BF16)
        wub_ref[...] = wu_ref[...].astype(BF16)
        wdb_ref[...] = wd_ref[...].astype(BF16)

    @pl.when(j < n_used)
    def _():
        o_ref[...] = _pack_rows(_swiglu_packed(x_ref, wgb_ref, wub_ref, wdb_ref))

    @pl.when(j >= n_used)
    def _():
        o_ref[...] = jnp.zeros_like(o_ref)


def _experts(blk_e, xs, wg, wu, wd, layer, geo):
    nb = _n_blocks(geo)
    rows = pl.BlockSpec((MOE_BLOCK, D_MODEL // 2), lambda j, be: (j, 0))

    def wspec(r, c):
        return pl.BlockSpec((None, None, r, c), lambda j, be: (layer, be[j], 0, 0))

    return pl.pallas_call(
        functools.partial(_expert_kernel, nb=nb),
        name="experts",
        out_shape=jax.ShapeDtypeStruct((nb * MOE_BLOCK, D_MODEL // 2), PACKED),
        grid_spec=pltpu.PrefetchScalarGridSpec(
            num_scalar_prefetch=1,
            grid=(nb,),
            in_specs=[rows, wspec(D_MODEL, EXPERT_FF), wspec(D_MODEL, EXPERT_FF),
                      wspec(EXPERT_FF, D_MODEL)],
            out_specs=rows,
            scratch_shapes=[pltpu.VMEM((D_MODEL, EXPERT_FF), BF16),
                            pltpu.VMEM((D_MODEL, EXPERT_FF), BF16),
                            pltpu.VMEM((EXPERT_FF, D_MODEL), BF16)]),
        compiler_params=_params("arbitrary"),
    )(blk_e, xs, wg, wu, wd)


def _sc_gather_rows(table, idx):
    nw = V7X_SC_CORES * V7X_SC_SUBCORES
    m = idx.shape[0]
    w = table.shape[1]
    per = m // nw
    n = per // SC_ROWS
    assert m % (nw * SC_ROWS * 2) == 0
    mesh = plsc.VectorSubcoreMesh(core_axis_name="c", subcore_axis_name="s")

    @functools.partial(
        pl.kernel, mesh=mesh,
        out_type=jax.ShapeDtypeStruct((m, w), table.dtype),
        scratch_types=[pltpu.VMEM((SC_ROWS,), jnp.int32)] * 2
        + [pltpu.VMEM((SC_ROWS, w), table.dtype)] * 2 + [pltpu.SemaphoreType.DMA] * 2,
    )
    def gather(table_hbm, idx_hbm, out_hbm, *scratch):
        idx_vs, rows_vs, sems = scratch[:2], scratch[2:4], scratch[4:]
        wid = lax.axis_index("s") * V7X_SC_CORES + lax.axis_index("c")
        base = wid * per

        def fetch(b):
            return pltpu.make_async_copy(table_hbm.at[idx_vs[b]], rows_vs[b], sems[b])

        def start(chunk, b):
            off = pl.multiple_of(base + chunk * SC_ROWS, 8)
            pltpu.sync_copy(idx_hbm.at[pl.ds(off, SC_ROWS)], idx_vs[b])
            fetch(b).start()

        start(0, 0)

        @pl.loop(0, n, step=2)
        def _(j):
            for b in range(2):
                cur = j + b
                pl.when(cur + 1 < n)(functools.partial(start, cur + 1, 1 - b))
                fetch(b).wait()
                off = pl.multiple_of(base + cur * SC_ROWS, 8)
                pltpu.sync_copy(rows_vs[b], out_hbm.at[pl.ds(off, SC_ROWS)])

    return gather(table, idx)


def _combine_kernel(w_ref, x_ref, h_ref, sg_ref, su_ref, sd_ref, m_ref, *rest, post):
    if post == "next":
        ng_ref, nsh_ref, nsc_ref, g_ref, o_ref, hn_ref = rest
    else:
        ng_ref, g_ref, o_ref = rest
    half = D_MODEL // 2
    shared = _swiglu_packed(h_ref, sg_ref, su_ref, sd_ref)
    w = w_ref[...]
    acc_lo = shared[:, :half]
    acc_hi = shared[:, half:]
    for k in range(TOP_K):
        lo, hi = _unpack_rows(g_ref[k])
        acc_lo = acc_lo + w[:, k:k + 1] * lo
        acc_hi = acc_hi + w[:, k:k + 1] * hi
    gate = m_ref[0]
    y_lo = x_ref[:, :half] + gate[:, :half] * acc_lo
    y_hi = x_ref[:, half:] + gate[:, half:] * acc_hi
    ms = (jnp.sum(y_lo * y_lo, axis=-1, keepdims=True)
          + jnp.sum(y_hi * y_hi, axis=-1, keepdims=True)) * (1.0 / D_MODEL)
    inv = lax.rsqrt(ms + EPS)
    ng = ng_ref[...]
    n_lo = y_lo * inv * ng[:, :half]
    n_hi = y_hi * inv * ng[:, half:]
    if post == "next":
        o_ref[:, :half] = y_lo
        o_ref[:, half:] = y_hi
        sc = nsc_ref[0]
        sft = nsh_ref[0]
        hn_ref[:, :half] = (n_lo * (1.0 + sc[:, :half]) + sft[:, :half]).astype(BF16)
        hn_ref[:, half:] = (n_hi * (1.0 + sc[:, half:]) + sft[:, half:]).astype(BF16)
    else:
        o_ref[:, :half] = n_lo
        o_ref[:, half:] = n_hi


def _combine(order, w, x, h_packed, sg, su, sd, ys, mod3, m_gate, post, norm_g, next_mod3, geo):
    half = D_MODEL // 2
    gathered = _sc_gather_rows(ys, order).reshape(TOP_K, geo.T, half)
    tt = _pick_tile(geo, (256, 128))
    n = geo.T // tt
    ff = sg.shape[1]
    rows = pl.BlockSpec((tt, D_MODEL), lambda i: (i, 0))
    vec = pl.BlockSpec((1, D_MODEL), lambda i: (0, 0))
    in_specs = [pl.BlockSpec((tt, TOP_K), lambda i: (i, 0)),
                rows,
                pl.BlockSpec((tt, half), lambda i: (i, 0)),
                pl.BlockSpec((D_MODEL, ff), lambda i: (0, 0)),
                pl.BlockSpec((D_MODEL, ff), lambda i: (0, 0)),
                pl.BlockSpec((ff, D_MODEL), lambda i: (0, 0)),
                _mod_spec(m_gate, tt, geo), vec]
    args = [w, x, h_packed, sg, su, sd, mod3, norm_g.reshape(1, D_MODEL)]
    out_shape = [jax.ShapeDtypeStruct((geo.T, D_MODEL), F32)]
    out_specs = [rows]
    if post == "next":
        in_specs += [_mod_spec(0, tt, geo), _mod_spec(1, tt, geo)]
        args += [next_mod3, next_mod3]
        out_shape.append(jax.ShapeDtypeStruct((geo.T, D_MODEL), BF16))
        out_specs.append(rows)
    in_specs.append(pl.BlockSpec((TOP_K, tt, half), lambda i: (0, i, 0)))
    args.append(gathered)
    return pl.pallas_call(
        functools.partial(_combine_kernel, post=post),
        name="combine",
        out_shape=out_shape,
        grid=(n,),
        in_specs=in_specs,
        out_specs=out_specs,
        compiler_params=_params("arbitrary"),
    )(*args)


def _moe(x, h_packed, mod3, rw, rb, wg, wu, wd, layer, sg, su, sd, post, norm_g, next_mod3, geo):
    eidx, rank, w, counts = _router(h_packed, rw, rb, geo)
    counts = counts.reshape(N_EXPERTS).astype(jnp.int32)
    dest, blk_e = _dest(counts, eidx, rank, geo)
    order = dest.reshape(geo.T, TOP_K).T.reshape(-1)
    xs = _dispatch(counts, order, h_packed, geo)
    ys = _experts(blk_e, xs, wg, wu, wd, layer, geo)
    return _combine(order, w, x, h_packed, sg.astype(BF16), su.astype(BF16), sd.astype(BF16), ys,
                    mod3, 5, post, norm_g, next_mod3, geo)


def kernel(x, c, ctx, c_ctx, ada_w, ada_b, norm_mix, norm_ffn, norm_final, ev_w_in, ev_w_out, hgrn_lb, hgrn_norm, conv_w, conv_b, conv_norm_g, conv_norm_b, ret_w_in, ret_w_out, ret_decay, router_w, router_b, exp_gate, exp_up, exp_down, sh_gate, sh_up, sh_down):
    b, n, d = x.shape
    lc = ctx.shape[1]
    depth = ada_w.shape[0]
    geo = _geo(b, n, lc)
    assert d == D_MODEL and b < MOD_ROWS
    assert n % RET_CHUNK == 0 and lc % RET_CHUNK == 0

    xs = jnp.concatenate([x.reshape(geo.BN, d), ctx.reshape(geo.BL, d)], axis=0)
    cond = jnp.zeros((MOD_ROWS, d), F32).at[:b].set(c).at[b].set(c_ctx)
    cos_tab, sin_tab = _rope_tables(n)

    mods = [_adaln(cond, ada_w, ada_b, l).reshape(MOD_ROWS * N_MOD, 1, d) for l in range(depth)]
    h = _normmod(xs, norm_mix[0], mods[0], 0, 1, geo, packed=False)
    for l in range(depth):
        j = l // 2
        last = l == depth - 1
        mod3 = mods[l]
        tail = geo._replace(T=geo.BN, BL=0, Lc=0) if last else geo
        if l % 2 == 0:
            p = _matmul(h, ev_w_in[j].astype(BF16), geo)
            of, ob = _hgrn_scan(p, hgrn_lb, l, geo)
            mix = _ev_readout(p, of, ob, hgrn_norm[j], conv_w[j], conv_b[j],
                              conv_norm_g[j], conv_norm_b[j], geo)
            xs = _matmul_resid(mix, ev_w_out[j].astype(BF16), xs, mod3, 2, tail)
        else:
            p = _matmul(h, ret_w_in[j].astype(BF16), geo)
            of, ob = _ret_scan(p, cos_tab, sin_tab, ret_decay[j], geo)
            mix = _ret_readout(p, of, ob, geo)
            xs = _matmul_resid(mix, ret_w_out[j].astype(BF16), xs, mod3, 2, tail)
        h_packed = _normmod(xs, norm_ffn[l], mod3, 3, 4, tail, packed=True)
        moe_w = (router_w[l], router_b[l], exp_gate, exp_up, exp_down, l, sh_gate[l], sh_up[l], sh_down[l])
        if last:
            (out,) = _moe(xs, h_packed, mod3, *moe_w, "final", norm_final, None, tail)
        else:
            xs, h = _moe(xs, h_packed, mod3, *moe_w, "next", norm_mix[l + 1], mods[l + 1], tail)
    return out.reshape(b, n, d)
```

```python
import collections
import functools

import jax
import jax.numpy as jnp
from jax import lax
from jax.experimental import pallas as pl
from jax.experimental.pallas import tpu as pltpu
from jax.experimental.pallas import tpu_sc as plsc

F32 = jnp.float32
BF16 = jnp.bfloat16

D_MODEL = 2048
N_MOD = 6
EPS = 1e-6
GRID_W = 64
ROPE_BASE = 10000.0

HG_HEADS = 8
HG_DK = 128
HG_DV = 128
HG_F = HG_HEADS * HG_DK
HG_V = HG_HEADS * HG_DV
CONV_C = D_MODEL // 2
CONV_W = 31
CONV_HALO = 16
HG_CHUNK = 128

RET_HEADS = 8
RET_DK = D_MODEL // RET_HEADS
RET_DV = 2 * RET_DK
RET_QK = RET_HEADS * RET_DK
RET_V = RET_HEADS * RET_DV
RET_CHUNK = 256

N_EXPERTS = 64
EXPERT_FF = D_MODEL // 4
TOP_K = 8
N_GROUPS = 8
GROUP_SIZE = N_EXPERTS // N_GROUPS
TOPK_GROUPS = 4
ROUTED_SCALE = 2.5
MOE_BLOCK = 512

V7X_SC_CORES = 2
V7X_SC_SUBCORES = 16
SC_ROWS = 32
MOD_ROWS = 16
VMEM_LIMIT = 56 * 1024 * 1024

Geo = collections.namedtuple("Geo", "B N Lc BN BL T")


def _geo(b, n, lc):
    return Geo(b, n, lc, b * n, b * lc, b * n + b * lc)


def _pick_tile(geo, cands):
    for t in cands:
        if geo.N % t == 0 and geo.BL % t == 0:
            return t
    raise ValueError("no row tile fits the sequence lengths")


def _mod_row(i, tm, geo):
    return jnp.where(i < geo.BN // tm, i // (geo.N // tm), geo.B)


def _mod_spec(m, tm, geo, ngrid=1):
    if ngrid == 1:
        return pl.BlockSpec((1, 1, D_MODEL), lambda i: (_mod_row(i, tm, geo) * N_MOD + m, 0, 0))
    return pl.BlockSpec((1, 1, D_MODEL), lambda i, j: (_mod_row(i, tm, geo) * N_MOD + m, 0, j))


def _params(*sem):
    return pltpu.CompilerParams(dimension_semantics=sem, vmem_limit_bytes=VMEM_LIMIT)


def _sigmoid(x):
    return jax.nn.sigmoid(x)


def _silu(x):
    return x * jax.nn.sigmoid(x)


def _adaln_kernel(c_ref, w_ref, b_ref, o_ref):
    a = _silu(c_ref[...]).astype(BF16)
    o_ref[...] = jnp.dot(a, w_ref[...].astype(BF16), preferred_element_type=F32) + b_ref[...]


def _adaln(cond, w, b, layer):
    depth, k, n = w.shape
    tn = 1024
    return pl.pallas_call(
        _adaln_kernel,
        name="adaln",
        out_shape=jax.ShapeDtypeStruct((MOD_ROWS, n), F32),
        grid=(n // tn,),
        in_specs=[pl.BlockSpec((MOD_ROWS, k), lambda j: (0, 0)),
                  pl.BlockSpec((None, k, tn), lambda j: (layer, 0, j)),
                  pl.BlockSpec((None, 1, tn), lambda j: (layer, 0, j))],
        out_specs=pl.BlockSpec((MOD_ROWS, tn), lambda j: (0, j)),
        compiler_params=_params("arbitrary"),
    )(cond, w, b.reshape(depth, 1, n))


PACKED = jnp.int32


def _pack_rows(x):
    n = x.shape[-1] // 2
    bits = lax.bitcast_convert_type(x.astype(BF16).astype(F32), jnp.uint32)
    words = (bits[:, n:] & jnp.uint32(0xFFFF0000)) | (bits[:, :n] >> 16)
    return lax.bitcast_convert_type(words, PACKED)


def _unpack_rows(p):
    u = lax.bitcast_convert_type(p, jnp.uint32)
    lo = lax.bitcast_convert_type(u << 16, F32)
    hi = lax.bitcast_convert_type(u & jnp.uint32(0xFFFF0000), F32)
    return lo, hi


def _normmod_kernel(x_ref, g_ref, sh_ref, sc_ref, o_ref, *, packed):
    x = x_ref[...]
    y = x * lax.rsqrt(jnp.mean(x * x, axis=-1, keepdims=True) + EPS) * g_ref[...]
    h = y * (1.0 + sc_ref[0]) + sh_ref[0]
    o_ref[...] = _pack_rows(h) if packed else h.astype(BF16)


def _normmod(x, g, mod3, m_shift, m_scale, geo, packed):
    tm = _pick_tile(geo, (256, 128))
    spec = pl.BlockSpec((tm, D_MODEL), lambda i: (i, 0))
    if packed:
        out_shape = jax.ShapeDtypeStruct((geo.T, D_MODEL // 2), PACKED)
        out_spec = pl.BlockSpec((tm, D_MODEL // 2), lambda i: (i, 0))
    else:
        out_shape = jax.ShapeDtypeStruct((geo.T, D_MODEL), BF16)
        out_spec = spec
    return pl.pallas_call(
        functools.partial(_normmod_kernel, packed=packed),
        name="normmod",
        out_shape=out_shape,
        grid=(geo.T // tm,),
        in_specs=[spec, pl.BlockSpec((1, D_MODEL), lambda i: (0, 0)),
                  _mod_spec(m_shift, tm, geo), _mod_spec(m_scale, tm, geo)],
        out_specs=out_spec,
        compiler_params=_params("arbitrary"),
    )(x, g.reshape(1, D_MODEL), mod3, mod3)


def _mm_kernel(a_ref, w_ref, o_ref):
    o_ref[...] = jnp.dot(a_ref[...], w_ref[...], preferred_element_type=F32).astype(o_ref.dtype)


def _mm_resid_kernel(a_ref, w_ref, x_ref, m_ref, o_ref):
    y = jnp.dot(a_ref[...], w_ref[...], preferred_element_type=F32)
    o_ref[...] = x_ref[...] + m_ref[0] * y


def _matmul(a, w, geo):
    k, n = w.shape
    tm = _pick_tile(geo, (1024, 512, 256, 128))
    tn = 1024 if n % 1024 == 0 else 512
    return pl.pallas_call(
        _mm_kernel,
        name="matmul",
        out_shape=jax.ShapeDtypeStruct((geo.T, n), BF16),
        grid=(geo.T // tm, n // tn),
        in_specs=[pl.BlockSpec((tm, k), lambda i, j: (i, 0)),
                  pl.BlockSpec((k, tn), lambda i, j: (0, j))],
        out_specs=pl.BlockSpec((tm, tn), lambda i, j: (i, j)),
        compiler_params=_params("arbitrary", "arbitrary"),
    )(a, w)


def _matmul_resid(a, w, x, mod3, m_gate, geo):
    k, n = w.shape
    tm = _pick_tile(geo, (1024, 512, 256, 128))
    tn = 512
    return pl.pallas_call(
        _mm_resid_kernel,
        name="matmul_resid",
        out_shape=jax.ShapeDtypeStruct((geo.T, n), F32),
        grid=(geo.T // tm, n // tn),
        in_specs=[pl.BlockSpec((tm, k), lambda i, j: (i, 0)),
                  pl.BlockSpec((k, tn), lambda i, j: (0, j)),
                  pl.BlockSpec((tm, tn), lambda i, j: (i, j)),
                  pl.BlockSpec((1, 1, tn), lambda i, j: (_mod_row(i, tm, geo) * N_MOD + m_gate, 0, j))],
        out_specs=pl.BlockSpec((tm, tn), lambda i, j: (i, j)),
        compiler_params=_params("arbitrary", "arbitrary"),
    )(a, w, x, mod3)


def _chunk_index(b, i, chunk, geo, reverse):
    nc = geo.Lc // chunk
    nl = geo.N // chunk
    ctx0 = (geo.BN + b * geo.Lc) // chunk
    lat0 = (b * geo.N) // chunk
    if reverse:
        return jnp.where(i < nc, ctx0 + (nc - 1 - i), lat0 + (nl - 1 - (i - nc)))
    return jnp.where(i < nc, ctx0 + i, lat0 + (i - nc))


def _split_dot(tri_bf, x):
    hi = x.astype(BF16)
    r1 = x - hi.astype(F32)
    mid = r1.astype(BF16)
    lo = (r1 - mid.astype(F32)).astype(BF16)
    return (jnp.dot(tri_bf, hi, preferred_element_type=F32)
            + jnp.dot(tri_bf, mid, preferred_element_type=F32)
            + jnp.dot(tri_bf, lo, preferred_element_type=F32))


def _hgrn_kernel(qf_ref, ff_ref, vf_ref, qb_ref, fb_ref, vb_ref, lbp_ref, of_ref, ob_ref,
                 stf_ref, stb_ref, *, layer):
    i = pl.program_id(1)

    @pl.when(i == 0)
    def _():
        stf_ref[...] = jnp.zeros_like(stf_ref)
        stb_ref[...] = jnp.zeros_like(stb_ref)

    lbp = lbp_ref[...]
    e = jnp.exp(lbp - jnp.max(lbp, axis=0, keepdims=True))
    sm = e / jnp.sum(e, axis=0, keepdims=True)
    lb = sm[0:1]
    for r in range(1, layer + 1):
        lb = lb + sm[r:r + 1]

    _hgrn_chunk(qf_ref, ff_ref, vf_ref, of_ref, stf_ref, lb, reverse=False)
    _hgrn_chunk(qb_ref, fb_ref, vb_ref, ob_ref, stb_ref, lb, reverse=True)


def _hgrn_chunk(q_ref, f_ref, v_ref, o_ref, st_ref, lb, *, reverse):
    c = HG_CHUNK
    row = lax.broadcasted_iota(jnp.int32, (c, c), 0)
    col = lax.broadcasted_iota(jnp.int32, (c, c), 1)
    tri = (col >= row) if reverse else (col <= row)
    tri_bf = jnp.where(tri, 1.0, 0.0).astype(BF16)
    nt = (((1,), (1,)), ((), ()))

    for h in range(HG_HEADS):
        sl = slice(h * HG_DK, (h + 1) * HG_DK)
        qh = _silu(q_ref[:, sl].astype(F32))
        lbh = lb[:, sl]
        fg = lbh + (1.0 - lbh) * _sigmoid(f_ref[:, sl].astype(F32))
        kh = 1.0 - fg
        bcum = _split_dot(tri_bf, jnp.log(fg))
        bmid = bcum[c // 2:c // 2 + 1]
        bend = bcum[0:1] if reverse else bcum[c - 1:c]
        vb = v_ref[:, sl]
        vh = vb.astype(F32)
        a = (qh * jnp.exp(bcum - bmid)).astype(BF16)
        kd = (kh * jnp.exp(bmid - bcum)).astype(BF16)
        s = lax.dot_general(a, kd, nt, preferred_element_type=F32)
        s = jnp.where(tri, s, 0.0)
        intra = jnp.dot(s.astype(BF16), vb, preferred_element_type=F32)
        st = st_ref[h]
        inter = lax.dot_general((qh * jnp.exp(bcum)).astype(BF16), st.astype(BF16), nt,
                                preferred_element_type=F32)
        o_ref[:, sl] = (inter + intra).astype(BF16)
        kd2 = (kh * jnp.exp(bend - bcum)).astype(BF16)
        st_ref[h] = st * jnp.exp(bend) + jnp.dot(vh.T.astype(BF16), kd2, preferred_element_type=F32)


def _hgrn_scan(p, lb_param, layer, geo):
    c = HG_CHUNK
    steps = (geo.Lc + geo.N) // c

    def spec(colblk, reverse):
        return pl.BlockSpec((c, HG_F), lambda b, i: (_chunk_index(b, i, c, geo, reverse), colblk))

    out = jax.ShapeDtypeStruct((geo.T, HG_V), BF16)
    state = pltpu.VMEM((HG_HEADS, HG_DV, HG_DK), F32)
    return pl.pallas_call(
        functools.partial(_hgrn_kernel, layer=layer),
        name="hgrn_scan",
        out_shape=[out, out],
        grid=(geo.B, steps),
        in_specs=[spec(0, False), spec(1, False), spec(3, False),
                  spec(0, True), spec(2, True), spec(3, True),
                  pl.BlockSpec(lb_param.shape, lambda b, i: (0, 0))],
        out_specs=[spec(0, False), spec(0, True)],
        scratch_shapes=[state, state],
        compiler_params=_params("arbitrary", "arbitrary"),
    )(p, p, p, p, p, p, lb_param)


def _ev_readout_kernel(of_ref, ob_ref, gate_ref, a_ref, b_ref, ap_ref, bp_ref, an_ref, bn_ref,
                       gain_ref, cw_ref, cb_ref, lng_ref, lnb_ref, o_ref, ext_ref, conv_ref, shift_ref,
                       *, tm, geo):
    i = pl.program_id(0)
    n_lat = geo.BN // tm
    tpl = geo.N // tm
    tpc = geo.Lc // tm
    j = jnp.where(i < n_lat, i % tpl, (i - n_lat) % tpc)
    per = jnp.where(i < n_lat, tpl, tpc)
    keep_prev = jnp.where(j == 0, 0.0, 1.0)
    keep_next = jnp.where(j == per - 1, 0.0, 1.0)

    o = of_ref[...].astype(F32) + ob_ref[...].astype(F32)
    r = o * lax.rsqrt(jnp.mean(o * o, axis=-1, keepdims=True) + EPS) * gain_ref[...]
    o_ref[:, :HG_V] = (r * _silu(gate_ref[...].astype(F32))).astype(BF16)

    def glu(x_ref, y_ref):
        return x_ref[...].astype(F32) * _sigmoid(y_ref[...].astype(F32))

    ext_ref[0:CONV_HALO] = glu(ap_ref, bp_ref) * keep_prev
    ext_ref[CONV_HALO:CONV_HALO + tm] = glu(a_ref, b_ref)
    ext_ref[CONV_HALO + tm:2 * CONV_HALO + tm] = glu(an_ref, bn_ref) * keep_next
    off = CONV_HALO - CONV_W // 2
    span = tm + 8 * ((off + CONV_W - 1) // 8)
    for s in range(1, 8):
        shift_ref[s - 1, 0:span] = ext_ref[s:s + span]
    for cj in range(CONV_C // 128):
        cs = slice(cj * 128, (cj + 1) * 128)
        acc = jnp.broadcast_to(cb_ref[:, cs], (tm, 128))
        for k in range(CONV_W):
            a, s = divmod(off + k, 8)
            src = ext_ref if s == 0 else shift_ref.at[s - 1]
            acc = acc + cw_ref[k:k + 1, cs] * src[8 * a:8 * a + tm, cs]
        conv_ref[:, cs] = acc
    acc = conv_ref[...]
    mu = jnp.mean(acc, axis=-1, keepdims=True)
    xc = acc - mu
    var = jnp.mean(xc * xc, axis=-1, keepdims=True)
    u = xc * lax.rsqrt(var + EPS) * lng_ref[...] + lnb_ref[...]
    o_ref[:, HG_V:] = _silu(u).astype(BF16)


def _ev_readout(p, of, ob, gain, cw, cb, lng, lnb, geo):
    tm = _pick_tile(geo, (128,))
    hb = tm // CONV_HALO
    nhalo = geo.T // CONV_HALO

    def row(colblk):
        return pl.BlockSpec((tm, HG_V), lambda i: (i, colblk))

    def prev(colblk):
        return pl.BlockSpec((CONV_HALO, CONV_C), lambda i: (jnp.maximum(i * hb - 1, 0), colblk))

    def nxt(colblk):
        return pl.BlockSpec((CONV_HALO, CONV_C), lambda i: (jnp.minimum((i + 1) * hb, nhalo - 1), colblk))

    def vec(n):
        return pl.BlockSpec((n, CONV_C), lambda i: (0, 0))

    return pl.pallas_call(
        functools.partial(_ev_readout_kernel, tm=tm, geo=geo),
        name="ev_readout",
        out_shape=jax.ShapeDtypeStruct((geo.T, HG_V + CONV_C), BF16),
        grid=(geo.T // tm,),
        in_specs=[row(0), row(0), row(4), row(5), row(6), prev(5), prev(6), nxt(5), nxt(6),
                  vec(1), vec(CONV_W), vec(1), vec(1), vec(1)],
        out_specs=pl.BlockSpec((tm, HG_V + CONV_C), lambda i: (i, 0)),
        scratch_shapes=[pltpu.VMEM((tm + 2 * CONV_HALO, CONV_C), F32), pltpu.VMEM((tm, CONV_C), F32),
                        pltpu.VMEM((7, tm + 2 * CONV_HALO, CONV_C), F32)],
        compiler_params=_params("arbitrary"),
    )(of, ob, p, p, p, p, p, p, p, gain.reshape(1, -1), cw, cb.reshape(1, -1),
      lng.reshape(1, -1), lnb.reshape(1, -1))


def _rope(x, cos, sin_signed):
    half = x.shape[-1] // 2
    rot = jnp.concatenate([pltpu.roll(x[:, :half], half // 2, axis=1),
                           pltpu.roll(x[:, half:], half // 2, axis=1)], axis=-1)
    return x * cos + rot * sin_signed


def _ret_kernel(qf_ref, kf_ref, vf_ref, cosf_ref, sinf_ref, qb_ref, kb_ref, vb_ref, cosb_ref, sinb_ref,
                dl_ref, of_ref, ob_ref, sf_ref, sb_ref):
    i = pl.program_id(1)

    @pl.when(i == 0)
    def _():
        sf_ref[...] = jnp.zeros_like(sf_ref)
        sb_ref[...] = jnp.zeros_like(sb_ref)

    _ret_chunk(qf_ref, kf_ref, vf_ref, cosf_ref, sinf_ref, dl_ref[0], of_ref, sf_ref, reverse=False)
    _ret_chunk(qb_ref, kb_ref, vb_ref, cosb_ref, sinb_ref, dl_ref[1], ob_ref, sb_ref, reverse=True)


def _ret_chunk(q_ref, k_ref, v_ref, cos_ref, sin_ref, dl, o_ref, s_ref, *, reverse):
    c = RET_CHUNK
    lg_all = -jnp.log1p(jnp.exp(-dl))
    row = lax.broadcasted_iota(jnp.int32, (c, c), 0)
    col = lax.broadcasted_iota(jnp.int32, (c, c), 1)
    idx = lax.broadcasted_iota(jnp.int32, (c, 1), 0).astype(F32)
    if reverse:
        live = col >= row
        dist = (col - row).astype(F32)
        q_pow = c - idx
        k_pow = idx
    else:
        live = row >= col
        dist = (row - col).astype(F32)
        q_pow = idx + 1.0
        k_pow = c - 1.0 - idx
    cos = cos_ref[...]
    sin = sin_ref[...]
    nt = (((1,), (1,)), ((), ()))

    for h in range(RET_HEADS):
        lg = lg_all[h:h + 1]
        dmask = jnp.where(live, jnp.exp(lg * dist), 0.0)
        q = _rope(q_ref[:, h * RET_DK:(h + 1) * RET_DK].astype(F32), cos, sin)
        k = _rope(k_ref[:, h * RET_DK:(h + 1) * RET_DK].astype(F32) * (RET_DK ** -0.5), cos, sin)
        vb = v_ref[:, h * RET_DV:(h + 1) * RET_DV]
        scores = lax.dot_general(q.astype(BF16), k.astype(BF16), nt, preferred_element_type=F32) * dmask
        intra = jnp.dot(scores.astype(BF16), vb, preferred_element_type=F32)
        s = s_ref[h]
        inter = jnp.dot((q * jnp.exp(lg * q_pow)).astype(BF16), s.astype(BF16),
                        preferred_element_type=F32)
        o_ref[:, h * RET_DV:(h + 1) * RET_DV] = (inter + intra).astype(BF16)
        kdec = (k * jnp.exp(lg * k_pow)).T.astype(BF16)
        s_ref[h] = jnp.exp(lg * c) * s + jnp.dot(kdec, vb, preferred_element_type=F32)


def _ret_scan(p, cos_tab, sin_tab, decay_logit, geo):
    c = RET_CHUNK
    steps = (geo.Lc + geo.N) // c
    nc = geo.Lc // c
    nl = geo.N // c

    def spec(width, colblk, reverse):
        return pl.BlockSpec((c, width), lambda b, i: (_chunk_index(b, i, c, geo, reverse), colblk))

    def tab(reverse):
        def index(b, i):
            lat = (nl - 1 - (i - nc)) if reverse else (i - nc)
            return (jnp.where(i < nc, nl, lat), 0)
        return pl.BlockSpec((c, RET_DK), index)

    def direction(reverse):
        return [spec(RET_QK, 0, reverse), spec(RET_QK, 1, reverse), spec(RET_V, 1, reverse),
                tab(reverse), tab(reverse)]

    out = jax.ShapeDtypeStruct((geo.T, RET_V), BF16)
    state = pltpu.VMEM((RET_HEADS, RET_DK, RET_DV), F32)
    return pl.pallas_call(
        _ret_kernel,
        name="ret_scan",
        out_shape=[out, out],
        grid=(geo.B, steps),
        in_specs=direction(False) + direction(True)
        + [pl.BlockSpec((2, RET_HEADS, 1), lambda b, i: (0, 0, 0))],
        out_specs=[spec(RET_V, 0, False), spec(RET_V, 0, True)],
        scratch_shapes=[state, state],
        compiler_params=_params("arbitrary", "arbitrary"),
    )(p, p, p, cos_tab, sin_tab, p, p, p, cos_tab, sin_tab, decay_logit.reshape(2, RET_HEADS, 1))


def _rope_tables(n):
    t = jnp.arange(n)
    quarter = RET_DK // 4
    inv = 1.0 / (ROPE_BASE ** (jnp.arange(quarter, dtype=F32) / quarter))
    ang_r = (t // GRID_W).astype(F32)[:, None] * inv
    ang_c = (t % GRID_W).astype(F32)[:, None] * inv
    cos = jnp.concatenate([jnp.cos(ang_r), jnp.cos(ang_r), jnp.cos(ang_c), jnp.cos(ang_c)], axis=-1)
    sin = jnp.concatenate([-jnp.sin(ang_r), jnp.sin(ang_r), -jnp.sin(ang_c), jnp.sin(ang_c)], axis=-1)
    cos = jnp.concatenate([cos, jnp.ones((RET_CHUNK, RET_DK), F32)], axis=0)
    sin = jnp.concatenate([sin, jnp.zeros((RET_CHUNK, RET_DK), F32)], axis=0)
    return cos, sin


def _ret_readout_kernel(of_ref, ob_ref, gate_ref, o_ref):
    for h in range(RET_HEADS):
        sl = slice(h * RET_DV, (h + 1) * RET_DV)
        o = of_ref[:, sl].astype(F32) + ob_ref[:, sl].astype(F32)
        r = o * lax.rsqrt(jnp.mean(o * o, axis=-1, keepdims=True) + EPS)
        o_ref[:, sl] = (_silu(gate_ref[:, sl].astype(F32)) * r).astype(BF16)


def _ret_readout(p, of, ob, geo):
    tm = _pick_tile(geo, (128,))
    spec = pl.BlockSpec((tm, RET_V), lambda i: (i, 0))
    return pl.pallas_call(
        _ret_readout_kernel,
        name="ret_readout",
        out_shape=jax.ShapeDtypeStruct((geo.T, RET_V), BF16),
        grid=(geo.T // tm,),
        in_specs=[spec, spec, pl.BlockSpec((tm, RET_V), lambda i: (i, 2))],
        out_specs=spec,
        compiler_params=_params("arbitrary"),
    )(of, ob, p)


def _router_kernel(h_ref, rw_ref, rb_ref, eidx_ref, rank_ref, w_ref, cnt_ref, carry_ref, *, tm):
    i = pl.program_id(0)

    @pl.when(i == 0)
    def _():
        carry_ref[...] = jnp.zeros_like(carry_ref)

    half = D_MODEL // 2
    h_lo, h_hi = _unpack_rows(h_ref[...])
    h_lo = h_lo.astype(BF16)
    h_hi = h_hi.astype(BF16)
    rest = rw_ref[...]
    logits = jnp.zeros((tm, N_EXPERTS), F32)
    for _ in range(3):
        part = rest.astype(BF16)
        rest = rest - part.astype(F32)
        logits = (logits + jnp.dot(h_lo, part[:half], preferred_element_type=F32)
                  + jnp.dot(h_hi, part[half:], preferred_element_type=F32))
    s = _sigmoid(logits)
    sel = s + rb_ref[...]
    lane = lax.broadcasted_iota(jnp.int32, (tm, N_EXPERTS), 1).astype(F32)
    grp = jnp.floor(lane * (1.0 / GROUP_SIZE))
    ninf = -jnp.inf
    none = float(N_EXPERTS)

    gscore = jnp.zeros((tm, N_EXPERTS), F32)
    gcols = []
    for g in range(N_GROUPS):
        in_g = grp == float(g)
        v1 = jnp.max(jnp.where(in_g, sel, ninf), axis=-1, keepdims=True)
        i1 = jnp.min(jnp.where(in_g & (sel == v1), lane, none), axis=-1, keepdims=True)
        v2 = jnp.max(jnp.where(in_g & (lane != i1), sel, ninf), axis=-1, keepdims=True)
        gcols.append(v1 + v2)
        gscore = jnp.where(in_g, v1 + v2, gscore)
    beaten = jnp.zeros((tm, N_EXPERTS), F32)
    for g in range(N_GROUPS):
        wins = (gcols[g] > gscore) | ((gcols[g] == gscore) & (float(g) < grp))
        beaten = beaten + jnp.where(wins, 1.0, 0.0)
    cand = jnp.where(beaten < float(TOPK_GROUPS), sel, ninf)

    lane_k = lax.broadcasted_iota(jnp.int32, (tm, TOP_K), 1)
    eidx = jnp.zeros((tm, TOP_K), F32)
    wsel = jnp.zeros((tm, TOP_K), F32)
    chosen = jnp.zeros((tm, N_EXPERTS), F32)
    picks = []
    for k in range(TOP_K):
        v = jnp.max(cand, axis=-1, keepdims=True)
        ik = jnp.min(jnp.where(cand == v, lane, none), axis=-1, keepdims=True)
        hit = lane == ik
        picks.append(ik)
        eidx = jnp.where(lane_k == k, ik, eidx)
        wsel = jnp.where(lane_k == k, jnp.sum(jnp.where(hit, s, 0.0), axis=-1, keepdims=True), wsel)
        chosen = jnp.where(hit, 1.0, chosen)
        cand = jnp.where(hit, ninf, cand)
    w_ref[...] = wsel / jnp.sum(wsel, axis=-1, keepdims=True) * ROUTED_SCALE
    eidx_ref[...] = eidx.astype(jnp.int32)

    r = lax.broadcasted_iota(jnp.int32, (tm, tm), 0)
    c = lax.broadcasted_iota(jnp.int32, (tm, tm), 1)
    below = jnp.where(c < r, 1.0, 0.0).astype(BF16)
    carry = carry_ref[...]
    pos = jnp.dot(below, chosen.astype(BF16), preferred_element_type=F32) + carry
    rank = jnp.zeros((tm, TOP_K), jnp.int32)
    for k in range(TOP_K):
        rk = jnp.sum(jnp.where(lane == picks[k], pos, 0.0), axis=-1, keepdims=True)
        rank = jnp.where(lane_k == k, rk.astype(jnp.int32), rank)
    rank_ref[...] = rank
    carry = carry + jnp.sum(chosen, axis=0, keepdims=True)
    carry_ref[...] = carry
    cnt_ref[...] = carry


def _router(h_packed, rw, rb, geo):
    tm = 256 if geo.T % 256 == 0 else 128
    tok = pl.BlockSpec((tm, TOP_K), lambda i: (i, 0))
    one = pl.BlockSpec((1, N_EXPERTS), lambda i: (0, 0))
    return pl.pallas_call(
        functools.partial(_router_kernel, tm=tm),
        name="router",
        out_shape=[jax.ShapeDtypeStruct((geo.T, TOP_K), jnp.int32),
                   jax.ShapeDtypeStruct((geo.T, TOP_K), jnp.int32),
                   jax.ShapeDtypeStruct((geo.T, TOP_K), F32),
                   jax.ShapeDtypeStruct((1, N_EXPERTS), F32)],
        grid=(geo.T // tm,),
        in_specs=[pl.BlockSpec((tm, D_MODEL // 2), lambda i: (i, 0)),
                  pl.BlockSpec((D_MODEL, N_EXPERTS), lambda i: (0, 0)), one],
        out_specs=[tok, tok, tok, one],
        scratch_shapes=[pltpu.VMEM((1, N_EXPERTS), F32)],
        compiler_params=_params("arbitrary"),
    )(h_packed, rw, rb.reshape(1, N_EXPERTS))


def _n_blocks(geo):
    return -(-(geo.T * TOP_K) // MOE_BLOCK) + N_EXPERTS


def _dest_kernel(cnt_ref, eidx_ref, rank_ref, dest_ref, blk_ref, *, n_blocks):
    eidx = eidx_ref[...]
    dest = rank_ref[...]
    blk_row = (lax.broadcasted_iota(jnp.int32, blk_ref.shape, 0) * 128
               + lax.broadcasted_iota(jnp.int32, blk_ref.shape, 1)) * MOE_BLOCK
    blk = jnp.zeros(blk_ref.shape, jnp.int32)
    start = jnp.int32(0)
    for e in range(N_EXPERTS):
        padded = (cnt_ref[e] + (MOE_BLOCK - 1)) // MOE_BLOCK * MOE_BLOCK
        dest = dest + jnp.where(eidx == e, start, 0)
        start = start + padded
        blk = blk + jnp.where(start <= blk_row, 1, 0)
    dest_ref[...] = dest
    blk_ref[...] = jnp.where(blk_row == n_blocks * MOE_BLOCK, start // MOE_BLOCK,
                             jnp.minimum(blk, N_EXPERTS - 1))


def _dest(counts, eidx, rank, geo):
    rows = geo.T * TOP_K // 128
    brow = -(-(_n_blocks(geo) + 1) // 128)
    full = pl.BlockSpec((rows, 128), lambda: (0, 0))
    dest, blk = pl.pallas_call(
        functools.partial(_dest_kernel, n_blocks=_n_blocks(geo)),
        name="dest",
        out_shape=[jax.ShapeDtypeStruct((rows, 128), jnp.int32),
                   jax.ShapeDtypeStruct((brow, 128), jnp.int32)],
        in_specs=[pl.BlockSpec(memory_space=pltpu.SMEM), full, full],
        out_specs=[full, pl.BlockSpec((brow, 128), lambda: (0, 0))],
    )(counts, eidx.reshape(rows, 128), rank.reshape(rows, 128))
    return dest.reshape(-1), blk.reshape(-1)[:_n_blocks(geo) + 1]


def _zero_fill(cnt_ref, xs_hbm, zero_ref, zsem, n_rows, wait):
    def piece(pos, size):
        if size >= 8:
            copies = [(pl.multiple_of(pos, 8), size)]
        else:
            copies = [(pos + r, 1) for r in range(size)]
        for p, s in copies:
            cp = pltpu.make_async_copy(zero_ref.at[pl.ds(0, s)], xs_hbm.at[pl.ds(p, s)], zsem)
            cp.wait() if wait else cp.start()

    def per_expert(e, start):
        cnt = cnt_ref[e]
        padded = (cnt + (MOE_BLOCK - 1)) // MOE_BLOCK * MOE_BLOCK
        pad = padded - cnt
        pos = start + cnt
        size = 1
        while size < MOE_BLOCK:
            take = (pad & size) != 0
            pl.when(take)(functools.partial(piece, pos, size))
            pos = pos + jnp.where(take, size, 0)
            size *= 2
        return start + padded

    end = lax.fori_loop(0, N_EXPERTS, per_expert, jnp.int32(0))

    def per_block(j, carry):
        piece(end + j * MOE_BLOCK, MOE_BLOCK)
        return carry

    lax.fori_loop(0, (n_rows - end) // MOE_BLOCK, per_block, 0)


def _sc_scatter_rows(src, order, n_rows):
    nw = V7X_SC_CORES * V7X_SC_SUBCORES
    t, w = src.shape
    per = t // nw
    n = per // SC_ROWS
    assert t % (nw * SC_ROWS * 2) == 0
    mesh = plsc.VectorSubcoreMesh(core_axis_name="c", subcore_axis_name="s")

    @functools.partial(
        pl.kernel, mesh=mesh,
        out_type=jax.ShapeDtypeStruct((n_rows, w), src.dtype),
        scratch_types=[pltpu.VMEM((SC_ROWS, w), src.dtype)] * 2
        + [pltpu.VMEM((SC_ROWS,), jnp.int32)] * (2 * TOP_K) + [pltpu.SemaphoreType.DMA] * 2,
    )
    def scatter(src_hbm, idx_hbm, out_hbm, *scratch):
        rows_vs = scratch[:2]
        idx_vs = (scratch[2:2 + TOP_K], scratch[2 + TOP_K:2 + 2 * TOP_K])
        sems = scratch[2 + 2 * TOP_K:]
        wid = lax.axis_index("s") * V7X_SC_CORES + lax.axis_index("c")
        base = wid * per

        def load(chunk, b):
            off = pl.multiple_of(base + chunk * SC_ROWS, 8)
            pltpu.sync_copy(src_hbm.at[pl.ds(off, SC_ROWS)], rows_vs[b])
            for k in range(TOP_K):
                pltpu.sync_copy(idx_hbm.at[pl.ds(pl.multiple_of(k * t + off, 8), SC_ROWS)], idx_vs[b][k])

        def copies(b):
            return [pltpu.make_async_copy(rows_vs[b], out_hbm.at[idx_vs[b][k]], sems[b])
                    for k in range(TOP_K)]

        load(0, 0)

        @pl.loop(0, n, step=2)
        def _(j):
            for b in range(2):
                cur = j + b
                for cp in copies(b):
                    cp.start()
                pl.when(cur + 1 < n)(functools.partial(load, cur + 1, 1 - b))
                for cp in copies(b):
                    cp.wait()

    return scatter(src, order)


def _zero_pad_kernel(cnt_ref, xs_in, xs_hbm, zero_ref, zsem, *, n_rows):
    del xs_in
    zero_ref[...] = jnp.zeros_like(zero_ref)
    _zero_fill(cnt_ref, xs_hbm, zero_ref, zsem, n_rows, wait=False)
    _zero_fill(cnt_ref, xs_hbm, zero_ref, zsem, n_rows, wait=True)


def _dispatch(counts, order, h_packed, geo):
    n_rows = _n_blocks(geo) * MOE_BLOCK
    width = D_MODEL // 2
    xs = _sc_scatter_rows(h_packed[:geo.T], order, n_rows)
    return pl.pallas_call(
        functools.partial(_zero_pad_kernel, n_rows=n_rows),
        name="zero_pad",
        out_shape=jax.ShapeDtypeStruct((n_rows, width), PACKED),
        in_specs=[pl.BlockSpec(memory_space=pltpu.SMEM), pl.BlockSpec(memory_space=pl.ANY)],
        out_specs=pl.BlockSpec(memory_space=pl.ANY),
        scratch_shapes=[pltpu.VMEM((MOE_BLOCK, width), PACKED), pltpu.SemaphoreType.DMA],
        input_output_aliases={1: 0},
    )(counts, xs)


def _swiglu_packed(x_ref, wg_ref, wu_ref, wd_ref):
    half = D_MODEL // 2
    lo, hi = _unpack_rows(x_ref[...])
    lo = lo.astype(BF16)
    hi = hi.astype(BF16)

    def proj(w_ref):
        return (jnp.dot(lo, w_ref[:half], preferred_element_type=F32)
                + jnp.dot(hi, w_ref[half:], preferred_element_type=F32))

    g = proj(wg_ref)
    u = proj(wu_ref)
    return jnp.dot((_silu(g) * u).astype(BF16), wd_ref[...], preferred_element_type=F32)


def _expert_kernel(blk_ref, x_ref, wg_ref, wu_ref, wd_ref, o_ref, wgb_ref, wub_ref, wdb_ref, *, nb):
    j = pl.program_id(0)
    n_used = blk_ref[nb]

    @pl.when((j == 0) | (blk_ref[j] != blk_ref[jnp.maximum(j - 1, 0)]))
    def _():
        wgb_ref[...] = wg_ref[...].astype(BF16)
        wub_ref[...] = wu_ref[...].astype(BF16)
        wdb_ref[...] = wd_ref[...].astype(BF16)

    @pl.when(j < n_used)
    def _():
        o_ref[...] = _pack_rows(_swiglu_packed(x_ref, wgb_ref, wub_ref, wdb_ref))

    @pl.when(j >= n_used)
    def _():
        o_ref[...] = jnp.zeros_like(o_ref)


def _experts(blk_e, xs, wg, wu, wd, layer, geo):
    nb = _n_blocks(geo)
    rows = pl.BlockSpec((MOE_BLOCK, D_MODEL // 2), lambda j, be: (j, 0))

    def wspec(r, c):
        return pl.BlockSpec((None, None, r, c), lambda j, be: (layer, be[j], 0, 0))

    return pl.pallas_call(
        functools.partial(_expert_kernel, nb=nb),
        name="experts",
        out_shape=jax.ShapeDtypeStruct((nb * MOE_BLOCK, D_MODEL // 2), PACKED),
        grid_spec=pltpu.PrefetchScalarGridSpec(
            num_scalar_prefetch=1,
            grid=(nb,),
            in_specs=[rows, wspec(D_MODEL, EXPERT_FF), wspec(D_MODEL, EXPERT_FF),
                      wspec(EXPERT_FF, D_MODEL)],
            out_specs=rows,
            scratch_shapes=[pltpu.VMEM((D_MODEL, EXPERT_FF), BF16),
                            pltpu.VMEM((D_MODEL, EXPERT_FF), BF16),
                            pltpu.VMEM((EXPERT_FF, D_MODEL), BF16)]),
        compiler_params=_params("arbitrary"),
    )(blk_e, xs, wg, wu, wd)


def _sc_gather_rows(table, idx):
    nw = V7X_SC_CORES * V7X_SC_SUBCORES
    m = idx.shape[0]
    w = table.shape[1]
    per = m // nw
    n = per // SC_ROWS
    assert m % (nw * SC_ROWS * 2) == 0
    mesh = plsc.VectorSubcoreMesh(core_axis_name="c", subcore_axis_name="s")

    @functools.partial(
        pl.kernel, mesh=mesh,
        out_type=jax.ShapeDtypeStruct((m, w), table.dtype),
        scratch_types=[pltpu.VMEM((SC_ROWS,), jnp.int32)] * 2
        + [pltpu.VMEM((SC_ROWS, w), table.dtype)] * 2 + [pltpu.SemaphoreType.DMA] * 2,
    )
    def gather(table_hbm, idx_hbm, out_hbm, *scratch):
        idx_vs, rows_vs, sems = scratch[:2], scratch[2:4], scratch[4:]
        wid = lax.axis_index("s") * V7X_SC_CORES + lax.axis_index("c")
        base = wid * per

        def fetch(b):
            return pltpu.make_async_copy(table_hbm.at[idx_vs[b]], rows_vs[b], sems[b])

        def start(chunk, b):
            off = pl.multiple_of(base + chunk * SC_ROWS, 8)
            pltpu.sync_copy(idx_hbm.at[pl.ds(off, SC_ROWS)], idx_vs[b])
            fetch(b).start()

        start(0, 0)

        @pl.loop(0, n, step=2)
        def _(j):
            for b in range(2):
                cur = j + b
                pl.when(cur + 1 < n)(functools.partial(start, cur + 1, 1 - b))
                fetch(b).wait()
                off = pl.multiple_of(base + cur * SC_ROWS, 8)
                pltpu.sync_copy(rows_vs[b], out_hbm.at[pl.ds(off, SC_ROWS)])

    return gather(table, idx)


def _shared_kernel(x_ref, wg_ref, wu_ref, wd_ref, o_ref):
    o_ref[...] = _swiglu_packed(x_ref, wg_ref, wu_ref, wd_ref)


def _shared_expert(h_packed, wg, wu, wd, geo):
    tm = _pick_tile(geo, (512, 256, 128))
    ff = wg.shape[1]
    return pl.pallas_call(
        _shared_kernel,
        name="shared_expert",
        out_shape=jax.ShapeDtypeStruct((geo.T, D_MODEL), F32),
        grid=(geo.T // tm,),
        in_specs=[pl.BlockSpec((tm, D_MODEL // 2), lambda i: (i, 0)),
                  pl.BlockSpec((D_MODEL, ff), lambda i: (0, 0)),
                  pl.BlockSpec((D_MODEL, ff), lambda i: (0, 0)),
                  pl.BlockSpec((ff, D_MODEL), lambda i: (0, 0))],
        out_specs=pl.BlockSpec((tm, D_MODEL), lambda i: (i, 0)),
        compiler_params=_params("arbitrary"),
    )(h_packed, wg, wu, wd)


def _combine_kernel(w_ref, x_ref, sh_ref, m_ref, *rest, post):
    if post == "next":
        ng_ref, nsh_ref, nsc_ref, g_ref, o_ref, hn_ref = rest
    else:
        ng_ref, g_ref, o_ref = rest
    half = D_MODEL // 2
    w = w_ref[...]
    acc_lo = sh_ref[:, :half]
    acc_hi = sh_ref[:, half:]
    for k in range(TOP_K):
        lo, hi = _unpack_rows(g_ref[k])
        acc_lo = acc_lo + w[:, k:k + 1] * lo
        acc_hi = acc_hi + w[:, k:k + 1] * hi
    gate = m_ref[0]
    y_lo = x_ref[:, :half] + gate[:, :half] * acc_lo
    y_hi = x_ref[:, half:] + gate[:, half:] * acc_hi
    ms = (jnp.sum(y_lo * y_lo, axis=-1, keepdims=True)
          + jnp.sum(y_hi * y_hi, axis=-1, keepdims=True)) * (1.0 / D_MODEL)
    inv = lax.rsqrt(ms + EPS)
    ng = ng_ref[...]
    n_lo = y_lo * inv * ng[:, :half]
    n_hi = y_hi * inv * ng[:, half:]
    if post == "next":
        o_ref[:, :half] = y_lo
        o_ref[:, half:] = y_hi
        sc = nsc_ref[0]
        sft = nsh_ref[0]
        hn_ref[:, :half] = (n_lo * (1.0 + sc[:, :half]) + sft[:, :half]).astype(BF16)
        hn_ref[:, half:] = (n_hi * (1.0 + sc[:, half:]) + sft[:, half:]).astype(BF16)
    else:
        o_ref[:, :half] = n_lo
        o_ref[:, half:] = n_hi


def _combine(order, w, x, sh, ys, mod3, m_gate, post, norm_g, next_mod3, geo):
    half = D_MODEL // 2
    gathered = _sc_gather_rows(ys, order).reshape(TOP_K, geo.T, half)
    tt = _pick_tile(geo, (256, 128))
    n = geo.T // tt
    rows = pl.BlockSpec((tt, D_MODEL), lambda i: (i, 0))
    vec = pl.BlockSpec((1, D_MODEL), lambda i: (0, 0))
    in_specs = [pl.BlockSpec((tt, TOP_K), lambda i: (i, 0)), rows, rows,
                _mod_spec(m_gate, tt, geo), vec]
    args = [w, x, sh, mod3, norm_g.reshape(1, D_MODEL)]
    out_shape = [jax.ShapeDtypeStruct((geo.T, D_MODEL), F32)]
    out_specs = [rows]
    if post == "next":
        in_specs += [_mod_spec(0, tt, geo), _mod_spec(1, tt, geo)]
        args += [next_mod3, next_mod3]
        out_shape.append(jax.ShapeDtypeStruct((geo.T, D_MODEL), BF16))
        out_specs.append(rows)
    in_specs.append(pl.BlockSpec((TOP_K, tt, half), lambda i: (0, i, 0)))
    args.append(gathered)
    return pl.pallas_call(
        functools.partial(_combine_kernel, post=post),
        name="combine",
        out_shape=out_shape,
        grid=(n,),
        in_specs=in_specs,
        out_specs=out_specs,
        compiler_params=_params("arbitrary"),
    )(*args)


def _moe(x, h_packed, mod3, rw, rb, wg, wu, wd, layer, sg, su, sd, post, norm_g, next_mod3, geo):
    eidx, rank, w, counts = _router(h_packed, rw, rb, geo)
    counts = counts.reshape(N_EXPERTS).astype(jnp.int32)
    dest, blk_e = _dest(counts, eidx, rank, geo)
    order = dest.reshape(geo.T, TOP_K).T.reshape(-1)
    xs = _dispatch(counts, order, h_packed, geo)
    sh = _shared_expert(h_packed, sg.astype(BF16), su.astype(BF16), sd.astype(BF16), geo)
    ys = _experts(blk_e, xs, wg, wu, wd, layer, geo)
    return _combine(order, w, x, sh, ys, mod3, 5, post, norm_g, next_mod3, geo)


def kernel(x, c, ctx, c_ctx, ada_w, ada_b, norm_mix, norm_ffn, norm_final, ev_w_in, ev_w_out, hgrn_lb, hgrn_norm, conv_w, conv_b, conv_norm_g, conv_norm_b, ret_w_in, ret_w_out, ret_decay, router_w, router_b, exp_gate, exp_up, exp_down, sh_gate, sh_up, sh_down):
    b, n, d = x.shape
    lc = ctx.shape[1]
    depth = ada_w.shape[0]
    geo = _geo(b, n, lc)
    assert d == D_MODEL and b < MOD_ROWS
    assert n % RET_CHUNK == 0 and lc % RET_CHUNK == 0

    xs = jnp.concatenate([x.reshape(geo.BN, d), ctx.reshape(geo.BL, d)], axis=0)
    cond = jnp.zeros((MOD_ROWS, d), F32).at[:b].set(c).at[b].set(c_ctx)
    cos_tab, sin_tab = _rope_tables(n)

    mods = [_adaln(cond, ada_w, ada_b, l).reshape(MOD_ROWS * N_MOD, 1, d) for l in range(depth)]
    h = _normmod(xs, norm_mix[0], mods[0], 0, 1, geo, packed=False)
    for l in range(depth):
        j = l // 2
        last = l == depth - 1
        mod3 = mods[l]
        tail = geo._replace(T=geo.BN, BL=0, Lc=0) if last else geo
        if l % 2 == 0:
            p = _matmul(h, ev_w_in[j].astype(BF16), geo)
            of, ob = _hgrn_scan(p, hgrn_lb, l, geo)
            mix = _ev_readout(p, of, ob, hgrn_norm[j], conv_w[j], conv_b[j],
                              conv_norm_g[j], conv_norm_b[j], geo)
            xs = _matmul_resid(mix, ev_w_out[j].astype(BF16), xs, mod3, 2, tail)
        else:
            p = _matmul(h, ret_w_in[j].astype(BF16), geo)
            of, ob = _ret_scan(p, cos_tab, sin_tab, ret_decay[j], geo)
            mix = _ret_readout(p, of, ob, geo)
            xs = _matmul_resid(mix, ret_w_out[j].astype(BF16), xs, mod3, 2, tail)
        h_packed = _normmod(xs, norm_ffn[l], mod3, 3, 4, tail, packed=True)
        moe_w = (router_w[l], router_b[l], exp_gate, exp_up, exp_down, l, sh_gate[l], sh_up[l], sh_down[l])
        if last:
            (out,) = _moe(xs, h_packed, mod3, *moe_w, "final", norm_final, None, tail)
        else:
            xs, h = _moe(xs, h_packed, mod3, *moe_w, "next", norm_mix[l + 1], mods[l + 1], tail)
    return out.reshape(b, n, d)
```

```python
import collections
import functools
import math

import jax
import jax.numpy as jnp
from jax import lax
from jax.experimental import pallas as pl
from jax.experimental.pallas import tpu as pltpu
from jax.experimental.pallas import tpu_sc as plsc

F32 = jnp.float32
BF16 = jnp.bfloat16

D_MODEL = 2048
N_MOD = 6
EPS = 1e-6
GRID_W = 64
ROPE_BASE = 10000.0

HG_HEADS = 8
HG_DK = 128
HG_DV = 128
HG_F = HG_HEADS * HG_DK
HG_V = HG_HEADS * HG_DV
CONV_C = D_MODEL // 2
CONV_W = 31
CONV_HALO = 16
HG_CHUNK = 128

RET_HEADS = 8
RET_DK = D_MODEL // RET_HEADS
RET_DV = 2 * RET_DK
RET_QK = RET_HEADS * RET_DK
RET_V = RET_HEADS * RET_DV
RET_CHUNK = 256

N_EXPERTS = 64
EXPERT_FF = D_MODEL // 4
TOP_K = 8
N_GROUPS = 8
GROUP_SIZE = N_EXPERTS // N_GROUPS
TOPK_GROUPS = 4
ROUTED_SCALE = 2.5
MOE_BLOCK = 512

V7X_SC_CORES = 2
V7X_SC_SUBCORES = 16
SC_ROWS = 32
MOD_ROWS = 16
VMEM_LIMIT = 56 * 1024 * 1024

Geo = collections.namedtuple("Geo", "B N Lc BN BL T")


def _geo(b, n, lc):
    return Geo(b, n, lc, b * n, b * lc, b * n + b * lc)


def _pick_tile(geo, cands):
    for t in cands:
        if geo.N % t == 0 and geo.BL % t == 0:
            return t
    raise ValueError("no row tile fits the sequence lengths")


def _mod_row(i, tm, geo):
    return jnp.where(i < geo.BN // tm, i // (geo.N // tm), geo.B)


def _mod_spec(m, tm, geo, ngrid=1):
    if ngrid == 1:
        return pl.BlockSpec((1, 1, D_MODEL), lambda i: (_mod_row(i, tm, geo) * N_MOD + m, 0, 0))
    return pl.BlockSpec((1, 1, D_MODEL), lambda i, j: (_mod_row(i, tm, geo) * N_MOD + m, 0, j))


def _params(*sem):
    return pltpu.CompilerParams(dimension_semantics=sem, vmem_limit_bytes=VMEM_LIMIT)


def _sigmoid(x):
    return jax.nn.sigmoid(x)


def _silu(x):
    return x * jax.nn.sigmoid(x)


def _adaln_kernel(c_ref, w_ref, b_ref, o_ref):
    a = _silu(c_ref[...]).astype(BF16)
    o_ref[...] = jnp.dot(a, w_ref[...].astype(BF16), preferred_element_type=F32) + b_ref[...]


def _adaln(cond, w, b, layer):
    depth, k, n = w.shape
    tn = 1024
    return pl.pallas_call(
        _adaln_kernel,
        name="adaln",
        out_shape=jax.ShapeDtypeStruct((MOD_ROWS, n), F32),
        grid=(n // tn,),
        in_specs=[pl.BlockSpec((MOD_ROWS, k), lambda j: (0, 0)),
                  pl.BlockSpec((None, k, tn), lambda j: (layer, 0, j)),
                  pl.BlockSpec((None, 1, tn), lambda j: (layer, 0, j))],
        out_specs=pl.BlockSpec((MOD_ROWS, tn), lambda j: (0, j)),
        compiler_params=_params("arbitrary"),
    )(cond, w, b.reshape(depth, 1, n))


PACKED = jnp.int32


def _pack_rows(x):
    n = x.shape[-1] // 2
    bits = lax.bitcast_convert_type(x.astype(BF16).astype(F32), jnp.uint32)
    words = (bits[:, n:] & jnp.uint32(0xFFFF0000)) | (bits[:, :n] >> 16)
    return lax.bitcast_convert_type(words, PACKED)


def _unpack_rows(p):
    u = lax.bitcast_convert_type(p, jnp.uint32)
    lo = lax.bitcast_convert_type(u << 16, F32)
    hi = lax.bitcast_convert_type(u & jnp.uint32(0xFFFF0000), F32)
    return lo, hi


def _normmod_kernel(x_ref, g_ref, sh_ref, sc_ref, o_ref, *, packed):
    x = x_ref[...]
    y = x * lax.rsqrt(jnp.mean(x * x, axis=-1, keepdims=True) + EPS) * g_ref[...]
    h = y * (1.0 + sc_ref[0]) + sh_ref[0]
    o_ref[...] = _pack_rows(h) if packed else h.astype(BF16)


def _normmod(x, g, mod3, m_shift, m_scale, geo, packed):
    tm = _pick_tile(geo, (256, 128))
    spec = pl.BlockSpec((tm, D_MODEL), lambda i: (i, 0))
    if packed:
        out_shape = jax.ShapeDtypeStruct((geo.T, D_MODEL // 2), PACKED)
        out_spec = pl.BlockSpec((tm, D_MODEL // 2), lambda i: (i, 0))
    else:
        out_shape = jax.ShapeDtypeStruct((geo.T, D_MODEL), BF16)
        out_spec = spec
    return pl.pallas_call(
        functools.partial(_normmod_kernel, packed=packed),
        name="normmod",
        out_shape=out_shape,
        grid=(geo.T // tm,),
        in_specs=[spec, pl.BlockSpec((1, D_MODEL), lambda i: (0, 0)),
                  _mod_spec(m_shift, tm, geo), _mod_spec(m_scale, tm, geo)],
        out_specs=out_spec,
        compiler_params=_params("arbitrary"),
    )(x, g.reshape(1, D_MODEL), mod3, mod3)


def _mm_kernel(a_ref, w_ref, o_ref):
    o_ref[...] = jnp.dot(a_ref[...], w_ref[...], preferred_element_type=F32).astype(o_ref.dtype)


def _mm_resid_kernel(a_ref, w_ref, x_ref, m_ref, o_ref):
    y = jnp.dot(a_ref[...], w_ref[...], preferred_element_type=F32)
    o_ref[...] = x_ref[...] + m_ref[0] * y


def _matmul(a, w, geo):
    k, n = w.shape
    tm = _pick_tile(geo, (1024, 512, 256, 128))
    tn = 1024 if n % 1024 == 0 else 512
    return pl.pallas_call(
        _mm_kernel,
        name="matmul",
        out_shape=jax.ShapeDtypeStruct((geo.T, n), BF16),
        grid=(geo.T // tm, n // tn),
        in_specs=[pl.BlockSpec((tm, k), lambda i, j: (i, 0)),
                  pl.BlockSpec((k, tn), lambda i, j: (0, j))],
        out_specs=pl.BlockSpec((tm, tn), lambda i, j: (i, j)),
        compiler_params=_params("arbitrary", "arbitrary"),
    )(a, w)


def _matmul_resid(a, w, x, mod3, m_gate, geo):
    k, n = w.shape
    tm = _pick_tile(geo, (1024, 512, 256, 128))
    tn = 512
    return pl.pallas_call(
        _mm_resid_kernel,
        name="matmul_resid",
        out_shape=jax.ShapeDtypeStruct((geo.T, n), F32),
        grid=(geo.T // tm, n // tn),
        in_specs=[pl.BlockSpec((tm, k), lambda i, j: (i, 0)),
                  pl.BlockSpec((k, tn), lambda i, j: (0, j)),
                  pl.BlockSpec((tm, tn), lambda i, j: (i, j)),
                  pl.BlockSpec((1, 1, tn), lambda i, j: (_mod_row(i, tm, geo) * N_MOD + m_gate, 0, j))],
        out_specs=pl.BlockSpec((tm, tn), lambda i, j: (i, j)),
        compiler_params=_params("arbitrary", "arbitrary"),
    )(a, w, x, mod3)


def _chunk_index(b, i, chunk, geo, reverse):
    nc = geo.Lc // chunk
    nl = geo.N // chunk
    ctx0 = (geo.BN + b * geo.Lc) // chunk
    lat0 = (b * geo.N) // chunk
    if reverse:
        return jnp.where(i < nc, ctx0 + (nc - 1 - i), lat0 + (nl - 1 - (i - nc)))
    return jnp.where(i < nc, ctx0 + i, lat0 + (i - nc))


def _split_dot(tri_bf, x):
    hi = x.astype(BF16)
    r1 = x - hi.astype(F32)
    mid = r1.astype(BF16)
    lo = (r1 - mid.astype(F32)).astype(BF16)
    return (jnp.dot(tri_bf, hi, preferred_element_type=F32)
            + jnp.dot(tri_bf, mid, preferred_element_type=F32)
            + jnp.dot(tri_bf, lo, preferred_element_type=F32))


def _hgrn_kernel(qf_ref, ff_ref, vf_ref, qb_ref, fb_ref, vb_ref, lbp_ref, of_ref, ob_ref,
                 stf_ref, stb_ref, *, layer):
    i = pl.program_id(1)

    @pl.when(i == 0)
    def _():
        stf_ref[...] = jnp.zeros_like(stf_ref)
        stb_ref[...] = jnp.zeros_like(stb_ref)

    lbp = lbp_ref[...]
    e = jnp.exp(lbp - jnp.max(lbp, axis=0, keepdims=True))
    sm = e / jnp.sum(e, axis=0, keepdims=True)
    lb = sm[0:1]
    for r in range(1, layer + 1):
        lb = lb + sm[r:r + 1]

    _hgrn_chunk(qf_ref, ff_ref, vf_ref, of_ref, stf_ref, lb, reverse=False)
    _hgrn_chunk(qb_ref, fb_ref, vb_ref, ob_ref, stb_ref, lb, reverse=True)


def _hgrn_chunk(q_ref, f_ref, v_ref, o_ref, st_ref, lb, *, reverse):
    c = HG_CHUNK
    row = lax.broadcasted_iota(jnp.int32, (c, c), 0)
    col = lax.broadcasted_iota(jnp.int32, (c, c), 1)
    tri = (col >= row) if reverse else (col <= row)
    tri_bf = jnp.where(tri, 1.0, 0.0).astype(BF16)
    nt = (((1,), (1,)), ((), ()))

    for h in range(HG_HEADS):
        sl = slice(h * HG_DK, (h + 1) * HG_DK)
        qh = _silu(q_ref[:, sl].astype(F32))
        lbh = lb[:, sl]
        fg = lbh + (1.0 - lbh) * _sigmoid(f_ref[:, sl].astype(F32))
        kh = 1.0 - fg
        bcum = _split_dot(tri_bf, jnp.log(fg))
        bmid = bcum[c // 2:c // 2 + 1]
        bend = bcum[0:1] if reverse else bcum[c - 1:c]
        vb = v_ref[:, sl]
        vh = vb.astype(F32)
        a = (qh * jnp.exp(bcum - bmid)).astype(BF16)
        kd = (kh * jnp.exp(bmid - bcum)).astype(BF16)
        s = lax.dot_general(a, kd, nt, preferred_element_type=F32)
        s = jnp.where(tri, s, 0.0)
        intra = jnp.dot(s.astype(BF16), vb, preferred_element_type=F32)
        st = st_ref[h]
        inter = lax.dot_general((qh * jnp.exp(bcum)).astype(BF16), st.astype(BF16), nt,
                                preferred_element_type=F32)
        o_ref[:, sl] = (inter + intra).astype(BF16)
        kd2 = (kh * jnp.exp(bend - bcum)).astype(BF16)
        st_ref[h] = st * jnp.exp(bend) + jnp.dot(vh.T.astype(BF16), kd2, preferred_element_type=F32)


def _hgrn_scan(p, lb_param, layer, geo):
    c = HG_CHUNK
    steps = (geo.Lc + geo.N) // c

    def spec(colblk, reverse):
        return pl.BlockSpec((c, HG_F), lambda b, i: (_chunk_index(b, i, c, geo, reverse), colblk))

    out = jax.ShapeDtypeStruct((geo.T, HG_V), BF16)
    state = pltpu.VMEM((HG_HEADS, HG_DV, HG_DK), F32)
    return pl.pallas_call(
        functools.partial(_hgrn_kernel, layer=layer),
        name="hgrn_scan",
        out_shape=[out, out],
        grid=(geo.B, steps),
        in_specs=[spec(0, False), spec(1, False), spec(3, False),
                  spec(0, True), spec(2, True), spec(3, True),
                  pl.BlockSpec(lb_param.shape, lambda b, i: (0, 0))],
        out_specs=[spec(0, False), spec(0, True)],
        scratch_shapes=[state, state],
        compiler_params=_params("arbitrary", "arbitrary"),
    )(p, p, p, p, p, p, lb_param)


def _ev_readout_kernel(of_ref, ob_ref, gate_ref, a_ref, b_ref, ap_ref, bp_ref, an_ref, bn_ref,
                       gain_ref, cw_ref, cb_ref, lng_ref, lnb_ref, o_ref, ext_ref, conv_ref, shift_ref,
                       *, tm, geo):
    i = pl.program_id(0)
    n_lat = geo.BN // tm
    tpl = geo.N // tm
    tpc = geo.Lc // tm
    j = jnp.where(i < n_lat, i % tpl, (i - n_lat) % tpc)
    per = jnp.where(i < n_lat, tpl, tpc)
    keep_prev = jnp.where(j == 0, 0.0, 1.0)
    keep_next = jnp.where(j == per - 1, 0.0, 1.0)

    o = of_ref[...].astype(F32) + ob_ref[...].astype(F32)
    r = o * lax.rsqrt(jnp.mean(o * o, axis=-1, keepdims=True) + EPS) * gain_ref[...]
    o_ref[:, :HG_V] = (r * _silu(gate_ref[...].astype(F32))).astype(BF16)

    def glu(x_ref, y_ref):
        return x_ref[...].astype(F32) * _sigmoid(y_ref[...].astype(F32))

    ext_ref[0:CONV_HALO] = glu(ap_ref, bp_ref) * keep_prev
    ext_ref[CONV_HALO:CONV_HALO + tm] = glu(a_ref, b_ref)
    ext_ref[CONV_HALO + tm:2 * CONV_HALO + tm] = glu(an_ref, bn_ref) * keep_next
    off = CONV_HALO - CONV_W // 2
    span = tm + 8 * ((off + CONV_W - 1) // 8)
    for s in range(1, 8):
        shift_ref[s - 1, 0:span] = ext_ref[s:s + span]
    for cj in range(CONV_C // 128):
        cs = slice(cj * 128, (cj + 1) * 128)
        acc = jnp.broadcast_to(cb_ref[:, cs], (tm, 128))
        for k in range(CONV_W):
            a, s = divmod(off + k, 8)
            src = ext_ref if s == 0 else shift_ref.at[s - 1]
            acc = acc + cw_ref[k:k + 1, cs] * src[8 * a:8 * a + tm, cs]
        conv_ref[:, cs] = acc
    acc = conv_ref[...]
    mu = jnp.mean(acc, axis=-1, keepdims=True)
    xc = acc - mu
    var = jnp.mean(xc * xc, axis=-1, keepdims=True)
    u = xc * lax.rsqrt(var + EPS) * lng_ref[...] + lnb_ref[...]
    o_ref[:, HG_V:] = _silu(u).astype(BF16)


def _ev_readout(p, of, ob, gain, cw, cb, lng, lnb, geo):
    tm = _pick_tile(geo, (128,))
    hb = tm // CONV_HALO
    nhalo = geo.T // CONV_HALO

    def row(colblk):
        return pl.BlockSpec((tm, HG_V), lambda i: (i, colblk))

    def prev(colblk):
        return pl.BlockSpec((CONV_HALO, CONV_C), lambda i: (jnp.maximum(i * hb - 1, 0), colblk))

    def nxt(colblk):
        return pl.BlockSpec((CONV_HALO, CONV_C), lambda i: (jnp.minimum((i + 1) * hb, nhalo - 1), colblk))

    def vec(n):
        return pl.BlockSpec((n, CONV_C), lambda i: (0, 0))

    return pl.pallas_call(
        functools.partial(_ev_readout_kernel, tm=tm, geo=geo),
        name="ev_readout",
        out_shape=jax.ShapeDtypeStruct((geo.T, HG_V + CONV_C), BF16),
        grid=(geo.T // tm,),
        in_specs=[row(0), row(0), row(4), row(5), row(6), prev(5), prev(6), nxt(5), nxt(6),
                  vec(1), vec(CONV_W), vec(1), vec(1), vec(1)],
        out_specs=pl.BlockSpec((tm, HG_V + CONV_C), lambda i: (i, 0)),
        scratch_shapes=[pltpu.VMEM((tm + 2 * CONV_HALO, CONV_C), F32), pltpu.VMEM((tm, CONV_C), F32),
                        pltpu.VMEM((7, tm + 2 * CONV_HALO, CONV_C), F32)],
        compiler_params=_params("arbitrary"),
    )(of, ob, p, p, p, p, p, p, p, gain.reshape(1, -1), cw, cb.reshape(1, -1),
      lng.reshape(1, -1), lnb.reshape(1, -1))


def _rope(x, cos, sin_signed):
    half = x.shape[-1] // 2
    rot = jnp.concatenate([pltpu.roll(x[:, :half], half // 2, axis=1),
                           pltpu.roll(x[:, half:], half // 2, axis=1)], axis=-1)
    return x * cos + rot * sin_signed


def _ret_kernel(qf_ref, kf_ref, vf_ref, cosf_ref, sinf_ref, qb_ref, kb_ref, vb_ref, cosb_ref, sinb_ref,
                dl_ref, of_ref, ob_ref, sf_ref, sb_ref):
    i = pl.program_id(1)

    @pl.when(i == 0)
    def _():
        sf_ref[...] = jnp.zeros_like(sf_ref)
        sb_ref[...] = jnp.zeros_like(sb_ref)

    _ret_chunk(qf_ref, kf_ref, vf_ref, cosf_ref, sinf_ref, dl_ref[0], of_ref, sf_ref, reverse=False)
    _ret_chunk(qb_ref, kb_ref, vb_ref, cosb_ref, sinb_ref, dl_ref[1], ob_ref, sb_ref, reverse=True)


def _ret_chunk(q_ref, k_ref, v_ref, cos_ref, sin_ref, dl, o_ref, s_ref, *, reverse):
    c = RET_CHUNK
    lg_all = -jnp.log1p(jnp.exp(-dl))
    row = lax.broadcasted_iota(jnp.int32, (c, c), 0)
    col = lax.broadcasted_iota(jnp.int32, (c, c), 1)
    idx = lax.broadcasted_iota(jnp.int32, (c, 1), 0).astype(F32)
    if reverse:
        live = col >= row
        dist = (col - row).astype(F32)
        q_pow = c - idx
        k_pow = idx
    else:
        live = row >= col
        dist = (row - col).astype(F32)
        q_pow = idx + 1.0
        k_pow = c - 1.0 - idx
    cos = cos_ref[...]
    sin = sin_ref[...]
    nt = (((1,), (1,)), ((), ()))

    for h in range(RET_HEADS):
        lg = lg_all[h:h + 1]
        dmask = jnp.where(live, jnp.exp(lg * dist), 0.0)
        q = _rope(q_ref[:, h * RET_DK:(h + 1) * RET_DK].astype(F32), cos, sin)
        k = _rope(k_ref[:, h * RET_DK:(h + 1) * RET_DK].astype(F32) * (RET_DK ** -0.5), cos, sin)
        vb = v_ref[:, h * RET_DV:(h + 1) * RET_DV]
        scores = lax.dot_general(q.astype(BF16), k.astype(BF16), nt, preferred_element_type=F32) * dmask
        intra = jnp.dot(scores.astype(BF16), vb, preferred_element_type=F32)
        s = s_ref[h]
        inter = jnp.dot((q * jnp.exp(lg * q_pow)).astype(BF16), s.astype(BF16),
                        preferred_element_type=F32)
        o_ref[:, h * RET_DV:(h + 1) * RET_DV] = (inter + intra).astype(BF16)
        kdec = (k * jnp.exp(lg * k_pow)).T.astype(BF16)
        s_ref[h] = jnp.exp(lg * c) * s + jnp.dot(kdec, vb, preferred_element_type=F32)


def _ret_scan(p, cos_tab, sin_tab, decay_logit, geo):
    c = RET_CHUNK
    steps = (geo.Lc + geo.N) // c
    nc = geo.Lc // c
    nl = geo.N // c

    def spec(width, colblk, reverse):
        return pl.BlockSpec((c, width), lambda b, i: (_chunk_index(b, i, c, geo, reverse), colblk))

    def tab(reverse):
        def index(b, i):
            lat = (nl - 1 - (i - nc)) if reverse else (i - nc)
            return (jnp.where(i < nc, nl, lat), 0)
        return pl.BlockSpec((c, RET_DK), index)

    def direction(reverse):
        return [spec(RET_QK, 0, reverse), spec(RET_QK, 1, reverse), spec(RET_V, 1, reverse),
                tab(reverse), tab(reverse)]

    out = jax.ShapeDtypeStruct((geo.T, RET_V), BF16)
    state = pltpu.VMEM((RET_HEADS, RET_DK, RET_DV), F32)
    return pl.pallas_call(
        _ret_kernel,
        name="ret_scan",
        out_shape=[out, out],
        grid=(geo.B, steps),
        in_specs=direction(False) + direction(True)
        + [pl.BlockSpec((2, RET_HEADS, 1), lambda b, i: (0, 0, 0))],
        out_specs=[spec(RET_V, 0, False), spec(RET_V, 0, True)],
        scratch_shapes=[state, state],
        compiler_params=_params("arbitrary", "arbitrary"),
    )(p, p, p, cos_tab, sin_tab, p, p, p, cos_tab, sin_tab, decay_logit.reshape(2, RET_HEADS, 1))


def _rope_tables(n):
    t = jnp.arange(n)
    quarter = RET_DK // 4
    inv = 1.0 / (ROPE_BASE ** (jnp.arange(quarter, dtype=F32) / quarter))
    ang_r = (t // GRID_W).astype(F32)[:, None] * inv
    ang_c = (t % GRID_W).astype(F32)[:, None] * inv
    cos = jnp.concatenate([jnp.cos(ang_r), jnp.cos(ang_r), jnp.cos(ang_c), jnp.cos(ang_c)], axis=-1)
    sin = jnp.concatenate([-jnp.sin(ang_r), jnp.sin(ang_r), -jnp.sin(ang_c), jnp.sin(ang_c)], axis=-1)
    cos = jnp.concatenate([cos, jnp.ones((RET_CHUNK, RET_DK), F32)], axis=0)
    sin = jnp.concatenate([sin, jnp.zeros((RET_CHUNK, RET_DK), F32)], axis=0)
    return cos, sin


def _ret_readout_kernel(of_ref, ob_ref, gate_ref, o_ref):
    for h in range(RET_HEADS):
        sl = slice(h * RET_DV, (h + 1) * RET_DV)
        o = of_ref[:, sl].astype(F32) + ob_ref[:, sl].astype(F32)
        r = o * lax.rsqrt(jnp.mean(o * o, axis=-1, keepdims=True) + EPS)
        o_ref[:, sl] = (_silu(gate_ref[:, sl].astype(F32)) * r).astype(BF16)


def _ret_readout(p, of, ob, geo):
    tm = _pick_tile(geo, (256, 128))
    spec = pl.BlockSpec((tm, RET_V), lambda i: (i, 0))
    return pl.pallas_call(
        _ret_readout_kernel,
        name="ret_readout",
        out_shape=jax.ShapeDtypeStruct((geo.T, RET_V), BF16),
        grid=(geo.T // tm,),
        in_specs=[spec, spec, pl.BlockSpec((tm, RET_V), lambda i: (i, 2))],
        out_specs=spec,
        compiler_params=_params("arbitrary"),
    )(of, ob, p)


def _router_kernel(h_ref, rw_ref, rb_ref, eidx_ref, rank_ref, w_ref, cnt_ref, carry_ref, *, tm):
    i = pl.program_id(0)

    @pl.when(i == 0)
    def _():
        carry_ref[...] = jnp.zeros_like(carry_ref)

    half = D_MODEL // 2
    h_lo, h_hi = _unpack_rows(h_ref[...])
    h_lo = h_lo.astype(BF16)
    h_hi = h_hi.astype(BF16)
    rest = rw_ref[...]
    logits = jnp.zeros((tm, N_EXPERTS), F32)
    for _ in range(3):
        part = rest.astype(BF16)
        rest = rest - part.astype(F32)
        logits = (logits + jnp.dot(h_lo, part[:half], preferred_element_type=F32)
                  + jnp.dot(h_hi, part[half:], preferred_element_type=F32))
    s = _sigmoid(logits)
    sel = s + rb_ref[...]
    lane = lax.broadcasted_iota(jnp.int32, (tm, N_EXPERTS), 1).astype(F32)
    grp = jnp.floor(lane * (1.0 / GROUP_SIZE))
    ninf = -jnp.inf
    none = float(N_EXPERTS)

    gscore = jnp.zeros((tm, N_EXPERTS), F32)
    gcols = []
    for g in range(N_GROUPS):
        in_g = grp == float(g)
        v1 = jnp.max(jnp.where(in_g, sel, ninf), axis=-1, keepdims=True)
        i1 = jnp.min(jnp.where(in_g & (sel == v1), lane, none), axis=-1, keepdims=True)
        v2 = jnp.max(jnp.where(in_g & (lane != i1), sel, ninf), axis=-1, keepdims=True)
        gcols.append(v1 + v2)
        gscore = jnp.where(in_g, v1 + v2, gscore)
    beaten = jnp.zeros((tm, N_EXPERTS), F32)
    for g in range(N_GROUPS):
        wins = (gcols[g] > gscore) | ((gcols[g] == gscore) & (float(g) < grp))
        beaten = beaten + jnp.where(wins, 1.0, 0.0)
    cand = jnp.where(beaten < float(TOPK_GROUPS), sel, ninf)

    lane_k = lax.broadcasted_iota(jnp.int32, (tm, TOP_K), 1)
    eidx = jnp.zeros((tm, TOP_K), F32)
    wsel = jnp.zeros((tm, TOP_K), F32)
    chosen = jnp.zeros((tm, N_EXPERTS), F32)
    picks = []
    for k in range(TOP_K):
        v = jnp.max(cand, axis=-1, keepdims=True)
        ik = jnp.min(jnp.where(cand == v, lane, none), axis=-1, keepdims=True)
        hit = lane == ik
        picks.append(ik)
        eidx = jnp.where(lane_k == k, ik, eidx)
        wsel = jnp.where(lane_k == k, jnp.sum(jnp.where(hit, s, 0.0), axis=-1, keepdims=True), wsel)
        chosen = jnp.where(hit, 1.0, chosen)
        cand = jnp.where(hit, ninf, cand)
    w_ref[...] = wsel / jnp.sum(wsel, axis=-1, keepdims=True) * ROUTED_SCALE
    eidx_ref[...] = eidx.astype(jnp.int32)

    r = lax.broadcasted_iota(jnp.int32, (tm, tm), 0)
    c = lax.broadcasted_iota(jnp.int32, (tm, tm), 1)
    below = jnp.where(c < r, 1.0, 0.0).astype(BF16)
    carry = carry_ref[...]
    pos = jnp.dot(below, chosen.astype(BF16), preferred_element_type=F32) + carry
    rank = jnp.zeros((tm, TOP_K), jnp.int32)
    for k in range(TOP_K):
        rk = jnp.sum(jnp.where(lane == picks[k], pos, 0.0), axis=-1, keepdims=True)
        rank = jnp.where(lane_k == k, rk.astype(jnp.int32), rank)
    rank_ref[...] = rank
    carry = carry + jnp.sum(chosen, axis=0, keepdims=True)
    carry_ref[...] = carry
    cnt_ref[...] = carry


def _router(h_packed, rw, rb, geo):
    tm = 256 if geo.T % 256 == 0 else 128
    tok = pl.BlockSpec((tm, TOP_K), lambda i: (i, 0))
    one = pl.BlockSpec((1, N_EXPERTS), lambda i: (0, 0))
    return pl.pallas_call(
        functools.partial(_router_kernel, tm=tm),
        name="router",
        out_shape=[jax.ShapeDtypeStruct((geo.T, TOP_K), jnp.int32),
                   jax.ShapeDtypeStruct((geo.T, TOP_K), jnp.int32),
                   jax.ShapeDtypeStruct((geo.T, TOP_K), F32),
                   jax.ShapeDtypeStruct((1, N_EXPERTS), F32)],
        grid=(geo.T // tm,),
        in_specs=[pl.BlockSpec((tm, D_MODEL // 2), lambda i: (i, 0)),
                  pl.BlockSpec((D_MODEL, N_EXPERTS), lambda i: (0, 0)), one],
        out_specs=[tok, tok, tok, one],
        scratch_shapes=[pltpu.VMEM((1, N_EXPERTS), F32)],
        compiler_params=_params("arbitrary"),
    )(h_packed, rw, rb.reshape(1, N_EXPERTS))


def _n_blocks(geo):
    return -(-(geo.T * TOP_K) // MOE_BLOCK) + N_EXPERTS


def _dest_kernel(cnt_ref, eidx_ref, rank_ref, dest_ref, blk_ref, *, n_blocks):
    eidx = eidx_ref[...]
    dest = rank_ref[...]
    blk_row = (lax.broadcasted_iota(jnp.int32, blk_ref.shape, 0) * 128
               + lax.broadcasted_iota(jnp.int32, blk_ref.shape, 1)) * MOE_BLOCK
    blk = jnp.zeros(blk_ref.shape, jnp.int32)
    start = jnp.int32(0)
    for e in range(N_EXPERTS):
        padded = (cnt_ref[e] + (MOE_BLOCK - 1)) // MOE_BLOCK * MOE_BLOCK
        dest = dest + jnp.where(eidx == e, start, 0)
        start = start + padded
        blk = blk + jnp.where(start <= blk_row, 1, 0)
    dest_ref[...] = dest
    blk_ref[...] = jnp.where(blk_row == n_blocks * MOE_BLOCK, start // MOE_BLOCK,
                             jnp.minimum(blk, N_EXPERTS - 1))


def _dest(counts, eidx, rank, geo):
    rows = geo.T * TOP_K // 128
    brow = -(-(_n_blocks(geo) + 1) // 128)
    full = pl.BlockSpec((rows, 128), lambda: (0, 0))
    dest, blk = pl.pallas_call(
        functools.partial(_dest_kernel, n_blocks=_n_blocks(geo)),
        name="dest",
        out_shape=[jax.ShapeDtypeStruct((rows, 128), jnp.int32),
                   jax.ShapeDtypeStruct((brow, 128), jnp.int32)],
        in_specs=[pl.BlockSpec(memory_space=pltpu.SMEM), full, full],
        out_specs=[full, pl.BlockSpec((brow, 128), lambda: (0, 0))],
    )(counts, eidx.reshape(rows, 128), rank.reshape(rows, 128))
    return dest.reshape(-1), blk.reshape(-1)[:_n_blocks(geo) + 1]


def _zero_fill(cnt_ref, xs_hbm, zero_ref, zsem, n_rows, wait):
    def piece(pos, size):
        if size >= 8:
            copies = [(pl.multiple_of(pos, 8), size)]
        else:
            copies = [(pos + r, 1) for r in range(size)]
        for p, s in copies:
            cp = pltpu.make_async_copy(zero_ref.at[pl.ds(0, s)], xs_hbm.at[pl.ds(p, s)], zsem)
            cp.wait() if wait else cp.start()

    def per_expert(e, start):
        cnt = cnt_ref[e]
        padded = (cnt + (MOE_BLOCK - 1)) // MOE_BLOCK * MOE_BLOCK
        pad = padded - cnt
        pos = start + cnt
        size = 1
        while size < MOE_BLOCK:
            take = (pad & size) != 0
            pl.when(take)(functools.partial(piece, pos, size))
            pos = pos + jnp.where(take, size, 0)
            size *= 2
        return start + padded

    end = lax.fori_loop(0, N_EXPERTS, per_expert, jnp.int32(0))

    def per_block(j, carry):
        piece(end + j * MOE_BLOCK, MOE_BLOCK)
        return carry

    lax.fori_loop(0, (n_rows - end) // MOE_BLOCK, per_block, 0)


def _sc_scatter_rows(src, order, n_rows):
    nw = V7X_SC_CORES * V7X_SC_SUBCORES
    t, w = src.shape
    per = t // nw
    n = per // SC_ROWS
    assert t % (nw * SC_ROWS * 2) == 0
    mesh = plsc.VectorSubcoreMesh(core_axis_name="c", subcore_axis_name="s")

    @functools.partial(
        pl.kernel, mesh=mesh,
        out_type=jax.ShapeDtypeStruct((n_rows, w), src.dtype),
        scratch_types=[pltpu.VMEM((SC_ROWS, w), src.dtype)] * 2
        + [pltpu.VMEM((SC_ROWS,), jnp.int32)] * (2 * TOP_K) + [pltpu.SemaphoreType.DMA] * 2,
    )
    def scatter(src_hbm, idx_hbm, out_hbm, *scratch):
        rows_vs = scratch[:2]
        idx_vs = (scratch[2:2 + TOP_K], scratch[2 + TOP_K:2 + 2 * TOP_K])
        sems = scratch[2 + 2 * TOP_K:]
        wid = lax.axis_index("s") * V7X_SC_CORES + lax.axis_index("c")
        base = wid * per

        def load(chunk, b):
            off = pl.multiple_of(base + chunk * SC_ROWS, 8)
            pltpu.sync_copy(src_hbm.at[pl.ds(off, SC_ROWS)], rows_vs[b])
            for k in range(TOP_K):
                pltpu.sync_copy(idx_hbm.at[pl.ds(pl.multiple_of(k * t + off, 8), SC_ROWS)], idx_vs[b][k])

        def copies(b):
            return [pltpu.make_async_copy(rows_vs[b], out_hbm.at[idx_vs[b][k]], sems[b])
                    for k in range(TOP_K)]

        load(0, 0)

        @pl.loop(0, n, step=2)
        def _(j):
            for b in range(2):
                cur = j + b
                for cp in copies(b):
                    cp.start()
                pl.when(cur + 1 < n)(functools.partial(load, cur + 1, 1 - b))
                for cp in copies(b):
                    cp.wait()

    return scatter(src, order)


def _zero_pad_kernel(cnt_ref, xs_in, xs_hbm, zero_ref, zsem, *, n_rows):
    del xs_in
    zero_ref[...] = jnp.zeros_like(zero_ref)
    _zero_fill(cnt_ref, xs_hbm, zero_ref, zsem, n_rows, wait=False)
    _zero_fill(cnt_ref, xs_hbm, zero_ref, zsem, n_rows, wait=True)


def _dispatch(counts, order, h_packed, geo):
    n_rows = _n_blocks(geo) * MOE_BLOCK
    width = D_MODEL // 2
    xs = _sc_scatter_rows(h_packed[:geo.T], order, n_rows)
    return pl.pallas_call(
        functools.partial(_zero_pad_kernel, n_rows=n_rows),
        name="zero_pad",
        out_shape=jax.ShapeDtypeStruct((n_rows, width), PACKED),
        in_specs=[pl.BlockSpec(memory_space=pltpu.SMEM), pl.BlockSpec(memory_space=pl.ANY)],
        out_specs=pl.BlockSpec(memory_space=pl.ANY),
        scratch_shapes=[pltpu.VMEM((MOE_BLOCK, width), PACKED), pltpu.SemaphoreType.DMA],
        input_output_aliases={1: 0},
    )(counts, xs)


def _swiglu_packed(x_ref, wg_ref, wu_ref, wd_ref):
    half = D_MODEL // 2
    lo, hi = _unpack_rows(x_ref[...])
    lo = lo.astype(BF16)
    hi = hi.astype(BF16)

    def proj(w_ref):
        return (jnp.dot(lo, w_ref[:half], preferred_element_type=F32)
                + jnp.dot(hi, w_ref[half:], preferred_element_type=F32))

    g = proj(wg_ref)
    u = proj(wu_ref)
    return jnp.dot((_silu(g) * u).astype(BF16), wd_ref[...], preferred_element_type=F32)


def _expert_kernel(blk_ref, x_ref, wg_ref, wu_ref, wd_ref, o_ref, wgb_ref, wub_ref, wdb_ref, *, nb):
    j = pl.program_id(0)
    n_used = blk_ref[nb]

    @pl.when((j == 0) | (blk_ref[j] != blk_ref[jnp.maximum(j - 1, 0)]))
    def _():
        wgb_ref[...] = wg_ref[...].astype(BF16)
        wub_ref[...] = wu_ref[...].astype(BF16)
        wdb_ref[...] = wd_ref[...].astype(BF16)

    @pl.when(j < n_used)
    def _():
        o_ref[...] = _pack_rows(_swiglu_packed(x_ref, wgb_ref, wub_ref, wdb_ref))

    @pl.when(j >= n_used)
    def _():
        o_ref[...] = jnp.zeros_like(o_ref)


def _experts(blk_e, xs, wg, wu, wd, layer, geo):
    nb = _n_blocks(geo)
    rows = pl.BlockSpec((MOE_BLOCK, D_MODEL // 2), lambda j, be: (j, 0))

    def wspec(r, c):
        return pl.BlockSpec((None, None, r, c), lambda j, be: (layer, be[j], 0, 0))

    return pl.pallas_call(
        functools.partial(_expert_kernel, nb=nb),
        name="experts",
        out_shape=jax.ShapeDtypeStruct((nb * MOE_BLOCK, D_MODEL // 2), PACKED),
        grid_spec=pltpu.PrefetchScalarGridSpec(
            num_scalar_prefetch=1,
            grid=(nb,),
            in_specs=[rows, wspec(D_MODEL, EXPERT_FF), wspec(D_MODEL, EXPERT_FF),
                      wspec(EXPERT_FF, D_MODEL)],
            out_specs=rows,
            scratch_shapes=[pltpu.VMEM((D_MODEL, EXPERT_FF), BF16),
                            pltpu.VMEM((D_MODEL, EXPERT_FF), BF16),
                            pltpu.VMEM((EXPERT_FF, D_MODEL), BF16)]),
        compiler_params=_params("arbitrary"),
    )(blk_e, xs, wg, wu, wd)


def _sc_gather_rows(table, idx):
    nw = V7X_SC_CORES * V7X_SC_SUBCORES
    m = idx.shape[0]
    w = table.shape[1]
    per = m // nw
    n = per // SC_ROWS
    assert m % (nw * SC_ROWS * 2) == 0
    mesh = plsc.VectorSubcoreMesh(core_axis_name="c", subcore_axis_name="s")

    @functools.partial(
        pl.kernel, mesh=mesh,
        out_type=jax.ShapeDtypeStruct((m, w), table.dtype),
        scratch_types=[pltpu.VMEM((SC_ROWS,), jnp.int32)] * 2
        + [pltpu.VMEM((SC_ROWS, w), table.dtype)] * 2 + [pltpu.SemaphoreType.DMA] * 2,
    )
    def gather(table_hbm, idx_hbm, out_hbm, *scratch):
        idx_vs, rows_vs, sems = scratch[:2], scratch[2:4], scratch[4:]
        wid = lax.axis_index("s") * V7X_SC_CORES + lax.axis_index("c")
        base = wid * per

        def fetch(b):
            return pltpu.make_async_copy(table_hbm.at[idx_vs[b]], rows_vs[b], sems[b])

        def start(chunk, b):
            off = pl.multiple_of(base + chunk * SC_ROWS, 8)
            pltpu.sync_copy(idx_hbm.at[pl.ds(off, SC_ROWS)], idx_vs[b])
            fetch(b).start()

        start(0, 0)

        @pl.loop(0, n, step=2)
        def _(j):
            for b in range(2):
                cur = j + b
                pl.when(cur + 1 < n)(functools.partial(start, cur + 1, 1 - b))
                fetch(b).wait()
                off = pl.multiple_of(base + cur * SC_ROWS, 8)
                pltpu.sync_copy(rows_vs[b], out_hbm.at[pl.ds(off, SC_ROWS)])

    return gather(table, idx)


def _combine_kernel(w_ref, x_ref, h_ref, sg_ref, su_ref, sd_ref, m_ref, *rest, post, n_prev):
    rest = rest[:-2 * n_prev] + rest[-n_prev:] if n_prev else rest
    if post == "next":
        ng_ref, nsh_ref, nsc_ref, g_ref, o_ref, hn_ref = rest
    else:
        ng_ref, g_ref, o_ref = rest
    half = D_MODEL // 2
    shared = _swiglu_packed(h_ref, sg_ref, su_ref, sd_ref)
    w = w_ref[...]
    acc_lo = shared[:, :half]
    acc_hi = shared[:, half:]
    for k in range(TOP_K):
        lo, hi = _unpack_rows(g_ref[k])
        acc_lo = acc_lo + w[:, k:k + 1] * lo
        acc_hi = acc_hi + w[:, k:k + 1] * hi
    gate = m_ref[0]
    y_lo = x_ref[:, :half] + gate[:, :half] * acc_lo
    y_hi = x_ref[:, half:] + gate[:, half:] * acc_hi
    ms = (jnp.sum(y_lo * y_lo, axis=-1, keepdims=True)
          + jnp.sum(y_hi * y_hi, axis=-1, keepdims=True)) * (1.0 / D_MODEL)
    inv = lax.rsqrt(ms + EPS)
    ng = ng_ref[...]
    n_lo = y_lo * inv * ng[:, :half]
    n_hi = y_hi * inv * ng[:, half:]
    if post == "next":
        o_ref[:, :half] = y_lo
        o_ref[:, half:] = y_hi
        sc = nsc_ref[0]
        sft = nsh_ref[0]
        hn_ref[:, :half] = (n_lo * (1.0 + sc[:, :half]) + sft[:, :half]).astype(BF16)
        hn_ref[:, half:] = (n_hi * (1.0 + sc[:, half:]) + sft[:, half:]).astype(BF16)
    else:
        o_ref[:, :half] = n_lo
        o_ref[:, half:] = n_hi


def _combine(dest, w, x, h_packed, sg, su, sd, ys, mod3, m_gate, post, norm_g, next_mod3, geo):
    half = D_MODEL // 2
    tt = _pick_tile(geo, (256, 128))
    ff = sg.shape[1]
    unit = math.lcm(tt, V7X_SC_CORES * V7X_SC_SUBCORES * SC_ROWS * 2 // TOP_K)
    cut = (geo.T // 2) // unit * unit
    parts = [(0, cut), (cut, geo.T)] if 0 < cut < geo.T else [(0, geo.T)]
    dest2 = dest.reshape(geo.T, TOP_K)
    outs = []
    for t0, t1 in parts:
        tile0, n = t0 // tt, (t1 - t0) // tt
        order = dest2[t0:t1].T.reshape(-1)
        gathered = _sc_gather_rows(ys, order).reshape(TOP_K, t1 - t0, half)
        rows = pl.BlockSpec((tt, D_MODEL), lambda i: (i + tile0, 0))
        vec = pl.BlockSpec((1, D_MODEL), lambda i: (0, 0))

        def mod(m):
            return pl.BlockSpec((1, 1, D_MODEL),
                                lambda i: (_mod_row(i + tile0, tt, geo) * N_MOD + m, 0, 0))

        in_specs = [pl.BlockSpec((tt, TOP_K), lambda i: (i + tile0, 0)),
                    rows,
                    pl.BlockSpec((tt, half), lambda i: (i + tile0, 0)),
                    pl.BlockSpec((D_MODEL, ff), lambda i: (0, 0)),
                    pl.BlockSpec((D_MODEL, ff), lambda i: (0, 0)),
                    pl.BlockSpec((ff, D_MODEL), lambda i: (0, 0)),
                    mod(m_gate), vec]
        args = [w, x, h_packed, sg, su, sd, mod3, norm_g.reshape(1, D_MODEL)]
        out_shape = [jax.ShapeDtypeStruct((geo.T, D_MODEL), F32)]
        out_specs = [rows]
        if post == "next":
            in_specs += [mod(0), mod(1)]
            args += [next_mod3, next_mod3]
            out_shape.append(jax.ShapeDtypeStruct((geo.T, D_MODEL), BF16))
            out_specs.append(rows)
        in_specs.append(pl.BlockSpec((TOP_K, tt, half), lambda i: (0, i, 0)))
        args.append(gathered)
        aliases = {len(args) + j: j for j in range(len(outs))}
        in_specs += [pl.BlockSpec(memory_space=pl.ANY)] * len(outs)
        args += outs
        outs = pl.pallas_call(
            functools.partial(_combine_kernel, post=post, n_prev=len(outs)),
            name="combine",
            out_shape=out_shape,
            grid=(n,),
            in_specs=in_specs,
            out_specs=out_specs,
            input_output_aliases=aliases,
            compiler_params=_params("arbitrary"),
        )(*args)
    return outs


def _moe(x, h_packed, mod3, rw, rb, wg, wu, wd, layer, sg, su, sd, post, norm_g, next_mod3, geo):
    eidx, rank, w, counts = _router(h_packed, rw, rb, geo)
    counts = counts.reshape(N_EXPERTS).astype(jnp.int32)
    dest, blk_e = _dest(counts, eidx, rank, geo)
    order = dest.reshape(geo.T, TOP_K).T.reshape(-1)
    xs = _dispatch(counts, order, h_packed, geo)
    ys = _experts(blk_e, xs, wg, wu, wd, layer, geo)
    return _combine(dest, w, x, h_packed, sg.astype(BF16), su.astype(BF16), sd.astype(BF16), ys,
                    mod3, 5, post, norm_g, next_mod3, geo)


def kernel(x, c, ctx, c_ctx, ada_w, ada_b, norm_mix, norm_ffn, norm_final, ev_w_in, ev_w_out, hgrn_lb, hgrn_norm, conv_w, conv_b, conv_norm_g, conv_norm_b, ret_w_in, ret_w_out, ret_decay, router_w, router_b, exp_gate, exp_up, exp_down, sh_gate, sh_up, sh_down):
    b, n, d = x.shape
    lc = ctx.shape[1]
    depth = ada_w.shape[0]
    geo = _geo(b, n, lc)
    assert d == D_MODEL and b < MOD_ROWS
    assert n % RET_CHUNK == 0 and lc % RET_CHUNK == 0

    xs = jnp.concatenate([x.reshape(geo.BN, d), ctx.reshape(geo.BL, d)], axis=0)
    cond = jnp.zeros((MOD_ROWS, d), F32).at[:b].set(c).at[b].set(c_ctx)
    cos_tab, sin_tab = _rope_tables(n)

    mods = [_adaln(cond, ada_w, ada_b, l).reshape(MOD_ROWS * N_MOD, 1, d) for l in range(depth)]
    h = _normmod(xs, norm_mix[0], mods[0], 0, 1, geo, packed=False)
    for l in range(depth):
        j = l // 2
        last = l == depth - 1
        mod3 = mods[l]
        tail = geo._replace(T=geo.BN, BL=0, Lc=0) if last else geo
        if l % 2 == 0:
            p = _matmul(h, ev_w_in[j].astype(BF16), geo)
            of, ob = _hgrn_scan(p, hgrn_lb, l, geo)
            mix = _ev_readout(p, of, ob, hgrn_norm[j], conv_w[j], conv_b[j],
                              conv_norm_g[j], conv_norm_b[j], geo)
            xs = _matmul_resid(mix, ev_w_out[j].astype(BF16), xs, mod3, 2, tail)
        else:
            p = _matmul(h, ret_w_in[j].astype(BF16), geo)
            of, ob = _ret_scan(p, cos_tab, sin_tab, ret_decay[j], geo)
            mix = _ret_readout(p, of, ob, geo)
            xs = _matmul_resid(mix, ret_w_out[j].astype(BF16), xs, mod3, 2, tail)
        h_packed = _normmod(xs, norm_ffn[l], mod3, 3, 4, tail, packed=True)
        moe_w = (router_w[l], router_b[l], exp_gate, exp_up, exp_down, l, sh_gate[l], sh_up[l], sh_down[l])
        if last:
            (out,) = _moe(xs, h_packed, mod3, *moe_w, "final", norm_final, None, tail)
        else:
            xs, h = _moe(xs, h_packed, mod3, *moe_w, "next", norm_mix[l + 1], mods[l + 1], tail)
    return out.reshape(b, n, d)
```

```python
import collections
import functools
import math

import jax
import jax.numpy as jnp
from jax import lax
from jax.experimental import pallas as pl
from jax.experimental.pallas import tpu as pltpu
from jax.experimental.pallas import tpu_sc as plsc

F32 = jnp.float32
BF16 = jnp.bfloat16

D_MODEL = 2048
N_MOD = 6
EPS = 1e-6
GRID_W = 64
ROPE_BASE = 10000.0

HG_HEADS = 8
HG_DK = 128
HG_DV = 128
HG_F = HG_HEADS * HG_DK
HG_V = HG_HEADS * HG_DV
CONV_C = D_MODEL // 2
CONV_W = 31
CONV_HALO = 16
HG_CHUNK = 128

RET_HEADS = 8
RET_DK = D_MODEL // RET_HEADS
RET_DV = 2 * RET_DK
RET_QK = RET_HEADS * RET_DK
RET_V = RET_HEADS * RET_DV
RET_CHUNK = 256

N_EXPERTS = 64
EXPERT_FF = D_MODEL // 4
TOP_K = 8
N_GROUPS = 8
GROUP_SIZE = N_EXPERTS // N_GROUPS
TOPK_GROUPS = 4
ROUTED_SCALE = 2.5
MOE_BLOCK = 512

V7X_SC_CORES = 2
V7X_SC_SUBCORES = 16
SC_ROWS = 32
COMBINE_PARTS = 4
MOD_ROWS = 16
VMEM_LIMIT = 56 * 1024 * 1024

Geo = collections.namedtuple("Geo", "B N Lc BN BL T")


def _geo(b, n, lc):
    return Geo(b, n, lc, b * n, b * lc, b * n + b * lc)


def _pick_tile(geo, cands):
    for t in cands:
        if geo.N % t == 0 and geo.BL % t == 0:
            return t
    raise ValueError("no row tile fits the sequence lengths")


def _mod_row(i, tm, geo):
    return jnp.where(i < geo.BN // tm, i // (geo.N // tm), geo.B)


def _mod_spec(m, tm, geo, ngrid=1):
    if ngrid == 1:
        return pl.BlockSpec((1, 1, D_MODEL), lambda i: (_mod_row(i, tm, geo) * N_MOD + m, 0, 0))
    return pl.BlockSpec((1, 1, D_MODEL), lambda i, j: (_mod_row(i, tm, geo) * N_MOD + m, 0, j))


def _params(*sem):
    return pltpu.CompilerParams(dimension_semantics=sem, vmem_limit_bytes=VMEM_LIMIT)


def _sigmoid(x):
    return jax.nn.sigmoid(x)


def _silu(x):
    return x * jax.nn.sigmoid(x)


def _adaln_kernel(c_ref, w_ref, b_ref, o_ref):
    a = _silu(c_ref[...]).astype(BF16)
    o_ref[...] = jnp.dot(a, w_ref[...].astype(BF16), preferred_element_type=F32) + b_ref[...]


def _adaln(cond, w, b, layer):
    depth, k, n = w.shape
    tn = 1024
    return pl.pallas_call(
        _adaln_kernel,
        name="adaln",
        out_shape=jax.ShapeDtypeStruct((MOD_ROWS, n), F32),
        grid=(n // tn,),
        in_specs=[pl.BlockSpec((MOD_ROWS, k), lambda j: (0, 0)),
                  pl.BlockSpec((None, k, tn), lambda j: (layer, 0, j)),
                  pl.BlockSpec((None, 1, tn), lambda j: (layer, 0, j))],
        out_specs=pl.BlockSpec((MOD_ROWS, tn), lambda j: (0, j)),
        compiler_params=_params("arbitrary"),
    )(cond, w, b.reshape(depth, 1, n))


PACKED = jnp.int32


def _pack_rows(x):
    n = x.shape[-1] // 2
    bits = lax.bitcast_convert_type(x.astype(BF16).astype(F32), jnp.uint32)
    words = (bits[:, n:] & jnp.uint32(0xFFFF0000)) | (bits[:, :n] >> 16)
    return lax.bitcast_convert_type(words, PACKED)


def _unpack_rows(p):
    u = lax.bitcast_convert_type(p, jnp.uint32)
    lo = lax.bitcast_convert_type(u << 16, F32)
    hi = lax.bitcast_convert_type(u & jnp.uint32(0xFFFF0000), F32)
    return lo, hi


def _normmod_kernel(x_ref, g_ref, sh_ref, sc_ref, o_ref, *, packed):
    x = x_ref[...]
    y = x * lax.rsqrt(jnp.mean(x * x, axis=-1, keepdims=True) + EPS) * g_ref[...]
    h = y * (1.0 + sc_ref[0]) + sh_ref[0]
    o_ref[...] = _pack_rows(h) if packed else h.astype(BF16)


def _normmod(x, g, mod3, m_shift, m_scale, geo, packed):
    tm = _pick_tile(geo, (256, 128))
    spec = pl.BlockSpec((tm, D_MODEL), lambda i: (i, 0))
    if packed:
        out_shape = jax.ShapeDtypeStruct((geo.T, D_MODEL // 2), PACKED)
        out_spec = pl.BlockSpec((tm, D_MODEL // 2), lambda i: (i, 0))
    else:
        out_shape = jax.ShapeDtypeStruct((geo.T, D_MODEL), BF16)
        out_spec = spec
    return pl.pallas_call(
        functools.partial(_normmod_kernel, packed=packed),
        name="normmod",
        out_shape=out_shape,
        grid=(geo.T // tm,),
        in_specs=[spec, pl.BlockSpec((1, D_MODEL), lambda i: (0, 0)),
                  _mod_spec(m_shift, tm, geo), _mod_spec(m_scale, tm, geo)],
        out_specs=out_spec,
        compiler_params=_params("arbitrary"),
    )(x, g.reshape(1, D_MODEL), mod3, mod3)


def _mm_kernel(a_ref, w_ref, o_ref):
    o_ref[...] = jnp.dot(a_ref[...], w_ref[...], preferred_element_type=F32).astype(o_ref.dtype)


def _mm_resid_kernel(a_ref, w_ref, x_ref, m_ref, o_ref):
    y = jnp.dot(a_ref[...], w_ref[...], preferred_element_type=F32)
    o_ref[...] = x_ref[...] + m_ref[0] * y


def _matmul(a, w, geo):
    k, n = w.shape
    tm = _pick_tile(geo, (1024, 512, 256, 128))
    tn = 1024 if n % 1024 == 0 else 512
    return pl.pallas_call(
        _mm_kernel,
        name="matmul",
        out_shape=jax.ShapeDtypeStruct((geo.T, n), BF16),
        grid=(geo.T // tm, n // tn),
        in_specs=[pl.BlockSpec((tm, k), lambda i, j: (i, 0)),
                  pl.BlockSpec((k, tn), lambda i, j: (0, j))],
        out_specs=pl.BlockSpec((tm, tn), lambda i, j: (i, j)),
        compiler_params=_params("arbitrary", "arbitrary"),
    )(a, w)


def _matmul_resid(a, w, x, mod3, m_gate, geo):
    k, n = w.shape
    tm = _pick_tile(geo, (1024, 512, 256, 128))
    tn = 512
    return pl.pallas_call(
        _mm_resid_kernel,
        name="matmul_resid",
        out_shape=jax.ShapeDtypeStruct((geo.T, n), F32),
        grid=(geo.T // tm, n // tn),
        in_specs=[pl.BlockSpec((tm, k), lambda i, j: (i, 0)),
                  pl.BlockSpec((k, tn), lambda i, j: (0, j)),
                  pl.BlockSpec((tm, tn), lambda i, j: (i, j)),
                  pl.BlockSpec((1, 1, tn), lambda i, j: (_mod_row(i, tm, geo) * N_MOD + m_gate, 0, j))],
        out_specs=pl.BlockSpec((tm, tn), lambda i, j: (i, j)),
        compiler_params=_params("arbitrary", "arbitrary"),
    )(a, w, x, mod3)


def _chunk_index(b, i, chunk, geo, reverse):
    nc = geo.Lc // chunk
    nl = geo.N // chunk
    ctx0 = (geo.BN + b * geo.Lc) // chunk
    lat0 = (b * geo.N) // chunk
    if reverse:
        return jnp.where(i < nc, ctx0 + (nc - 1 - i), lat0 + (nl - 1 - (i - nc)))
    return jnp.where(i < nc, ctx0 + i, lat0 + (i - nc))


def _split_dot(tri_bf, x):
    hi = x.astype(BF16)
    r1 = x - hi.astype(F32)
    mid = r1.astype(BF16)
    lo = (r1 - mid.astype(F32)).astype(BF16)
    return (jnp.dot(tri_bf, hi, preferred_element_type=F32)
            + jnp.dot(tri_bf, mid, preferred_element_type=F32)
            + jnp.dot(tri_bf, lo, preferred_element_type=F32))


def _hgrn_kernel(qf_ref, ff_ref, vf_ref, qb_ref, fb_ref, vb_ref, lbp_ref, of_ref, ob_ref,
                 stf_ref, stb_ref, *, layer):
    i = pl.program_id(1)

    @pl.when(i == 0)
    def _():
        stf_ref[...] = jnp.zeros_like(stf_ref)
        stb_ref[...] = jnp.zeros_like(stb_ref)

    lbp = lbp_ref[...]
    e = jnp.exp(lbp - jnp.max(lbp, axis=0, keepdims=True))
    sm = e / jnp.sum(e, axis=0, keepdims=True)
    lb = sm[0:1]
    for r in range(1, layer + 1):
        lb = lb + sm[r:r + 1]

    _hgrn_chunk(qf_ref, ff_ref, vf_ref, of_ref, stf_ref, lb, reverse=False)
    _hgrn_chunk(qb_ref, fb_ref, vb_ref, ob_ref, stb_ref, lb, reverse=True)


def _hgrn_chunk(q_ref, f_ref, v_ref, o_ref, st_ref, lb, *, reverse):
    c = HG_CHUNK
    row = lax.broadcasted_iota(jnp.int32, (c, c), 0)
    col = lax.broadcasted_iota(jnp.int32, (c, c), 1)
    tri = (col >= row) if reverse else (col <= row)
    tri_bf = jnp.where(tri, 1.0, 0.0).astype(BF16)
    nt = (((1,), (1,)), ((), ()))

    for h in range(HG_HEADS):
        sl = slice(h * HG_DK, (h + 1) * HG_DK)
        qh = _silu(q_ref[:, sl].astype(F32))
        lbh = lb[:, sl]
        fg = lbh + (1.0 - lbh) * _sigmoid(f_ref[:, sl].astype(F32))
        kh = 1.0 - fg
        bcum = _split_dot(tri_bf, jnp.log(fg))
        bmid = bcum[c // 2:c // 2 + 1]
        bend = bcum[0:1] if reverse else bcum[c - 1:c]
        vb = v_ref[:, sl]
        vh = vb.astype(F32)
        a = (qh * jnp.exp(bcum - bmid)).astype(BF16)
        kd = (kh * jnp.exp(bmid - bcum)).astype(BF16)
        s = lax.dot_general(a, kd, nt, preferred_element_type=F32)
        s = jnp.where(tri, s, 0.0)
        intra = jnp.dot(s.astype(BF16), vb, preferred_element_type=F32)
        st = st_ref[h]
        inter = lax.dot_general((qh * jnp.exp(bcum)).astype(BF16), st.astype(BF16), nt,
                                preferred_element_type=F32)
        o_ref[:, sl] = (inter + intra).astype(BF16)
        kd2 = (kh * jnp.exp(bend - bcum)).astype(BF16)
        st_ref[h] = st * jnp.exp(bend) + jnp.dot(vh.T.astype(BF16), kd2, preferred_element_type=F32)


def _hgrn_scan(p, lb_param, layer, geo):
    c = HG_CHUNK
    steps = (geo.Lc + geo.N) // c

    def spec(colblk, reverse):
        return pl.BlockSpec((c, HG_F), lambda b, i: (_chunk_index(b, i, c, geo, reverse), colblk))

    out = jax.ShapeDtypeStruct((geo.T, HG_V), BF16)
    state = pltpu.VMEM((HG_HEADS, HG_DV, HG_DK), F32)
    return pl.pallas_call(
        functools.partial(_hgrn_kernel, layer=layer),
        name="hgrn_scan",
        out_shape=[out, out],
        grid=(geo.B, steps),
        in_specs=[spec(0, False), spec(1, False), spec(3, False),
                  spec(0, True), spec(2, True), spec(3, True),
                  pl.BlockSpec(lb_param.shape, lambda b, i: (0, 0))],
        out_specs=[spec(0, False), spec(0, True)],
        scratch_shapes=[state, state],
        compiler_params=_params("arbitrary", "arbitrary"),
    )(p, p, p, p, p, p, lb_param)


def _ev_readout_kernel(of_ref, ob_ref, gate_ref, a_ref, b_ref, ap_ref, bp_ref, an_ref, bn_ref,
                       gain_ref, cw_ref, cb_ref, lng_ref, lnb_ref, o_ref, ext_ref, conv_ref, shift_ref,
                       *, tm, geo):
    i = pl.program_id(0)
    n_lat = geo.BN // tm
    tpl = geo.N // tm
    tpc = geo.Lc // tm
    j = jnp.where(i < n_lat, i % tpl, (i - n_lat) % tpc)
    per = jnp.where(i < n_lat, tpl, tpc)
    keep_prev = jnp.where(j == 0, 0.0, 1.0)
    keep_next = jnp.where(j == per - 1, 0.0, 1.0)

    o = of_ref[...].astype(F32) + ob_ref[...].astype(F32)
    r = o * lax.rsqrt(jnp.mean(o * o, axis=-1, keepdims=True) + EPS) * gain_ref[...]
    o_ref[:, :HG_V] = (r * _silu(gate_ref[...].astype(F32))).astype(BF16)

    def glu(x_ref, y_ref):
        return x_ref[...].astype(F32) * _sigmoid(y_ref[...].astype(F32))

    ext_ref[0:CONV_HALO] = glu(ap_ref, bp_ref) * keep_prev
    ext_ref[CONV_HALO:CONV_HALO + tm] = glu(a_ref, b_ref)
    ext_ref[CONV_HALO + tm:2 * CONV_HALO + tm] = glu(an_ref, bn_ref) * keep_next
    off = CONV_HALO - CONV_W // 2
    span = tm + 8 * ((off + CONV_W - 1) // 8)
    for s in range(1, 8):
        shift_ref[s - 1, 0:span] = ext_ref[s:s + span]
    for cj in range(CONV_C // 128):
        cs = slice(cj * 128, (cj + 1) * 128)
        acc = jnp.broadcast_to(cb_ref[:, cs], (tm, 128))
        for k in range(CONV_W):
            a, s = divmod(off + k, 8)
            src = ext_ref if s == 0 else shift_ref.at[s - 1]
            acc = acc + cw_ref[k:k + 1, cs] * src[8 * a:8 * a + tm, cs]
        conv_ref[:, cs] = acc
    acc = conv_ref[...]
    mu = jnp.mean(acc, axis=-1, keepdims=True)
    xc = acc - mu
    var = jnp.mean(xc * xc, axis=-1, keepdims=True)
    u = xc * lax.rsqrt(var + EPS) * lng_ref[...] + lnb_ref[...]
    o_ref[:, HG_V:] = _silu(u).astype(BF16)


def _ev_readout(p, of, ob, gain, cw, cb, lng, lnb, geo):
    tm = _pick_tile(geo, (128,))
    hb = tm // CONV_HALO
    nhalo = geo.T // CONV_HALO

    def row(colblk):
        return pl.BlockSpec((tm, HG_V), lambda i: (i, colblk))

    def prev(colblk):
        return pl.BlockSpec((CONV_HALO, CONV_C), lambda i: (jnp.maximum(i * hb - 1, 0), colblk))

    def nxt(colblk):
        return pl.BlockSpec((CONV_HALO, CONV_C), lambda i: (jnp.minimum((i + 1) * hb, nhalo - 1), colblk))

    def vec(n):
        return pl.BlockSpec((n, CONV_C), lambda i: (0, 0))

    return pl.pallas_call(
        functools.partial(_ev_readout_kernel, tm=tm, geo=geo),
        name="ev_readout",
        out_shape=jax.ShapeDtypeStruct((geo.T, HG_V + CONV_C), BF16),
        grid=(geo.T // tm,),
        in_specs=[row(0), row(0), row(4), row(5), row(6), prev(5), prev(6), nxt(5), nxt(6),
                  vec(1), vec(CONV_W), vec(1), vec(1), vec(1)],
        out_specs=pl.BlockSpec((tm, HG_V + CONV_C), lambda i: (i, 0)),
        scratch_shapes=[pltpu.VMEM((tm + 2 * CONV_HALO, CONV_C), F32), pltpu.VMEM((tm, CONV_C), F32),
                        pltpu.VMEM((7, tm + 2 * CONV_HALO, CONV_C), F32)],
        compiler_params=_params("arbitrary"),
    )(of, ob, p, p, p, p, p, p, p, gain.reshape(1, -1), cw, cb.reshape(1, -1),
      lng.reshape(1, -1), lnb.reshape(1, -1))


def _rope(x, cos, sin_signed):
    half = x.shape[-1] // 2
    rot = jnp.concatenate([pltpu.roll(x[:, :half], half // 2, axis=1),
                           pltpu.roll(x[:, half:], half // 2, axis=1)], axis=-1)
    return x * cos + rot * sin_signed


def _ret_kernel(qf_ref, kf_ref, vf_ref, cosf_ref, sinf_ref, qb_ref, kb_ref, vb_ref, cosb_ref, sinb_ref,
                dl_ref, of_ref, ob_ref, sf_ref, sb_ref):
    i = pl.program_id(1)

    @pl.when(i == 0)
    def _():
        sf_ref[...] = jnp.zeros_like(sf_ref)
        sb_ref[...] = jnp.zeros_like(sb_ref)

    _ret_chunk(qf_ref, kf_ref, vf_ref, cosf_ref, sinf_ref, dl_ref[0], of_ref, sf_ref, reverse=False)
    _ret_chunk(qb_ref, kb_ref, vb_ref, cosb_ref, sinb_ref, dl_ref[1], ob_ref, sb_ref, reverse=True)


def _ret_chunk(q_ref, k_ref, v_ref, cos_ref, sin_ref, dl, o_ref, s_ref, *, reverse):
    c = RET_CHUNK
    lg_all = -jnp.log1p(jnp.exp(-dl))
    row = lax.broadcasted_iota(jnp.int32, (c, c), 0)
    col = lax.broadcasted_iota(jnp.int32, (c, c), 1)
    idx = lax.broadcasted_iota(jnp.int32, (c, 1), 0).astype(F32)
    if reverse:
        live = col >= row
        dist = (col - row).astype(F32)
        q_pow = c - idx
        k_pow = idx
    else:
        live = row >= col
        dist = (row - col).astype(F32)
        q_pow = idx + 1.0
        k_pow = c - 1.0 - idx
    cos = cos_ref[...]
    sin = sin_ref[...]
    nt = (((1,), (1,)), ((), ()))

    for h in range(RET_HEADS):
        lg = lg_all[h:h + 1]
        dmask = jnp.where(live, jnp.exp(lg * dist), 0.0)
        q = _rope(q_ref[:, h * RET_DK:(h + 1) * RET_DK].astype(F32), cos, sin)
        k = _rope(k_ref[:, h * RET_DK:(h + 1) * RET_DK].astype(F32) * (RET_DK ** -0.5), cos, sin)
        vb = v_ref[:, h * RET_DV:(h + 1) * RET_DV]
        scores = lax.dot_general(q.astype(BF16), k.astype(BF16), nt, preferred_element_type=F32) * dmask
        intra = jnp.dot(scores.astype(BF16), vb, preferred_element_type=F32)
        s = s_ref[h]
        inter = jnp.dot((q * jnp.exp(lg * q_pow)).astype(BF16), s.astype(BF16),
                        preferred_element_type=F32)
        o_ref[:, h * RET_DV:(h + 1) * RET_DV] = (inter + intra).astype(BF16)
        kdec = (k * jnp.exp(lg * k_pow)).T.astype(BF16)
        s_ref[h] = jnp.exp(lg * c) * s + jnp.dot(kdec, vb, preferred_element_type=F32)


def _ret_scan(p, cos_tab, sin_tab, decay_logit, geo):
    c = RET_CHUNK
    steps = (geo.Lc + geo.N) // c
    nc = geo.Lc // c
    nl = geo.N // c

    def spec(width, colblk, reverse):
        return pl.BlockSpec((c, width), lambda b, i: (_chunk_index(b, i, c, geo, reverse), colblk))

    def tab(reverse):
        def index(b, i):
            lat = (nl - 1 - (i - nc)) if reverse else (i - nc)
            return (jnp.where(i < nc, nl, lat), 0)
        return pl.BlockSpec((c, RET_DK), index)

    def direction(reverse):
        return [spec(RET_QK, 0, reverse), spec(RET_QK, 1, reverse), spec(RET_V, 1, reverse),
                tab(reverse), tab(reverse)]

    out = jax.ShapeDtypeStruct((geo.T, RET_V), BF16)
    state = pltpu.VMEM((RET_HEADS, RET_DK, RET_DV), F32)
    return pl.pallas_call(
        _ret_kernel,
        name="ret_scan",
        out_shape=[out, out],
        grid=(geo.B, steps),
        in_specs=direction(False) + direction(True)
        + [pl.BlockSpec((2, RET_HEADS, 1), lambda b, i: (0, 0, 0))],
        out_specs=[spec(RET_V, 0, False), spec(RET_V, 0, True)],
        scratch_shapes=[state, state],
        compiler_params=_params("arbitrary", "arbitrary"),
    )(p, p, p, cos_tab, sin_tab, p, p, p, cos_tab, sin_tab, decay_logit.reshape(2, RET_HEADS, 1))


def _rope_tables(n):
    t = jnp.arange(n)
    quarter = RET_DK // 4
    inv = 1.0 / (ROPE_BASE ** (jnp.arange(quarter, dtype=F32) / quarter))
    ang_r = (t // GRID_W).astype(F32)[:, None] * inv
    ang_c = (t % GRID_W).astype(F32)[:, None] * inv
    cos = jnp.concatenate([jnp.cos(ang_r), jnp.cos(ang_r), jnp.cos(ang_c), jnp.cos(ang_c)], axis=-1)
    sin = jnp.concatenate([-jnp.sin(ang_r), jnp.sin(ang_r), -jnp.sin(ang_c), jnp.sin(ang_c)], axis=-1)
    cos = jnp.concatenate([cos, jnp.ones((RET_CHUNK, RET_DK), F32)], axis=0)
    sin = jnp.concatenate([sin, jnp.zeros((RET_CHUNK, RET_DK), F32)], axis=0)
    return cos, sin


def _ret_readout_kernel(of_ref, ob_ref, gate_ref, o_ref):
    for h in range(RET_HEADS):
        sl = slice(h * RET_DV, (h + 1) * RET_DV)
        o = of_ref[:, sl].astype(F32) + ob_ref[:, sl].astype(F32)
        r = o * lax.rsqrt(jnp.mean(o * o, axis=-1, keepdims=True) + EPS)
        o_ref[:, sl] = (_silu(gate_ref[:, sl].astype(F32)) * r).astype(BF16)


def _ret_readout(p, of, ob, geo):
    tm = _pick_tile(geo, (256, 128))
    spec = pl.BlockSpec((tm, RET_V), lambda i: (i, 0))
    return pl.pallas_call(
        _ret_readout_kernel,
        name="ret_readout",
        out_shape=jax.ShapeDtypeStruct((geo.T, RET_V), BF16),
        grid=(geo.T // tm,),
        in_specs=[spec, spec, pl.BlockSpec((tm, RET_V), lambda i: (i, 2))],
        out_specs=spec,
        compiler_params=_params("arbitrary"),
    )(of, ob, p)


def _router_kernel(h_ref, rw_ref, rb_ref, eidx_ref, rank_ref, w_ref, cnt_ref, carry_ref, *, tm):
    i = pl.program_id(0)

    @pl.when(i == 0)
    def _():
        carry_ref[...] = jnp.zeros_like(carry_ref)

    half = D_MODEL // 2
    h_lo, h_hi = _unpack_rows(h_ref[...])
    h_lo = h_lo.astype(BF16)
    h_hi = h_hi.astype(BF16)
    rest = rw_ref[...]
    logits = jnp.zeros((tm, N_EXPERTS), F32)
    for _ in range(3):
        part = rest.astype(BF16)
        rest = rest - part.astype(F32)
        logits = (logits + jnp.dot(h_lo, part[:half], preferred_element_type=F32)
                  + jnp.dot(h_hi, part[half:], preferred_element_type=F32))
    s = _sigmoid(logits)
    sel = s + rb_ref[...]
    lane = lax.broadcasted_iota(jnp.int32, (tm, N_EXPERTS), 1).astype(F32)
    grp = jnp.floor(lane * (1.0 / GROUP_SIZE))
    ninf = -jnp.inf
    none = float(N_EXPERTS)

    gscore = jnp.zeros((tm, N_EXPERTS), F32)
    gcols = []
    for g in range(N_GROUPS):
        in_g = grp == float(g)
        v1 = jnp.max(jnp.where(in_g, sel, ninf), axis=-1, keepdims=True)
        i1 = jnp.min(jnp.where(in_g & (sel == v1), lane, none), axis=-1, keepdims=True)
        v2 = jnp.max(jnp.where(in_g & (lane != i1), sel, ninf), axis=-1, keepdims=True)
        gcols.append(v1 + v2)
        gscore = jnp.where(in_g, v1 + v2, gscore)
    beaten = jnp.zeros((tm, N_EXPERTS), F32)
    for g in range(N_GROUPS):
        wins = (gcols[g] > gscore) | ((gcols[g] == gscore) & (float(g) < grp))
        beaten = beaten + jnp.where(wins, 1.0, 0.0)
    cand = jnp.where(beaten < float(TOPK_GROUPS), sel, ninf)

    lane_k = lax.broadcasted_iota(jnp.int32, (tm, TOP_K), 1)
    eidx = jnp.zeros((tm, TOP_K), F32)
    wsel = jnp.zeros((tm, TOP_K), F32)
    chosen = jnp.zeros((tm, N_EXPERTS), F32)
    picks = []
    for k in range(TOP_K):
        v = jnp.max(cand, axis=-1, keepdims=True)
        ik = jnp.min(jnp.where(cand == v, lane, none), axis=-1, keepdims=True)
        hit = lane == ik
        picks.append(ik)
        eidx = jnp.where(lane_k == k, ik, eidx)
        wsel = jnp.where(lane_k == k, jnp.sum(jnp.where(hit, s, 0.0), axis=-1, keepdims=True), wsel)
        chosen = jnp.where(hit, 1.0, chosen)
        cand = jnp.where(hit, ninf, cand)
    w_ref[...] = wsel / jnp.sum(wsel, axis=-1, keepdims=True) * ROUTED_SCALE
    eidx_ref[...] = eidx.astype(jnp.int32)

    r = lax.broadcasted_iota(jnp.int32, (tm, tm), 0)
    c = lax.broadcasted_iota(jnp.int32, (tm, tm), 1)
    below = jnp.where(c < r, 1.0, 0.0).astype(BF16)
    carry = carry_ref[...]
    pos = jnp.dot(below, chosen.astype(BF16), preferred_element_type=F32) + carry
    rank = jnp.zeros((tm, TOP_K), jnp.int32)
    for k in range(TOP_K):
        rk = jnp.sum(jnp.where(lane == picks[k], pos, 0.0), axis=-1, keepdims=True)
        rank = jnp.where(lane_k == k, rk.astype(jnp.int32), rank)
    rank_ref[...] = rank
    carry = carry + jnp.sum(chosen, axis=0, keepdims=True)
    carry_ref[...] = carry
    cnt_ref[...] = carry


def _router(h_packed, rw, rb, geo):
    tm = 256 if geo.T % 256 == 0 else 128
    tok = pl.BlockSpec((tm, TOP_K), lambda i: (i, 0))
    one = pl.BlockSpec((1, N_EXPERTS), lambda i: (0, 0))
    return pl.pallas_call(
        functools.partial(_router_kernel, tm=tm),
        name="router",
        out_shape=[jax.ShapeDtypeStruct((geo.T, TOP_K), jnp.int32),
                   jax.ShapeDtypeStruct((geo.T, TOP_K), jnp.int32),
                   jax.ShapeDtypeStruct((geo.T, TOP_K), F32),
                   jax.ShapeDtypeStruct((1, N_EXPERTS), F32)],
        grid=(geo.T // tm,),
        in_specs=[pl.BlockSpec((tm, D_MODEL // 2), lambda i: (i, 0)),
                  pl.BlockSpec((D_MODEL, N_EXPERTS), lambda i: (0, 0)), one],
        out_specs=[tok, tok, tok, one],
        scratch_shapes=[pltpu.VMEM((1, N_EXPERTS), F32)],
        compiler_params=_params("arbitrary"),
    )(h_packed, rw, rb.reshape(1, N_EXPERTS))


def _n_blocks(geo):
    return -(-(geo.T * TOP_K) // MOE_BLOCK) + N_EXPERTS


def _dest_kernel(cnt_ref, eidx_ref, rank_ref, dest_ref, blk_ref, *, n_blocks):
    eidx = eidx_ref[...]
    dest = rank_ref[...]
    blk_row = (lax.broadcasted_iota(jnp.int32, blk_ref.shape, 0) * 128
               + lax.broadcasted_iota(jnp.int32, blk_ref.shape, 1)) * MOE_BLOCK
    blk = jnp.zeros(blk_ref.shape, jnp.int32)
    start = jnp.int32(0)
    for e in range(N_EXPERTS):
        padded = (cnt_ref[e] + (MOE_BLOCK - 1)) // MOE_BLOCK * MOE_BLOCK
        dest = dest + jnp.where(eidx == e, start, 0)
        start = start + padded
        blk = blk + jnp.where(start <= blk_row, 1, 0)
    dest_ref[...] = dest
    blk_ref[...] = jnp.where(blk_row == n_blocks * MOE_BLOCK, start // MOE_BLOCK,
                             jnp.minimum(blk, N_EXPERTS - 1))


def _dest(counts, eidx, rank, geo):
    rows = geo.T * TOP_K // 128
    brow = -(-(_n_blocks(geo) + 1) // 128)
    full = pl.BlockSpec((rows, 128), lambda: (0, 0))
    dest, blk = pl.pallas_call(
        functools.partial(_dest_kernel, n_blocks=_n_blocks(geo)),
        name="dest",
        out_shape=[jax.ShapeDtypeStruct((rows, 128), jnp.int32),
                   jax.ShapeDtypeStruct((brow, 128), jnp.int32)],
        in_specs=[pl.BlockSpec(memory_space=pltpu.SMEM), full, full],
        out_specs=[full, pl.BlockSpec((brow, 128), lambda: (0, 0))],
    )(counts, eidx.reshape(rows, 128), rank.reshape(rows, 128))
    return dest.reshape(-1), blk.reshape(-1)[:_n_blocks(geo) + 1]


def _zero_fill(cnt_ref, xs_hbm, zero_ref, zsem, n_rows, wait):
    def piece(pos, size):
        if size >= 8:
            copies = [(pl.multiple_of(pos, 8), size)]
        else:
            copies = [(pos + r, 1) for r in range(size)]
        for p, s in copies:
            cp = pltpu.make_async_copy(zero_ref.at[pl.ds(0, s)], xs_hbm.at[pl.ds(p, s)], zsem)
            cp.wait() if wait else cp.start()

    def per_expert(e, start):
        cnt = cnt_ref[e]
        padded = (cnt + (MOE_BLOCK - 1)) // MOE_BLOCK * MOE_BLOCK
        pad = padded - cnt
        pos = start + cnt
        size = 1
        while size < MOE_BLOCK:
            take = (pad & size) != 0
            pl.when(take)(functools.partial(piece, pos, size))
            pos = pos + jnp.where(take, size, 0)
            size *= 2
        return start + padded

    end = lax.fori_loop(0, N_EXPERTS, per_expert, jnp.int32(0))

    def per_block(j, carry):
        piece(end + j * MOE_BLOCK, MOE_BLOCK)
        return carry

    lax.fori_loop(0, (n_rows - end) // MOE_BLOCK, per_block, 0)


def _sc_scatter_rows(src, order, n_rows):
    nw = V7X_SC_CORES * V7X_SC_SUBCORES
    t, w = src.shape
    per = t // nw
    n = per // SC_ROWS
    assert t % (nw * SC_ROWS * 2) == 0
    mesh = plsc.VectorSubcoreMesh(core_axis_name="c", subcore_axis_name="s")

    @functools.partial(
        pl.kernel, mesh=mesh,
        out_type=jax.ShapeDtypeStruct((n_rows, w), src.dtype),
        scratch_types=[pltpu.VMEM((SC_ROWS, w), src.dtype)] * 2
        + [pltpu.VMEM((SC_ROWS,), jnp.int32)] * (2 * TOP_K) + [pltpu.SemaphoreType.DMA] * 2,
    )
    def scatter(src_hbm, idx_hbm, out_hbm, *scratch):
        rows_vs = scratch[:2]
        idx_vs = (scratch[2:2 + TOP_K], scratch[2 + TOP_K:2 + 2 * TOP_K])
        sems = scratch[2 + 2 * TOP_K:]
        wid = lax.axis_index("s") * V7X_SC_CORES + lax.axis_index("c")
        base = wid * per

        def load(chunk, b):
            off = pl.multiple_of(base + chunk * SC_ROWS, 8)
            pltpu.sync_copy(src_hbm.at[pl.ds(off, SC_ROWS)], rows_vs[b])
            for k in range(TOP_K):
                pltpu.sync_copy(idx_hbm.at[pl.ds(pl.multiple_of(k * t + off, 8), SC_ROWS)], idx_vs[b][k])

        def copies(b):
            return [pltpu.make_async_copy(rows_vs[b], out_hbm.at[idx_vs[b][k]], sems[b])
                    for k in range(TOP_K)]

        load(0, 0)

        @pl.loop(0, n, step=2)
        def _(j):
            for b in range(2):
                cur = j + b
                for cp in copies(b):
                    cp.start()
                pl.when(cur + 1 < n)(functools.partial(load, cur + 1, 1 - b))
                for cp in copies(b):
                    cp.wait()

    return scatter(src, order)


def _zero_pad_kernel(cnt_ref, xs_in, xs_hbm, zero_ref, zsem, *, n_rows):
    del xs_in
    zero_ref[...] = jnp.zeros_like(zero_ref)
    _zero_fill(cnt_ref, xs_hbm, zero_ref, zsem, n_rows, wait=False)
    _zero_fill(cnt_ref, xs_hbm, zero_ref, zsem, n_rows, wait=True)


def _dispatch(counts, order, h_packed, geo):
    n_rows = _n_blocks(geo) * MOE_BLOCK
    width = D_MODEL // 2
    xs = _sc_scatter_rows(h_packed[:geo.T], order, n_rows)
    return pl.pallas_call(
        functools.partial(_zero_pad_kernel, n_rows=n_rows),
        name="zero_pad",
        out_shape=jax.ShapeDtypeStruct((n_rows, width), PACKED),
        in_specs=[pl.BlockSpec(memory_space=pltpu.SMEM), pl.BlockSpec(memory_space=pl.ANY)],
        out_specs=pl.BlockSpec(memory_space=pl.ANY),
        scratch_shapes=[pltpu.VMEM((MOE_BLOCK, width), PACKED), pltpu.SemaphoreType.DMA],
        input_output_aliases={1: 0},
    )(counts, xs)


def _swiglu_packed(x_ref, wg_ref, wu_ref, wd_ref):
    half = D_MODEL // 2
    lo, hi = _unpack_rows(x_ref[...])
    lo = lo.astype(BF16)
    hi = hi.astype(BF16)

    def proj(w_ref):
        return (jnp.dot(lo, w_ref[:half], preferred_element_type=F32)
                + jnp.dot(hi, w_ref[half:], preferred_element_type=F32))

    g = proj(wg_ref)
    u = proj(wu_ref)
    return jnp.dot((_silu(g) * u).astype(BF16), wd_ref[...], preferred_element_type=F32)


def _expert_kernel(blk_ref, x_ref, wg_ref, wu_ref, wd_ref, o_ref, wgb_ref, wub_ref, wdb_ref, *, nb):
    j = pl.program_id(0)
    n_used = blk_ref[nb]

    @pl.when((j == 0) | (blk_ref[j] != blk_ref[jnp.maximum(j - 1, 0)]))
    def _():
        wgb_ref[...] = wg_ref[...].astype(BF16)
        wub_ref[...] = wu_ref[...].astype(BF16)
        wdb_ref[...] = wd_ref[...].astype(BF16)

    @pl.when(j < n_used)
    def _():
        o_ref[...] = _pack_rows(_swiglu_packed(x_ref, wgb_ref, wub_ref, wdb_ref))

    @pl.when(j >= n_used)
    def _():
        o_ref[...] = jnp.zeros_like(o_ref)


def _experts(blk_e, xs, wg, wu, wd, layer, geo):
    nb = _n_blocks(geo)
    rows = pl.BlockSpec((MOE_BLOCK, D_MODEL // 2), lambda j, be: (j, 0))

    def wspec(r, c):
        return pl.BlockSpec((None, None, r, c), lambda j, be: (layer, be[j], 0, 0))

    return pl.pallas_call(
        functools.partial(_expert_kernel, nb=nb),
        name="experts",
        out_shape=jax.ShapeDtypeStruct((nb * MOE_BLOCK, D_MODEL // 2), PACKED),
        grid_spec=pltpu.PrefetchScalarGridSpec(
            num_scalar_prefetch=1,
            grid=(nb,),
            in_specs=[rows, wspec(D_MODEL, EXPERT_FF), wspec(D_MODEL, EXPERT_FF),
                      wspec(EXPERT_FF, D_MODEL)],
            out_specs=rows,
            scratch_shapes=[pltpu.VMEM((D_MODEL, EXPERT_FF), BF16),
                            pltpu.VMEM((D_MODEL, EXPERT_FF), BF16),
                            pltpu.VMEM((EXPERT_FF, D_MODEL), BF16)]),
        compiler_params=_params("arbitrary"),
    )(blk_e, xs, wg, wu, wd)


def _sc_gather_rows(table, idx):
    nw = V7X_SC_CORES * V7X_SC_SUBCORES
    m = idx.shape[0]
    w = table.shape[1]
    per = m // nw
    n = per // SC_ROWS
    assert m % (nw * SC_ROWS * 2) == 0
    mesh = plsc.VectorSubcoreMesh(core_axis_name="c", subcore_axis_name="s")

    @functools.partial(
        pl.kernel, mesh=mesh,
        out_type=jax.ShapeDtypeStruct((m, w), table.dtype),
        scratch_types=[pltpu.VMEM((SC_ROWS,), jnp.int32)] * 2
        + [pltpu.VMEM((SC_ROWS, w), table.dtype)] * 2 + [pltpu.SemaphoreType.DMA] * 2,
    )
    def gather(table_hbm, idx_hbm, out_hbm, *scratch):
        idx_vs, rows_vs, sems = scratch[:2], scratch[2:4], scratch[4:]
        wid = lax.axis_index("s") * V7X_SC_CORES + lax.axis_index("c")
        base = wid * per

        def fetch(b):
            return pltpu.make_async_copy(table_hbm.at[idx_vs[b]], rows_vs[b], sems[b])

        def start(chunk, b):
            off = pl.multiple_of(base + chunk * SC_ROWS, 8)
            pltpu.sync_copy(idx_hbm.at[pl.ds(off, SC_ROWS)], idx_vs[b])
            fetch(b).start()

        start(0, 0)

        @pl.loop(0, n, step=2)
        def _(j):
            for b in range(2):
                cur = j + b
                pl.when(cur + 1 < n)(functools.partial(start, cur + 1, 1 - b))
                fetch(b).wait()
                off = pl.multiple_of(base + cur * SC_ROWS, 8)
                pltpu.sync_copy(rows_vs[b], out_hbm.at[pl.ds(off, SC_ROWS)])

    return gather(table, idx)


def _combine_kernel(w_ref, x_ref, h_ref, sg_ref, su_ref, sd_ref, m_ref, *rest, post, n_prev):
    rest = rest[:-2 * n_prev] + rest[-n_prev:] if n_prev else rest
    if post == "next":
        ng_ref, nsh_ref, nsc_ref, g_ref, o_ref, hn_ref = rest
    else:
        ng_ref, g_ref, o_ref = rest
    half = D_MODEL // 2
    shared = _swiglu_packed(h_ref, sg_ref, su_ref, sd_ref)
    w = w_ref[...]
    acc_lo = shared[:, :half]
    acc_hi = shared[:, half:]
    for k in range(TOP_K):
        lo, hi = _unpack_rows(g_ref[k])
        acc_lo = acc_lo + w[:, k:k + 1] * lo
        acc_hi = acc_hi + w[:, k:k + 1] * hi
    gate = m_ref[0]
    y_lo = x_ref[:, :half] + gate[:, :half] * acc_lo
    y_hi = x_ref[:, half:] + gate[:, half:] * acc_hi
    ms = (jnp.sum(y_lo * y_lo, axis=-1, keepdims=True)
          + jnp.sum(y_hi * y_hi, axis=-1, keepdims=True)) * (1.0 / D_MODEL)
    inv = lax.rsqrt(ms + EPS)
    ng = ng_ref[...]
    n_lo = y_lo * inv * ng[:, :half]
    n_hi = y_hi * inv * ng[:, half:]
    if post == "next":
        o_ref[:, :half] = y_lo
        o_ref[:, half:] = y_hi
        sc = nsc_ref[0]
        sft = nsh_ref[0]
        hn_ref[:, :half] = (n_lo * (1.0 + sc[:, :half]) + sft[:, :half]).astype(BF16)
        hn_ref[:, half:] = (n_hi * (1.0 + sc[:, half:]) + sft[:, half:]).astype(BF16)
    else:
        o_ref[:, :half] = n_lo
        o_ref[:, half:] = n_hi


def _combine(dest, w, x, h_packed, sg, su, sd, ys, mod3, m_gate, post, norm_g, next_mod3, geo):
    half = D_MODEL // 2
    tt = _pick_tile(geo, (256, 128))
    ff = sg.shape[1]
    unit = math.lcm(tt, V7X_SC_CORES * V7X_SC_SUBCORES * SC_ROWS * 2 // TOP_K)
    cuts = sorted({geo.T * j // COMBINE_PARTS // unit * unit for j in range(COMBINE_PARTS)} | {geo.T})
    parts = list(zip(cuts[:-1], cuts[1:]))
    dest2 = dest.reshape(geo.T, TOP_K)
    outs = []
    for t0, t1 in parts:
        tile0, n = t0 // tt, (t1 - t0) // tt
        order = dest2[t0:t1].T.reshape(-1)
        gathered = _sc_gather_rows(ys, order).reshape(TOP_K, t1 - t0, half)
        rows = pl.BlockSpec((tt, D_MODEL), lambda i: (i + tile0, 0))
        vec = pl.BlockSpec((1, D_MODEL), lambda i: (0, 0))

        def mod(m):
            return pl.BlockSpec((1, 1, D_MODEL),
                                lambda i: (_mod_row(i + tile0, tt, geo) * N_MOD + m, 0, 0))

        in_specs = [pl.BlockSpec((tt, TOP_K), lambda i: (i + tile0, 0)),
                    rows,
                    pl.BlockSpec((tt, half), lambda i: (i + tile0, 0)),
                    pl.BlockSpec((D_MODEL, ff), lambda i: (0, 0)),
                    pl.BlockSpec((D_MODEL, ff), lambda i: (0, 0)),
                    pl.BlockSpec((ff, D_MODEL), lambda i: (0, 0)),
                    mod(m_gate), vec]
        args = [w, x, h_packed, sg, su, sd, mod3, norm_g.reshape(1, D_MODEL)]
        out_shape = [jax.ShapeDtypeStruct((geo.T, D_MODEL), F32)]
        out_specs = [rows]
        if post == "next":
            in_specs += [mod(0), mod(1)]
            args += [next_mod3, next_mod3]
            out_shape.append(jax.ShapeDtypeStruct((geo.T, D_MODEL), BF16))
            out_specs.append(rows)
        in_specs.append(pl.BlockSpec((TOP_K, tt, half), lambda i: (0, i, 0)))
        args.append(gathered)
        aliases = {len(args) + j: j for j in range(len(outs))}
        in_specs += [pl.BlockSpec(memory_space=pl.ANY)] * len(outs)
        args += outs
        outs = pl.pallas_call(
            functools.partial(_combine_kernel, post=post, n_prev=len(outs)),
            name="combine",
            out_shape=out_shape,
            grid=(n,),
            in_specs=in_specs,
            out_specs=out_specs,
            input_output_aliases=aliases,
            compiler_params=_params("arbitrary"),
        )(*args)
    return outs


def _moe(x, h_packed, mod3, rw, rb, wg, wu, wd, layer, sg, su, sd, post, norm_g, next_mod3, geo):
    eidx, rank, w, counts = _router(h_packed, rw, rb, geo)
    counts = counts.reshape(N_EXPERTS).astype(jnp.int32)
    dest, blk_e = _dest(counts, eidx, rank, geo)
    order = dest.reshape(geo.T, TOP_K).T.reshape(-1)
    xs = _dispatch(counts, order, h_packed, geo)
    ys = _experts(blk_e, xs, wg, wu, wd, layer, geo)
    return _combine(dest, w, x, h_packed, sg.astype(BF16), su.astype(BF16), sd.astype(BF16), ys,
                    mod3, 5, post, norm_g, next_mod3, geo)


def kernel(x, c, ctx, c_ctx, ada_w, ada_b, norm_mix, norm_ffn, norm_final, ev_w_in, ev_w_out, hgrn_lb, hgrn_norm, conv_w, conv_b, conv_norm_g, conv_norm_b, ret_w_in, ret_w_out, ret_decay, router_w, router_b, exp_gate, exp_up, exp_down, sh_gate, sh_up, sh_down):
    b, n, d = x.shape
    lc = ctx.shape[1]
    depth = ada_w.shape[0]
    geo = _geo(b, n, lc)
    assert d == D_MODEL and b < MOD_ROWS
    assert n % RET_CHUNK == 0 and lc % RET_CHUNK == 0

    xs = jnp.concatenate([x.reshape(geo.BN, d), ctx.reshape(geo.BL, d)], axis=0)
    cond = jnp.zeros((MOD_ROWS, d), F32).at[:b].set(c).at[b].set(c_ctx)
    cos_tab, sin_tab = _rope_tables(n)

    mods = [_adaln(cond, ada_w, ada_b, l).reshape(MOD_ROWS * N_MOD, 1, d) for l in range(depth)]
    h = _normmod(xs, norm_mix[0], mods[0], 0, 1, geo, packed=False)
    for l in range(depth):
        j = l // 2
        last = l == depth - 1
        mod3 = mods[l]
        tail = geo._replace(T=geo.BN, BL=0, Lc=0) if last else geo
        if l % 2 == 0:
            p = _matmul(h, ev_w_in[j].astype(BF16), geo)
            of, ob = _hgrn_scan(p, hgrn_lb, l, geo)
            mix = _ev_readout(p, of, ob, hgrn_norm[j], conv_w[j], conv_b[j],
                              conv_norm_g[j], conv_norm_b[j], geo)
            xs = _matmul_resid(mix, ev_w_out[j].astype(BF16), xs, mod3, 2, tail)
        else:
            p = _matmul(h, ret_w_in[j].astype(BF16), geo)
            of, ob = _ret_scan(p, cos_tab, sin_tab, ret_decay[j], geo)
            mix = _ret_readout(p, of, ob, geo)
            xs = _matmul_resid(mix, ret_w_out[j].astype(BF16), xs, mod3, 2, tail)
        h_packed = _normmod(xs, norm_ffn[l], mod3, 3, 4, tail, packed=True)
        moe_w = (router_w[l], router_b[l], exp_gate, exp_up, exp_down, l, sh_gate[l], sh_up[l], sh_down[l])
        if last:
            (out,) = _moe(xs, h_packed, mod3, *moe_w, "final", norm_final, None, tail)
        else:
            xs, h = _moe(xs, h_packed, mod3, *moe_w, "next", norm_mix[l + 1], mods[l + 1], tail)
    return out.reshape(b, n, d)
```

```python
import collections
import functools
import math

import jax
import jax.numpy as jnp
from jax import lax
from jax.experimental import pallas as pl
from jax.experimental.pallas import tpu as pltpu
from jax.experimental.pallas import tpu_sc as plsc

F32 = jnp.float32
BF16 = jnp.bfloat16

D_MODEL = 2048
N_MOD = 6
EPS = 1e-6
GRID_W = 64
ROPE_BASE = 10000.0

HG_HEADS = 8
HG_DK = 128
HG_DV = 128
HG_F = HG_HEADS * HG_DK
HG_V = HG_HEADS * HG_DV
CONV_C = D_MODEL // 2
CONV_W = 31
CONV_HALO = 16
HG_CHUNK = 128

RET_HEADS = 8
RET_DK = D_MODEL // RET_HEADS
RET_DV = 2 * RET_DK
RET_QK = RET_HEADS * RET_DK
RET_V = RET_HEADS * RET_DV
RET_CHUNK = 256

N_EXPERTS = 64
EXPERT_FF = D_MODEL // 4
TOP_K = 8
N_GROUPS = 8
GROUP_SIZE = N_EXPERTS // N_GROUPS
TOPK_GROUPS = 4
ROUTED_SCALE = 2.5
MOE_BLOCK = 512

V7X_SC_CORES = 2
V7X_SC_SUBCORES = 16
SC_ROWS = 32
MOD_ROWS = 16
VMEM_LIMIT = 56 * 1024 * 1024

Geo = collections.namedtuple("Geo", "B N Lc BN BL T")


def _geo(b, n, lc):
    return Geo(b, n, lc, b * n, b * lc, b * n + b * lc)


def _pick_tile(geo, cands):
    for t in cands:
        if geo.N % t == 0 and geo.BL % t == 0:
            return t
    raise ValueError("no row tile fits the sequence lengths")


def _mod_row(i, tm, geo):
    return jnp.where(i < geo.BN // tm, i // (geo.N // tm), geo.B)


def _mod_spec(m, tm, geo, ngrid=1):
    if ngrid == 1:
        return pl.BlockSpec((1, 1, D_MODEL), lambda i: (_mod_row(i, tm, geo) * N_MOD + m, 0, 0))
    return pl.BlockSpec((1, 1, D_MODEL), lambda i, j: (_mod_row(i, tm, geo) * N_MOD + m, 0, j))


def _params(*sem):
    return pltpu.CompilerParams(dimension_semantics=sem, vmem_limit_bytes=VMEM_LIMIT)


def _sigmoid(x):
    return jax.nn.sigmoid(x)


def _silu(x):
    return x * jax.nn.sigmoid(x)


def _adaln_kernel(c_ref, w_ref, b_ref, o_ref):
    a = _silu(c_ref[...]).astype(BF16)
    o_ref[...] = jnp.dot(a, w_ref[...].astype(BF16), preferred_element_type=F32) + b_ref[...]


def _adaln(cond, w, b, layer):
    depth, k, n = w.shape
    tn = 1024
    return pl.pallas_call(
        _adaln_kernel,
        name="adaln",
        out_shape=jax.ShapeDtypeStruct((MOD_ROWS, n), F32),
        grid=(n // tn,),
        in_specs=[pl.BlockSpec((MOD_ROWS, k), lambda j: (0, 0)),
                  pl.BlockSpec((None, k, tn), lambda j: (layer, 0, j)),
                  pl.BlockSpec((None, 1, tn), lambda j: (layer, 0, j))],
        out_specs=pl.BlockSpec((MOD_ROWS, tn), lambda j: (0, j)),
        compiler_params=_params("arbitrary"),
    )(cond, w, b.reshape(depth, 1, n))


PACKED = jnp.int32


def _pack_rows(x):
    n = x.shape[-1] // 2
    bits = lax.bitcast_convert_type(x.astype(BF16).astype(F32), jnp.uint32)
    words = (bits[:, n:] & jnp.uint32(0xFFFF0000)) | (bits[:, :n] >> 16)
    return lax.bitcast_convert_type(words, PACKED)


def _unpack_rows(p):
    u = lax.bitcast_convert_type(p, jnp.uint32)
    lo = lax.bitcast_convert_type(u << 16, F32)
    hi = lax.bitcast_convert_type(u & jnp.uint32(0xFFFF0000), F32)
    return lo, hi


def _stream_specs(stream, tm, width, geo, ngrid):
    col = (lambda j: j) if ngrid == 2 else (lambda *_: 0)
    if not isinstance(stream, tuple):
        return [pl.BlockSpec((tm, width), lambda i, *j: (i, col(*j)))], [stream]
    n_lat = geo.BN // tm
    return ([pl.BlockSpec((tm, width), lambda i, *j: (jnp.minimum(i, n_lat - 1), col(*j))),
             pl.BlockSpec((tm, width), lambda i, *j: (jnp.maximum(i - n_lat, 0), col(*j)))],
            list(stream))


def _stream_tile(refs, tm, geo):
    if len(refs) == 1:
        return refs[0][...]
    return jnp.where(pl.program_id(0) < geo.BN // tm, refs[0][...], refs[1][...])


def _normmod_kernel(*refs, packed, tm, geo):
    g_ref, sh_ref, sc_ref, o_ref = refs[-4:]
    x = _stream_tile(refs[:-4], tm, geo)
    y = x * lax.rsqrt(jnp.mean(x * x, axis=-1, keepdims=True) + EPS) * g_ref[...]
    h = y * (1.0 + sc_ref[0]) + sh_ref[0]
    o_ref[...] = _pack_rows(h) if packed else h.astype(BF16)


def _normmod(x, g, mod3, m_shift, m_scale, geo, packed):
    tm = _pick_tile(geo, (256, 128))
    x_specs, x_args = _stream_specs(x, tm, D_MODEL, geo, 1)
    if packed:
        out_shape = jax.ShapeDtypeStruct((geo.T, D_MODEL // 2), PACKED)
        out_spec = pl.BlockSpec((tm, D_MODEL // 2), lambda i: (i, 0))
    else:
        out_shape = jax.ShapeDtypeStruct((geo.T, D_MODEL), BF16)
        out_spec = pl.BlockSpec((tm, D_MODEL), lambda i: (i, 0))
    return pl.pallas_call(
        functools.partial(_normmod_kernel, packed=packed, tm=tm, geo=geo),
        name="normmod",
        out_shape=out_shape,
        grid=(geo.T // tm,),
        in_specs=x_specs + [pl.BlockSpec((1, D_MODEL), lambda i: (0, 0)),
                            _mod_spec(m_shift, tm, geo), _mod_spec(m_scale, tm, geo)],
        out_specs=out_spec,
        compiler_params=_params("arbitrary"),
    )(*x_args, g.reshape(1, D_MODEL), mod3, mod3)


def _mm_kernel(a_ref, w_ref, o_ref):
    o_ref[...] = jnp.dot(a_ref[...], w_ref[...], preferred_element_type=F32).astype(o_ref.dtype)


def _mm_resid_kernel(a_ref, w_ref, *refs, tm, geo):
    m_ref, o_ref = refs[-2:]
    y = jnp.dot(a_ref[...], w_ref[...], preferred_element_type=F32)
    o_ref[...] = _stream_tile(refs[:-2], tm, geo) + m_ref[0] * y


def _matmul(a, w, geo):
    k, n = w.shape
    tm = _pick_tile(geo, (1024, 512, 256, 128))
    tn = 1024 if n % 1024 == 0 else 512
    return pl.pallas_call(
        _mm_kernel,
        name="matmul",
        out_shape=jax.ShapeDtypeStruct((geo.T, n), BF16),
        grid=(geo.T // tm, n // tn),
        in_specs=[pl.BlockSpec((tm, k), lambda i, j: (i, 0)),
                  pl.BlockSpec((k, tn), lambda i, j: (0, j))],
        out_specs=pl.BlockSpec((tm, tn), lambda i, j: (i, j)),
        compiler_params=_params("arbitrary", "arbitrary"),
    )(a, w)


def _matmul_resid(a, w, x, mod3, m_gate, geo):
    k, n = w.shape
    tm = _pick_tile(geo, (1024, 512, 256, 128))
    tn = 512
    x_specs, x_args = _stream_specs(x, tm, tn, geo, 2)
    return pl.pallas_call(
        functools.partial(_mm_resid_kernel, tm=tm, geo=geo),
        name="matmul_resid",
        out_shape=jax.ShapeDtypeStruct((geo.T, n), F32),
        grid=(geo.T // tm, n // tn),
        in_specs=[pl.BlockSpec((tm, k), lambda i, j: (i, 0)),
                  pl.BlockSpec((k, tn), lambda i, j: (0, j))] + x_specs
        + [pl.BlockSpec((1, 1, tn), lambda i, j: (_mod_row(i, tm, geo) * N_MOD + m_gate, 0, j))],
        out_specs=pl.BlockSpec((tm, tn), lambda i, j: (i, j)),
        compiler_params=_params("arbitrary", "arbitrary"),
    )(a, w, *x_args, mod3)


def _chunk_index(b, i, chunk, geo, reverse):
    nc = geo.Lc // chunk
    nl = geo.N // chunk
    ctx0 = (geo.BN + b * geo.Lc) // chunk
    lat0 = (b * geo.N) // chunk
    if reverse:
        return jnp.where(i < nc, ctx0 + (nc - 1 - i), lat0 + (nl - 1 - (i - nc)))
    return jnp.where(i < nc, ctx0 + i, lat0 + (i - nc))


def _split_dot(tri_bf, x):
    hi = x.astype(BF16)
    r1 = x - hi.astype(F32)
    mid = r1.astype(BF16)
    lo = (r1 - mid.astype(F32)).astype(BF16)
    return (jnp.dot(tri_bf, hi, preferred_element_type=F32)
            + jnp.dot(tri_bf, mid, preferred_element_type=F32)
            + jnp.dot(tri_bf, lo, preferred_element_type=F32))


def _hgrn_kernel(qf_ref, ff_ref, vf_ref, qb_ref, fb_ref, vb_ref, lbp_ref, of_ref, ob_ref,
                 stf_ref, stb_ref, *, layer):
    i = pl.program_id(1)

    @pl.when(i == 0)
    def _():
        stf_ref[...] = jnp.zeros_like(stf_ref)
        stb_ref[...] = jnp.zeros_like(stb_ref)

    lbp = lbp_ref[...]
    e = jnp.exp(lbp - jnp.max(lbp, axis=0, keepdims=True))
    sm = e / jnp.sum(e, axis=0, keepdims=True)
    lb = sm[0:1]
    for r in range(1, layer + 1):
        lb = lb + sm[r:r + 1]

    _hgrn_chunk(qf_ref, ff_ref, vf_ref, of_ref, stf_ref, lb, reverse=False)
    _hgrn_chunk(qb_ref, fb_ref, vb_ref, ob_ref, stb_ref, lb, reverse=True)


def _hgrn_chunk(q_ref, f_ref, v_ref, o_ref, st_ref, lb, *, reverse):
    c = HG_CHUNK
    row = lax.broadcasted_iota(jnp.int32, (c, c), 0)
    col = lax.broadcasted_iota(jnp.int32, (c, c), 1)
    tri = (col >= row) if reverse else (col <= row)
    tri_bf = jnp.where(tri, 1.0, 0.0).astype(BF16)
    nt = (((1,), (1,)), ((), ()))

    for h in range(HG_HEADS):
        sl = slice(h * HG_DK, (h + 1) * HG_DK)
        qh = _silu(q_ref[:, sl].astype(F32))
        lbh = lb[:, sl]
        fg = lbh + (1.0 - lbh) * _sigmoid(f_ref[:, sl].astype(F32))
        kh = 1.0 - fg
        bcum = _split_dot(tri_bf, jnp.log(fg))
        bmid = bcum[c // 2:c // 2 + 1]
        bend = bcum[0:1] if reverse else bcum[c - 1:c]
        vb = v_ref[:, sl]
        vh = vb.astype(F32)
        a = (qh * jnp.exp(bcum - bmid)).astype(BF16)
        kd = (kh * jnp.exp(bmid - bcum)).astype(BF16)
        s = lax.dot_general(a, kd, nt, preferred_element_type=F32)
        s = jnp.where(tri, s, 0.0)
        intra = jnp.dot(s.astype(BF16), vb, preferred_element_type=F32)
        st = st_ref[h]
        inter = lax.dot_general((qh * jnp.exp(bcum)).astype(BF16), st.astype(BF16), nt,
                                preferred_element_type=F32)
        o_ref[:, sl] = (inter + intra).astype(BF16)
        kd2 = (kh * jnp.exp(bend - bcum)).astype(BF16)
        st_ref[h] = st * jnp.exp(bend) + jnp.dot(vh.T.astype(BF16), kd2, preferred_element_type=F32)


def _hgrn_scan(p, lb_param, layer, geo):
    c = HG_CHUNK
    steps = (geo.Lc + geo.N) // c

    def spec(colblk, reverse):
        return pl.BlockSpec((c, HG_F), lambda b, i: (_chunk_index(b, i, c, geo, reverse), colblk))

    out = jax.ShapeDtypeStruct((geo.T, HG_V), BF16)
    state = pltpu.VMEM((HG_HEADS, HG_DV, HG_DK), F32)
    return pl.pallas_call(
        functools.partial(_hgrn_kernel, layer=layer),
        name="hgrn_scan",
        out_shape=[out, out],
        grid=(geo.B, steps),
        in_specs=[spec(0, False), spec(1, False), spec(3, False),
                  spec(0, True), spec(2, True), spec(3, True),
                  pl.BlockSpec(lb_param.shape, lambda b, i: (0, 0))],
        out_specs=[spec(0, False), spec(0, True)],
        scratch_shapes=[state, state],
        compiler_params=_params("arbitrary", "arbitrary"),
    )(p, p, p, p, p, p, lb_param)


def _ev_readout_kernel(of_ref, ob_ref, gate_ref, a_ref, b_ref, ap_ref, bp_ref, an_ref, bn_ref,
                       gain_ref, cw_ref, cb_ref, lng_ref, lnb_ref, o_ref, ext_ref, conv_ref, shift_ref,
                       *, tm, geo):
    i = pl.program_id(0)
    n_lat = geo.BN // tm
    tpl = geo.N // tm
    tpc = geo.Lc // tm
    j = jnp.where(i < n_lat, i % tpl, (i - n_lat) % tpc)
    per = jnp.where(i < n_lat, tpl, tpc)
    keep_prev = jnp.where(j == 0, 0.0, 1.0)
    keep_next = jnp.where(j == per - 1, 0.0, 1.0)

    o = of_ref[...].astype(F32) + ob_ref[...].astype(F32)
    r = o * lax.rsqrt(jnp.mean(o * o, axis=-1, keepdims=True) + EPS) * gain_ref[...]
    o_ref[:, :HG_V] = (r * _silu(gate_ref[...].astype(F32))).astype(BF16)

    def glu(x_ref, y_ref):
        return x_ref[...].astype(F32) * _sigmoid(y_ref[...].astype(F32))

    ext_ref[0:CONV_HALO] = glu(ap_ref, bp_ref) * keep_prev
    ext_ref[CONV_HALO:CONV_HALO + tm] = glu(a_ref, b_ref)
    ext_ref[CONV_HALO + tm:2 * CONV_HALO + tm] = glu(an_ref, bn_ref) * keep_next
    off = CONV_HALO - CONV_W // 2
    span = tm + 8 * ((off + CONV_W - 1) // 8)
    for s in range(1, 8):
        shift_ref[s - 1, 0:span] = ext_ref[s:s + span]
    for cj in range(CONV_C // 128):
        cs = slice(cj * 128, (cj + 1) * 128)
        acc = jnp.broadcast_to(cb_ref[:, cs], (tm, 128))
        for k in range(CONV_W):
            a, s = divmod(off + k, 8)
            src = ext_ref if s == 0 else shift_ref.at[s - 1]
            acc = acc + cw_ref[k:k + 1, cs] * src[8 * a:8 * a + tm, cs]
        conv_ref[:, cs] = acc
    acc = conv_ref[...]
    mu = jnp.mean(acc, axis=-1, keepdims=True)
    xc = acc - mu
    var = jnp.mean(xc * xc, axis=-1, keepdims=True)
    u = xc * lax.rsqrt(var + EPS) * lng_ref[...] + lnb_ref[...]
    o_ref[:, HG_V:] = _silu(u).astype(BF16)


def _ev_readout(p, of, ob, gain, cw, cb, lng, lnb, geo):
    tm = _pick_tile(geo, (128,))
    hb = tm // CONV_HALO
    nhalo = geo.T // CONV_HALO

    def row(colblk):
        return pl.BlockSpec((tm, HG_V), lambda i: (i, colblk))

    def prev(colblk):
        return pl.BlockSpec((CONV_HALO, CONV_C), lambda i: (jnp.maximum(i * hb - 1, 0), colblk))

    def nxt(colblk):
        return pl.BlockSpec((CONV_HALO, CONV_C), lambda i: (jnp.minimum((i + 1) * hb, nhalo - 1), colblk))

    def vec(n):
        return pl.BlockSpec((n, CONV_C), lambda i: (0, 0))

    return pl.pallas_call(
        functools.partial(_ev_readout_kernel, tm=tm, geo=geo),
        name="ev_readout",
        out_shape=jax.ShapeDtypeStruct((geo.T, HG_V + CONV_C), BF16),
        grid=(geo.T // tm,),
        in_specs=[row(0), row(0), row(4), row(5), row(6), prev(5), prev(6), nxt(5), nxt(6),
                  vec(1), vec(CONV_W), vec(1), vec(1), vec(1)],
        out_specs=pl.BlockSpec((tm, HG_V + CONV_C), lambda i: (i, 0)),
        scratch_shapes=[pltpu.VMEM((tm + 2 * CONV_HALO, CONV_C), F32), pltpu.VMEM((tm, CONV_C), F32),
                        pltpu.VMEM((7, tm + 2 * CONV_HALO, CONV_C), F32)],
        compiler_params=_params("arbitrary"),
    )(of, ob, p, p, p, p, p, p, p, gain.reshape(1, -1), cw, cb.reshape(1, -1),
      lng.reshape(1, -1), lnb.reshape(1, -1))


def _rope(x, cos, sin_signed):
    half = x.shape[-1] // 2
    rot = jnp.concatenate([pltpu.roll(x[:, :half], half // 2, axis=1),
                           pltpu.roll(x[:, half:], half // 2, axis=1)], axis=-1)
    return x * cos + rot * sin_signed


def _ret_kernel(qf_ref, kf_ref, vf_ref, cosf_ref, sinf_ref, qb_ref, kb_ref, vb_ref, cosb_ref, sinb_ref,
                dl_ref, of_ref, ob_ref, sf_ref, sb_ref):
    i = pl.program_id(1)

    @pl.when(i == 0)
    def _():
        sf_ref[...] = jnp.zeros_like(sf_ref)
        sb_ref[...] = jnp.zeros_like(sb_ref)

    _ret_chunk(qf_ref, kf_ref, vf_ref, cosf_ref, sinf_ref, dl_ref[0], of_ref, sf_ref, reverse=False)
    _ret_chunk(qb_ref, kb_ref, vb_ref, cosb_ref, sinb_ref, dl_ref[1], ob_ref, sb_ref, reverse=True)


def _ret_chunk(q_ref, k_ref, v_ref, cos_ref, sin_ref, dl, o_ref, s_ref, *, reverse):
    c = RET_CHUNK
    lg_all = -jnp.log1p(jnp.exp(-dl))
    row = lax.broadcasted_iota(jnp.int32, (c, c), 0)
    col = lax.broadcasted_iota(jnp.int32, (c, c), 1)
    idx = lax.broadcasted_iota(jnp.int32, (c, 1), 0).astype(F32)
    if reverse:
        live = col >= row
        dist = (col - row).astype(F32)
        q_pow = c - idx
        k_pow = idx
    else:
        live = row >= col
        dist = (row - col).astype(F32)
        q_pow = idx + 1.0
        k_pow = c - 1.0 - idx
    cos = cos_ref[...]
    sin = sin_ref[...]
    nt = (((1,), (1,)), ((), ()))

    for h in range(RET_HEADS):
        lg = lg_all[h:h + 1]
        dmask = jnp.where(live, jnp.exp(lg * dist), 0.0)
        q = _rope(q_ref[:, h * RET_DK:(h + 1) * RET_DK].astype(F32), cos, sin)
        k = _rope(k_ref[:, h * RET_DK:(h + 1) * RET_DK].astype(F32) * (RET_DK ** -0.5), cos, sin)
        vb = v_ref[:, h * RET_DV:(h + 1) * RET_DV]
        scores = lax.dot_general(q.astype(BF16), k.astype(BF16), nt, preferred_element_type=F32) * dmask
        intra = jnp.dot(scores.astype(BF16), vb, preferred_element_type=F32)
        s = s_ref[h]
        inter = jnp.dot((q * jnp.exp(lg * q_pow)).astype(BF16), s.astype(BF16),
                        preferred_element_type=F32)
        o_ref[:, h * RET_DV:(h + 1) * RET_DV] = (inter + intra).astype(BF16)
        kdec = (k * jnp.exp(lg * k_pow)).T.astype(BF16)
        s_ref[h] = jnp.exp(lg * c) * s + jnp.dot(kdec, vb, preferred_element_type=F32)


def _ret_scan(p, cos_tab, sin_tab, decay_logit, geo):
    c = RET_CHUNK
    steps = (geo.Lc + geo.N) // c
    nc = geo.Lc // c
    nl = geo.N // c

    def spec(width, colblk, reverse):
        return pl.BlockSpec((c, width), lambda b, i: (_chunk_index(b, i, c, geo, reverse), colblk))

    def tab(reverse):
        def index(b, i):
            lat = (nl - 1 - (i - nc)) if reverse else (i - nc)
            return (jnp.where(i < nc, nl, lat), 0)
        return pl.BlockSpec((c, RET_DK), index)

    def direction(reverse):
        return [spec(RET_QK, 0, reverse), spec(RET_QK, 1, reverse), spec(RET_V, 1, reverse),
                tab(reverse), tab(reverse)]

    out = jax.ShapeDtypeStruct((geo.T, RET_V), BF16)
    state = pltpu.VMEM((RET_HEADS, RET_DK, RET_DV), F32)
    return pl.pallas_call(
        _ret_kernel,
        name="ret_scan",
        out_shape=[out, out],
        grid=(geo.B, steps),
        in_specs=direction(False) + direction(True)
        + [pl.BlockSpec((2, RET_HEADS, 1), lambda b, i: (0, 0, 0))],
        out_specs=[spec(RET_V, 0, False), spec(RET_V, 0, True)],
        scratch_shapes=[state, state],
        compiler_params=_params("arbitrary", "arbitrary"),
    )(p, p, p, cos_tab, sin_tab, p, p, p, cos_tab, sin_tab, decay_logit.reshape(2, RET_HEADS, 1))


def _rope_tables(n):
    t = jnp.arange(n)
    quarter = RET_DK // 4
    inv = 1.0 / (ROPE_BASE ** (jnp.arange(quarter, dtype=F32) / quarter))
    ang_r = (t // GRID_W).astype(F32)[:, None] * inv
    ang_c = (t % GRID_W).astype(F32)[:, None] * inv
    cos = jnp.concatenate([jnp.cos(ang_r), jnp.cos(ang_r), jnp.cos(ang_c), jnp.cos(ang_c)], axis=-1)
    sin = jnp.concatenate([-jnp.sin(ang_r), jnp.sin(ang_r), -jnp.sin(ang_c), jnp.sin(ang_c)], axis=-1)
    cos = jnp.concatenate([cos, jnp.ones((RET_CHUNK, RET_DK), F32)], axis=0)
    sin = jnp.concatenate([sin, jnp.zeros((RET_CHUNK, RET_DK), F32)], axis=0)
    return cos, sin


def _ret_readout_kernel(of_ref, ob_ref, gate_ref, o_ref):
    for h in range(RET_HEADS):
        sl = slice(h * RET_DV, (h + 1) * RET_DV)
        o = of_ref[:, sl].astype(F32) + ob_ref[:, sl].astype(F32)
        r = o * lax.rsqrt(jnp.mean(o * o, axis=-1, keepdims=True) + EPS)
        o_ref[:, sl] = (_silu(gate_ref[:, sl].astype(F32)) * r).astype(BF16)


def _ret_readout(p, of, ob, geo):
    tm = _pick_tile(geo, (256, 128))
    spec = pl.BlockSpec((tm, RET_V), lambda i: (i, 0))
    return pl.pallas_call(
        _ret_readout_kernel,
        name="ret_readout",
        out_shape=jax.ShapeDtypeStruct((geo.T, RET_V), BF16),
        grid=(geo.T // tm,),
        in_specs=[spec, spec, pl.BlockSpec((tm, RET_V), lambda i: (i, 2))],
        out_specs=spec,
        compiler_params=_params("arbitrary"),
    )(of, ob, p)


def _router_kernel(h_ref, rw_ref, rb_ref, eidx_ref, rank_ref, w_ref, cnt_ref, carry_ref, *, tm):
    i = pl.program_id(0)

    @pl.when(i == 0)
    def _():
        carry_ref[...] = jnp.zeros_like(carry_ref)

    half = D_MODEL // 2
    h_lo, h_hi = _unpack_rows(h_ref[...])
    h_lo = h_lo.astype(BF16)
    h_hi = h_hi.astype(BF16)
    rest = rw_ref[...]
    logits = jnp.zeros((tm, N_EXPERTS), F32)
    for _ in range(3):
        part = rest.astype(BF16)
        rest = rest - part.astype(F32)
        logits = (logits + jnp.dot(h_lo, part[:half], preferred_element_type=F32)
                  + jnp.dot(h_hi, part[half:], preferred_element_type=F32))
    s = _sigmoid(logits)
    sel = s + rb_ref[...]
    lane = lax.broadcasted_iota(jnp.int32, (tm, N_EXPERTS), 1).astype(F32)
    grp = jnp.floor(lane * (1.0 / GROUP_SIZE))
    ninf = -jnp.inf
    none = float(N_EXPERTS)

    gscore = jnp.zeros((tm, N_EXPERTS), F32)
    gcols = []
    for g in range(N_GROUPS):
        in_g = grp == float(g)
        v1 = jnp.max(jnp.where(in_g, sel, ninf), axis=-1, keepdims=True)
        i1 = jnp.min(jnp.where(in_g & (sel == v1), lane, none), axis=-1, keepdims=True)
        v2 = jnp.max(jnp.where(in_g & (lane != i1), sel, ninf), axis=-1, keepdims=True)
        gcols.append(v1 + v2)
        gscore = jnp.where(in_g, v1 + v2, gscore)
    beaten = jnp.zeros((tm, N_EXPERTS), F32)
    for g in range(N_GROUPS):
        wins = (gcols[g] > gscore) | ((gcols[g] == gscore) & (float(g) < grp))
        beaten = beaten + jnp.where(wins, 1.0, 0.0)
    cand = jnp.where(beaten < float(TOPK_GROUPS), sel, ninf)

    lane_k = lax.broadcasted_iota(jnp.int32, (tm, TOP_K), 1)
    eidx = jnp.zeros((tm, TOP_K), F32)
    wsel = jnp.zeros((tm, TOP_K), F32)
    chosen = jnp.zeros((tm, N_EXPERTS), F32)
    picks = []
    for k in range(TOP_K):
        v = jnp.max(cand, axis=-1, keepdims=True)
        ik = jnp.min(jnp.where(cand == v, lane, none), axis=-1, keepdims=True)
        hit = lane == ik
        picks.append(ik)
        eidx = jnp.where(lane_k == k, ik, eidx)
        wsel = jnp.where(lane_k == k, jnp.sum(jnp.where(hit, s, 0.0), axis=-1, keepdims=True), wsel)
        chosen = jnp.where(hit, 1.0, chosen)
        cand = jnp.where(hit, ninf, cand)
    w_ref[...] = wsel / jnp.sum(wsel, axis=-1, keepdims=True) * ROUTED_SCALE
    eidx_ref[...] = eidx.astype(jnp.int32)

    r = lax.broadcasted_iota(jnp.int32, (tm, tm), 0)
    c = lax.broadcasted_iota(jnp.int32, (tm, tm), 1)
    below = jnp.where(c < r, 1.0, 0.0).astype(BF16)
    carry = carry_ref[...]
    pos = jnp.dot(below, chosen.astype(BF16), preferred_element_type=F32) + carry
    rank = jnp.zeros((tm, TOP_K), jnp.int32)
    for k in range(TOP_K):
        rk = jnp.sum(jnp.where(lane == picks[k], pos, 0.0), axis=-1, keepdims=True)
        rank = jnp.where(lane_k == k, rk.astype(jnp.int32), rank)
    rank_ref[...] = rank
    carry = carry + jnp.sum(chosen, axis=0, keepdims=True)
    carry_ref[...] = carry
    cnt_ref[...] = carry


def _router(h_packed, rw, rb, geo):
    tm = 256 if geo.T % 256 == 0 else 128
    tok = pl.BlockSpec((tm, TOP_K), lambda i: (i, 0))
    one = pl.BlockSpec((1, N_EXPERTS), lambda i: (0, 0))
    return pl.pallas_call(
        functools.partial(_router_kernel, tm=tm),
        name="router",
        out_shape=[jax.ShapeDtypeStruct((geo.T, TOP_K), jnp.int32),
                   jax.ShapeDtypeStruct((geo.T, TOP_K), jnp.int32),
                   jax.ShapeDtypeStruct((geo.T, TOP_K), F32),
                   jax.ShapeDtypeStruct((1, N_EXPERTS), F32)],
        grid=(geo.T // tm,),
        in_specs=[pl.BlockSpec((tm, D_MODEL // 2), lambda i: (i, 0)),
                  pl.BlockSpec((D_MODEL, N_EXPERTS), lambda i: (0, 0)), one],
        out_specs=[tok, tok, tok, one],
        scratch_shapes=[pltpu.VMEM((1, N_EXPERTS), F32)],
        compiler_params=_params("arbitrary"),
    )(h_packed, rw, rb.reshape(1, N_EXPERTS))


def _n_blocks(geo):
    return -(-(geo.T * TOP_K) // MOE_BLOCK) + N_EXPERTS


def _dest_kernel(cnt_ref, eidx_ref, rank_ref, dest_ref, blk_ref, *, n_blocks):
    eidx = eidx_ref[...]
    dest = rank_ref[...]
    blk_row = (lax.broadcasted_iota(jnp.int32, blk_ref.shape, 0) * 128
               + lax.broadcasted_iota(jnp.int32, blk_ref.shape, 1)) * MOE_BLOCK
    blk = jnp.zeros(blk_ref.shape, jnp.int32)
    start = jnp.int32(0)
    for e in range(N_EXPERTS):
        padded = (cnt_ref[e] + (MOE_BLOCK - 1)) // MOE_BLOCK * MOE_BLOCK
        dest = dest + jnp.where(eidx == e, start, 0)
        start = start + padded
        blk = blk + jnp.where(start <= blk_row, 1, 0)
    dest_ref[...] = dest
    blk_ref[...] = jnp.where(blk_row == n_blocks * MOE_BLOCK, start // MOE_BLOCK,
                             jnp.minimum(blk, N_EXPERTS - 1))


def _dest(counts, eidx, rank, geo):
    rows = geo.T * TOP_K // 128
    brow = -(-(_n_blocks(geo) + 1) // 128)
    full = pl.BlockSpec((rows, 128), lambda: (0, 0))
    dest, blk = pl.pallas_call(
        functools.partial(_dest_kernel, n_blocks=_n_blocks(geo)),
        name="dest",
        out_shape=[jax.ShapeDtypeStruct((rows, 128), jnp.int32),
                   jax.ShapeDtypeStruct((brow, 128), jnp.int32)],
        in_specs=[pl.BlockSpec(memory_space=pltpu.SMEM), full, full],
        out_specs=[full, pl.BlockSpec((brow, 128), lambda: (0, 0))],
    )(counts, eidx.reshape(rows, 128), rank.reshape(rows, 128))
    return dest.reshape(-1), blk.reshape(-1)[:_n_blocks(geo) + 1]


def _zero_fill(cnt_ref, xs_hbm, zero_ref, zsem, n_rows, wait):
    def piece(pos, size):
        if size >= 8:
            copies = [(pl.multiple_of(pos, 8), size)]
        else:
            copies = [(pos + r, 1) for r in range(size)]
        for p, s in copies:
            cp = pltpu.make_async_copy(zero_ref.at[pl.ds(0, s)], xs_hbm.at[pl.ds(p, s)], zsem)
            cp.wait() if wait else cp.start()

    def per_expert(e, start):
        cnt = cnt_ref[e]
        padded = (cnt + (MOE_BLOCK - 1)) // MOE_BLOCK * MOE_BLOCK
        pad = padded - cnt
        pos = start + cnt
        size = 1
        while size < MOE_BLOCK:
            take = (pad & size) != 0
            pl.when(take)(functools.partial(piece, pos, size))
            pos = pos + jnp.where(take, size, 0)
            size *= 2
        return start + padded

    end = lax.fori_loop(0, N_EXPERTS, per_expert, jnp.int32(0))

    def per_block(j, carry):
        piece(end + j * MOE_BLOCK, MOE_BLOCK)
        return carry

    lax.fori_loop(0, (n_rows - end) // MOE_BLOCK, per_block, 0)


def _sc_scatter_rows(src, order, n_rows):
    nw = V7X_SC_CORES * V7X_SC_SUBCORES
    rows = 2 * SC_ROWS
    t, w = src.shape
    per = t // nw
    assert t % (nw * rows) == 0
    mesh = plsc.VectorSubcoreMesh(core_axis_name="c", subcore_axis_name="s")

    @functools.partial(
        pl.kernel, mesh=mesh,
        out_type=jax.ShapeDtypeStruct((n_rows, w), src.dtype),
        scratch_types=[pltpu.VMEM((rows, w), src.dtype)]
        + [pltpu.VMEM((rows,), jnp.int32)] * TOP_K + [pltpu.SemaphoreType.DMA],
    )
    def scatter(src_hbm, idx_hbm, out_hbm, rows_v, *rest):
        idx_vs, sem = rest[:TOP_K], rest[TOP_K]
        wid = lax.axis_index("s") * V7X_SC_CORES + lax.axis_index("c")
        base = wid * per

        @pl.loop(0, per // rows)
        def _(j):
            off = pl.multiple_of(base + j * rows, 8)
            pltpu.sync_copy(src_hbm.at[pl.ds(off, rows)], rows_v)
            for k in range(TOP_K):
                pltpu.sync_copy(idx_hbm.at[pl.ds(pl.multiple_of(k * t + off, 8), rows)], idx_vs[k])
            copies = [pltpu.async_copy(rows_v, out_hbm.at[idx_vs[k]], sem) for k in range(TOP_K)]
            for cp in copies:
                cp.wait()

    return scatter(src, order)


def _zero_pad_kernel(cnt_ref, xs_in, xs_hbm, zero_ref, zsem, *, n_rows):
    del xs_in
    zero_ref[...] = jnp.zeros_like(zero_ref)
    _zero_fill(cnt_ref, xs_hbm, zero_ref, zsem, n_rows, wait=False)
    _zero_fill(cnt_ref, xs_hbm, zero_ref, zsem, n_rows, wait=True)


def _dispatch(counts, order, h_packed, geo):
    n_rows = _n_blocks(geo) * MOE_BLOCK
    width = D_MODEL // 2
    xs = _sc_scatter_rows(h_packed[:geo.T], order, n_rows)
    return pl.pallas_call(
        functools.partial(_zero_pad_kernel, n_rows=n_rows),
        name="zero_pad",
        out_shape=jax.ShapeDtypeStruct((n_rows, width), PACKED),
        in_specs=[pl.BlockSpec(memory_space=pltpu.SMEM), pl.BlockSpec(memory_space=pl.ANY)],
        out_specs=pl.BlockSpec(memory_space=pl.ANY),
        scratch_shapes=[pltpu.VMEM((MOE_BLOCK, width), PACKED), pltpu.SemaphoreType.DMA],
        input_output_aliases={1: 0},
    )(counts, xs)


def _swiglu_packed(x_ref, wg_ref, wu_ref, wd_ref):
    half = D_MODEL // 2
    lo, hi = _unpack_rows(x_ref[...])
    lo = lo.astype(BF16)
    hi = hi.astype(BF16)

    def proj(w_ref):
        return (jnp.dot(lo, w_ref[:half], preferred_element_type=F32)
                + jnp.dot(hi, w_ref[half:], preferred_element_type=F32))

    g = proj(wg_ref)
    u = proj(wu_ref)
    return jnp.dot((_silu(g) * u).astype(BF16), wd_ref[...], preferred_element_type=F32)


def _expert_kernel(blk_ref, x_ref, wg_ref, wu_ref, wd_ref, o_ref, wgb_ref, wub_ref, wdb_ref, *, nb):
    j = pl.program_id(0)
    n_used = blk_ref[nb]

    @pl.when((j == 0) | (blk_ref[j] != blk_ref[jnp.maximum(j - 1, 0)]))
    def _():
        wgb_ref[...] = wg_ref[...].astype(BF16)
        wub_ref[...] = wu_ref[...].astype(BF16)
        wdb_ref[...] = wd_ref[...].astype(BF16)

    @pl.when(j < n_used)
    def _():
        o_ref[...] = _pack_rows(_swiglu_packed(x_ref, wgb_ref, wub_ref, wdb_ref))

    @pl.when(j >= n_used)
    def _():
        o_ref[...] = jnp.zeros_like(o_ref)


def _experts(blk_e, xs, wg, wu, wd, layer, geo):
    nb = _n_blocks(geo)
    rows = pl.BlockSpec((MOE_BLOCK, D_MODEL // 2), lambda j, be: (j, 0))

    def wspec(r, c):
        return pl.BlockSpec((None, None, r, c), lambda j, be: (layer, be[j], 0, 0))

    return pl.pallas_call(
        functools.partial(_expert_kernel, nb=nb),
        name="experts",
        out_shape=jax.ShapeDtypeStruct((nb * MOE_BLOCK, D_MODEL // 2), PACKED),
        grid_spec=pltpu.PrefetchScalarGridSpec(
            num_scalar_prefetch=1,
            grid=(nb,),
            in_specs=[rows, wspec(D_MODEL, EXPERT_FF), wspec(D_MODEL, EXPERT_FF),
                      wspec(EXPERT_FF, D_MODEL)],
            out_specs=rows,
            scratch_shapes=[pltpu.VMEM((D_MODEL, EXPERT_FF), BF16),
                            pltpu.VMEM((D_MODEL, EXPERT_FF), BF16),
                            pltpu.VMEM((EXPERT_FF, D_MODEL), BF16)]),
        compiler_params=_params("arbitrary"),
    )(blk_e, xs, wg, wu, wd)


def _sc_gather_rows(table, idx):
    nw = V7X_SC_CORES * V7X_SC_SUBCORES
    m = idx.shape[0]
    w = table.shape[1]
    per = m // nw
    n = per // SC_ROWS
    assert m % (nw * SC_ROWS * 2) == 0
    mesh = plsc.VectorSubcoreMesh(core_axis_name="c", subcore_axis_name="s")

    @functools.partial(
        pl.kernel, mesh=mesh,
        out_type=jax.ShapeDtypeStruct((m, w), table.dtype),
        scratch_types=[pltpu.VMEM((SC_ROWS,), jnp.int32)] * 2
        + [pltpu.VMEM((SC_ROWS, w), table.dtype)] * 2 + [pltpu.SemaphoreType.DMA] * 2,
    )
    def gather(table_hbm, idx_hbm, out_hbm, *scratch):
        idx_vs, rows_vs, sems = scratch[:2], scratch[2:4], scratch[4:]
        wid = lax.axis_index("s") * V7X_SC_CORES + lax.axis_index("c")
        base = wid * per

        def fetch(b):
            return pltpu.make_async_copy(table_hbm.at[idx_vs[b]], rows_vs[b], sems[b])

        def start(chunk, b):
            off = pl.multiple_of(base + chunk * SC_ROWS, 8)
            pltpu.sync_copy(idx_hbm.at[pl.ds(off, SC_ROWS)], idx_vs[b])
            fetch(b).start()

        start(0, 0)

        @pl.loop(0, n, step=2)
        def _(j):
            for b in range(2):
                cur = j + b
                pl.when(cur + 1 < n)(functools.partial(start, cur + 1, 1 - b))
                fetch(b).wait()
                off = pl.multiple_of(base + cur * SC_ROWS, 8)
                pltpu.sync_copy(rows_vs[b], out_hbm.at[pl.ds(off, SC_ROWS)])

    return gather(table, idx)


def _combine_kernel(w_ref, x_ref, h_ref, sg_ref, su_ref, sd_ref, m_ref, *rest, post, n_prev):
    rest = rest[:-2 * n_prev] + rest[-n_prev:] if n_prev else rest
    if post == "next":
        ng_ref, nsh_ref, nsc_ref, g_ref, o_ref, hn_ref = rest
    else:
        ng_ref, g_ref, o_ref = rest
    half = D_MODEL // 2
    shared = _swiglu_packed(h_ref, sg_ref, su_ref, sd_ref)
    w = w_ref[...]
    acc_lo = shared[:, :half]
    acc_hi = shared[:, half:]
    for k in range(TOP_K):
        lo, hi = _unpack_rows(g_ref[k])
        acc_lo = acc_lo + w[:, k:k + 1] * lo
        acc_hi = acc_hi + w[:, k:k + 1] * hi
    gate = m_ref[0]
    y_lo = x_ref[:, :half] + gate[:, :half] * acc_lo
    y_hi = x_ref[:, half:] + gate[:, half:] * acc_hi
    ms = (jnp.sum(y_lo * y_lo, axis=-1, keepdims=True)
          + jnp.sum(y_hi * y_hi, axis=-1, keepdims=True)) * (1.0 / D_MODEL)
    inv = lax.rsqrt(ms + EPS)
    ng = ng_ref[...]
    n_lo = y_lo * inv * ng[:, :half]
    n_hi = y_hi * inv * ng[:, half:]
    if post == "next":
        o_ref[:, :half] = y_lo
        o_ref[:, half:] = y_hi
        sc = nsc_ref[0]
        sft = nsh_ref[0]
        hn_ref[:, :half] = (n_lo * (1.0 + sc[:, :half]) + sft[:, :half]).astype(BF16)
        hn_ref[:, half:] = (n_hi * (1.0 + sc[:, half:]) + sft[:, half:]).astype(BF16)
    else:
        o_ref[:, :half] = n_lo
        o_ref[:, half:] = n_hi


def _combine(dest, w, x, h_packed, sg, su, sd, ys, mod3, m_gate, post, norm_g, next_mod3, geo):
    half = D_MODEL // 2
    tt = _pick_tile(geo, (256, 128))
    ff = sg.shape[1]
    unit = math.lcm(tt, V7X_SC_CORES * V7X_SC_SUBCORES * SC_ROWS * 2 // TOP_K)
    cut = (geo.T // 2) // unit * unit
    parts = [(0, cut), (cut, geo.T)] if 0 < cut < geo.T else [(0, geo.T)]
    dest2 = dest.reshape(geo.T, TOP_K)
    outs = []
    for t0, t1 in parts:
        tile0, n = t0 // tt, (t1 - t0) // tt
        order = dest2[t0:t1].T.reshape(-1)
        gathered = _sc_gather_rows(ys, order).reshape(TOP_K, t1 - t0, half)
        rows = pl.BlockSpec((tt, D_MODEL), lambda i: (i + tile0, 0))
        vec = pl.BlockSpec((1, D_MODEL), lambda i: (0, 0))

        def mod(m):
            return pl.BlockSpec((1, 1, D_MODEL),
                                lambda i: (_mod_row(i + tile0, tt, geo) * N_MOD + m, 0, 0))

        in_specs = [pl.BlockSpec((tt, TOP_K), lambda i: (i + tile0, 0)),
                    rows,
                    pl.BlockSpec((tt, half), lambda i: (i + tile0, 0)),
                    pl.BlockSpec((D_MODEL, ff), lambda i: (0, 0)),
                    pl.BlockSpec((D_MODEL, ff), lambda i: (0, 0)),
                    pl.BlockSpec((ff, D_MODEL), lambda i: (0, 0)),
                    mod(m_gate), vec]
        args = [w, x, h_packed, sg, su, sd, mod3, norm_g.reshape(1, D_MODEL)]
        out_shape = [jax.ShapeDtypeStruct((geo.T, D_MODEL), F32)]
        out_specs = [rows]
        if post == "next":
            in_specs += [mod(0), mod(1)]
            args += [next_mod3, next_mod3]
            out_shape.append(jax.ShapeDtypeStruct((geo.T, D_MODEL), BF16))
            out_specs.append(rows)
        in_specs.append(pl.BlockSpec((TOP_K, tt, half), lambda i: (0, i, 0)))
        args.append(gathered)
        aliases = {len(args) + j: j for j in range(len(outs))}
        in_specs += [pl.BlockSpec(memory_space=pl.ANY)] * len(outs)
        args += outs
        outs = pl.pallas_call(
            functools.partial(_combine_kernel, post=post, n_prev=len(outs)),
            name="combine",
            out_shape=out_shape,
            grid=(n,),
            in_specs=in_specs,
            out_specs=out_specs,
            input_output_aliases=aliases,
            compiler_params=_params("arbitrary"),
        )(*args)
    return outs


def _moe(x, h_packed, mod3, rw, rb, wg, wu, wd, layer, sg, su, sd, post, norm_g, next_mod3, geo):
    eidx, rank, w, counts = _router(h_packed, rw, rb, geo)
    counts = counts.reshape(N_EXPERTS).astype(jnp.int32)
    dest, blk_e = _dest(counts, eidx, rank, geo)
    order = dest.reshape(geo.T, TOP_K).T.reshape(-1)
    xs = _dispatch(counts, order, h_packed, geo)
    ys = _experts(blk_e, xs, wg, wu, wd, layer, geo)
    return _combine(dest, w, x, h_packed, sg.astype(BF16), su.astype(BF16), sd.astype(BF16), ys,
                    mod3, 5, post, norm_g, next_mod3, geo)


def kernel(x, c, ctx, c_ctx, ada_w, ada_b, norm_mix, norm_ffn, norm_final, ev_w_in, ev_w_out, hgrn_lb, hgrn_norm, conv_w, conv_b, conv_norm_g, conv_norm_b, ret_w_in, ret_w_out, ret_decay, router_w, router_b, exp_gate, exp_up, exp_down, sh_gate, sh_up, sh_down):
    b, n, d = x.shape
    lc = ctx.shape[1]
    depth = ada_w.shape[0]
    geo = _geo(b, n, lc)
    assert d == D_MODEL and b < MOD_ROWS
    assert n % RET_CHUNK == 0 and lc % RET_CHUNK == 0

    xs = (x.reshape(geo.BN, d), ctx.reshape(geo.BL, d))
    cond = jnp.zeros((MOD_ROWS, d), F32).at[:b].set(c).at[b].set(c_ctx)
    cos_tab, sin_tab = _rope_tables(n)

    mods = [_adaln(cond, ada_w, ada_b, l).reshape(MOD_ROWS * N_MOD, 1, d) for l in range(depth)]
    h = _normmod(xs, norm_mix[0], mods[0], 0, 1, geo, packed=False)
    for l in range(depth):
        j = l // 2
        last = l == depth - 1
        mod3 = mods[l]
        tail = geo._replace(T=geo.BN, BL=0, Lc=0) if last else geo
        if l % 2 == 0:
            p = _matmul(h, ev_w_in[j].astype(BF16), geo)
            of, ob = _hgrn_scan(p, hgrn_lb, l, geo)
            mix = _ev_readout(p, of, ob, hgrn_norm[j], conv_w[j], conv_b[j],
                              conv_norm_g[j], conv_norm_b[j], geo)
            xs = _matmul_resid(mix, ev_w_out[j].astype(BF16), xs, mod3, 2, tail)
        else:
            p = _matmul(h, ret_w_in[j].astype(BF16), geo)
            of, ob = _ret_scan(p, cos_tab, sin_tab, ret_decay[j], geo)
            mix = _ret_readout(p, of, ob, geo)
            xs = _matmul_resid(mix, ret_w_out[j].astype(BF16), xs, mod3, 2, tail)
        h_packed = _normmod(xs, norm_ffn[l], mod3, 3, 4, tail, packed=True)
        moe_w = (router_w[l], router_b[l], exp_gate, exp_up, exp_down, l, sh_gate[l], sh_up[l], sh_down[l])
        if last:
            (out,) = _moe(xs, h_packed, mod3, *moe_w, "final", norm_final, None, tail)
        else:
            xs, h = _moe(xs, h_packed, mod3, *moe_w, "next", norm_mix[l + 1], mods[l + 1], tail)
    return out.reshape(b, n, d)
```

```python
import collections
import functools
import math

import jax
import jax.numpy as jnp
from jax import lax
from jax.experimental import pallas as pl
from jax.experimental.pallas import tpu as pltpu
from jax.experimental.pallas import tpu_sc as plsc

F32 = jnp.float32
BF16 = jnp.bfloat16

D_MODEL = 2048
N_MOD = 6
EPS = 1e-6
GRID_W = 64
ROPE_BASE = 10000.0

HG_HEADS = 8
HG_DK = 128
HG_DV = 128
HG_F = HG_HEADS * HG_DK
HG_V = HG_HEADS * HG_DV
CONV_C = D_MODEL // 2
CONV_W = 31
CONV_HALO = 16
HG_CHUNK = 128

RET_HEADS = 8
RET_DK = D_MODEL // RET_HEADS
RET_DV = 2 * RET_DK
RET_QK = RET_HEADS * RET_DK
RET_V = RET_HEADS * RET_DV
RET_CHUNK = 256

N_EXPERTS = 64
EXPERT_FF = D_MODEL // 4
TOP_K = 8
N_GROUPS = 8
GROUP_SIZE = N_EXPERTS // N_GROUPS
TOPK_GROUPS = 4
ROUTED_SCALE = 2.5
MOE_BLOCK = 512

V7X_SC_CORES = 2
V7X_SC_SUBCORES = 16
SC_ROWS = 32
MOD_ROWS = 16
VMEM_LIMIT = 56 * 1024 * 1024

Geo = collections.namedtuple("Geo", "B N Lc BN BL T")


def _geo(b, n, lc):
    return Geo(b, n, lc, b * n, b * lc, b * n + b * lc)


def _pick_tile(geo, cands):
    for t in cands:
        if geo.N % t == 0 and geo.BL % t == 0:
            return t
    raise ValueError("no row tile fits the sequence lengths")


def _mod_row(i, tm, geo):
    return jnp.where(i < geo.BN // tm, i // (geo.N // tm), geo.B)


def _mod_spec(m, tm, geo, ngrid=1):
    if ngrid == 1:
        return pl.BlockSpec((1, 1, D_MODEL), lambda i: (_mod_row(i, tm, geo) * N_MOD + m, 0, 0))
    return pl.BlockSpec((1, 1, D_MODEL), lambda i, j: (_mod_row(i, tm, geo) * N_MOD + m, 0, j))


def _params(*sem):
    return pltpu.CompilerParams(dimension_semantics=sem, vmem_limit_bytes=VMEM_LIMIT)


def _sigmoid(x):
    return jax.nn.sigmoid(x)


def _silu(x):
    return x * jax.nn.sigmoid(x)


def _adaln_kernel(c_ref, w_ref, b_ref, o_ref):
    a = _silu(c_ref[...]).astype(BF16)
    o_ref[...] = jnp.dot(a, w_ref[...].astype(BF16), preferred_element_type=F32) + b_ref[...]


def _adaln(cond, w, b, layer):
    depth, k, n = w.shape
    tn = 1024
    return pl.pallas_call(
        _adaln_kernel,
        name="adaln",
        out_shape=jax.ShapeDtypeStruct((MOD_ROWS, n), F32),
        grid=(n // tn,),
        in_specs=[pl.BlockSpec((MOD_ROWS, k), lambda j: (0, 0)),
                  pl.BlockSpec((None, k, tn), lambda j: (layer, 0, j)),
                  pl.BlockSpec((None, 1, tn), lambda j: (layer, 0, j))],
        out_specs=pl.BlockSpec((MOD_ROWS, tn), lambda j: (0, j)),
        compiler_params=_params("arbitrary"),
    )(cond, w, b.reshape(depth, 1, n))


PACKED = jnp.int32


def _pack_rows(x):
    n = x.shape[-1] // 2
    bits = lax.bitcast_convert_type(x.astype(BF16).astype(F32), jnp.uint32)
    words = (bits[:, n:] & jnp.uint32(0xFFFF0000)) | (bits[:, :n] >> 16)
    return lax.bitcast_convert_type(words, PACKED)


def _unpack_rows(p):
    u = lax.bitcast_convert_type(p, jnp.uint32)
    lo = lax.bitcast_convert_type(u << 16, F32)
    hi = lax.bitcast_convert_type(u & jnp.uint32(0xFFFF0000), F32)
    return lo, hi


def _stream_specs(stream, tm, width, geo, ngrid):
    col = (lambda j: j) if ngrid == 2 else (lambda *_: 0)
    if not isinstance(stream, tuple):
        return [pl.BlockSpec((tm, width), lambda i, *j: (i, col(*j)))], [stream]
    n_lat = geo.BN // tm
    return ([pl.BlockSpec((tm, width), lambda i, *j: (jnp.minimum(i, n_lat - 1), col(*j))),
             pl.BlockSpec((tm, width), lambda i, *j: (jnp.maximum(i - n_lat, 0), col(*j)))],
            list(stream))


def _stream_tile(refs, tm, geo):
    if len(refs) == 1:
        return refs[0][...]
    return jnp.where(pl.program_id(0) < geo.BN // tm, refs[0][...], refs[1][...])


def _normmod_kernel(*refs, packed, tm, geo):
    g_ref, sh_ref, sc_ref, o_ref = refs[-4:]
    x = _stream_tile(refs[:-4], tm, geo)
    y = x * lax.rsqrt(jnp.mean(x * x, axis=-1, keepdims=True) + EPS) * g_ref[...]
    h = y * (1.0 + sc_ref[0]) + sh_ref[0]
    o_ref[...] = _pack_rows(h) if packed else h.astype(BF16)


def _normmod(x, g, mod3, m_shift, m_scale, geo, packed):
    tm = _pick_tile(geo, (256, 128))
    x_specs, x_args = _stream_specs(x, tm, D_MODEL, geo, 1)
    if packed:
        out_shape = jax.ShapeDtypeStruct((geo.T, D_MODEL // 2), PACKED)
        out_spec = pl.BlockSpec((tm, D_MODEL // 2), lambda i: (i, 0))
    else:
        out_shape = jax.ShapeDtypeStruct((geo.T, D_MODEL), BF16)
        out_spec = pl.BlockSpec((tm, D_MODEL), lambda i: (i, 0))
    return pl.pallas_call(
        functools.partial(_normmod_kernel, packed=packed, tm=tm, geo=geo),
        name="normmod",
        out_shape=out_shape,
        grid=(geo.T // tm,),
        in_specs=x_specs + [pl.BlockSpec((1, D_MODEL), lambda i: (0, 0)),
                            _mod_spec(m_shift, tm, geo), _mod_spec(m_scale, tm, geo)],
        out_specs=out_spec,
        compiler_params=_params("arbitrary"),
    )(*x_args, g.reshape(1, D_MODEL), mod3, mod3)


def _mm_kernel(a_ref, w_ref, o_ref):
    o_ref[...] = jnp.dot(a_ref[...], w_ref[...], preferred_element_type=F32).astype(o_ref.dtype)


def _mm_resid_kernel(a_ref, w_ref, *refs, tm, geo):
    m_ref, o_ref = refs[-2:]
    y = jnp.dot(a_ref[...], w_ref[...], preferred_element_type=F32)
    o_ref[...] = _stream_tile(refs[:-2], tm, geo) + m_ref[0] * y


def _matmul(a, w, geo):
    k, n = w.shape
    tm = _pick_tile(geo, (1024, 512, 256, 128))
    tn = 1024 if n % 1024 == 0 else 512
    return pl.pallas_call(
        _mm_kernel,
        name="matmul",
        out_shape=jax.ShapeDtypeStruct((geo.T, n), BF16),
        grid=(geo.T // tm, n // tn),
        in_specs=[pl.BlockSpec((tm, k), lambda i, j: (i, 0)),
                  pl.BlockSpec((k, tn), lambda i, j: (0, j))],
        out_specs=pl.BlockSpec((tm, tn), lambda i, j: (i, j)),
        compiler_params=_params("arbitrary", "arbitrary"),
    )(a, w)


def _matmul_resid(a, w, x, mod3, m_gate, geo):
    k, n = w.shape
    tm = _pick_tile(geo, (1024, 512, 256, 128))
    tn = 512
    x_specs, x_args = _stream_specs(x, tm, tn, geo, 2)
    return pl.pallas_call(
        functools.partial(_mm_resid_kernel, tm=tm, geo=geo),
        name="matmul_resid",
        out_shape=jax.ShapeDtypeStruct((geo.T, n), F32),
        grid=(geo.T // tm, n // tn),
        in_specs=[pl.BlockSpec((tm, k), lambda i, j: (i, 0)),
                  pl.BlockSpec((k, tn), lambda i, j: (0, j))] + x_specs
        + [pl.BlockSpec((1, 1, tn), lambda i, j: (_mod_row(i, tm, geo) * N_MOD + m_gate, 0, j))],
        out_specs=pl.BlockSpec((tm, tn), lambda i, j: (i, j)),
        compiler_params=_params("arbitrary", "arbitrary"),
    )(a, w, *x_args, mod3)


def _chunk_index(b, i, chunk, geo, reverse):
    nc = geo.Lc // chunk
    nl = geo.N // chunk
    ctx0 = (geo.BN + b * geo.Lc) // chunk
    lat0 = (b * geo.N) // chunk
    if reverse:
        return jnp.where(i < nc, ctx0 + (nc - 1 - i), lat0 + (nl - 1 - (i - nc)))
    return jnp.where(i < nc, ctx0 + i, lat0 + (i - nc))


def _split_dot(tri_bf, x):
    hi = x.astype(BF16)
    r1 = x - hi.astype(F32)
    mid = r1.astype(BF16)
    lo = (r1 - mid.astype(F32)).astype(BF16)
    return (jnp.dot(tri_bf, hi, preferred_element_type=F32)
            + jnp.dot(tri_bf, mid, preferred_element_type=F32)
            + jnp.dot(tri_bf, lo, preferred_element_type=F32))


def _hgrn_kernel(qf_ref, ff_ref, vf_ref, qb_ref, fb_ref, vb_ref, lbp_ref, of_ref, ob_ref,
                 stf_ref, stb_ref, *, layer):
    i = pl.program_id(1)

    @pl.when(i == 0)
    def _():
        stf_ref[...] = jnp.zeros_like(stf_ref)
        stb_ref[...] = jnp.zeros_like(stb_ref)

    lbp = lbp_ref[...]
    e = jnp.exp(lbp - jnp.max(lbp, axis=0, keepdims=True))
    sm = e / jnp.sum(e, axis=0, keepdims=True)
    lb = sm[0:1]
    for r in range(1, layer + 1):
        lb = lb + sm[r:r + 1]

    _hgrn_chunk(qf_ref, ff_ref, vf_ref, of_ref, stf_ref, lb, reverse=False)
    _hgrn_chunk(qb_ref, fb_ref, vb_ref, ob_ref, stb_ref, lb, reverse=True)


def _hgrn_chunk(q_ref, f_ref, v_ref, o_ref, st_ref, lb, *, reverse):
    c = HG_CHUNK
    row = lax.broadcasted_iota(jnp.int32, (c, c), 0)
    col = lax.broadcasted_iota(jnp.int32, (c, c), 1)
    tri = (col >= row) if reverse else (col <= row)
    tri_bf = jnp.where(tri, 1.0, 0.0).astype(BF16)
    nt = (((1,), (1,)), ((), ()))

    fg_all = lb + (1.0 - lb) * _sigmoid(f_ref[...].astype(F32))
    bcum_all = _split_dot(tri_bf, jnp.log(fg_all))

    for h in range(HG_HEADS):
        sl = slice(h * HG_DK, (h + 1) * HG_DK)
        qh = _silu(q_ref[:, sl].astype(F32))
        kh = 1.0 - fg_all[:, sl]
        bcum = bcum_all[:, sl]
        bmid = bcum[c // 2:c // 2 + 1]
        bend = bcum[0:1] if reverse else bcum[c - 1:c]
        vb = v_ref[:, sl]
        vh = vb.astype(F32)
        a = (qh * jnp.exp(bcum - bmid)).astype(BF16)
        kd = (kh * jnp.exp(bmid - bcum)).astype(BF16)
        s = lax.dot_general(a, kd, nt, preferred_element_type=F32)
        s = jnp.where(tri, s, 0.0)
        intra = jnp.dot(s.astype(BF16), vb, preferred_element_type=F32)
        st = st_ref[h]
        inter = lax.dot_general((qh * jnp.exp(bcum)).astype(BF16), st.astype(BF16), nt,
                                preferred_element_type=F32)
        o_ref[:, sl] = (inter + intra).astype(BF16)
        kd2 = (kh * jnp.exp(bend - bcum)).astype(BF16)
        st_ref[h] = st * jnp.exp(bend) + jnp.dot(vh.T.astype(BF16), kd2, preferred_element_type=F32)


def _hgrn_scan(p, lb_param, layer, geo):
    c = HG_CHUNK
    steps = (geo.Lc + geo.N) // c

    def spec(colblk, reverse):
        return pl.BlockSpec((c, HG_F), lambda b, i: (_chunk_index(b, i, c, geo, reverse), colblk))

    out = jax.ShapeDtypeStruct((geo.T, HG_V), BF16)
    state = pltpu.VMEM((HG_HEADS, HG_DV, HG_DK), F32)
    return pl.pallas_call(
        functools.partial(_hgrn_kernel, layer=layer),
        name="hgrn_scan",
        out_shape=[out, out],
        grid=(geo.B, steps),
        in_specs=[spec(0, False), spec(1, False), spec(3, False),
                  spec(0, True), spec(2, True), spec(3, True),
                  pl.BlockSpec(lb_param.shape, lambda b, i: (0, 0))],
        out_specs=[spec(0, False), spec(0, True)],
        scratch_shapes=[state, state],
        compiler_params=_params("arbitrary", "arbitrary"),
    )(p, p, p, p, p, p, lb_param)


def _ev_readout_kernel(of_ref, ob_ref, gate_ref, a_ref, b_ref, ap_ref, bp_ref, an_ref, bn_ref,
                       gain_ref, cw_ref, cb_ref, lng_ref, lnb_ref, o_ref, ext_ref, conv_ref, shift_ref,
                       *, tm, geo):
    i = pl.program_id(0)
    n_lat = geo.BN // tm
    tpl = geo.N // tm
    tpc = geo.Lc // tm
    j = jnp.where(i < n_lat, i % tpl, (i - n_lat) % tpc)
    per = jnp.where(i < n_lat, tpl, tpc)
    keep_prev = jnp.where(j == 0, 0.0, 1.0)
    keep_next = jnp.where(j == per - 1, 0.0, 1.0)

    o = of_ref[...].astype(F32) + ob_ref[...].astype(F32)
    r = o * lax.rsqrt(jnp.mean(o * o, axis=-1, keepdims=True) + EPS) * gain_ref[...]
    o_ref[:, :HG_V] = (r * _silu(gate_ref[...].astype(F32))).astype(BF16)

    def glu(x_ref, y_ref):
        return x_ref[...].astype(F32) * _sigmoid(y_ref[...].astype(F32))

    ext_ref[0:CONV_HALO] = glu(ap_ref, bp_ref) * keep_prev
    ext_ref[CONV_HALO:CONV_HALO + tm] = glu(a_ref, b_ref)
    ext_ref[CONV_HALO + tm:2 * CONV_HALO + tm] = glu(an_ref, bn_ref) * keep_next
    off = CONV_HALO - CONV_W // 2
    span = tm + 8 * ((off + CONV_W - 1) // 8)
    for s in range(1, 8):
        shift_ref[s - 1, 0:span] = ext_ref[s:s + span]
    for cj in range(CONV_C // 128):
        cs = slice(cj * 128, (cj + 1) * 128)
        acc = jnp.broadcast_to(cb_ref[:, cs], (tm, 128))
        for k in range(CONV_W):
            a, s = divmod(off + k, 8)
            src = ext_ref if s == 0 else shift_ref.at[s - 1]
            acc = acc + cw_ref[k:k + 1, cs] * src[8 * a:8 * a + tm, cs]
        conv_ref[:, cs] = acc
    acc = conv_ref[...]
    mu = jnp.mean(acc, axis=-1, keepdims=True)
    xc = acc - mu
    var = jnp.mean(xc * xc, axis=-1, keepdims=True)
    u = xc * lax.rsqrt(var + EPS) * lng_ref[...] + lnb_ref[...]
    o_ref[:, HG_V:] = _silu(u).astype(BF16)


def _ev_readout(p, of, ob, gain, cw, cb, lng, lnb, geo):
    tm = _pick_tile(geo, (128,))
    hb = tm // CONV_HALO
    nhalo = geo.T // CONV_HALO

    def row(colblk):
        return pl.BlockSpec((tm, HG_V), lambda i: (i, colblk))

    def prev(colblk):
        return pl.BlockSpec((CONV_HALO, CONV_C), lambda i: (jnp.maximum(i * hb - 1, 0), colblk))

    def nxt(colblk):
        return pl.BlockSpec((CONV_HALO, CONV_C), lambda i: (jnp.minimum((i + 1) * hb, nhalo - 1), colblk))

    def vec(n):
        return pl.BlockSpec((n, CONV_C), lambda i: (0, 0))

    return pl.pallas_call(
        functools.partial(_ev_readout_kernel, tm=tm, geo=geo),
        name="ev_readout",
        out_shape=jax.ShapeDtypeStruct((geo.T, HG_V + CONV_C), BF16),
        grid=(geo.T // tm,),
        in_specs=[row(0), row(0), row(4), row(5), row(6), prev(5), prev(6), nxt(5), nxt(6),
                  vec(1), vec(CONV_W), vec(1), vec(1), vec(1)],
        out_specs=pl.BlockSpec((tm, HG_V + CONV_C), lambda i: (i, 0)),
        scratch_shapes=[pltpu.VMEM((tm + 2 * CONV_HALO, CONV_C), F32), pltpu.VMEM((tm, CONV_C), F32),
                        pltpu.VMEM((7, tm + 2 * CONV_HALO, CONV_C), F32)],
        compiler_params=_params("arbitrary"),
    )(of, ob, p, p, p, p, p, p, p, gain.reshape(1, -1), cw, cb.reshape(1, -1),
      lng.reshape(1, -1), lnb.reshape(1, -1))


def _rope(x, cos, sin_signed):
    half = x.shape[-1] // 2
    rot = jnp.concatenate([pltpu.roll(x[:, :half], half // 2, axis=1),
                           pltpu.roll(x[:, half:], half // 2, axis=1)], axis=-1)
    return x * cos + rot * sin_signed


def _ret_kernel(qf_ref, kf_ref, vf_ref, cosf_ref, sinf_ref, qb_ref, kb_ref, vb_ref, cosb_ref, sinb_ref,
                dl_ref, of_ref, ob_ref, sf_ref, sb_ref):
    i = pl.program_id(1)

    @pl.when(i == 0)
    def _():
        sf_ref[...] = jnp.zeros_like(sf_ref)
        sb_ref[...] = jnp.zeros_like(sb_ref)

    _ret_chunk(qf_ref, kf_ref, vf_ref, cosf_ref, sinf_ref, dl_ref[0], of_ref, sf_ref, reverse=False)
    _ret_chunk(qb_ref, kb_ref, vb_ref, cosb_ref, sinb_ref, dl_ref[1], ob_ref, sb_ref, reverse=True)


def _ret_chunk(q_ref, k_ref, v_ref, cos_ref, sin_ref, dl, o_ref, s_ref, *, reverse):
    c = RET_CHUNK
    lg_all = -jnp.log1p(jnp.exp(-dl))
    row = lax.broadcasted_iota(jnp.int32, (c, c), 0)
    col = lax.broadcasted_iota(jnp.int32, (c, c), 1)
    idx = lax.broadcasted_iota(jnp.int32, (c, 1), 0).astype(F32)
    if reverse:
        live = col >= row
        dist = (col - row).astype(F32)
        q_pow = c - idx
        k_pow = idx
    else:
        live = row >= col
        dist = (row - col).astype(F32)
        q_pow = idx + 1.0
        k_pow = c - 1.0 - idx
    cos = cos_ref[...]
    sin = sin_ref[...]
    nt = (((1,), (1,)), ((), ()))

    for h in range(RET_HEADS):
        lg = lg_all[h:h + 1]
        dmask = jnp.where(live, jnp.exp(lg * dist), 0.0)
        q = _rope(q_ref[:, h * RET_DK:(h + 1) * RET_DK].astype(F32), cos, sin)
        k = _rope(k_ref[:, h * RET_DK:(h + 1) * RET_DK].astype(F32) * (RET_DK ** -0.5), cos, sin)
        vb = v_ref[:, h * RET_DV:(h + 1) * RET_DV]
        scores = lax.dot_general(q.astype(BF16), k.astype(BF16), nt, preferred_element_type=F32) * dmask
        intra = jnp.dot(scores.astype(BF16), vb, preferred_element_type=F32)
        s = s_ref[h]
        inter = jnp.dot((q * jnp.exp(lg * q_pow)).astype(BF16), s.astype(BF16),
                        preferred_element_type=F32)
        o_ref[:, h * RET_DV:(h + 1) * RET_DV] = (inter + intra).astype(BF16)
        kdec = (k * jnp.exp(lg * k_pow)).T.astype(BF16)
        s_ref[h] = jnp.exp(lg * c) * s + jnp.dot(kdec, vb, preferred_element_type=F32)


def _ret_scan(p, cos_tab, sin_tab, decay_logit, geo):
    c = RET_CHUNK
    steps = (geo.Lc + geo.N) // c
    nc = geo.Lc // c
    nl = geo.N // c

    def spec(width, colblk, reverse):
        return pl.BlockSpec((c, width), lambda b, i: (_chunk_index(b, i, c, geo, reverse), colblk))

    def tab(reverse):
        def index(b, i):
            lat = (nl - 1 - (i - nc)) if reverse else (i - nc)
            return (jnp.where(i < nc, nl, lat), 0)
        return pl.BlockSpec((c, RET_DK), index)

    def direction(reverse):
        return [spec(RET_QK, 0, reverse), spec(RET_QK, 1, reverse), spec(RET_V, 1, reverse),
                tab(reverse), tab(reverse)]

    out = jax.ShapeDtypeStruct((geo.T, RET_V), BF16)
    state = pltpu.VMEM((RET_HEADS, RET_DK, RET_DV), F32)
    return pl.pallas_call(
        _ret_kernel,
        name="ret_scan",
        out_shape=[out, out],
        grid=(geo.B, steps),
        in_specs=direction(False) + direction(True)
        + [pl.BlockSpec((2, RET_HEADS, 1), lambda b, i: (0, 0, 0))],
        out_specs=[spec(RET_V, 0, False), spec(RET_V, 0, True)],
        scratch_shapes=[state, state],
        compiler_params=_params("arbitrary", "arbitrary"),
    )(p, p, p, cos_tab, sin_tab, p, p, p, cos_tab, sin_tab, decay_logit.reshape(2, RET_HEADS, 1))


def _rope_tables(n):
    t = jnp.arange(n)
    quarter = RET_DK // 4
    inv = 1.0 / (ROPE_BASE ** (jnp.arange(quarter, dtype=F32) / quarter))
    ang_r = (t // GRID_W).astype(F32)[:, None] * inv
    ang_c = (t % GRID_W).astype(F32)[:, None] * inv
    cos = jnp.concatenate([jnp.cos(ang_r), jnp.cos(ang_r), jnp.cos(ang_c), jnp.cos(ang_c)], axis=-1)
    sin = jnp.concatenate([-jnp.sin(ang_r), jnp.sin(ang_r), -jnp.sin(ang_c), jnp.sin(ang_c)], axis=-1)
    cos = jnp.concatenate([cos, jnp.ones((RET_CHUNK, RET_DK), F32)], axis=0)
    sin = jnp.concatenate([sin, jnp.zeros((RET_CHUNK, RET_DK), F32)], axis=0)
    return cos, sin


def _ret_readout_kernel(of_ref, ob_ref, gate_ref, o_ref):
    for h in range(RET_HEADS):
        sl = slice(h * RET_DV, (h + 1) * RET_DV)
        o = of_ref[:, sl].astype(F32) + ob_ref[:, sl].astype(F32)
        r = o * lax.rsqrt(jnp.mean(o * o, axis=-1, keepdims=True) + EPS)
        o_ref[:, sl] = (_silu(gate_ref[:, sl].astype(F32)) * r).astype(BF16)


def _ret_readout(p, of, ob, geo):
    tm = _pick_tile(geo, (256, 128))
    spec = pl.BlockSpec((tm, RET_V), lambda i: (i, 0))
    return pl.pallas_call(
        _ret_readout_kernel,
        name="ret_readout",
        out_shape=jax.ShapeDtypeStruct((geo.T, RET_V), BF16),
        grid=(geo.T // tm,),
        in_specs=[spec, spec, pl.BlockSpec((tm, RET_V), lambda i: (i, 2))],
        out_specs=spec,
        compiler_params=_params("arbitrary"),
    )(of, ob, p)


def _router_kernel(h_ref, rw_ref, rb_ref, eidx_ref, rank_ref, w_ref, cnt_ref, carry_ref, *, tm):
    i = pl.program_id(0)

    @pl.when(i == 0)
    def _():
        carry_ref[...] = jnp.zeros_like(carry_ref)

    half = D_MODEL // 2
    h_lo, h_hi = _unpack_rows(h_ref[...])
    h_lo = h_lo.astype(BF16)
    h_hi = h_hi.astype(BF16)
    rest = rw_ref[...]
    logits = jnp.zeros((tm, N_EXPERTS), F32)
    for _ in range(3):
        part = rest.astype(BF16)
        rest = rest - part.astype(F32)
        logits = (logits + jnp.dot(h_lo, part[:half], preferred_element_type=F32)
                  + jnp.dot(h_hi, part[half:], preferred_element_type=F32))
    s = _sigmoid(logits)
    sel = s + rb_ref[...]
    lane = lax.broadcasted_iota(jnp.int32, (tm, N_EXPERTS), 1).astype(F32)
    grp = jnp.floor(lane * (1.0 / GROUP_SIZE))
    ninf = -jnp.inf
    none = float(N_EXPERTS)

    gscore = jnp.zeros((tm, N_EXPERTS), F32)
    gcols = []
    for g in range(N_GROUPS):
        in_g = grp == float(g)
        v1 = jnp.max(jnp.where(in_g, sel, ninf), axis=-1, keepdims=True)
        i1 = jnp.min(jnp.where(in_g & (sel == v1), lane, none), axis=-1, keepdims=True)
        v2 = jnp.max(jnp.where(in_g & (lane != i1), sel, ninf), axis=-1, keepdims=True)
        gcols.append(v1 + v2)
        gscore = jnp.where(in_g, v1 + v2, gscore)
    beaten = jnp.zeros((tm, N_EXPERTS), F32)
    for g in range(N_GROUPS):
        wins = (gcols[g] > gscore) | ((gcols[g] == gscore) & (float(g) < grp))
        beaten = beaten + jnp.where(wins, 1.0, 0.0)
    cand = jnp.where(beaten < float(TOPK_GROUPS), sel, ninf)

    lane_k = lax.broadcasted_iota(jnp.int32, (tm, TOP_K), 1)
    eidx = jnp.zeros((tm, TOP_K), F32)
    wsel = jnp.zeros((tm, TOP_K), F32)
    chosen = jnp.zeros((tm, N_EXPERTS), F32)
    picks = []
    for k in range(TOP_K):
        v = jnp.max(cand, axis=-1, keepdims=True)
        ik = jnp.min(jnp.where(cand == v, lane, none), axis=-1, keepdims=True)
        hit = lane == ik
        picks.append(ik)
        eidx = jnp.where(lane_k == k, ik, eidx)
        wsel = jnp.where(lane_k == k, jnp.sum(jnp.where(hit, s, 0.0), axis=-1, keepdims=True), wsel)
        chosen = jnp.where(hit, 1.0, chosen)
        cand = jnp.where(hit, ninf, cand)
    w_ref[...] = wsel / jnp.sum(wsel, axis=-1, keepdims=True) * ROUTED_SCALE
    eidx_ref[...] = eidx.astype(jnp.int32)

    r = lax.broadcasted_iota(jnp.int32, (tm, tm), 0)
    c = lax.broadcasted_iota(jnp.int32, (tm, tm), 1)
    below = jnp.where(c < r, 1.0, 0.0).astype(BF16)
    carry = carry_ref[...]
    pos = jnp.dot(below, chosen.astype(BF16), preferred_element_type=F32) + carry
    rank = jnp.zeros((tm, TOP_K), jnp.int32)
    for k in range(TOP_K):
        rk = jnp.sum(jnp.where(lane == picks[k], pos, 0.0), axis=-1, keepdims=True)
        rank = jnp.where(lane_k == k, rk.astype(jnp.int32), rank)
    rank_ref[...] = rank
    carry = carry + jnp.sum(chosen, axis=0, keepdims=True)
    carry_ref[...] = carry
    cnt_ref[...] = carry


def _router(h_packed, rw, rb, geo):
    tm = 256 if geo.T % 256 == 0 else 128
    tok = pl.BlockSpec((tm, TOP_K), lambda i: (i, 0))
    one = pl.BlockSpec((1, N_EXPERTS), lambda i: (0, 0))
    return pl.pallas_call(
        functools.partial(_router_kernel, tm=tm),
        name="router",
        out_shape=[jax.ShapeDtypeStruct((geo.T, TOP_K), jnp.int32),
                   jax.ShapeDtypeStruct((geo.T, TOP_K), jnp.int32),
                   jax.ShapeDtypeStruct((geo.T, TOP_K), F32),
                   jax.ShapeDtypeStruct((1, N_EXPERTS), F32)],
        grid=(geo.T // tm,),
        in_specs=[pl.BlockSpec((tm, D_MODEL // 2), lambda i: (i, 0)),
                  pl.BlockSpec((D_MODEL, N_EXPERTS), lambda i: (0, 0)), one],
        out_specs=[tok, tok, tok, one],
        scratch_shapes=[pltpu.VMEM((1, N_EXPERTS), F32)],
        compiler_params=_params("arbitrary"),
    )(h_packed, rw, rb.reshape(1, N_EXPERTS))


def _n_blocks(geo):
    return -(-(geo.T * TOP_K) // MOE_BLOCK) + N_EXPERTS


def _dest_kernel(cnt_ref, eidx_ref, rank_ref, dest_ref, blk_ref, *, n_blocks):
    eidx = eidx_ref[...]
    dest = rank_ref[...]
    blk_row = (lax.broadcasted_iota(jnp.int32, blk_ref.shape, 0) * 128
               + lax.broadcasted_iota(jnp.int32, blk_ref.shape, 1)) * MOE_BLOCK
    blk = jnp.zeros(blk_ref.shape, jnp.int32)
    start = jnp.int32(0)
    for e in range(N_EXPERTS):
        padded = (cnt_ref[e] + (MOE_BLOCK - 1)) // MOE_BLOCK * MOE_BLOCK
        dest = dest + jnp.where(eidx == e, start, 0)
        start = start + padded
        blk = blk + jnp.where(start <= blk_row, 1, 0)
    dest_ref[...] = dest
    blk_ref[...] = jnp.where(blk_row == n_blocks * MOE_BLOCK, start // MOE_BLOCK,
                             jnp.minimum(blk, N_EXPERTS - 1))


def _dest(counts, eidx, rank, geo):
    rows = geo.T * TOP_K // 128
    brow = -(-(_n_blocks(geo) + 1) // 128)
    full = pl.BlockSpec((rows, 128), lambda: (0, 0))
    dest, blk = pl.pallas_call(
        functools.partial(_dest_kernel, n_blocks=_n_blocks(geo)),
        name="dest",
        out_shape=[jax.ShapeDtypeStruct((rows, 128), jnp.int32),
                   jax.ShapeDtypeStruct((brow, 128), jnp.int32)],
        in_specs=[pl.BlockSpec(memory_space=pltpu.SMEM), full, full],
        out_specs=[full, pl.BlockSpec((brow, 128), lambda: (0, 0))],
    )(counts, eidx.reshape(rows, 128), rank.reshape(rows, 128))
    return dest.reshape(-1), blk.reshape(-1)[:_n_blocks(geo) + 1]


def _zero_fill(cnt_ref, xs_hbm, zero_ref, zsem, n_rows, wait):
    def piece(pos, size):
        if size >= 8:
            copies = [(pl.multiple_of(pos, 8), size)]
        else:
            copies = [(pos + r, 1) for r in range(size)]
        for p, s in copies:
            cp = pltpu.make_async_copy(zero_ref.at[pl.ds(0, s)], xs_hbm.at[pl.ds(p, s)], zsem)
            cp.wait() if wait else cp.start()

    def per_expert(e, start):
        cnt = cnt_ref[e]
        padded = (cnt + (MOE_BLOCK - 1)) // MOE_BLOCK * MOE_BLOCK
        pad = padded - cnt
        pos = start + cnt
        size = 1
        while size < MOE_BLOCK:
            take = (pad & size) != 0
            pl.when(take)(functools.partial(piece, pos, size))
            pos = pos + jnp.where(take, size, 0)
            size *= 2
        return start + padded

    end = lax.fori_loop(0, N_EXPERTS, per_expert, jnp.int32(0))

    def per_block(j, carry):
        piece(end + j * MOE_BLOCK, MOE_BLOCK)
        return carry

    lax.fori_loop(0, (n_rows - end) // MOE_BLOCK, per_block, 0)


def _sc_scatter_rows(src, order, n_rows):
    nw = V7X_SC_CORES * V7X_SC_SUBCORES
    rows = 2 * SC_ROWS
    t, w = src.shape
    per = t // nw
    assert t % (nw * rows) == 0
    mesh = plsc.VectorSubcoreMesh(core_axis_name="c", subcore_axis_name="s")

    @functools.partial(
        pl.kernel, mesh=mesh,
        out_type=jax.ShapeDtypeStruct((n_rows, w), src.dtype),
        scratch_types=[pltpu.VMEM((rows, w), src.dtype)]
        + [pltpu.VMEM((rows,), jnp.int32)] * TOP_K + [pltpu.SemaphoreType.DMA],
    )
    def scatter(src_hbm, idx_hbm, out_hbm, rows_v, *rest):
        idx_vs, sem = rest[:TOP_K], rest[TOP_K]
        wid = lax.axis_index("s") * V7X_SC_CORES + lax.axis_index("c")
        base = wid * per

        @pl.loop(0, per // rows)
        def _(j):
            off = pl.multiple_of(base + j * rows, 8)
            pltpu.sync_copy(src_hbm.at[pl.ds(off, rows)], rows_v)
            for k in range(TOP_K):
                pltpu.sync_copy(idx_hbm.at[pl.ds(pl.multiple_of(k * t + off, 8), rows)], idx_vs[k])
            copies = [pltpu.async_copy(rows_v, out_hbm.at[idx_vs[k]], sem) for k in range(TOP_K)]
            for cp in copies:
                cp.wait()

    return scatter(src, order)


def _zero_pad_kernel(cnt_ref, xs_in, xs_hbm, zero_ref, zsem, *, n_rows):
    del xs_in
    zero_ref[...] = jnp.zeros_like(zero_ref)
    _zero_fill(cnt_ref, xs_hbm, zero_ref, zsem, n_rows, wait=False)
    _zero_fill(cnt_ref, xs_hbm, zero_ref, zsem, n_rows, wait=True)


def _dispatch(counts, order, h_packed, geo):
    n_rows = _n_blocks(geo) * MOE_BLOCK
    width = D_MODEL // 2
    xs = _sc_scatter_rows(h_packed[:geo.T], order, n_rows)
    return pl.pallas_call(
        functools.partial(_zero_pad_kernel, n_rows=n_rows),
        name="zero_pad",
        out_shape=jax.ShapeDtypeStruct((n_rows, width), PACKED),
        in_specs=[pl.BlockSpec(memory_space=pltpu.SMEM), pl.BlockSpec(memory_space=pl.ANY)],
        out_specs=pl.BlockSpec(memory_space=pl.ANY),
        scratch_shapes=[pltpu.VMEM((MOE_BLOCK, width), PACKED), pltpu.SemaphoreType.DMA],
        input_output_aliases={1: 0},
    )(counts, xs)


def _swiglu_packed(x_ref, wg_ref, wu_ref, wd_ref):
    half = D_MODEL // 2
    lo, hi = _unpack_rows(x_ref[...])
    lo = lo.astype(BF16)
    hi = hi.astype(BF16)

    def proj(w_ref):
        return (jnp.dot(lo, w_ref[:half], preferred_element_type=F32)
                + jnp.dot(hi, w_ref[half:], preferred_element_type=F32))

    g = proj(wg_ref)
    u = proj(wu_ref)
    return jnp.dot((_silu(g) * u).astype(BF16), wd_ref[...], preferred_element_type=F32)


def _expert_kernel(blk_ref, x_ref, wg_ref, wu_ref, wd_ref, o_ref, wgb_ref, wub_ref, wdb_ref, *, nb):
    j = pl.program_id(0)
    n_used = blk_ref[nb]

    @pl.when((j == 0) | (blk_ref[j] != blk_ref[jnp.maximum(j - 1, 0)]))
    def _():
        wgb_ref[...] = wg_ref[...].astype(BF16)
        wub_ref[...] = wu_ref[...].astype(BF16)
        wdb_ref[...] = wd_ref[...].astype(BF16)

    @pl.when(j < n_used)
    def _():
        o_ref[...] = _pack_rows(_swiglu_packed(x_ref, wgb_ref, wub_ref, wdb_ref))

    @pl.when(j >= n_used)
    def _():
        o_ref[...] = jnp.zeros_like(o_ref)


def _experts(blk_e, xs, wg, wu, wd, layer, geo):
    nb = _n_blocks(geo)
    rows = pl.BlockSpec((MOE_BLOCK, D_MODEL // 2), lambda j, be: (j, 0))

    def wspec(r, c):
        return pl.BlockSpec((None, None, r, c), lambda j, be: (layer, be[j], 0, 0))

    return pl.pallas_call(
        functools.partial(_expert_kernel, nb=nb),
        name="experts",
        out_shape=jax.ShapeDtypeStruct((nb * MOE_BLOCK, D_MODEL // 2), PACKED),
        grid_spec=pltpu.PrefetchScalarGridSpec(
            num_scalar_prefetch=1,
            grid=(nb,),
            in_specs=[rows, wspec(D_MODEL, EXPERT_FF), wspec(D_MODEL, EXPERT_FF),
                      wspec(EXPERT_FF, D_MODEL)],
            out_specs=rows,
            scratch_shapes=[pltpu.VMEM((D_MODEL, EXPERT_FF), BF16),
                            pltpu.VMEM((D_MODEL, EXPERT_FF), BF16),
                            pltpu.VMEM((EXPERT_FF, D_MODEL), BF16)]),
        compiler_params=_params("arbitrary"),
    )(blk_e, xs, wg, wu, wd)


def _sc_gather_rows(table, idx):
    nw = V7X_SC_CORES * V7X_SC_SUBCORES
    m = idx.shape[0]
    w = table.shape[1]
    per = m // nw
    n = per // SC_ROWS
    assert m % (nw * SC_ROWS * 2) == 0
    mesh = plsc.VectorSubcoreMesh(core_axis_name="c", subcore_axis_name="s")

    @functools.partial(
        pl.kernel, mesh=mesh,
        out_type=jax.ShapeDtypeStruct((m, w), table.dtype),
        scratch_types=[pltpu.VMEM((SC_ROWS,), jnp.int32)] * 2
        + [pltpu.VMEM((SC_ROWS, w), table.dtype)] * 2 + [pltpu.SemaphoreType.DMA] * 2,
    )
    def gather(table_hbm, idx_hbm, out_hbm, *scratch):
        idx_vs, rows_vs, sems = scratch[:2], scratch[2:4], scratch[4:]
        wid = lax.axis_index("s") * V7X_SC_CORES + lax.axis_index("c")
        base = wid * per

        def fetch(b):
            return pltpu.make_async_copy(table_hbm.at[idx_vs[b]], rows_vs[b], sems[b])

        def start(chunk, b):
            off = pl.multiple_of(base + chunk * SC_ROWS, 8)
            pltpu.sync_copy(idx_hbm.at[pl.ds(off, SC_ROWS)], idx_vs[b])
            fetch(b).start()

        start(0, 0)

        @pl.loop(0, n, step=2)
        def _(j):
            for b in range(2):
                cur = j + b
                pl.when(cur + 1 < n)(functools.partial(start, cur + 1, 1 - b))
                fetch(b).wait()
                off = pl.multiple_of(base + cur * SC_ROWS, 8)
                pltpu.sync_copy(rows_vs[b], out_hbm.at[pl.ds(off, SC_ROWS)])

    return gather(table, idx)


def _combine_kernel(w_ref, x_ref, h_ref, sg_ref, su_ref, sd_ref, m_ref, *rest, post, n_prev):
    rest = rest[:-2 * n_prev] + rest[-n_prev:] if n_prev else rest
    if post == "next":
        ng_ref, nsh_ref, nsc_ref, g_ref, o_ref, hn_ref = rest
    else:
        ng_ref, g_ref, o_ref = rest
    half = D_MODEL // 2
    shared = _swiglu_packed(h_ref, sg_ref, su_ref, sd_ref)
    w = w_ref[...]
    acc_lo = shared[:, :half]
    acc_hi = shared[:, half:]
    for k in range(TOP_K):
        lo, hi = _unpack_rows(g_ref[k])
        acc_lo = acc_lo + w[:, k:k + 1] * lo
        acc_hi = acc_hi + w[:, k:k + 1] * hi
    gate = m_ref[0]
    y_lo = x_ref[:, :half] + gate[:, :half] * acc_lo
    y_hi = x_ref[:, half:] + gate[:, half:] * acc_hi
    ms = (jnp.sum(y_lo * y_lo, axis=-1, keepdims=True)
          + jnp.sum(y_hi * y_hi, axis=-1, keepdims=True)) * (1.0 / D_MODEL)
    inv = lax.rsqrt(ms + EPS)
    ng = ng_ref[...]
    n_lo = y_lo * inv * ng[:, :half]
    n_hi = y_hi * inv * ng[:, half:]
    if post == "next":
        o_ref[:, :half] = y_lo
        o_ref[:, half:] = y_hi
        sc = nsc_ref[0]
        sft = nsh_ref[0]
        hn_ref[:, :half] = (n_lo * (1.0 + sc[:, :half]) + sft[:, :half]).astype(BF16)
        hn_ref[:, half:] = (n_hi * (1.0 + sc[:, half:]) + sft[:, half:]).astype(BF16)
    else:
        o_ref[:, :half] = n_lo
        o_ref[:, half:] = n_hi


def _combine(dest, w, x, h_packed, sg, su, sd, ys, mod3, m_gate, post, norm_g, next_mod3, geo):
    half = D_MODEL // 2
    tt = _pick_tile(geo, (256, 128))
    ff = sg.shape[1]
    unit = math.lcm(tt, V7X_SC_CORES * V7X_SC_SUBCORES * SC_ROWS * 2 // TOP_K)
    cut = (geo.T // 2) // unit * unit
    parts = [(0, cut), (cut, geo.T)] if 0 < cut < geo.T else [(0, geo.T)]
    dest2 = dest.reshape(geo.T, TOP_K)
    outs = []
    for t0, t1 in parts:
        tile0, n = t0 // tt, (t1 - t0) // tt
        order = dest2[t0:t1].T.reshape(-1)
        gathered = _sc_gather_rows(ys, order).reshape(TOP_K, t1 - t0, half)
        rows = pl.BlockSpec((tt, D_MODEL), lambda i: (i + tile0, 0))
        vec = pl.BlockSpec((1, D_MODEL), lambda i: (0, 0))

        def mod(m):
            return pl.BlockSpec((1, 1, D_MODEL),
                                lambda i: (_mod_row(i + tile0, tt, geo) * N_MOD + m, 0, 0))

        in_specs = [pl.BlockSpec((tt, TOP_K), lambda i: (i + tile0, 0)),
                    rows,
                    pl.BlockSpec((tt, half), lambda i: (i + tile0, 0)),
                    pl.BlockSpec((D_MODEL, ff), lambda i: (0, 0)),
                    pl.BlockSpec((D_MODEL, ff), lambda i: (0, 0)),
                    pl.BlockSpec((ff, D_MODEL), lambda i: (0, 0)),
                    mod(m_gate), vec]
        args = [w, x, h_packed, sg, su, sd, mod3, norm_g.reshape(1, D_MODEL)]
        out_shape = [jax.ShapeDtypeStruct((geo.T, D_MODEL), F32)]
        out_specs = [rows]
        if post == "next":
            in_specs += [mod(0), mod(1)]
            args += [next_mod3, next_mod3]
            out_shape.append(jax.ShapeDtypeStruct((geo.T, D_MODEL), BF16))
            out_specs.append(rows)
        in_specs.append(pl.BlockSpec((TOP_K, tt, half), lambda i: (0, i, 0)))
        args.append(gathered)
        aliases = {len(args) + j: j for j in range(len(outs))}
        in_specs += [pl.BlockSpec(memory_space=pl.ANY)] * len(outs)
        args += outs
        outs = pl.pallas_call(
            functools.partial(_combine_kernel, post=post, n_prev=len(outs)),
            name="combine",
            out_shape=out_shape,
            grid=(n,),
            in_specs=in_specs,
            out_specs=out_specs,
            input_output_aliases=aliases,
            compiler_params=_params("arbitrary"),
        )(*args)
    return outs


def _moe(x, h_packed, mod3, rw, rb, wg, wu, wd, layer, sg, su, sd, post, norm_g, next_mod3, geo):
    eidx, rank, w, counts = _router(h_packed, rw, rb, geo)
    counts = counts.reshape(N_EXPERTS).astype(jnp.int32)
    dest, blk_e = _dest(counts, eidx, rank, geo)
    order = dest.reshape(geo.T, TOP_K).T.reshape(-1)
    xs = _dispatch(counts, order, h_packed, geo)
    ys = _experts(blk_e, xs, wg, wu, wd, layer, geo)
    return _combine(dest, w, x, h_packed, sg.astype(BF16), su.astype(BF16), sd.astype(BF16), ys,
                    mod3, 5, post, norm_g, next_mod3, geo)


def kernel(x, c, ctx, c_ctx, ada_w, ada_b, norm_mix, norm_ffn, norm_final, ev_w_in, ev_w_out, hgrn_lb, hgrn_norm, conv_w, conv_b, conv_norm_g, conv_norm_b, ret_w_in, ret_w_out, ret_decay, router_w, router_b, exp_gate, exp_up, exp_down, sh_gate, sh_up, sh_down):
    b, n, d = x.shape
    lc = ctx.shape[1]
    depth = ada_w.shape[0]
    geo = _geo(b, n, lc)
    assert d == D_MODEL and b < MOD_ROWS
    assert n % RET_CHUNK == 0 and lc % RET_CHUNK == 0

    xs = (x.reshape(geo.BN, d), ctx.reshape(geo.BL, d))
    cond = jnp.zeros((MOD_ROWS, d), F32).at[:b].set(c).at[b].set(c_ctx)
    cos_tab, sin_tab = _rope_tables(n)

    mods = [_adaln(cond, ada_w, ada_b, l).reshape(MOD_ROWS * N_MOD, 1, d) for l in range(depth)]
    h = _normmod(xs, norm_mix[0], mods[0], 0, 1, geo, packed=False)
    for l in range(depth):
        j = l // 2
        last = l == depth - 1
        mod3 = mods[l]
        tail = geo._replace(T=geo.BN, BL=0, Lc=0) if last else geo
        if l % 2 == 0:
            p = _matmul(h, ev_w_in[j].astype(BF16), geo)
            of, ob = _hgrn_scan(p, hgrn_lb, l, geo)
            mix = _ev_readout(p, of, ob, hgrn_norm[j], conv_w[j], conv_b[j],
                              conv_norm_g[j], conv_norm_b[j], geo)
            xs = _matmul_resid(mix, ev_w_out[j].astype(BF16), xs, mod3, 2, tail)
        else:
            p = _matmul(h, ret_w_in[j].astype(BF16), geo)
            of, ob = _ret_scan(p, cos_tab, sin_tab, ret_decay[j], geo)
            mix = _ret_readout(p, of, ob, geo)
            xs = _matmul_resid(mix, ret_w_out[j].astype(BF16), xs, mod3, 2, tail)
        h_packed = _normmod(xs, norm_ffn[l], mod3, 3, 4, tail, packed=True)
        moe_w = (router_w[l], router_b[l], exp_gate, exp_up, exp_down, l, sh_gate[l], sh_up[l], sh_down[l])
        if last:
            (out,) = _moe(xs, h_packed, mod3, *moe_w, "final", norm_final, None, tail)
        else:
            xs, h = _moe(xs, h_packed, mod3, *moe_w, "next", norm_mix[l + 1], mods[l + 1], tail)
    return out.reshape(b, n, d)
```

```python
import collections
import functools
import math

import jax
import jax.numpy as jnp
from jax import lax
from jax.experimental import pallas as pl
from jax.experimental.pallas import tpu as pltpu
from jax.experimental.pallas import tpu_sc as plsc

F32 = jnp.float32
BF16 = jnp.bfloat16

D_MODEL = 2048
N_MOD = 6
EPS = 1e-6
GRID_W = 64
ROPE_BASE = 10000.0

HG_HEADS = 8
HG_DK = 128
HG_DV = 128
HG_F = HG_HEADS * HG_DK
HG_V = HG_HEADS * HG_DV
CONV_C = D_MODEL // 2
CONV_W = 31
CONV_HALO = 16
HG_CHUNK = 128

RET_HEADS = 8
RET_DK = D_MODEL // RET_HEADS
RET_DV = 2 * RET_DK
RET_QK = RET_HEADS * RET_DK
RET_V = RET_HEADS * RET_DV
RET_CHUNK = 256

N_EXPERTS = 64
EXPERT_FF = D_MODEL // 4
TOP_K = 8
N_GROUPS = 8
GROUP_SIZE = N_EXPERTS // N_GROUPS
TOPK_GROUPS = 4
ROUTED_SCALE = 2.5
MOE_BLOCK = 512

V7X_SC_CORES = 2
V7X_SC_SUBCORES = 16
SC_ROWS = 32
MOD_ROWS = 16
VMEM_LIMIT = 56 * 1024 * 1024

Geo = collections.namedtuple("Geo", "B N Lc BN BL T")


def _geo(b, n, lc):
    return Geo(b, n, lc, b * n, b * lc, b * n + b * lc)


def _pick_tile(geo, cands):
    for t in cands:
        if geo.N % t == 0 and geo.BL % t == 0:
            return t
    raise ValueError("no row tile fits the sequence lengths")


def _mod_row(i, tm, geo):
    return jnp.where(i < geo.BN // tm, i // (geo.N // tm), geo.B)


def _mod_spec(m, tm, geo, ngrid=1):
    if ngrid == 1:
        return pl.BlockSpec((1, 1, D_MODEL), lambda i: (_mod_row(i, tm, geo) * N_MOD + m, 0, 0))
    return pl.BlockSpec((1, 1, D_MODEL), lambda i, j: (_mod_row(i, tm, geo) * N_MOD + m, 0, j))


def _params(*sem):
    return pltpu.CompilerParams(dimension_semantics=sem, vmem_limit_bytes=VMEM_LIMIT)


def _sigmoid(x):
    return jax.nn.sigmoid(x)


def _silu(x):
    return x * jax.nn.sigmoid(x)


def _adaln_kernel(c_ref, w_ref, b_ref, o_ref):
    a = _silu(c_ref[...]).astype(BF16)
    o_ref[...] = jnp.dot(a, w_ref[...].astype(BF16), preferred_element_type=F32) + b_ref[...]


def _adaln(cond, w, b, layer):
    depth, k, n = w.shape
    tn = 1024
    return pl.pallas_call(
        _adaln_kernel,
        name="adaln",
        out_shape=jax.ShapeDtypeStruct((MOD_ROWS, n), F32),
        grid=(n // tn,),
        in_specs=[pl.BlockSpec((MOD_ROWS, k), lambda j: (0, 0)),
                  pl.BlockSpec((None, k, tn), lambda j: (layer, 0, j)),
                  pl.BlockSpec((None, 1, tn), lambda j: (layer, 0, j))],
        out_specs=pl.BlockSpec((MOD_ROWS, tn), lambda j: (0, j)),
        compiler_params=_params("arbitrary"),
    )(cond, w, b.reshape(depth, 1, n))


PACKED = jnp.int32


def _pack_rows(x):
    n = x.shape[-1] // 2
    bits = lax.bitcast_convert_type(x.astype(BF16).astype(F32), jnp.uint32)
    words = (bits[:, n:] & jnp.uint32(0xFFFF0000)) | (bits[:, :n] >> 16)
    return lax.bitcast_convert_type(words, PACKED)


def _unpack_rows(p):
    u = lax.bitcast_convert_type(p, jnp.uint32)
    lo = lax.bitcast_convert_type(u << 16, F32)
    hi = lax.bitcast_convert_type(u & jnp.uint32(0xFFFF0000), F32)
    return lo, hi


def _stream_specs(stream, tm, width, geo, ngrid):
    col = (lambda j: j) if ngrid == 2 else (lambda *_: 0)
    if not isinstance(stream, tuple):
        return [pl.BlockSpec((tm, width), lambda i, *j: (i, col(*j)))], [stream]
    n_lat = geo.BN // tm
    return ([pl.BlockSpec((tm, width), lambda i, *j: (jnp.minimum(i, n_lat - 1), col(*j))),
             pl.BlockSpec((tm, width), lambda i, *j: (jnp.maximum(i - n_lat, 0), col(*j)))],
            list(stream))


def _stream_tile(refs, tm, geo):
    if len(refs) == 1:
        return refs[0][...]
    return jnp.where(pl.program_id(0) < geo.BN // tm, refs[0][...], refs[1][...])


def _normmod_kernel(*refs, packed, tm, geo):
    g_ref, sh_ref, sc_ref, o_ref = refs[-4:]
    x = _stream_tile(refs[:-4], tm, geo)
    y = x * lax.rsqrt(jnp.mean(x * x, axis=-1, keepdims=True) + EPS) * g_ref[...]
    h = y * (1.0 + sc_ref[0]) + sh_ref[0]
    o_ref[...] = _pack_rows(h) if packed else h.astype(BF16)


def _normmod(x, g, mod3, m_shift, m_scale, geo, packed):
    tm = _pick_tile(geo, (256, 128))
    x_specs, x_args = _stream_specs(x, tm, D_MODEL, geo, 1)
    if packed:
        out_shape = jax.ShapeDtypeStruct((geo.T, D_MODEL // 2), PACKED)
        out_spec = pl.BlockSpec((tm, D_MODEL // 2), lambda i: (i, 0))
    else:
        out_shape = jax.ShapeDtypeStruct((geo.T, D_MODEL), BF16)
        out_spec = pl.BlockSpec((tm, D_MODEL), lambda i: (i, 0))
    return pl.pallas_call(
        functools.partial(_normmod_kernel, packed=packed, tm=tm, geo=geo),
        name="normmod",
        out_shape=out_shape,
        grid=(geo.T // tm,),
        in_specs=x_specs + [pl.BlockSpec((1, D_MODEL), lambda i: (0, 0)),
                            _mod_spec(m_shift, tm, geo), _mod_spec(m_scale, tm, geo)],
        out_specs=out_spec,
        compiler_params=_params("arbitrary"),
    )(*x_args, g.reshape(1, D_MODEL), mod3, mod3)


def _mm_kernel(a_ref, w_ref, o_ref):
    o_ref[...] = jnp.dot(a_ref[...], w_ref[...], preferred_element_type=F32).astype(o_ref.dtype)


def _mm_resid_kernel(a_ref, w_ref, *refs, tm, geo):
    m_ref, o_ref = refs[-2:]
    y = jnp.dot(a_ref[...], w_ref[...], preferred_element_type=F32)
    o_ref[...] = _stream_tile(refs[:-2], tm, geo) + m_ref[0] * y


def _mm_rope_kernel(a_ref, w_ref, cos_ref, sin_ref, o_ref, *, tn):
    y = jnp.dot(a_ref[...], w_ref[...], preferred_element_type=F32)
    j = pl.program_id(1)

    @pl.when(j < 2 * RET_QK // tn)
    def _():
        cos = cos_ref[...]
        sin = sin_ref[...]
        scale = jnp.where(j >= RET_QK // tn, RET_DK ** -0.5, 1.0)
        for h in range(tn // RET_DK):
            sl = slice(h * RET_DK, (h + 1) * RET_DK)
            o_ref[:, sl] = _rope(y[:, sl] * scale, cos, sin).astype(o_ref.dtype)

    @pl.when(j >= 2 * RET_QK // tn)
    def _():
        o_ref[...] = y.astype(o_ref.dtype)


def _matmul(a, w, geo, rope=None):
    k, n = w.shape
    tm = _pick_tile(geo, (1024, 512, 256, 128))
    tn = 1024 if n % 1024 == 0 else 512
    in_specs = [pl.BlockSpec((tm, k), lambda i, j: (i, 0)),
                pl.BlockSpec((k, tn), lambda i, j: (0, j))]
    args = [a, w]
    body = _mm_kernel
    if rope is not None:
        assert RET_QK % tn == 0 and tn % RET_DK == 0
        tab = pl.BlockSpec((tm, RET_DK), lambda i, j: (
            jnp.where(i < geo.BN // tm, i % (geo.N // tm), geo.N // tm), 0))
        in_specs += [tab, tab]
        args += list(rope)
        body = functools.partial(_mm_rope_kernel, tn=tn)
    return pl.pallas_call(
        body,
        name="matmul",
        out_shape=jax.ShapeDtypeStruct((geo.T, n), BF16),
        grid=(geo.T // tm, n // tn),
        in_specs=in_specs,
        out_specs=pl.BlockSpec((tm, tn), lambda i, j: (i, j)),
        compiler_params=_params("arbitrary", "arbitrary"),
    )(*args)


def _matmul_resid(a, w, x, mod3, m_gate, geo):
    k, n = w.shape
    tm = _pick_tile(geo, (1024, 512, 256, 128))
    tn = 512
    x_specs, x_args = _stream_specs(x, tm, tn, geo, 2)
    return pl.pallas_call(
        functools.partial(_mm_resid_kernel, tm=tm, geo=geo),
        name="matmul_resid",
        out_shape=jax.ShapeDtypeStruct((geo.T, n), F32),
        grid=(geo.T // tm, n // tn),
        in_specs=[pl.BlockSpec((tm, k), lambda i, j: (i, 0)),
                  pl.BlockSpec((k, tn), lambda i, j: (0, j))] + x_specs
        + [pl.BlockSpec((1, 1, tn), lambda i, j: (_mod_row(i, tm, geo) * N_MOD + m_gate, 0, j))],
        out_specs=pl.BlockSpec((tm, tn), lambda i, j: (i, j)),
        compiler_params=_params("arbitrary", "arbitrary"),
    )(a, w, *x_args, mod3)


def _chunk_index(b, i, chunk, geo, reverse):
    nc = geo.Lc // chunk
    nl = geo.N // chunk
    ctx0 = (geo.BN + b * geo.Lc) // chunk
    lat0 = (b * geo.N) // chunk
    if reverse:
        return jnp.where(i < nc, ctx0 + (nc - 1 - i), lat0 + (nl - 1 - (i - nc)))
    return jnp.where(i < nc, ctx0 + i, lat0 + (i - nc))


def _split_dot(tri_bf, x):
    hi = x.astype(BF16)
    r1 = x - hi.astype(F32)
    mid = r1.astype(BF16)
    lo = (r1 - mid.astype(F32)).astype(BF16)
    return (jnp.dot(tri_bf, hi, preferred_element_type=F32)
            + jnp.dot(tri_bf, mid, preferred_element_type=F32)
            + jnp.dot(tri_bf, lo, preferred_element_type=F32))


def _hgrn_kernel(qf_ref, ff_ref, vf_ref, qb_ref, fb_ref, vb_ref, lbp_ref, of_ref, ob_ref,
                 stf_ref, stb_ref, *, layer):
    i = pl.program_id(1)

    @pl.when(i == 0)
    def _():
        stf_ref[...] = jnp.zeros_like(stf_ref)
        stb_ref[...] = jnp.zeros_like(stb_ref)

    lbp = lbp_ref[...]
    e = jnp.exp(lbp - jnp.max(lbp, axis=0, keepdims=True))
    sm = e / jnp.sum(e, axis=0, keepdims=True)
    lb = sm[0:1]
    for r in range(1, layer + 1):
        lb = lb + sm[r:r + 1]

    _hgrn_chunk(qf_ref, ff_ref, vf_ref, of_ref, stf_ref, lb, reverse=False)
    _hgrn_chunk(qb_ref, fb_ref, vb_ref, ob_ref, stb_ref, lb, reverse=True)


def _hgrn_chunk(q_ref, f_ref, v_ref, o_ref, st_ref, lb, *, reverse):
    c = HG_CHUNK
    row = lax.broadcasted_iota(jnp.int32, (c, c), 0)
    col = lax.broadcasted_iota(jnp.int32, (c, c), 1)
    tri = (col >= row) if reverse else (col <= row)
    tri_bf = jnp.where(tri, 1.0, 0.0).astype(BF16)
    nt = (((1,), (1,)), ((), ()))

    fg_all = lb + (1.0 - lb) * _sigmoid(f_ref[...].astype(F32))
    bcum_all = _split_dot(tri_bf, jnp.log(fg_all))

    for h in range(HG_HEADS):
        sl = slice(h * HG_DK, (h + 1) * HG_DK)
        qh = _silu(q_ref[:, sl].astype(F32))
        kh = 1.0 - fg_all[:, sl]
        bcum = bcum_all[:, sl]
        bmid = bcum[c // 2:c // 2 + 1]
        bend = bcum[0:1] if reverse else bcum[c - 1:c]
        vb = v_ref[:, sl]
        vh = vb.astype(F32)
        a = (qh * jnp.exp(bcum - bmid)).astype(BF16)
        kd = (kh * jnp.exp(bmid - bcum)).astype(BF16)
        s = lax.dot_general(a, kd, nt, preferred_element_type=F32)
        s = jnp.where(tri, s, 0.0)
        intra = jnp.dot(s.astype(BF16), vb, preferred_element_type=F32)
        st = st_ref[h]
        inter = lax.dot_general((qh * jnp.exp(bcum)).astype(BF16), st.astype(BF16), nt,
                                preferred_element_type=F32)
        o_ref[:, sl] = (inter + intra).astype(BF16)
        kd2 = (kh * jnp.exp(bend - bcum)).astype(BF16)
        st_ref[h] = st * jnp.exp(bend) + jnp.dot(vh.T.astype(BF16), kd2, preferred_element_type=F32)


def _hgrn_scan(p, lb_param, layer, geo):
    c = HG_CHUNK
    steps = (geo.Lc + geo.N) // c

    def spec(colblk, reverse):
        return pl.BlockSpec((c, HG_F), lambda b, i: (_chunk_index(b, i, c, geo, reverse), colblk))

    out = jax.ShapeDtypeStruct((geo.T, HG_V), BF16)
    state = pltpu.VMEM((HG_HEADS, HG_DV, HG_DK), F32)
    return pl.pallas_call(
        functools.partial(_hgrn_kernel, layer=layer),
        name="hgrn_scan",
        out_shape=[out, out],
        grid=(geo.B, steps),
        in_specs=[spec(0, False), spec(1, False), spec(3, False),
                  spec(0, True), spec(2, True), spec(3, True),
                  pl.BlockSpec(lb_param.shape, lambda b, i: (0, 0))],
        out_specs=[spec(0, False), spec(0, True)],
        scratch_shapes=[state, state],
        compiler_params=_params("arbitrary", "arbitrary"),
    )(p, p, p, p, p, p, lb_param)


def _ev_readout_kernel(of_ref, ob_ref, gate_ref, a_ref, b_ref, ap_ref, bp_ref, an_ref, bn_ref,
                       gain_ref, cw_ref, cb_ref, lng_ref, lnb_ref, o_ref, ext_ref, conv_ref, shift_ref,
                       *, tm, geo):
    i = pl.program_id(0)
    n_lat = geo.BN // tm
    tpl = geo.N // tm
    tpc = geo.Lc // tm
    j = jnp.where(i < n_lat, i % tpl, (i - n_lat) % tpc)
    per = jnp.where(i < n_lat, tpl, tpc)
    keep_prev = jnp.where(j == 0, 0.0, 1.0)
    keep_next = jnp.where(j == per - 1, 0.0, 1.0)

    o = of_ref[...].astype(F32) + ob_ref[...].astype(F32)
    r = o * lax.rsqrt(jnp.mean(o * o, axis=-1, keepdims=True) + EPS) * gain_ref[...]
    o_ref[:, :HG_V] = (r * _silu(gate_ref[...].astype(F32))).astype(BF16)

    def glu(x_ref, y_ref):
        return x_ref[...].astype(F32) * _sigmoid(y_ref[...].astype(F32))

    ext_ref[0:CONV_HALO] = glu(ap_ref, bp_ref) * keep_prev
    ext_ref[CONV_HALO:CONV_HALO + tm] = glu(a_ref, b_ref)
    ext_ref[CONV_HALO + tm:2 * CONV_HALO + tm] = glu(an_ref, bn_ref) * keep_next
    off = CONV_HALO - CONV_W // 2
    span = tm + 8 * ((off + CONV_W - 1) // 8)
    for s in range(1, 8):
        shift_ref[s - 1, 0:span] = ext_ref[s:s + span]
    for cj in range(CONV_C // 128):
        cs = slice(cj * 128, (cj + 1) * 128)
        acc = jnp.broadcast_to(cb_ref[:, cs], (tm, 128))
        for k in range(CONV_W):
            a, s = divmod(off + k, 8)
            src = ext_ref if s == 0 else shift_ref.at[s - 1]
            acc = acc + cw_ref[k:k + 1, cs] * src[8 * a:8 * a + tm, cs]
        conv_ref[:, cs] = acc
    acc = conv_ref[...]
    mu = jnp.mean(acc, axis=-1, keepdims=True)
    xc = acc - mu
    var = jnp.mean(xc * xc, axis=-1, keepdims=True)
    u = xc * lax.rsqrt(var + EPS) * lng_ref[...] + lnb_ref[...]
    o_ref[:, HG_V:] = _silu(u).astype(BF16)


def _ev_readout(p, of, ob, gain, cw, cb, lng, lnb, geo):
    tm = _pick_tile(geo, (128,))
    hb = tm // CONV_HALO
    nhalo = geo.T // CONV_HALO

    def row(colblk):
        return pl.BlockSpec((tm, HG_V), lambda i: (i, colblk))

    def prev(colblk):
        return pl.BlockSpec((CONV_HALO, CONV_C), lambda i: (jnp.maximum(i * hb - 1, 0), colblk))

    def nxt(colblk):
        return pl.BlockSpec((CONV_HALO, CONV_C), lambda i: (jnp.minimum((i + 1) * hb, nhalo - 1), colblk))

    def vec(n):
        return pl.BlockSpec((n, CONV_C), lambda i: (0, 0))

    return pl.pallas_call(
        functools.partial(_ev_readout_kernel, tm=tm, geo=geo),
        name="ev_readout",
        out_shape=jax.ShapeDtypeStruct((geo.T, HG_V + CONV_C), BF16),
        grid=(geo.T // tm,),
        in_specs=[row(0), row(0), row(4), row(5), row(6), prev(5), prev(6), nxt(5), nxt(6),
                  vec(1), vec(CONV_W), vec(1), vec(1), vec(1)],
        out_specs=pl.BlockSpec((tm, HG_V + CONV_C), lambda i: (i, 0)),
        scratch_shapes=[pltpu.VMEM((tm + 2 * CONV_HALO, CONV_C), F32), pltpu.VMEM((tm, CONV_C), F32),
                        pltpu.VMEM((7, tm + 2 * CONV_HALO, CONV_C), F32)],
        compiler_params=_params("arbitrary"),
    )(of, ob, p, p, p, p, p, p, p, gain.reshape(1, -1), cw, cb.reshape(1, -1),
      lng.reshape(1, -1), lnb.reshape(1, -1))


def _rope(x, cos, sin_signed):
    half = x.shape[-1] // 2
    rot = jnp.concatenate([pltpu.roll(x[:, :half], half // 2, axis=1),
                           pltpu.roll(x[:, half:], half // 2, axis=1)], axis=-1)
    return x * cos + rot * sin_signed


def _ret_kernel(qf_ref, kf_ref, vf_ref, qb_ref, kb_ref, vb_ref, dl_ref, of_ref, ob_ref, sf_ref, sb_ref):
    i = pl.program_id(1)

    @pl.when(i == 0)
    def _():
        sf_ref[...] = jnp.zeros_like(sf_ref)
        sb_ref[...] = jnp.zeros_like(sb_ref)

    _ret_chunk(qf_ref, kf_ref, vf_ref, dl_ref[0], of_ref, sf_ref, reverse=False)
    _ret_chunk(qb_ref, kb_ref, vb_ref, dl_ref[1], ob_ref, sb_ref, reverse=True)


def _ret_chunk(q_ref, k_ref, v_ref, dl, o_ref, s_ref, *, reverse):
    c = RET_CHUNK
    lg_all = -jnp.log1p(jnp.exp(-dl))
    row = lax.broadcasted_iota(jnp.int32, (c, c), 0)
    col = lax.broadcasted_iota(jnp.int32, (c, c), 1)
    idx = lax.broadcasted_iota(jnp.int32, (c, 1), 0).astype(F32)
    if reverse:
        live = col >= row
        dist = (col - row).astype(F32)
        q_pow = c - idx
        k_pow = idx
    else:
        live = row >= col
        dist = (row - col).astype(F32)
        q_pow = idx + 1.0
        k_pow = c - 1.0 - idx
    nt = (((1,), (1,)), ((), ()))

    for h in range(RET_HEADS):
        lg = lg_all[h:h + 1]
        dmask = jnp.where(live, jnp.exp(lg * dist), 0.0)
        qb = q_ref[:, h * RET_DK:(h + 1) * RET_DK]
        kb = k_ref[:, h * RET_DK:(h + 1) * RET_DK]
        q = qb.astype(F32)
        k = kb.astype(F32)
        vb = v_ref[:, h * RET_DV:(h + 1) * RET_DV]
        scores = lax.dot_general(qb, kb, nt, preferred_element_type=F32) * dmask
        intra = jnp.dot(scores.astype(BF16), vb, preferred_element_type=F32)
        s = s_ref[h]
        inter = jnp.dot((q * jnp.exp(lg * q_pow)).astype(BF16), s.astype(BF16),
                        preferred_element_type=F32)
        o_ref[:, h * RET_DV:(h + 1) * RET_DV] = (inter + intra).astype(BF16)
        kdec = (k * jnp.exp(lg * k_pow)).T.astype(BF16)
        s_ref[h] = jnp.exp(lg * c) * s + jnp.dot(kdec, vb, preferred_element_type=F32)


def _ret_scan(p, decay_logit, geo):
    c = RET_CHUNK
    steps = (geo.Lc + geo.N) // c

    def spec(width, colblk, reverse):
        return pl.BlockSpec((c, width), lambda b, i: (_chunk_index(b, i, c, geo, reverse), colblk))

    def direction(reverse):
        return [spec(RET_QK, 0, reverse), spec(RET_QK, 1, reverse), spec(RET_V, 1, reverse)]

    out = jax.ShapeDtypeStruct((geo.T, RET_V), BF16)
    state = pltpu.VMEM((RET_HEADS, RET_DK, RET_DV), F32)
    return pl.pallas_call(
        _ret_kernel,
        name="ret_scan",
        out_shape=[out, out],
        grid=(geo.B, steps),
        in_specs=direction(False) + direction(True)
        + [pl.BlockSpec((2, RET_HEADS, 1), lambda b, i: (0, 0, 0))],
        out_specs=[spec(RET_V, 0, False), spec(RET_V, 0, True)],
        scratch_shapes=[state, state],
        compiler_params=_params("arbitrary", "arbitrary"),
    )(p, p, p, p, p, p, decay_logit.reshape(2, RET_HEADS, 1))


def _rope_tables(n, tail):
    t = jnp.arange(n)
    quarter = RET_DK // 4
    inv = 1.0 / (ROPE_BASE ** (jnp.arange(quarter, dtype=F32) / quarter))
    ang_r = (t // GRID_W).astype(F32)[:, None] * inv
    ang_c = (t % GRID_W).astype(F32)[:, None] * inv
    cos = jnp.concatenate([jnp.cos(ang_r), jnp.cos(ang_r), jnp.cos(ang_c), jnp.cos(ang_c)], axis=-1)
    sin = jnp.concatenate([-jnp.sin(ang_r), jnp.sin(ang_r), -jnp.sin(ang_c), jnp.sin(ang_c)], axis=-1)
    cos = jnp.concatenate([cos, jnp.ones((tail, RET_DK), F32)], axis=0)
    sin = jnp.concatenate([sin, jnp.zeros((tail, RET_DK), F32)], axis=0)
    return cos, sin


def _ret_readout_kernel(of_ref, ob_ref, gate_ref, o_ref):
    for h in range(RET_HEADS):
        sl = slice(h * RET_DV, (h + 1) * RET_DV)
        o = of_ref[:, sl].astype(F32) + ob_ref[:, sl].astype(F32)
        r = o * lax.rsqrt(jnp.mean(o * o, axis=-1, keepdims=True) + EPS)
        o_ref[:, sl] = (_silu(gate_ref[:, sl].astype(F32)) * r).astype(BF16)


def _ret_readout(p, of, ob, geo):
    tm = _pick_tile(geo, (256, 128))
    spec = pl.BlockSpec((tm, RET_V), lambda i: (i, 0))
    return pl.pallas_call(
        _ret_readout_kernel,
        name="ret_readout",
        out_shape=jax.ShapeDtypeStruct((geo.T, RET_V), BF16),
        grid=(geo.T // tm,),
        in_specs=[spec, spec, pl.BlockSpec((tm, RET_V), lambda i: (i, 2))],
        out_specs=spec,
        compiler_params=_params("arbitrary"),
    )(of, ob, p)


def _router_kernel(h_ref, rw_ref, rb_ref, eidx_ref, rank_ref, w_ref, cnt_ref, carry_ref, *, tm):
    i = pl.program_id(0)

    @pl.when(i == 0)
    def _():
        carry_ref[...] = jnp.zeros_like(carry_ref)

    half = D_MODEL // 2
    h_lo, h_hi = _unpack_rows(h_ref[...])
    h_lo = h_lo.astype(BF16)
    h_hi = h_hi.astype(BF16)
    rest = rw_ref[...]
    logits = jnp.zeros((tm, N_EXPERTS), F32)
    for _ in range(3):
        part = rest.astype(BF16)
        rest = rest - part.astype(F32)
        logits = (logits + jnp.dot(h_lo, part[:half], preferred_element_type=F32)
                  + jnp.dot(h_hi, part[half:], preferred_element_type=F32))
    s = _sigmoid(logits)
    sel = s + rb_ref[...]
    lane = lax.broadcasted_iota(jnp.int32, (tm, N_EXPERTS), 1).astype(F32)
    grp = jnp.floor(lane * (1.0 / GROUP_SIZE))
    ninf = -jnp.inf
    none = float(N_EXPERTS)

    gscore = jnp.zeros((tm, N_EXPERTS), F32)
    gcols = []
    for g in range(N_GROUPS):
        in_g = grp == float(g)
        v1 = jnp.max(jnp.where(in_g, sel, ninf), axis=-1, keepdims=True)
        i1 = jnp.min(jnp.where(in_g & (sel == v1), lane, none), axis=-1, keepdims=True)
        v2 = jnp.max(jnp.where(in_g & (lane != i1), sel, ninf), axis=-1, keepdims=True)
        gcols.append(v1 + v2)
        gscore = jnp.where(in_g, v1 + v2, gscore)
    beaten = jnp.zeros((tm, N_EXPERTS), F32)
    for g in range(N_GROUPS):
        wins = (gcols[g] > gscore) | ((gcols[g] == gscore) & (float(g) < grp))
        beaten = beaten + jnp.where(wins, 1.0, 0.0)
    cand = jnp.where(beaten < float(TOPK_GROUPS), sel, ninf)

    lane_k = lax.broadcasted_iota(jnp.int32, (tm, TOP_K), 1)
    eidx = jnp.zeros((tm, TOP_K), F32)
    wsel = jnp.zeros((tm, TOP_K), F32)
    chosen = jnp.zeros((tm, N_EXPERTS), F32)
    picks = []
    for k in range(TOP_K):
        v = jnp.max(cand, axis=-1, keepdims=True)
        ik = jnp.min(jnp.where(cand == v, lane, none), axis=-1, keepdims=True)
        hit = lane == ik
        picks.append(ik)
        eidx = jnp.where(lane_k == k, ik, eidx)
        wsel = jnp.where(lane_k == k, jnp.sum(jnp.where(hit, s, 0.0), axis=-1, keepdims=True), wsel)
        chosen = jnp.where(hit, 1.0, chosen)
        cand = jnp.where(hit, ninf, cand)
    w_ref[...] = wsel / jnp.sum(wsel, axis=-1, keepdims=True) * ROUTED_SCALE
    eidx_ref[...] = eidx.astype(jnp.int32)

    r = lax.broadcasted_iota(jnp.int32, (tm, tm), 0)
    c = lax.broadcasted_iota(jnp.int32, (tm, tm), 1)
    below = jnp.where(c < r, 1.0, 0.0).astype(BF16)
    carry = carry_ref[...]
    pos = jnp.dot(below, chosen.astype(BF16), preferred_element_type=F32) + carry
    rank = jnp.zeros((tm, TOP_K), jnp.int32)
    for k in range(TOP_K):
        rk = jnp.sum(jnp.where(lane == picks[k], pos, 0.0), axis=-1, keepdims=True)
        rank = jnp.where(lane_k == k, rk.astype(jnp.int32), rank)
    rank_ref[...] = rank
    carry = carry + jnp.sum(chosen, axis=0, keepdims=True)
    carry_ref[...] = carry
    cnt_ref[...] = carry


def _router(h_packed, rw, rb, geo):
    tm = 256 if geo.T % 256 == 0 else 128
    tok = pl.BlockSpec((tm, TOP_K), lambda i: (i, 0))
    one = pl.BlockSpec((1, N_EXPERTS), lambda i: (0, 0))
    return pl.pallas_call(
        functools.partial(_router_kernel, tm=tm),
        name="router",
        out_shape=[jax.ShapeDtypeStruct((geo.T, TOP_K), jnp.int32),
                   jax.ShapeDtypeStruct((geo.T, TOP_K), jnp.int32),
                   jax.ShapeDtypeStruct((geo.T, TOP_K), F32),
                   jax.ShapeDtypeStruct((1, N_EXPERTS), F32)],
        grid=(geo.T // tm,),
        in_specs=[pl.BlockSpec((tm, D_MODEL // 2), lambda i: (i, 0)),
                  pl.BlockSpec((D_MODEL, N_EXPERTS), lambda i: (0, 0)), one],
        out_specs=[tok, tok, tok, one],
        scratch_shapes=[pltpu.VMEM((1, N_EXPERTS), F32)],
        compiler_params=_params("arbitrary"),
    )(h_packed, rw, rb.reshape(1, N_EXPERTS))


def _n_blocks(geo):
    return -(-(geo.T * TOP_K) // MOE_BLOCK) + N_EXPERTS


def _dest_kernel(cnt_ref, eidx_ref, rank_ref, dest_ref, blk_ref, *, n_blocks):
    eidx = eidx_ref[...]
    dest = rank_ref[...]
    blk_row = (lax.broadcasted_iota(jnp.int32, blk_ref.shape, 0) * 128
               + lax.broadcasted_iota(jnp.int32, blk_ref.shape, 1)) * MOE_BLOCK
    blk = jnp.zeros(blk_ref.shape, jnp.int32)
    start = jnp.int32(0)
    for e in range(N_EXPERTS):
        padded = (cnt_ref[e] + (MOE_BLOCK - 1)) // MOE_BLOCK * MOE_BLOCK
        dest = dest + jnp.where(eidx == e, start, 0)
        start = start + padded
        blk = blk + jnp.where(start <= blk_row, 1, 0)
    dest_ref[...] = dest
    blk_ref[...] = jnp.where(blk_row == n_blocks * MOE_BLOCK, start // MOE_BLOCK,
                             jnp.minimum(blk, N_EXPERTS - 1))


def _dest(counts, eidx, rank, geo):
    rows = geo.T * TOP_K // 128
    brow = -(-(_n_blocks(geo) + 1) // 128)
    full = pl.BlockSpec((rows, 128), lambda: (0, 0))
    dest, blk = pl.pallas_call(
        functools.partial(_dest_kernel, n_blocks=_n_blocks(geo)),
        name="dest",
        out_shape=[jax.ShapeDtypeStruct((rows, 128), jnp.int32),
                   jax.ShapeDtypeStruct((brow, 128), jnp.int32)],
        in_specs=[pl.BlockSpec(memory_space=pltpu.SMEM), full, full],
        out_specs=[full, pl.BlockSpec((brow, 128), lambda: (0, 0))],
    )(counts, eidx.reshape(rows, 128), rank.reshape(rows, 128))
    return dest.reshape(-1), blk.reshape(-1)[:_n_blocks(geo) + 1]


def _zero_fill(cnt_ref, xs_hbm, zero_ref, zsem, n_rows, wait):
    def piece(pos, size):
        if size >= 8:
            copies = [(pl.multiple_of(pos, 8), size)]
        else:
            copies = [(pos + r, 1) for r in range(size)]
        for p, s in copies:
            cp = pltpu.make_async_copy(zero_ref.at[pl.ds(0, s)], xs_hbm.at[pl.ds(p, s)], zsem)
            cp.wait() if wait else cp.start()

    def per_expert(e, start):
        cnt = cnt_ref[e]
        padded = (cnt + (MOE_BLOCK - 1)) // MOE_BLOCK * MOE_BLOCK
        pad = padded - cnt
        pos = start + cnt
        size = 1
        while size < MOE_BLOCK:
            take = (pad & size) != 0
            pl.when(take)(functools.partial(piece, pos, size))
            pos = pos + jnp.where(take, size, 0)
            size *= 2
        return start + padded

    end = lax.fori_loop(0, N_EXPERTS, per_expert, jnp.int32(0))

    def per_block(j, carry):
        piece(end + j * MOE_BLOCK, MOE_BLOCK)
        return carry

    lax.fori_loop(0, (n_rows - end) // MOE_BLOCK, per_block, 0)


def _sc_scatter_rows(src, order, n_rows):
    nw = V7X_SC_CORES * V7X_SC_SUBCORES
    rows = 2 * SC_ROWS
    t, w = src.shape
    per = t // nw
    assert t % (nw * rows) == 0
    mesh = plsc.VectorSubcoreMesh(core_axis_name="c", subcore_axis_name="s")

    @functools.partial(
        pl.kernel, mesh=mesh,
        out_type=jax.ShapeDtypeStruct((n_rows, w), src.dtype),
        scratch_types=[pltpu.VMEM((rows, w), src.dtype)]
        + [pltpu.VMEM((rows,), jnp.int32)] * TOP_K + [pltpu.SemaphoreType.DMA],
    )
    def scatter(src_hbm, idx_hbm, out_hbm, rows_v, *rest):
        idx_vs, sem = rest[:TOP_K], rest[TOP_K]
        wid = lax.axis_index("s") * V7X_SC_CORES + lax.axis_index("c")
        base = wid * per

        @pl.loop(0, per // rows)
        def _(j):
            off = pl.multiple_of(base + j * rows, 8)
            pltpu.sync_copy(src_hbm.at[pl.ds(off, rows)], rows_v)
            for k in range(TOP_K):
                pltpu.sync_copy(idx_hbm.at[pl.ds(pl.multiple_of(k * t + off, 8), rows)], idx_vs[k])
            copies = [pltpu.async_copy(rows_v, out_hbm.at[idx_vs[k]], sem) for k in range(TOP_K)]
            for cp in copies:
                cp.wait()

    return scatter(src, order)


def _zero_pad_kernel(cnt_ref, xs_in, xs_hbm, zero_ref, zsem, *, n_rows):
    del xs_in
    zero_ref[...] = jnp.zeros_like(zero_ref)
    _zero_fill(cnt_ref, xs_hbm, zero_ref, zsem, n_rows, wait=False)
    _zero_fill(cnt_ref, xs_hbm, zero_ref, zsem, n_rows, wait=True)


def _dispatch(counts, order, h_packed, geo):
    n_rows = _n_blocks(geo) * MOE_BLOCK
    width = D_MODEL // 2
    xs = _sc_scatter_rows(h_packed[:geo.T], order, n_rows)
    return pl.pallas_call(
        functools.partial(_zero_pad_kernel, n_rows=n_rows),
        name="zero_pad",
        out_shape=jax.ShapeDtypeStruct((n_rows, width), PACKED),
        in_specs=[pl.BlockSpec(memory_space=pltpu.SMEM), pl.BlockSpec(memory_space=pl.ANY)],
        out_specs=pl.BlockSpec(memory_space=pl.ANY),
        scratch_shapes=[pltpu.VMEM((MOE_BLOCK, width), PACKED), pltpu.SemaphoreType.DMA],
        input_output_aliases={1: 0},
    )(counts, xs)


def _swiglu_packed(x_ref, wg_ref, wu_ref, wd_ref):
    half = D_MODEL // 2
    lo, hi = _unpack_rows(x_ref[...])
    lo = lo.astype(BF16)
    hi = hi.astype(BF16)

    def proj(w_ref):
        return (jnp.dot(lo, w_ref[:half], preferred_element_type=F32)
                + jnp.dot(hi, w_ref[half:], preferred_element_type=F32))

    g = proj(wg_ref)
    u = proj(wu_ref)
    return jnp.dot((_silu(g) * u).astype(BF16), wd_ref[...], preferred_element_type=F32)


def _expert_kernel(blk_ref, x_ref, wg_ref, wu_ref, wd_ref, o_ref, wgb_ref, wub_ref, wdb_ref, *, nb):
    j = pl.program_id(0)
    n_used = blk_ref[nb]

    @pl.when((j == 0) | (blk_ref[j] != blk_ref[jnp.maximum(j - 1, 0)]))
    def _():
        wgb_ref[...] = wg_ref[...].astype(BF16)
        wub_ref[...] = wu_ref[...].astype(BF16)
        wdb_ref[...] = wd_ref[...].astype(BF16)

    @pl.when(j < n_used)
    def _():
        o_ref[...] = _pack_rows(_swiglu_packed(x_ref, wgb_ref, wub_ref, wdb_ref))

    @pl.when(j >= n_used)
    def _():
        o_ref[...] = jnp.zeros_like(o_ref)


def _experts(blk_e, xs, wg, wu, wd, layer, geo):
    nb = _n_blocks(geo)
    rows = pl.BlockSpec((MOE_BLOCK, D_MODEL // 2), lambda j, be: (j, 0))

    def wspec(r, c):
        return pl.BlockSpec((None, None, r, c), lambda j, be: (layer, be[j], 0, 0))

    return pl.pallas_call(
        functools.partial(_expert_kernel, nb=nb),
        name="experts",
        out_shape=jax.ShapeDtypeStruct((nb * MOE_BLOCK, D_MODEL // 2), PACKED),
        grid_spec=pltpu.PrefetchScalarGridSpec(
            num_scalar_prefetch=1,
            grid=(nb,),
            in_specs=[rows, wspec(D_MODEL, EXPERT_FF), wspec(D_MODEL, EXPERT_FF),
                      wspec(EXPERT_FF, D_MODEL)],
            out_specs=rows,
            scratch_shapes=[pltpu.VMEM((D_MODEL, EXPERT_FF), BF16),
                            pltpu.VMEM((D_MODEL, EXPERT_FF), BF16),
                            pltpu.VMEM((EXPERT_FF, D_MODEL), BF16)]),
        compiler_params=_params("arbitrary"),
    )(blk_e, xs, wg, wu, wd)


def _sc_gather_rows(table, idx):
    nw = V7X_SC_CORES * V7X_SC_SUBCORES
    m = idx.shape[0]
    w = table.shape[1]
    per = m // nw
    n = per // SC_ROWS
    assert m % (nw * SC_ROWS * 2) == 0
    mesh = plsc.VectorSubcoreMesh(core_axis_name="c", subcore_axis_name="s")

    @functools.partial(
        pl.kernel, mesh=mesh,
        out_type=jax.ShapeDtypeStruct((m, w), table.dtype),
        scratch_types=[pltpu.VMEM((SC_ROWS,), jnp.int32)] * 2
        + [pltpu.VMEM((SC_ROWS, w), table.dtype)] * 2 + [pltpu.SemaphoreType.DMA] * 2,
    )
    def gather(table_hbm, idx_hbm, out_hbm, *scratch):
        idx_vs, rows_vs, sems = scratch[:2], scratch[2:4], scratch[4:]
        wid = lax.axis_index("s") * V7X_SC_CORES + lax.axis_index("c")
        base = wid * per

        def fetch(b):
            return pltpu.make_async_copy(table_hbm.at[idx_vs[b]], rows_vs[b], sems[b])

        def start(chunk, b):
            off = pl.multiple_of(base + chunk * SC_ROWS, 8)
            pltpu.sync_copy(idx_hbm.at[pl.ds(off, SC_ROWS)], idx_vs[b])
            fetch(b).start()

        start(0, 0)

        @pl.loop(0, n, step=2)
        def _(j):
            for b in range(2):
                cur = j + b
                pl.when(cur + 1 < n)(functools.partial(start, cur + 1, 1 - b))
                fetch(b).wait()
                off = pl.multiple_of(base + cur * SC_ROWS, 8)
                pltpu.sync_copy(rows_vs[b], out_hbm.at[pl.ds(off, SC_ROWS)])

    return gather(table, idx)


def _combine_kernel(w_ref, x_ref, h_ref, sg_ref, su_ref, sd_ref, m_ref, *rest, post, n_prev):
    rest = rest[:-2 * n_prev] + rest[-n_prev:] if n_prev else rest
    if post == "next":
        ng_ref, nsh_ref, nsc_ref, g_ref, o_ref, hn_ref = rest
    else:
        ng_ref, g_ref, o_ref = rest
    half = D_MODEL // 2
    shared = _swiglu_packed(h_ref, sg_ref, su_ref, sd_ref)
    w = w_ref[...]
    acc_lo = shared[:, :half]
    acc_hi = shared[:, half:]
    for k in range(TOP_K):
        lo, hi = _unpack_rows(g_ref[k])
        acc_lo = acc_lo + w[:, k:k + 1] * lo
        acc_hi = acc_hi + w[:, k:k + 1] * hi
    gate = m_ref[0]
    y_lo = x_ref[:, :half] + gate[:, :half] * acc_lo
    y_hi = x_ref[:, half:] + gate[:, half:] * acc_hi
    ms = (jnp.sum(y_lo * y_lo, axis=-1, keepdims=True)
          + jnp.sum(y_hi * y_hi, axis=-1, keepdims=True)) * (1.0 / D_MODEL)
    inv = lax.rsqrt(ms + EPS)
    ng = ng_ref[...]
    n_lo = y_lo * inv * ng[:, :half]
    n_hi = y_hi * inv * ng[:, half:]
    if post == "next":
        o_ref[:, :half] = y_lo
        o_ref[:, half:] = y_hi
        sc = nsc_ref[0]
        sft = nsh_ref[0]
        hn_ref[:, :half] = (n_lo * (1.0 + sc[:, :half]) + sft[:, :half]).astype(BF16)
        hn_ref[:, half:] = (n_hi * (1.0 + sc[:, half:]) + sft[:, half:]).astype(BF16)
    else:
        o_ref[:, :half] = n_lo
        o_ref[:, half:] = n_hi


def _combine(dest, w, x, h_packed, sg, su, sd, ys, mod3, m_gate, post, norm_g, next_mod3, geo):
    half = D_MODEL // 2
    tt = _pick_tile(geo, (256, 128))
    ff = sg.shape[1]
    unit = math.lcm(tt, V7X_SC_CORES * V7X_SC_SUBCORES * SC_ROWS * 2 // TOP_K)
    cut = (geo.T // 2) // unit * unit
    parts = [(0, cut), (cut, geo.T)] if 0 < cut < geo.T else [(0, geo.T)]
    dest2 = dest.reshape(geo.T, TOP_K)
    outs = []
    for t0, t1 in parts:
        tile0, n = t0 // tt, (t1 - t0) // tt
        order = dest2[t0:t1].T.reshape(-1)
        gathered = _sc_gather_rows(ys, order).reshape(TOP_K, t1 - t0, half)
        rows = pl.BlockSpec((tt, D_MODEL), lambda i: (i + tile0, 0))
        vec = pl.BlockSpec((1, D_MODEL), lambda i: (0, 0))

        def mod(m):
            return pl.BlockSpec((1, 1, D_MODEL),
                                lambda i: (_mod_row(i + tile0, tt, geo) * N_MOD + m, 0, 0))

        in_specs = [pl.BlockSpec((tt, TOP_K), lambda i: (i + tile0, 0)),
                    rows,
                    pl.BlockSpec((tt, half), lambda i: (i + tile0, 0)),
                    pl.BlockSpec((D_MODEL, ff), lambda i: (0, 0)),
                    pl.BlockSpec((D_MODEL, ff), lambda i: (0, 0)),
                    pl.BlockSpec((ff, D_MODEL), lambda i: (0, 0)),
                    mod(m_gate), vec]
        args = [w, x, h_packed, sg, su, sd, mod3, norm_g.reshape(1, D_MODEL)]
        out_shape = [jax.ShapeDtypeStruct((geo.T, D_MODEL), F32)]
        out_specs = [rows]
        if post == "next":
            in_specs += [mod(0), mod(1)]
            args += [next_mod3, next_mod3]
            out_shape.append(jax.ShapeDtypeStruct((geo.T, D_MODEL), BF16))
            out_specs.append(rows)
        in_specs.append(pl.BlockSpec((TOP_K, tt, half), lambda i: (0, i, 0)))
        args.append(gathered)
        aliases = {len(args) + j: j for j in range(len(outs))}
        in_specs += [pl.BlockSpec(memory_space=pl.ANY)] * len(outs)
        args += outs
        outs = pl.pallas_call(
            functools.partial(_combine_kernel, post=post, n_prev=len(outs)),
            name="combine",
            out_shape=out_shape,
            grid=(n,),
            in_specs=in_specs,
            out_specs=out_specs,
            input_output_aliases=aliases,
            compiler_params=_params("arbitrary"),
        )(*args)
    return outs


def _moe(x, h_packed, mod3, rw, rb, wg, wu, wd, layer, sg, su, sd, post, norm_g, next_mod3, geo):
    eidx, rank, w, counts = _router(h_packed, rw, rb, geo)
    counts = counts.reshape(N_EXPERTS).astype(jnp.int32)
    dest, blk_e = _dest(counts, eidx, rank, geo)
    order = dest.reshape(geo.T, TOP_K).T.reshape(-1)
    xs = _dispatch(counts, order, h_packed, geo)
    ys = _experts(blk_e, xs, wg, wu, wd, layer, geo)
    return _combine(dest, w, x, h_packed, sg.astype(BF16), su.astype(BF16), sd.astype(BF16), ys,
                    mod3, 5, post, norm_g, next_mod3, geo)


def kernel(x, c, ctx, c_ctx, ada_w, ada_b, norm_mix, norm_ffn, norm_final, ev_w_in, ev_w_out, hgrn_lb, hgrn_norm, conv_w, conv_b, conv_norm_g, conv_norm_b, ret_w_in, ret_w_out, ret_decay, router_w, router_b, exp_gate, exp_up, exp_down, sh_gate, sh_up, sh_down):
    b, n, d = x.shape
    lc = ctx.shape[1]
    depth = ada_w.shape[0]
    geo = _geo(b, n, lc)
    assert d == D_MODEL and b < MOD_ROWS
    assert n % RET_CHUNK == 0 and lc % RET_CHUNK == 0

    xs = (x.reshape(geo.BN, d), ctx.reshape(geo.BL, d))
    cond = jnp.zeros((MOD_ROWS, d), F32).at[:b].set(c).at[b].set(c_ctx)
    rope = _rope_tables(n, _pick_tile(geo, (1024, 512, 256, 128)))

    mods = [_adaln(cond, ada_w, ada_b, l).reshape(MOD_ROWS * N_MOD, 1, d) for l in range(depth)]
    h = _normmod(xs, norm_mix[0], mods[0], 0, 1, geo, packed=False)
    for l in range(depth):
        j = l // 2
        last = l == depth - 1
        mod3 = mods[l]
        tail = geo._replace(T=geo.BN, BL=0, Lc=0) if last else geo
        if l % 2 == 0:
            p = _matmul(h, ev_w_in[j].astype(BF16), geo)
            of, ob = _hgrn_scan(p, hgrn_lb, l, geo)
            mix = _ev_readout(p, of, ob, hgrn_norm[j], conv_w[j], conv_b[j],
                              conv_norm_g[j], conv_norm_b[j], geo)
            xs = _matmul_resid(mix, ev_w_out[j].astype(BF16), xs, mod3, 2, tail)
        else:
            p = _matmul(h, ret_w_in[j].astype(BF16), geo, rope=rope)
            of, ob = _ret_scan(p, ret_decay[j], geo)
            mix = _ret_readout(p, of, ob, geo)
            xs = _matmul_resid(mix, ret_w_out[j].astype(BF16), xs, mod3, 2, tail)
        h_packed = _normmod(xs, norm_ffn[l], mod3, 3, 4, tail, packed=True)
        moe_w = (router_w[l], router_b[l], exp_gate, exp_up, exp_down, l, sh_gate[l], sh_up[l], sh_down[l])
        if last:
            (out,) = _moe(xs, h_packed, mod3, *moe_w, "final", norm_final, None, tail)
        else:
            xs, h = _moe(xs, h_packed, mod3, *moe_w, "next", norm_mix[l + 1], mods[l + 1], tail)
    return out.reshape(b, n, d)
```

```python
import collections
import functools
import math

import jax
import jax.numpy as jnp
from jax import lax
from jax.experimental import pallas as pl
from jax.experimental.pallas import tpu as pltpu
from jax.experimental.pallas import tpu_sc as plsc

F32 = jnp.float32
BF16 = jnp.bfloat16

D_MODEL = 2048
N_MOD = 6
EPS = 1e-6
GRID_W = 64
ROPE_BASE = 10000.0

HG_HEADS = 8
HG_DK = 128
HG_DV = 128
HG_F = HG_HEADS * HG_DK
HG_V = HG_HEADS * HG_DV
CONV_C = D_MODEL // 2
CONV_W = 31
CONV_HALO = 16
HG_CHUNK = 128

RET_HEADS = 8
RET_DK = D_MODEL // RET_HEADS
RET_DV = 2 * RET_DK
RET_QK = RET_HEADS * RET_DK
RET_V = RET_HEADS * RET_DV
RET_CHUNK = 256

N_EXPERTS = 64
EXPERT_FF = D_MODEL // 4
TOP_K = 8
N_GROUPS = 8
GROUP_SIZE = N_EXPERTS // N_GROUPS
TOPK_GROUPS = 4
ROUTED_SCALE = 2.5
MOE_BLOCK = 512

V7X_SC_CORES = 2
V7X_SC_SUBCORES = 16
SC_ROWS = 32
MOD_ROWS = 16
VMEM_LIMIT = 56 * 1024 * 1024

Geo = collections.namedtuple("Geo", "B N Lc BN BL T")


def _geo(b, n, lc):
    return Geo(b, n, lc, b * n, b * lc, b * n + b * lc)


def _pick_tile(geo, cands):
    for t in cands:
        if geo.N % t == 0 and geo.BL % t == 0:
            return t
    raise ValueError("no row tile fits the sequence lengths")


def _mod_row(i, tm, geo):
    return jnp.where(i < geo.BN // tm, i // (geo.N // tm), geo.B)


def _mod_spec(m, tm, geo, ngrid=1):
    if ngrid == 1:
        return pl.BlockSpec((1, 1, D_MODEL), lambda i: (_mod_row(i, tm, geo) * N_MOD + m, 0, 0))
    return pl.BlockSpec((1, 1, D_MODEL), lambda i, j: (_mod_row(i, tm, geo) * N_MOD + m, 0, j))


def _params(*sem):
    return pltpu.CompilerParams(dimension_semantics=sem, vmem_limit_bytes=VMEM_LIMIT)


def _sigmoid(x):
    return jax.nn.sigmoid(x)


def _silu(x):
    return x * jax.nn.sigmoid(x)


def _adaln_kernel(c_ref, w_ref, b_ref, o_ref):
    a = _silu(c_ref[...]).astype(BF16)
    o_ref[...] = jnp.dot(a, w_ref[...].astype(BF16), preferred_element_type=F32) + b_ref[...]


def _adaln(cond, w, b, layer):
    depth, k, n = w.shape
    tn = 1024
    return pl.pallas_call(
        _adaln_kernel,
        name="adaln",
        out_shape=jax.ShapeDtypeStruct((MOD_ROWS, n), F32),
        grid=(n // tn,),
        in_specs=[pl.BlockSpec((MOD_ROWS, k), lambda j: (0, 0)),
                  pl.BlockSpec((None, k, tn), lambda j: (layer, 0, j)),
                  pl.BlockSpec((None, 1, tn), lambda j: (layer, 0, j))],
        out_specs=pl.BlockSpec((MOD_ROWS, tn), lambda j: (0, j)),
        compiler_params=_params("arbitrary"),
    )(cond, w, b.reshape(depth, 1, n))


PACKED = jnp.int32


def _pack_rows(x):
    n = x.shape[-1] // 2
    bits = lax.bitcast_convert_type(x.astype(BF16).astype(F32), jnp.uint32)
    words = (bits[:, n:] & jnp.uint32(0xFFFF0000)) | (bits[:, :n] >> 16)
    return lax.bitcast_convert_type(words, PACKED)


def _unpack_rows(p):
    u = lax.bitcast_convert_type(p, jnp.uint32)
    lo = lax.bitcast_convert_type(u << 16, F32)
    hi = lax.bitcast_convert_type(u & jnp.uint32(0xFFFF0000), F32)
    return lo, hi


def _stream_specs(stream, tm, width, geo, ngrid):
    col = (lambda j: j) if ngrid == 2 else (lambda *_: 0)
    if not isinstance(stream, tuple):
        return [pl.BlockSpec((tm, width), lambda i, *j: (i, col(*j)))], [stream]
    n_lat = geo.BN // tm
    return ([pl.BlockSpec((tm, width), lambda i, *j: (jnp.minimum(i, n_lat - 1), col(*j))),
             pl.BlockSpec((tm, width), lambda i, *j: (jnp.maximum(i - n_lat, 0), col(*j)))],
            list(stream))


def _stream_tile(refs, tm, geo):
    if len(refs) == 1:
        return refs[0][...]
    return jnp.where(pl.program_id(0) < geo.BN // tm, refs[0][...], refs[1][...])


def _normmod_kernel(*refs, packed, tm, geo):
    g_ref, sh_ref, sc_ref, o_ref = refs[-4:]
    x = _stream_tile(refs[:-4], tm, geo)
    y = x * lax.rsqrt(jnp.mean(x * x, axis=-1, keepdims=True) + EPS) * g_ref[...]
    h = y * (1.0 + sc_ref[0]) + sh_ref[0]
    o_ref[...] = _pack_rows(h) if packed else h.astype(BF16)


def _normmod(x, g, mod3, m_shift, m_scale, geo, packed):
    tm = _pick_tile(geo, (256, 128))
    x_specs, x_args = _stream_specs(x, tm, D_MODEL, geo, 1)
    if packed:
        out_shape = jax.ShapeDtypeStruct((geo.T, D_MODEL // 2), PACKED)
        out_spec = pl.BlockSpec((tm, D_MODEL // 2), lambda i: (i, 0))
    else:
        out_shape = jax.ShapeDtypeStruct((geo.T, D_MODEL), BF16)
        out_spec = pl.BlockSpec((tm, D_MODEL), lambda i: (i, 0))
    return pl.pallas_call(
        functools.partial(_normmod_kernel, packed=packed, tm=tm, geo=geo),
        name="normmod",
        out_shape=out_shape,
        grid=(geo.T // tm,),
        in_specs=x_specs + [pl.BlockSpec((1, D_MODEL), lambda i: (0, 0)),
                            _mod_spec(m_shift, tm, geo), _mod_spec(m_scale, tm, geo)],
        out_specs=out_spec,
        compiler_params=_params("arbitrary"),
    )(*x_args, g.reshape(1, D_MODEL), mod3, mod3)


def _mm_kernel(a_ref, w_ref, o_ref):
    o_ref[...] = jnp.dot(a_ref[...], w_ref[...], preferred_element_type=F32).astype(o_ref.dtype)


def _mm_resid_kernel(a_ref, w_ref, *refs, tm, geo):
    m_ref, o_ref = refs[-2:]
    y = jnp.dot(a_ref[...], w_ref[...], preferred_element_type=F32)
    o_ref[...] = _stream_tile(refs[:-2], tm, geo) + m_ref[0] * y


def _matmul(a, w, geo):
    k, n = w.shape
    tm = _pick_tile(geo, (1024, 512, 256, 128))
    tn = 1024 if n % 1024 == 0 else 512
    return pl.pallas_call(
        _mm_kernel,
        name="matmul",
        out_shape=jax.ShapeDtypeStruct((geo.T, n), BF16),
        grid=(geo.T // tm, n // tn),
        in_specs=[pl.BlockSpec((tm, k), lambda i, j: (i, 0)),
                  pl.BlockSpec((k, tn), lambda i, j: (0, j))],
        out_specs=pl.BlockSpec((tm, tn), lambda i, j: (i, j)),
        compiler_params=_params("arbitrary", "arbitrary"),
    )(a, w)


def _matmul_resid(a, w, x, mod3, m_gate, geo):
    k, n = w.shape
    tm = _pick_tile(geo, (1024, 512, 256, 128))
    tn = 512
    x_specs, x_args = _stream_specs(x, tm, tn, geo, 2)
    return pl.pallas_call(
        functools.partial(_mm_resid_kernel, tm=tm, geo=geo),
        name="matmul_resid",
        out_shape=jax.ShapeDtypeStruct((geo.T, n), F32),
        grid=(geo.T // tm, n // tn),
        in_specs=[pl.BlockSpec((tm, k), lambda i, j: (i, 0)),
                  pl.BlockSpec((k, tn), lambda i, j: (0, j))] + x_specs
        + [pl.BlockSpec((1, 1, tn), lambda i, j: (_mod_row(i, tm, geo) * N_MOD + m_gate, 0, j))],
        out_specs=pl.BlockSpec((tm, tn), lambda i, j: (i, j)),
        compiler_params=_params("arbitrary", "arbitrary"),
    )(a, w, *x_args, mod3)


def _chunk_index(b, i, chunk, geo, reverse):
    nc = geo.Lc // chunk
    nl = geo.N // chunk
    ctx0 = (geo.BN + b * geo.Lc) // chunk
    lat0 = (b * geo.N) // chunk
    if reverse:
        return jnp.where(i < nc, ctx0 + (nc - 1 - i), lat0 + (nl - 1 - (i - nc)))
    return jnp.where(i < nc, ctx0 + i, lat0 + (i - nc))


def _split_dot(tri_bf, x):
    hi = x.astype(BF16)
    r1 = x - hi.astype(F32)
    mid = r1.astype(BF16)
    lo = (r1 - mid.astype(F32)).astype(BF16)
    return (jnp.dot(tri_bf, hi, preferred_element_type=F32)
            + jnp.dot(tri_bf, mid, preferred_element_type=F32)
            + jnp.dot(tri_bf, lo, preferred_element_type=F32))


def _hgrn_kernel(qf_ref, ff_ref, vf_ref, qb_ref, fb_ref, vb_ref, lbp_ref, of_ref, ob_ref,
                 stf_ref, stb_ref, *, layer):
    i = pl.program_id(1)

    @pl.when(i == 0)
    def _():
        stf_ref[...] = jnp.zeros_like(stf_ref)
        stb_ref[...] = jnp.zeros_like(stb_ref)

    lbp = lbp_ref[...]
    e = jnp.exp(lbp - jnp.max(lbp, axis=0, keepdims=True))
    sm = e / jnp.sum(e, axis=0, keepdims=True)
    lb = sm[0:1]
    for r in range(1, layer + 1):
        lb = lb + sm[r:r + 1]

    _hgrn_chunk(qf_ref, ff_ref, vf_ref, of_ref, stf_ref, lb, reverse=False)
    _hgrn_chunk(qb_ref, fb_ref, vb_ref, ob_ref, stb_ref, lb, reverse=True)


def _hgrn_chunk(q_ref, f_ref, v_ref, o_ref, st_ref, lb, *, reverse):
    c = HG_CHUNK
    row = lax.broadcasted_iota(jnp.int32, (c, c), 0)
    col = lax.broadcasted_iota(jnp.int32, (c, c), 1)
    tri = (col >= row) if reverse else (col <= row)
    tri_bf = jnp.where(tri, 1.0, 0.0).astype(BF16)
    nt = (((1,), (1,)), ((), ()))

    fg_all = lb + (1.0 - lb) * _sigmoid(f_ref[...].astype(F32))
    bcum_all = _split_dot(tri_bf, jnp.log(fg_all))

    for h in range(HG_HEADS):
        sl = slice(h * HG_DK, (h + 1) * HG_DK)
        qh = _silu(q_ref[:, sl].astype(F32))
        kh = 1.0 - fg_all[:, sl]
        bcum = bcum_all[:, sl]
        bmid = bcum[c // 2:c // 2 + 1]
        bend = bcum[0:1] if reverse else bcum[c - 1:c]
        vb = v_ref[:, sl]
        vh = vb.astype(F32)
        a = (qh * jnp.exp(bcum - bmid)).astype(BF16)
        kd = (kh * jnp.exp(bmid - bcum)).astype(BF16)
        s = lax.dot_general(a, kd, nt, preferred_element_type=F32)
        s = jnp.where(tri, s, 0.0)
        intra = jnp.dot(s.astype(BF16), vb, preferred_element_type=F32)
        st = st_ref[h]
        inter = lax.dot_general((qh * jnp.exp(bcum)).astype(BF16), st.astype(BF16), nt,
                                preferred_element_type=F32)
        o_ref[:, sl] = (inter + intra).astype(BF16)
        kd2 = (kh * jnp.exp(bend - bcum)).astype(BF16)
        st_ref[h] = st * jnp.exp(bend) + jnp.dot(vh.T.astype(BF16), kd2, preferred_element_type=F32)


def _hgrn_scan(p, lb_param, layer, geo):
    c = HG_CHUNK
    steps = (geo.Lc + geo.N) // c

    def spec(colblk, reverse):
        return pl.BlockSpec((c, HG_F), lambda b, i: (_chunk_index(b, i, c, geo, reverse), colblk))

    out = jax.ShapeDtypeStruct((geo.T, HG_V), BF16)
    state = pltpu.VMEM((HG_HEADS, HG_DV, HG_DK), F32)
    return pl.pallas_call(
        functools.partial(_hgrn_kernel, layer=layer),
        name="hgrn_scan",
        out_shape=[out, out],
        grid=(geo.B, steps),
        in_specs=[spec(0, False), spec(1, False), spec(3, False),
                  spec(0, True), spec(2, True), spec(3, True),
                  pl.BlockSpec(lb_param.shape, lambda b, i: (0, 0))],
        out_specs=[spec(0, False), spec(0, True)],
        scratch_shapes=[state, state],
        compiler_params=_params("arbitrary", "arbitrary"),
    )(p, p, p, p, p, p, lb_param)


def _ev_readout_kernel(of_ref, ob_ref, gate_ref, a_ref, b_ref, ap_ref, bp_ref, an_ref, bn_ref,
                       gain_ref, cw_ref, cb_ref, lng_ref, lnb_ref, o_ref, ext_ref, conv_ref, shift_ref,
                       *, tm, geo):
    i = pl.program_id(0)
    n_lat = geo.BN // tm
    tpl = geo.N // tm
    tpc = geo.Lc // tm
    j = jnp.where(i < n_lat, i % tpl, (i - n_lat) % tpc)
    per = jnp.where(i < n_lat, tpl, tpc)
    keep_prev = jnp.where(j == 0, 0.0, 1.0)
    keep_next = jnp.where(j == per - 1, 0.0, 1.0)

    o = of_ref[...].astype(F32) + ob_ref[...].astype(F32)
    r = o * lax.rsqrt(jnp.mean(o * o, axis=-1, keepdims=True) + EPS) * gain_ref[...]
    o_ref[:, :HG_V] = (r * _silu(gate_ref[...].astype(F32))).astype(BF16)

    def glu(x_ref, y_ref):
        return x_ref[...].astype(F32) * _sigmoid(y_ref[...].astype(F32))

    ext_ref[0:CONV_HALO] = glu(ap_ref, bp_ref) * keep_prev
    ext_ref[CONV_HALO:CONV_HALO + tm] = glu(a_ref, b_ref)
    ext_ref[CONV_HALO + tm:2 * CONV_HALO + tm] = glu(an_ref, bn_ref) * keep_next
    off = CONV_HALO - CONV_W // 2
    span = tm + 8 * ((off + CONV_W - 1) // 8)
    for s in range(1, 8):
        shift_ref[s - 1, 0:span] = ext_ref[s:s + span]
    for cj in range(CONV_C // 128):
        cs = slice(cj * 128, (cj + 1) * 128)
        acc = jnp.broadcast_to(cb_ref[:, cs], (tm, 128))
        for k in range(CONV_W):
            a, s = divmod(off + k, 8)
            src = ext_ref if s == 0 else shift_ref.at[s - 1]
            acc = acc + cw_ref[k:k + 1, cs] * src[8 * a:8 * a + tm, cs]
        conv_ref[:, cs] = acc
    acc = conv_ref[...]
    mu = jnp.mean(acc, axis=-1, keepdims=True)
    xc = acc - mu
    var = jnp.mean(xc * xc, axis=-1, keepdims=True)
    u = xc * lax.rsqrt(var + EPS) * lng_ref[...] + lnb_ref[...]
    o_ref[:, HG_V:] = _silu(u).astype(BF16)


def _ev_readout(p, of, ob, gain, cw, cb, lng, lnb, geo):
    tm = _pick_tile(geo, (128,))
    hb = tm // CONV_HALO
    nhalo = geo.T // CONV_HALO

    def row(colblk):
        return pl.BlockSpec((tm, HG_V), lambda i: (i, colblk))

    def prev(colblk):
        return pl.BlockSpec((CONV_HALO, CONV_C), lambda i: (jnp.maximum(i * hb - 1, 0), colblk))

    def nxt(colblk):
        return pl.BlockSpec((CONV_HALO, CONV_C), lambda i: (jnp.minimum((i + 1) * hb, nhalo - 1), colblk))

    def vec(n):
        return pl.BlockSpec((n, CONV_C), lambda i: (0, 0))

    return pl.pallas_call(
        functools.partial(_ev_readout_kernel, tm=tm, geo=geo),
        name="ev_readout",
        out_shape=jax.ShapeDtypeStruct((geo.T, HG_V + CONV_C), BF16),
        grid=(geo.T // tm,),
        in_specs=[row(0), row(0), row(4), row(5), row(6), prev(5), prev(6), nxt(5), nxt(6),
                  vec(1), vec(CONV_W), vec(1), vec(1), vec(1)],
        out_specs=pl.BlockSpec((tm, HG_V + CONV_C), lambda i: (i, 0)),
        scratch_shapes=[pltpu.VMEM((tm + 2 * CONV_HALO, CONV_C), F32), pltpu.VMEM((tm, CONV_C), F32),
                        pltpu.VMEM((7, tm + 2 * CONV_HALO, CONV_C), F32)],
        compiler_params=_params("arbitrary"),
    )(of, ob, p, p, p, p, p, p, p, gain.reshape(1, -1), cw, cb.reshape(1, -1),
      lng.reshape(1, -1), lnb.reshape(1, -1))


def _rope(x, cos, sin_signed):
    half = x.shape[-1] // 2
    rot = jnp.concatenate([pltpu.roll(x[:, :half], half // 2, axis=1),
                           pltpu.roll(x[:, half:], half // 2, axis=1)], axis=-1)
    return x * cos + rot * sin_signed


def _ret_kernel(qf_ref, kf_ref, vf_ref, cosf_ref, sinf_ref, qb_ref, kb_ref, vb_ref, cosb_ref, sinb_ref,
                dl_ref, of_ref, ob_ref, sf_ref, sb_ref, dm_ref):
    i = pl.program_id(1)

    @pl.when(i == 0)
    def _():
        sf_ref[...] = jnp.zeros_like(sf_ref)
        sb_ref[...] = jnp.zeros_like(sb_ref)

    @pl.when(i == 0)
    def _():
        c = RET_CHUNK
        row = lax.broadcasted_iota(jnp.int32, (c, c), 0)
        col = lax.broadcasted_iota(jnp.int32, (c, c), 1)
        for d in range(2):
            lg_all = -jnp.log1p(jnp.exp(-dl_ref[d]))
            dist = ((col - row) if d else (row - col)).astype(F32)
            for h in range(RET_HEADS):
                dm_ref[d, h] = jnp.where(dist >= 0.0, jnp.exp(lg_all[h:h + 1] * dist), 0.0)

    _ret_chunk(qf_ref, kf_ref, vf_ref, cosf_ref, sinf_ref, dl_ref[0], dm_ref.at[0], of_ref, sf_ref,
               reverse=False)
    _ret_chunk(qb_ref, kb_ref, vb_ref, cosb_ref, sinb_ref, dl_ref[1], dm_ref.at[1], ob_ref, sb_ref,
               reverse=True)


def _ret_chunk(q_ref, k_ref, v_ref, cos_ref, sin_ref, dl, dm_ref, o_ref, s_ref, *, reverse):
    c = RET_CHUNK
    lg_all = -jnp.log1p(jnp.exp(-dl))
    idx = lax.broadcasted_iota(jnp.int32, (c, 1), 0).astype(F32)
    if reverse:
        q_pow = c - idx
        k_pow = idx
    else:
        q_pow = idx + 1.0
        k_pow = c - 1.0 - idx
    cos = cos_ref[...]
    sin = sin_ref[...]
    nt = (((1,), (1,)), ((), ()))

    for h in range(RET_HEADS):
        lg = lg_all[h:h + 1]
        dmask = dm_ref[h]
        q = _rope(q_ref[:, h * RET_DK:(h + 1) * RET_DK].astype(F32), cos, sin)
        k = _rope(k_ref[:, h * RET_DK:(h + 1) * RET_DK].astype(F32) * (RET_DK ** -0.5), cos, sin)
        vb = v_ref[:, h * RET_DV:(h + 1) * RET_DV]
        scores = lax.dot_general(q.astype(BF16), k.astype(BF16), nt, preferred_element_type=F32) * dmask
        intra = jnp.dot(scores.astype(BF16), vb, preferred_element_type=F32)
        s = s_ref[h]
        inter = jnp.dot((q * jnp.exp(lg * q_pow)).astype(BF16), s.astype(BF16),
                        preferred_element_type=F32)
        o_ref[:, h * RET_DV:(h + 1) * RET_DV] = (inter + intra).astype(BF16)
        kdec = (k * jnp.exp(lg * k_pow)).T.astype(BF16)
        s_ref[h] = jnp.exp(lg * c) * s + jnp.dot(kdec, vb, preferred_element_type=F32)


def _ret_scan(p, cos_tab, sin_tab, decay_logit, geo):
    c = RET_CHUNK
    steps = (geo.Lc + geo.N) // c
    nc = geo.Lc // c
    nl = geo.N // c

    def spec(width, colblk, reverse):
        return pl.BlockSpec((c, width), lambda b, i: (_chunk_index(b, i, c, geo, reverse), colblk))

    def tab(reverse):
        def index(b, i):
            lat = (nl - 1 - (i - nc)) if reverse else (i - nc)
            return (jnp.where(i < nc, nl, lat), 0)
        return pl.BlockSpec((c, RET_DK), index)

    def direction(reverse):
        return [spec(RET_QK, 0, reverse), spec(RET_QK, 1, reverse), spec(RET_V, 1, reverse),
                tab(reverse), tab(reverse)]

    out = jax.ShapeDtypeStruct((geo.T, RET_V), BF16)
    state = pltpu.VMEM((RET_HEADS, RET_DK, RET_DV), F32)
    return pl.pallas_call(
        _ret_kernel,
        name="ret_scan",
        out_shape=[out, out],
        grid=(geo.B, steps),
        in_specs=direction(False) + direction(True)
        + [pl.BlockSpec((2, RET_HEADS, 1), lambda b, i: (0, 0, 0))],
        out_specs=[spec(RET_V, 0, False), spec(RET_V, 0, True)],
        scratch_shapes=[state, state, pltpu.VMEM((2, RET_HEADS, c, c), F32)],
        compiler_params=_params("arbitrary", "arbitrary"),
    )(p, p, p, cos_tab, sin_tab, p, p, p, cos_tab, sin_tab, decay_logit.reshape(2, RET_HEADS, 1))


def _rope_tables(n):
    t = jnp.arange(n)
    quarter = RET_DK // 4
    inv = 1.0 / (ROPE_BASE ** (jnp.arange(quarter, dtype=F32) / quarter))
    ang_r = (t // GRID_W).astype(F32)[:, None] * inv
    ang_c = (t % GRID_W).astype(F32)[:, None] * inv
    cos = jnp.concatenate([jnp.cos(ang_r), jnp.cos(ang_r), jnp.cos(ang_c), jnp.cos(ang_c)], axis=-1)
    sin = jnp.concatenate([-jnp.sin(ang_r), jnp.sin(ang_r), -jnp.sin(ang_c), jnp.sin(ang_c)], axis=-1)
    cos = jnp.concatenate([cos, jnp.ones((RET_CHUNK, RET_DK), F32)], axis=0)
    sin = jnp.concatenate([sin, jnp.zeros((RET_CHUNK, RET_DK), F32)], axis=0)
    return cos, sin


def _ret_readout_kernel(of_ref, ob_ref, gate_ref, o_ref):
    for h in range(RET_HEADS):
        sl = slice(h * RET_DV, (h + 1) * RET_DV)
        o = of_ref[:, sl].astype(F32) + ob_ref[:, sl].astype(F32)
        r = o * lax.rsqrt(jnp.mean(o * o, axis=-1, keepdims=True) + EPS)
        o_ref[:, sl] = (_silu(gate_ref[:, sl].astype(F32)) * r).astype(BF16)


def _ret_readout(p, of, ob, geo):
    tm = _pick_tile(geo, (256, 128))
    spec = pl.BlockSpec((tm, RET_V), lambda i: (i, 0))
    return pl.pallas_call(
        _ret_readout_kernel,
        name="ret_readout",
        out_shape=jax.ShapeDtypeStruct((geo.T, RET_V), BF16),
        grid=(geo.T // tm,),
        in_specs=[spec, spec, pl.BlockSpec((tm, RET_V), lambda i: (i, 2))],
        out_specs=spec,
        compiler_params=_params("arbitrary"),
    )(of, ob, p)


def _router_kernel(h_ref, rw_ref, rb_ref, eidx_ref, rank_ref, w_ref, cnt_ref, carry_ref, *, tm):
    i = pl.program_id(0)

    @pl.when(i == 0)
    def _():
        carry_ref[...] = jnp.zeros_like(carry_ref)

    half = D_MODEL // 2
    h_lo, h_hi = _unpack_rows(h_ref[...])
    h_lo = h_lo.astype(BF16)
    h_hi = h_hi.astype(BF16)
    rest = rw_ref[...]
    logits = jnp.zeros((tm, N_EXPERTS), F32)
    for _ in range(3):
        part = rest.astype(BF16)
        rest = rest - part.astype(F32)
        logits = (logits + jnp.dot(h_lo, part[:half], preferred_element_type=F32)
                  + jnp.dot(h_hi, part[half:], preferred_element_type=F32))
    s = _sigmoid(logits)
    sel = s + rb_ref[...]
    lane = lax.broadcasted_iota(jnp.int32, (tm, N_EXPERTS), 1).astype(F32)
    grp = jnp.floor(lane * (1.0 / GROUP_SIZE))
    ninf = -jnp.inf
    none = float(N_EXPERTS)

    gscore = jnp.zeros((tm, N_EXPERTS), F32)
    gcols = []
    for g in range(N_GROUPS):
        in_g = grp == float(g)
        v1 = jnp.max(jnp.where(in_g, sel, ninf), axis=-1, keepdims=True)
        i1 = jnp.min(jnp.where(in_g & (sel == v1), lane, none), axis=-1, keepdims=True)
        v2 = jnp.max(jnp.where(in_g & (lane != i1), sel, ninf), axis=-1, keepdims=True)
        gcols.append(v1 + v2)
        gscore = jnp.where(in_g, v1 + v2, gscore)
    beaten = jnp.zeros((tm, N_EXPERTS), F32)
    for g in range(N_GROUPS):
        wins = (gcols[g] > gscore) | ((gcols[g] == gscore) & (float(g) < grp))
        beaten = beaten + jnp.where(wins, 1.0, 0.0)
    cand = jnp.where(beaten < float(TOPK_GROUPS), sel, ninf)

    lane_k = lax.broadcasted_iota(jnp.int32, (tm, TOP_K), 1)
    eidx = jnp.zeros((tm, TOP_K), F32)
    wsel = jnp.zeros((tm, TOP_K), F32)
    chosen = jnp.zeros((tm, N_EXPERTS), F32)
    picks = []
    for k in range(TOP_K):
        v = jnp.max(cand, axis=-1, keepdims=True)
        ik = jnp.min(jnp.where(cand == v, lane, none), axis=-1, keepdims=True)
        hit = lane == ik
        picks.append(ik)
        eidx = jnp.where(lane_k == k, ik, eidx)
        wsel = jnp.where(lane_k == k, jnp.sum(jnp.where(hit, s, 0.0), axis=-1, keepdims=True), wsel)
        chosen = jnp.where(hit, 1.0, chosen)
        cand = jnp.where(hit, ninf, cand)
    w_ref[...] = wsel / jnp.sum(wsel, axis=-1, keepdims=True) * ROUTED_SCALE
    eidx_ref[...] = eidx.astype(jnp.int32)

    r = lax.broadcasted_iota(jnp.int32, (tm, tm), 0)
    c = lax.broadcasted_iota(jnp.int32, (tm, tm), 1)
    below = jnp.where(c < r, 1.0, 0.0).astype(BF16)
    carry = carry_ref[...]
    pos = jnp.dot(below, chosen.astype(BF16), preferred_element_type=F32) + carry
    rank = jnp.zeros((tm, TOP_K), jnp.int32)
    for k in range(TOP_K):
        rk = jnp.sum(jnp.where(lane == picks[k], pos, 0.0), axis=-1, keepdims=True)
        rank = jnp.where(lane_k == k, rk.astype(jnp.int32), rank)
    rank_ref[...] = rank
    carry = carry + jnp.sum(chosen, axis=0, keepdims=True)
    carry_ref[...] = carry
    cnt_ref[...] = carry


def _router(h_packed, rw, rb, geo):
    tm = 256 if geo.T % 256 == 0 else 128
    tok = pl.BlockSpec((tm, TOP_K), lambda i: (i, 0))
    one = pl.BlockSpec((1, N_EXPERTS), lambda i: (0, 0))
    return pl.pallas_call(
        functools.partial(_router_kernel, tm=tm),
        name="router",
        out_shape=[jax.ShapeDtypeStruct((geo.T, TOP_K), jnp.int32),
                   jax.ShapeDtypeStruct((geo.T, TOP_K), jnp.int32),
                   jax.ShapeDtypeStruct((geo.T, TOP_K), F32),
                   jax.ShapeDtypeStruct((1, N_EXPERTS), F32)],
        grid=(geo.T // tm,),
        in_specs=[pl.BlockSpec((tm, D_MODEL // 2), lambda i: (i, 0)),
                  pl.BlockSpec((D_MODEL, N_EXPERTS), lambda i: (0, 0)), one],
        out_specs=[tok, tok, tok, one],
        scratch_shapes=[pltpu.VMEM((1, N_EXPERTS), F32)],
        compiler_params=_params("arbitrary"),
    )(h_packed, rw, rb.reshape(1, N_EXPERTS))


def _n_blocks(geo):
    return -(-(geo.T * TOP_K) // MOE_BLOCK) + N_EXPERTS


def _dest_kernel(cnt_ref, eidx_ref, rank_ref, dest_ref, blk_ref, *, n_blocks):
    eidx = eidx_ref[...]
    dest = rank_ref[...]
    blk_row = (lax.broadcasted_iota(jnp.int32, blk_ref.shape, 0) * 128
               + lax.broadcasted_iota(jnp.int32, blk_ref.shape, 1)) * MOE_BLOCK
    blk = jnp.zeros(blk_ref.shape, jnp.int32)
    start = jnp.int32(0)
    for e in range(N_EXPERTS):
        padded = (cnt_ref[e] + (MOE_BLOCK - 1)) // MOE_BLOCK * MOE_BLOCK
        dest = dest + jnp.where(eidx == e, start, 0)
        start = start + padded
        blk = blk + jnp.where(start <= blk_row, 1, 0)
    dest_ref[...] = dest
    blk_ref[...] = jnp.where(blk_row == n_blocks * MOE_BLOCK, start // MOE_BLOCK,
                             jnp.minimum(blk, N_EXPERTS - 1))


def _dest(counts, eidx, rank, geo):
    rows = geo.T * TOP_K // 128
    brow = -(-(_n_blocks(geo) + 1) // 128)
    full = pl.BlockSpec((rows, 128), lambda: (0, 0))
    dest, blk = pl.pallas_call(
        functools.partial(_dest_kernel, n_blocks=_n_blocks(geo)),
        name="dest",
        out_shape=[jax.ShapeDtypeStruct((rows, 128), jnp.int32),
                   jax.ShapeDtypeStruct((brow, 128), jnp.int32)],
        in_specs=[pl.BlockSpec(memory_space=pltpu.SMEM), full, full],
        out_specs=[full, pl.BlockSpec((brow, 128), lambda: (0, 0))],
    )(counts, eidx.reshape(rows, 128), rank.reshape(rows, 128))
    return dest.reshape(-1), blk.reshape(-1)[:_n_blocks(geo) + 1]


def _zero_fill(cnt_ref, xs_hbm, zero_ref, zsem, n_rows, wait):
    def piece(pos, size):
        if size >= 8:
            copies = [(pl.multiple_of(pos, 8), size)]
        else:
            copies = [(pos + r, 1) for r in range(size)]
        for p, s in copies:
            cp = pltpu.make_async_copy(zero_ref.at[pl.ds(0, s)], xs_hbm.at[pl.ds(p, s)], zsem)
            cp.wait() if wait else cp.start()

    def per_expert(e, start):
        cnt = cnt_ref[e]
        padded = (cnt + (MOE_BLOCK - 1)) // MOE_BLOCK * MOE_BLOCK
        pad = padded - cnt
        pos = start + cnt
        size = 1
        while size < MOE_BLOCK:
            take = (pad & size) != 0
            pl.when(take)(functools.partial(piece, pos, size))
            pos = pos + jnp.where(take, size, 0)
            size *= 2
        return start + padded

    end = lax.fori_loop(0, N_EXPERTS, per_expert, jnp.int32(0))

    def per_block(j, carry):
        piece(end + j * MOE_BLOCK, MOE_BLOCK)
        return carry

    lax.fori_loop(0, (n_rows - end) // MOE_BLOCK, per_block, 0)


def _sc_scatter_rows(src, order, n_rows):
    nw = V7X_SC_CORES * V7X_SC_SUBCORES
    rows = 2 * SC_ROWS
    t, w = src.shape
    per = t // nw
    assert t % (nw * rows) == 0
    mesh = plsc.VectorSubcoreMesh(core_axis_name="c", subcore_axis_name="s")

    @functools.partial(
        pl.kernel, mesh=mesh,
        out_type=jax.ShapeDtypeStruct((n_rows, w), src.dtype),
        scratch_types=[pltpu.VMEM((rows, w), src.dtype)]
        + [pltpu.VMEM((rows,), jnp.int32)] * TOP_K + [pltpu.SemaphoreType.DMA],
    )
    def scatter(src_hbm, idx_hbm, out_hbm, rows_v, *rest):
        idx_vs, sem = rest[:TOP_K], rest[TOP_K]
        wid = lax.axis_index("s") * V7X_SC_CORES + lax.axis_index("c")
        base = wid * per

        @pl.loop(0, per // rows)
        def _(j):
            off = pl.multiple_of(base + j * rows, 8)
            pltpu.sync_copy(src_hbm.at[pl.ds(off, rows)], rows_v)
            for k in range(TOP_K):
                pltpu.sync_copy(idx_hbm.at[pl.ds(pl.multiple_of(k * t + off, 8), rows)], idx_vs[k])
            copies = [pltpu.async_copy(rows_v, out_hbm.at[idx_vs[k]], sem) for k in range(TOP_K)]
            for cp in copies:
                cp.wait()

    return scatter(src, order)


def _zero_pad_kernel(cnt_ref, xs_in, xs_hbm, zero_ref, zsem, *, n_rows):
    del xs_in
    zero_ref[...] = jnp.zeros_like(zero_ref)
    _zero_fill(cnt_ref, xs_hbm, zero_ref, zsem, n_rows, wait=False)
    _zero_fill(cnt_ref, xs_hbm, zero_ref, zsem, n_rows, wait=True)


def _dispatch(counts, order, h_packed, geo):
    n_rows = _n_blocks(geo) * MOE_BLOCK
    width = D_MODEL // 2
    xs = _sc_scatter_rows(h_packed[:geo.T], order, n_rows)
    return pl.pallas_call(
        functools.partial(_zero_pad_kernel, n_rows=n_rows),
        name="zero_pad",
        out_shape=jax.ShapeDtypeStruct((n_rows, width), PACKED),
        in_specs=[pl.BlockSpec(memory_space=pltpu.SMEM), pl.BlockSpec(memory_space=pl.ANY)],
        out_specs=pl.BlockSpec(memory_space=pl.ANY),
        scratch_shapes=[pltpu.VMEM((MOE_BLOCK, width), PACKED), pltpu.SemaphoreType.DMA],
        input_output_aliases={1: 0},
    )(counts, xs)


def _swiglu_packed(x_ref, wg_ref, wu_ref, wd_ref):
    half = D_MODEL // 2
    lo, hi = _unpack_rows(x_ref[...])
    lo = lo.astype(BF16)
    hi = hi.astype(BF16)

    def proj(w_ref):
        return (jnp.dot(lo, w_ref[:half], preferred_element_type=F32)
                + jnp.dot(hi, w_ref[half:], preferred_element_type=F32))

    g = proj(wg_ref)
    u = proj(wu_ref)
    return jnp.dot((_silu(g) * u).astype(BF16), wd_ref[...], preferred_element_type=F32)


def _expert_kernel(blk_ref, x_ref, wg_ref, wu_ref, wd_ref, o_ref, wgb_ref, wub_ref, wdb_ref, *, nb):
    j = pl.program_id(0)
    n_used = blk_ref[nb]

    @pl.when((j == 0) | (blk_ref[j] != blk_ref[jnp.maximum(j - 1, 0)]))
    def _():
        wgb_ref[...] = wg_ref[...].astype(BF16)
        wub_ref[...] = wu_ref[...].astype(BF16)
        wdb_ref[...] = wd_ref[...].astype(BF16)

    @pl.when(j < n_used)
    def _():
        o_ref[...] = _pack_rows(_swiglu_packed(x_ref, wgb_ref, wub_ref, wdb_ref))

    @pl.when(j >= n_used)
    def _():
        o_ref[...] = jnp.zeros_like(o_ref)


def _experts(blk_e, xs, wg, wu, wd, layer, geo):
    nb = _n_blocks(geo)
    rows = pl.BlockSpec((MOE_BLOCK, D_MODEL // 2), lambda j, be: (j, 0))

    def wspec(r, c):
        return pl.BlockSpec((None, None, r, c), lambda j, be: (layer, be[j], 0, 0))

    return pl.pallas_call(
        functools.partial(_expert_kernel, nb=nb),
        name="experts",
        out_shape=jax.ShapeDtypeStruct((nb * MOE_BLOCK, D_MODEL // 2), PACKED),
        grid_spec=pltpu.PrefetchScalarGridSpec(
            num_scalar_prefetch=1,
            grid=(nb,),
            in_specs=[rows, wspec(D_MODEL, EXPERT_FF), wspec(D_MODEL, EXPERT_FF),
                      wspec(EXPERT_FF, D_MODEL)],
            out_specs=rows,
            scratch_shapes=[pltpu.VMEM((D_MODEL, EXPERT_FF), BF16),
                            pltpu.VMEM((D_MODEL, EXPERT_FF), BF16),
                            pltpu.VMEM((EXPERT_FF, D_MODEL), BF16)]),
        compiler_params=_params("arbitrary"),
    )(blk_e, xs, wg, wu, wd)


def _sc_gather_rows(table, idx):
    nw = V7X_SC_CORES * V7X_SC_SUBCORES
    m = idx.shape[0]
    w = table.shape[1]
    per = m // nw
    n = per // SC_ROWS
    assert m % (nw * SC_ROWS * 2) == 0
    mesh = plsc.VectorSubcoreMesh(core_axis_name="c", subcore_axis_name="s")

    @functools.partial(
        pl.kernel, mesh=mesh,
        out_type=jax.ShapeDtypeStruct((m, w), table.dtype),
        scratch_types=[pltpu.VMEM((SC_ROWS,), jnp.int32)] * 2
        + [pltpu.VMEM((SC_ROWS, w), table.dtype)] * 2 + [pltpu.SemaphoreType.DMA] * 2,
    )
    def gather(table_hbm, idx_hbm, out_hbm, *scratch):
        idx_vs, rows_vs, sems = scratch[:2], scratch[2:4], scratch[4:]
        wid = lax.axis_index("s") * V7X_SC_CORES + lax.axis_index("c")
        base = wid * per

        def fetch(b):
            return pltpu.make_async_copy(table_hbm.at[idx_vs[b]], rows_vs[b], sems[b])

        def start(chunk, b):
            off = pl.multiple_of(base + chunk * SC_ROWS, 8)
            pltpu.sync_copy(idx_hbm.at[pl.ds(off, SC_ROWS)], idx_vs[b])
            fetch(b).start()

        start(0, 0)

        @pl.loop(0, n, step=2)
        def _(j):
            for b in range(2):
                cur = j + b
                pl.when(cur + 1 < n)(functools.partial(start, cur + 1, 1 - b))
                fetch(b).wait()
                off = pl.multiple_of(base + cur * SC_ROWS, 8)
                pltpu.sync_copy(rows_vs[b], out_hbm.at[pl.ds(off, SC_ROWS)])

    return gather(table, idx)


def _combine_kernel(w_ref, x_ref, h_ref, sg_ref, su_ref, sd_ref, m_ref, *rest, post, n_prev):
    rest = rest[:-2 * n_prev] + rest[-n_prev:] if n_prev else rest
    if post == "next":
        ng_ref, nsh_ref, nsc_ref, g_ref, o_ref, hn_ref = rest
    else:
        ng_ref, g_ref, o_ref = rest
    half = D_MODEL // 2
    shared = _swiglu_packed(h_ref, sg_ref, su_ref, sd_ref)
    w = w_ref[...]
    acc_lo = shared[:, :half]
    acc_hi = shared[:, half:]
    for k in range(TOP_K):
        lo, hi = _unpack_rows(g_ref[k])
        acc_lo = acc_lo + w[:, k:k + 1] * lo
        acc_hi = acc_hi + w[:, k:k + 1] * hi
    gate = m_ref[0]
    y_lo = x_ref[:, :half] + gate[:, :half] * acc_lo
    y_hi = x_ref[:, half:] + gate[:, half:] * acc_hi
    ms = (jnp.sum(y_lo * y_lo, axis=-1, keepdims=True)
          + jnp.sum(y_hi * y_hi, axis=-1, keepdims=True)) * (1.0 / D_MODEL)
    inv = lax.rsqrt(ms + EPS)
    ng = ng_ref[...]
    n_lo = y_lo * inv * ng[:, :half]
    n_hi = y_hi * inv * ng[:, half:]
    if post == "next":
        o_ref[:, :half] = y_lo
        o_ref[:, half:] = y_hi
        sc = nsc_ref[0]
        sft = nsh_ref[0]
        hn_ref[:, :half] = (n_lo * (1.0 + sc[:, :half]) + sft[:, :half]).astype(BF16)
        hn_ref[:, half:] = (n_hi * (1.0 + sc[:, half:]) + sft[:, half:]).astype(BF16)
    else:
        o_ref[:, :half] = n_lo
        o_ref[:, half:] = n_hi


def _combine(dest, w, x, h_packed, sg, su, sd, ys, mod3, m_gate, post, norm_g, next_mod3, geo):
    half = D_MODEL // 2
    tt = _pick_tile(geo, (256, 128))
    ff = sg.shape[1]
    unit = math.lcm(tt, V7X_SC_CORES * V7X_SC_SUBCORES * SC_ROWS * 2 // TOP_K)
    cut = (geo.T // 2) // unit * unit
    parts = [(0, cut), (cut, geo.T)] if 0 < cut < geo.T else [(0, geo.T)]
    dest2 = dest.reshape(geo.T, TOP_K)
    outs = []
    for t0, t1 in parts:
        tile0, n = t0 // tt, (t1 - t0) // tt
        order = dest2[t0:t1].T.reshape(-1)
        gathered = _sc_gather_rows(ys, order).reshape(TOP_K, t1 - t0, half)
        rows = pl.BlockSpec((tt, D_MODEL), lambda i: (i + tile0, 0))
        vec = pl.BlockSpec((1, D_MODEL), lambda i: (0, 0))

        def mod(m):
            return pl.BlockSpec((1, 1, D_MODEL),
                                lambda i: (_mod_row(i + tile0, tt, geo) * N_MOD + m, 0, 0))

        in_specs = [pl.BlockSpec((tt, TOP_K), lambda i: (i + tile0, 0)),
                    rows,
                    pl.BlockSpec((tt, half), lambda i: (i + tile0, 0)),
                    pl.BlockSpec((D_MODEL, ff), lambda i: (0, 0)),
                    pl.BlockSpec((D_MODEL, ff), lambda i: (0, 0)),
                    pl.BlockSpec((ff, D_MODEL), lambda i: (0, 0)),
                    mod(m_gate), vec]
        args = [w, x, h_packed, sg, su, sd, mod3, norm_g.reshape(1, D_MODEL)]
        out_shape = [jax.ShapeDtypeStruct((geo.T, D_MODEL), F32)]
        out_specs = [rows]
        if post == "next":
            in_specs += [mod(0), mod(1)]
            args += [next_mod3, next_mod3]
            out_shape.append(jax.ShapeDtypeStruct((geo.T, D_MODEL), BF16))
            out_specs.append(rows)
        in_specs.append(pl.BlockSpec((TOP_K, tt, half), lambda i: (0, i, 0)))
        args.append(gathered)
        aliases = {len(args) + j: j for j in range(len(outs))}
        in_specs += [pl.BlockSpec(memory_space=pl.ANY)] * len(outs)
        args += outs
        outs = pl.pallas_call(
            functools.partial(_combine_kernel, post=post, n_prev=len(outs)),
            name="combine",
            out_shape=out_shape,
            grid=(n,),
            in_specs=in_specs,
            out_specs=out_specs,
            input_output_aliases=aliases,
            compiler_params=_params("arbitrary"),
        )(*args)
    return outs


def _moe(x, h_packed, mod3, rw, rb, wg, wu, wd, layer, sg, su, sd, post, norm_g, next_mod3, geo):
    eidx, rank, w, counts = _router(h_packed, rw, rb, geo)
    counts = counts.reshape(N_EXPERTS).astype(jnp.int32)
    dest, blk_e = _dest(counts, eidx, rank, geo)
    order = dest.reshape(geo.T, TOP_K).T.reshape(-1)
    xs = _dispatch(counts, order, h_packed, geo)
    ys = _experts(blk_e, xs, wg, wu, wd, layer, geo)
    return _combine(dest, w, x, h_packed, sg.astype(BF16), su.astype(BF16), sd.astype(BF16), ys,
                    mod3, 5, post, norm_g, next_mod3, geo)


def kernel(x, c, ctx, c_ctx, ada_w, ada_b, norm_mix, norm_ffn, norm_final, ev_w_in, ev_w_out, hgrn_lb, hgrn_norm, conv_w, conv_b, conv_norm_g, conv_norm_b, ret_w_in, ret_w_out, ret_decay, router_w, router_b, exp_gate, exp_up, exp_down, sh_gate, sh_up, sh_down):
    b, n, d = x.shape
    lc = ctx.shape[1]
    depth = ada_w.shape[0]
    geo = _geo(b, n, lc)
    assert d == D_MODEL and b < MOD_ROWS
    assert n % RET_CHUNK == 0 and lc % RET_CHUNK == 0

    xs = (x.reshape(geo.BN, d), ctx.reshape(geo.BL, d))
    cond = jnp.zeros((MOD_ROWS, d), F32).at[:b].set(c).at[b].set(c_ctx)
    cos_tab, sin_tab = _rope_tables(n)

    mods = [_adaln(cond, ada_w, ada_b, l).reshape(MOD_ROWS * N_MOD, 1, d) for l in range(depth)]
    h = _normmod(xs, norm_mix[0], mods[0], 0, 1, geo, packed=False)
    for l in range(depth):
        j = l // 2
        last = l == depth - 1
        mod3 = mods[l]
        tail = geo._replace(T=geo.BN, BL=0, Lc=0) if last else geo
        if l % 2 == 0:
            p = _matmul(h, ev_w_in[j].astype(BF16), geo)
            of, ob = _hgrn_scan(p, hgrn_lb, l, geo)
            mix = _ev_readout(p, of, ob, hgrn_norm[j], conv_w[j], conv_b[j],
                              conv_norm_g[j], conv_norm_b[j], geo)
            xs = _matmul_resid(mix, ev_w_out[j].astype(BF16), xs, mod3, 2, tail)
        else:
            p = _matmul(h, ret_w_in[j].astype(BF16), geo)
            of, ob = _ret_scan(p, cos_tab, sin_tab, ret_decay[j], geo)
            mix = _ret_readout(p, of, ob, geo)
            xs = _matmul_resid(mix, ret_w_out[j].astype(BF16), xs, mod3, 2, tail)
        h_packed = _normmod(xs, norm_ffn[l], mod3, 3, 4, tail, packed=True)
        moe_w = (router_w[l], router_b[l], exp_gate, exp_up, exp_down, l, sh_gate[l], sh_up[l], sh_down[l])
        if last:
            (out,) = _moe(xs, h_packed, mod3, *moe_w, "final", norm_final, None, tail)
        else:
            xs, h = _moe(xs, h_packed, mod3, *moe_w, "next", norm_mix[l + 1], mods[l + 1], tail)
    return out.reshape(b, n, d)
```
